```python
import math
import jax, jax.numpy as jnp
from jax import lax
import numpy as np

D_MODEL = 1024
BATCH = 8
SEQ = 4096
DEPTH = 4

BRANCH_WIDTH = D_MODEL // 2
N_BRANCH = 3
S5_WIDTH = BRANCH_WIDTH
S5_GROUP = 16
S5_GROUPS = S5_WIDTH // S5_GROUP
S5_STATE = 64
S5_DT_MIN = 0.001
S5_DT_MAX = 0.1
LRU_WIDTH = BRANCH_WIDTH
LRU_HEADS = 8
LRU_HEAD_DIM = LRU_WIDTH // LRU_HEADS
LRU_C = 8.0
LRU_A_MIN = 0.9
LRU_A_MAX = 0.999
CONV_WIDTH = 4
FOX_HEAD_DIM = 64
FOX_HEADS = BRANCH_WIDTH // FOX_HEAD_DIM
FOX_WIDTH = FOX_HEADS * FOX_HEAD_DIM
FOX_FORGET_BIAS = 3.0
Q_BLOCK = 128
FFN_HIDDEN = ((8 * D_MODEL + 3 * 256 - 1) // (3 * 256)) * 256
ALPHA = (2.0 * DEPTH) ** 0.25
BETA = (8.0 * DEPTH) ** -0.25
LN_EPS = 1e-5
IN_SIZES = (S5_WIDTH, LRU_WIDTH, LRU_WIDTH, FOX_WIDTH, FOX_WIDTH, FOX_WIDTH, FOX_HEADS, N_BRANCH * D_MODEL)
IN_TOTAL = sum(IN_SIZES)

kernel_name = "hybrid_s5_rglru_fox_deepnorm"


def _layer_norm(x, g, b):
    x32 = x.astype(jnp.float32)
    mu = jnp.mean(x32, axis=-1, keepdims=True)
    var = jnp.mean(jnp.square(x32 - mu), axis=-1, keepdims=True)
    y = (x32 - mu) * lax.rsqrt(var + LN_EPS) * g.astype(jnp.float32) + b.astype(jnp.float32)
    return y.astype(x.dtype)


def _linear_scan(a, b):
    def combine(left, right):
        a_l, b_l = left
        a_r, b_r = right
        return a_r * a_l, a_r * b_l + b_r
    _, h = lax.associative_scan(combine, (a, b), axis=1)
    return h


def _s5_branch(u, a_re, a_im, log_dt, b_re, b_im, c_re, c_im, d_skip, w_glu, b_glu):
    bsz, L, _ = u.shape
    f32 = jnp.float32
    ug = u.astype(f32).reshape(bsz, L, S5_GROUPS, S5_GROUP)
    lam = lax.complex(a_re.astype(f32), a_im.astype(f32))
    dt = jnp.exp(log_dt.astype(f32))[:, None]
    lam_bar = jnp.exp(lam * dt)
    b_c = lax.complex(b_re.astype(f32), b_im.astype(f32))
    b_bar = ((lam_bar - 1.0) / lam)[:, :, None] * b_c
    bu = jnp.einsum('blgc,gpc->blgp', ug.astype(jnp.complex64), b_bar)
    h = _linear_scan(jnp.broadcast_to(lam_bar, bu.shape), bu)
    c_c = lax.complex(c_re.astype(f32), c_im.astype(f32))
    y = jnp.einsum('blgp,gcp->blgc', h, c_c).real + d_skip.astype(f32).reshape(S5_GROUPS, S5_GROUP) * ug
    y = jax.nn.gelu(y.reshape(bsz, L, S5_WIDTH))
    y = y * jax.nn.sigmoid(y @ w_glu.astype(f32) + b_glu.astype(f32))
    return y.astype(u.dtype)


def _rglru_branch(xb, gate, conv_w, conv_b, w_a, b_a, w_x, b_x, lam):
    bsz, L, _ = xb.shape
    f32 = jnp.float32
    xp = jnp.pad(xb, ((0, 0), (CONV_WIDTH - 1, 0), (0, 0)))
    xc = conv_b + sum(conv_w[k] * xp[:, k:k + L] for k in range(CONV_WIDTH))
    xh = xc.reshape(bsz, L, LRU_HEADS, LRU_HEAD_DIM)
    r = jax.nn.sigmoid(jnp.einsum('blhi,hij->blhj', xh, w_a) + b_a).astype(f32)
    i = jax.nn.sigmoid(jnp.einsum('blhi,hij->blhj', xh, w_x) + b_x).astype(f32)
    log_a = -LRU_C * jax.nn.softplus(-lam.astype(f32).reshape(LRU_HEADS, LRU_HEAD_DIM)) * r
    a = jnp.exp(log_a)
    mult = jnp.sqrt(-jnp.expm1(2.0 * log_a))
    h = _linear_scan(a, mult * (i * xh.astype(f32)))
    y = jax.nn.gelu(gate.astype(f32)) * h.reshape(bsz, L, LRU_WIDTH)
    return y.astype(xb.dtype)


def _fox_branch(q, k, v, fg_logit, b_f):
    bsz, L, _ = q.shape
    f32 = jnp.float32
    q = q.reshape(bsz, L, FOX_HEADS, FOX_HEAD_DIM)
    k = k.reshape(bsz, L, FOX_HEADS, FOX_HEAD_DIM)
    v = v.reshape(bsz, L, FOX_HEADS, FOX_HEAD_DIM)
    log_f = jax.nn.log_sigmoid((fg_logit + b_f).astype(f32))
    cum = jnp.cumsum(log_f, axis=1).transpose(0, 2, 1)
    kpos = jnp.arange(L)
    scale = FOX_HEAD_DIM ** -0.5

    def one_block(blk):
        start = blk * Q_BLOCK
        qb = lax.dynamic_slice_in_dim(q, start, Q_BLOCK, axis=1)
        cq = lax.dynamic_slice_in_dim(cum, start, Q_BLOCK, axis=2)
        s = jnp.einsum('bqhd,bkhd->bhqk', qb, k).astype(f32) * scale
        s = s + cq[..., None] - cum[:, :, None, :]
        qpos = start + jnp.arange(Q_BLOCK)
        s = jnp.where(kpos[None, :] <= qpos[:, None], s, -jnp.inf)
        p = jax.nn.softmax(s, axis=-1)
        return jnp.einsum('bhqk,bkhd->bqhd', p.astype(v.dtype), v)

    out = lax.map(one_block, jnp.arange(L // Q_BLOCK))
    return out.transpose(1, 0, 2, 3, 4).reshape(bsz, L, FOX_WIDTH)


def _fwd_setup_inputs(seed: int = 0) -> dict:
    key = jax.random.key(seed)
    ks = iter(jax.random.split(key, 40))
    f32 = jnp.float32

    def nrm(shape, scale):
        return scale * jax.random.normal(next(ks), shape, f32)

    n = jnp.arange(S5_STATE, dtype=f32)
    x = nrm((BATCH, SEQ, D_MODEL), 1.0)
    w_in = nrm((DEPTH, D_MODEL, IN_TOTAL), D_MODEL ** -0.5)
    b_f = FOX_FORGET_BIAS + nrm((DEPTH, FOX_HEADS), 0.1)
    b_gate = nrm((DEPTH, N_BRANCH * D_MODEL), 0.01)
    s5_a_re = -0.5 + nrm((DEPTH, S5_GROUPS, S5_STATE), 0.01)
    s5_a_im = math.pi * n + nrm((DEPTH, S5_GROUPS, S5_STATE), 0.01)
    s5_log_dt = jax.random.uniform(next(ks), (DEPTH, S5_GROUPS), f32, math.log(S5_DT_MIN), math.log(S5_DT_MAX))
    s5_b_re = nrm((DEPTH, S5_GROUPS, S5_STATE, S5_GROUP), (2 * S5_GROUP) ** -0.5)
    s5_b_im = nrm((DEPTH, S5_GROUPS, S5_STATE, S5_GROUP), (2 * S5_GROUP) ** -0.5)
    s5_c_re = nrm((DEPTH, S5_GROUPS, S5_GROUP, S5_STATE), (2 * S5_STATE) ** -0.5)
    s5_c_im = nrm((DEPTH, S5_GROUPS, S5_GROUP, S5_STATE), (2 * S5_STATE) ** -0.5)
    s5_d = nrm((DEPTH, S5_WIDTH), 1.0)
    s5_w_glu = nrm((DEPTH, S5_WIDTH, S5_WIDTH), S5_WIDTH ** -0.5)
    s5_b_glu = nrm((DEPTH, S5_WIDTH), 0.01)
    lru_conv_w = nrm((DEPTH, CONV_WIDTH, LRU_WIDTH), CONV_WIDTH ** -0.5)
    lru_conv_b = nrm((DEPTH, LRU_WIDTH), 0.01)
    lru_w_a = nrm((DEPTH, LRU_HEADS, LRU_HEAD_DIM, LRU_HEAD_DIM), LRU_HEAD_DIM ** -0.5)
    lru_b_a = nrm((DEPTH, LRU_HEADS, LRU_HEAD_DIM), 0.01)
    lru_w_x = nrm((DEPTH, LRU_HEADS, LRU_HEAD_DIM, LRU_HEAD_DIM), LRU_HEAD_DIM ** -0.5)
    lru_b_x = nrm((DEPTH, LRU_HEADS, LRU_HEAD_DIM), 0.01)
    a_c = jax.random.uniform(next(ks), (DEPTH, LRU_WIDTH), f32, LRU_A_MIN, LRU_A_MAX)
    sig = a_c ** (1.0 / LRU_C)
    lru_lambda = jnp.log(sig) - jnp.log1p(-sig)
    w_branch = nrm((DEPTH, N_BRANCH, BRANCH_WIDTH, D_MODEL), BRANCH_WIDTH ** -0.5)
    w_out = nrm((DEPTH, D_MODEL, D_MODEL), BETA * D_MODEL ** -0.5)
    ln1_g = 1.0 + nrm((DEPTH, D_MODEL), 0.01)
    ln1_b = nrm((DEPTH, D_MODEL), 0.01)
    w_ffn_gate = nrm((DEPTH, D_MODEL, FFN_HIDDEN), D_MODEL ** -0.5)
    w_ffn_up = nrm((DEPTH, D_MODEL, FFN_HIDDEN), D_MODEL ** -0.5)
    w_ffn_down = nrm((DEPTH, FFN_HIDDEN, D_MODEL), BETA * FFN_HIDDEN ** -0.5)
    ln2_g = 1.0 + nrm((DEPTH, D_MODEL), 0.01)
    ln2_b = nrm((DEPTH, D_MODEL), 0.01)
    return {"x": x, "w_in": w_in, "b_f": b_f, "b_gate": b_gate,
            "s5_a_re": s5_a_re, "s5_a_im": s5_a_im, "s5_log_dt": s5_log_dt,
            "s5_b_re": s5_b_re, "s5_b_im": s5_b_im, "s5_c_re": s5_c_re, "s5_c_im": s5_c_im,
            "s5_d": s5_d, "s5_w_glu": s5_w_glu, "s5_b_glu": s5_b_glu,
            "lru_conv_w": lru_conv_w, "lru_conv_b": lru_conv_b,
            "lru_w_a": lru_w_a, "lru_b_a": lru_b_a, "lru_w_x": lru_w_x, "lru_b_x": lru_b_x,
            "lru_lambda": lru_lambda, "w_branch": w_branch, "w_out": w_out,
            "ln1_g": ln1_g, "ln1_b": ln1_b,
            "w_ffn_gate": w_ffn_gate, "w_ffn_up": w_ffn_up, "w_ffn_down": w_ffn_down,
            "ln2_g": ln2_g, "ln2_b": ln2_b}


def _fwd_reference(x, w_in, b_f, b_gate, s5_a_re, s5_a_im, s5_log_dt, s5_b_re, s5_b_im, s5_c_re, s5_c_im,
              s5_d, s5_w_glu, s5_b_glu, lru_conv_w, lru_conv_b, lru_w_a, lru_b_a, lru_w_x, lru_b_x,
              lru_lambda, w_branch, w_out, ln1_g, ln1_b, w_ffn_gate, w_ffn_up, w_ffn_down, ln2_g, ln2_b):
    split_at = np.cumsum(IN_SIZES)[:-1].tolist()
    bsz, L, _ = x.shape
    for l in range(DEPTH):
        z = x @ w_in[l]
        u_s5, x_lru, g_lru, q, k, v, fg, gate_logits = jnp.split(z, split_at, axis=-1)
        y_s5 = _s5_branch(u_s5, s5_a_re[l], s5_a_im[l], s5_log_dt[l], s5_b_re[l], s5_b_im[l],
                          s5_c_re[l], s5_c_im[l], s5_d[l], s5_w_glu[l], s5_b_glu[l])
        y_lru = _rglru_branch(x_lru, g_lru, lru_conv_w[l], lru_conv_b[l], lru_w_a[l], lru_b_a[l],
                              lru_w_x[l], lru_b_x[l], lru_lambda[l])
        y_fox = _fox_branch(q, k, v, fg, b_f[l])
        ys = jnp.stack([y_s5, y_lru, y_fox], axis=2)
        proj = jnp.einsum('blkc,kcd->blkd', ys, w_branch[l])
        gates = jax.nn.sigmoid(gate_logits + b_gate[l]).reshape(bsz, L, N_BRANCH, D_MODEL)
        mixed = jnp.sum(gates * proj, axis=2) @ w_out[l]
        x = _layer_norm(ALPHA * x + mixed, ln1_g[l], ln1_b[l])
        hid = jax.nn.silu(x @ w_ffn_gate[l]) * (x @ w_ffn_up[l])
        x = _layer_norm(ALPHA * x + hid @ w_ffn_down[l], ln2_g[l], ln2_b[l])
    return x


import jax as _jax
import jax.numpy as _jnp

TWIN_FORMAT = 'train_step'
FWD_PARAMS = ['x', 'w_in', 'b_f', 'b_gate', 's5_a_re', 's5_a_im', 's5_log_dt', 's5_b_re', 's5_b_im', 's5_c_re', 's5_c_im', 's5_d', 's5_w_glu', 's5_b_glu', 'lru_conv_w', 'lru_conv_b', 'lru_w_a', 'lru_b_a', 'lru_w_x', 'lru_b_x', 'lru_lambda', 'w_branch', 'w_out', 'ln1_g', 'ln1_b', 'w_ffn_gate', 'w_ffn_up', 'w_ffn_down', 'ln2_g', 'ln2_b']
TWIN_WEIGHTS = ['w_in', 'b_f', 'b_gate', 's5_a_re', 's5_a_im', 's5_log_dt', 's5_b_re', 's5_b_im', 's5_c_re', 's5_c_im', 's5_d', 's5_w_glu', 's5_b_glu', 'lru_conv_w', 'lru_conv_b', 'lru_w_a', 'lru_b_a', 'lru_w_x', 'lru_b_x', 'lru_lambda', 'w_branch', 'w_out', 'ln1_g', 'ln1_b', 'w_ffn_gate', 'w_ffn_up', 'w_ffn_down', 'ln2_g', 'ln2_b']
TWIN_DIFF_INPUT = 'x'
TWIN_INPUTS = ['x', 'w_in', 'b_f', 'b_gate', 's5_a_re', 's5_a_im', 's5_log_dt', 's5_b_re', 's5_b_im', 's5_c_re', 's5_c_im', 's5_d', 's5_w_glu', 's5_b_glu', 'lru_conv_w', 'lru_conv_b', 'lru_w_a', 'lru_b_a', 'lru_w_x', 'lru_b_x', 'lru_lambda', 'w_branch', 'w_out', 'ln1_g', 'ln1_b', 'w_ffn_gate', 'w_ffn_up', 'w_ffn_down', 'ln2_g', 'ln2_b', 'loss_target', 'm_w_in', 'm_b_f', 'm_b_gate', 'm_s5_a_re', 'm_s5_a_im', 'm_s5_log_dt', 'm_s5_b_re', 'm_s5_b_im', 'm_s5_c_re', 'm_s5_c_im', 'm_s5_d', 'm_s5_w_glu', 'm_s5_b_glu', 'm_lru_conv_w', 'm_lru_conv_b', 'm_lru_w_a', 'm_lru_b_a', 'm_lru_w_x', 'm_lru_b_x', 'm_lru_lambda', 'm_w_branch', 'm_w_out', 'm_ln1_g', 'm_ln1_b', 'm_w_ffn_gate', 'm_w_ffn_up', 'm_w_ffn_down', 'm_ln2_g', 'm_ln2_b', 'v_w_in', 'v_b_f', 'v_b_gate', 'v_s5_a_re', 'v_s5_a_im', 'v_s5_log_dt', 'v_s5_b_re', 'v_s5_b_im', 'v_s5_c_re', 'v_s5_c_im', 'v_s5_d', 'v_s5_w_glu', 'v_s5_b_glu', 'v_lru_conv_w', 'v_lru_conv_b', 'v_lru_w_a', 'v_lru_b_a', 'v_lru_w_x', 'v_lru_b_x', 'v_lru_lambda', 'v_w_branch', 'v_w_out', 'v_ln1_g', 'v_ln1_b', 'v_w_ffn_gate', 'v_w_ffn_up', 'v_w_ffn_down', 'v_ln2_g', 'v_ln2_b']
TWIN_OUTPUTS = ['loss', 'grad_x', 'grad_w_in', 'grad_b_f', 'grad_b_gate', 'grad_s5_a_re', 'grad_s5_a_im', 'grad_s5_log_dt', 'grad_s5_b_re', 'grad_s5_b_im', 'grad_s5_c_re', 'grad_s5_c_im', 'grad_s5_d', 'grad_s5_w_glu', 'grad_s5_b_glu', 'grad_lru_conv_w', 'grad_lru_conv_b', 'grad_lru_w_a', 'grad_lru_b_a', 'grad_lru_w_x', 'grad_lru_b_x', 'grad_lru_lambda', 'grad_w_branch', 'grad_w_out', 'grad_ln1_g', 'grad_ln1_b', 'grad_w_ffn_gate', 'grad_w_ffn_up', 'grad_w_ffn_down', 'grad_ln2_g', 'grad_ln2_b', 'delta_w_in', 'delta_b_f', 'delta_b_gate', 'delta_s5_a_re', 'delta_s5_a_im', 'delta_s5_log_dt', 'delta_s5_b_re', 'delta_s5_b_im', 'delta_s5_c_re', 'delta_s5_c_im', 'delta_s5_d', 'delta_s5_w_glu', 'delta_s5_b_glu', 'delta_lru_conv_w', 'delta_lru_conv_b', 'delta_lru_w_a', 'delta_lru_b_a', 'delta_lru_w_x', 'delta_lru_b_x', 'delta_lru_lambda', 'delta_w_branch', 'delta_w_out', 'delta_ln1_g', 'delta_ln1_b', 'delta_w_ffn_gate', 'delta_w_ffn_up', 'delta_w_ffn_down', 'delta_ln2_g', 'delta_ln2_b', 'new_m_w_in', 'new_m_b_f', 'new_m_b_gate', 'new_m_s5_a_re', 'new_m_s5_a_im', 'new_m_s5_log_dt', 'new_m_s5_b_re', 'new_m_s5_b_im', 'new_m_s5_c_re', 'new_m_s5_c_im', 'new_m_s5_d', 'new_m_s5_w_glu', 'new_m_s5_b_glu', 'new_m_lru_conv_w', 'new_m_lru_conv_b', 'new_m_lru_w_a', 'new_m_lru_b_a', 'new_m_lru_w_x', 'new_m_lru_b_x', 'new_m_lru_lambda', 'new_m_w_branch', 'new_m_w_out', 'new_m_ln1_g', 'new_m_ln1_b', 'new_m_w_ffn_gate', 'new_m_w_ffn_up', 'new_m_w_ffn_down', 'new_m_ln2_g', 'new_m_ln2_b', 'new_v_w_in', 'new_v_b_f', 'new_v_b_gate', 'new_v_s5_a_re', 'new_v_s5_a_im', 'new_v_s5_log_dt', 'new_v_s5_b_re', 'new_v_s5_b_im', 'new_v_s5_c_re', 'new_v_s5_c_im', 'new_v_s5_d', 'new_v_s5_w_glu', 'new_v_s5_b_glu', 'new_v_lru_conv_w', 'new_v_lru_conv_b', 'new_v_lru_w_a', 'new_v_lru_b_a', 'new_v_lru_w_x', 'new_v_lru_b_x', 'new_v_lru_lambda', 'new_v_w_branch', 'new_v_w_out', 'new_v_ln1_g', 'new_v_ln1_b', 'new_v_w_ffn_gate', 'new_v_w_ffn_up', 'new_v_w_ffn_down', 'new_v_ln2_g', 'new_v_ln2_b']
TWIN_LEAF_KINDS = {'loss': 'loss', 'grad_x': 'grad_x', 'grad_w_in': 'grad_w', 'grad_b_f': 'grad_w', 'grad_b_gate': 'grad_w', 'grad_s5_a_re': 'grad_w', 'grad_s5_a_im': 'grad_w', 'grad_s5_log_dt': 'grad_w', 'grad_s5_b_re': 'grad_w', 'grad_s5_b_im': 'grad_w', 'grad_s5_c_re': 'grad_w', 'grad_s5_c_im': 'grad_w', 'grad_s5_d': 'grad_w', 'grad_s5_w_glu': 'grad_w', 'grad_s5_b_glu': 'grad_w', 'grad_lru_conv_w': 'grad_w', 'grad_lru_conv_b': 'grad_w', 'grad_lru_w_a': 'grad_w', 'grad_lru_b_a': 'grad_w', 'grad_lru_w_x': 'grad_w', 'grad_lru_b_x': 'grad_w', 'grad_lru_lambda': 'grad_w', 'grad_w_branch': 'grad_w', 'grad_w_out': 'grad_w', 'grad_ln1_g': 'grad_w', 'grad_ln1_b': 'grad_w', 'grad_w_ffn_gate': 'grad_w', 'grad_w_ffn_up': 'grad_w', 'grad_w_ffn_down': 'grad_w', 'grad_ln2_g': 'grad_w', 'grad_ln2_b': 'grad_w', 'delta_w_in': 'delta_w', 'delta_b_f': 'delta_w', 'delta_b_gate': 'delta_w', 'delta_s5_a_re': 'delta_w', 'delta_s5_a_im': 'delta_w', 'delta_s5_log_dt': 'delta_w', 'delta_s5_b_re': 'delta_w', 'delta_s5_b_im': 'delta_w', 'delta_s5_c_re': 'delta_w', 'delta_s5_c_im': 'delta_w', 'delta_s5_d': 'delta_w', 'delta_s5_w_glu': 'delta_w', 'delta_s5_b_glu': 'delta_w', 'delta_lru_conv_w': 'delta_w', 'delta_lru_conv_b': 'delta_w', 'delta_lru_w_a': 'delta_w', 'delta_lru_b_a': 'delta_w', 'delta_lru_w_x': 'delta_w', 'delta_lru_b_x': 'delta_w', 'delta_lru_lambda': 'delta_w', 'delta_w_branch': 'delta_w', 'delta_w_out': 'delta_w', 'delta_ln1_g': 'delta_w', 'delta_ln1_b': 'delta_w', 'delta_w_ffn_gate': 'delta_w', 'delta_w_ffn_up': 'delta_w', 'delta_w_ffn_down': 'delta_w', 'delta_ln2_g': 'delta_w', 'delta_ln2_b': 'delta_w', 'new_m_w_in': 'new_m', 'new_m_b_f': 'new_m', 'new_m_b_gate': 'new_m', 'new_m_s5_a_re': 'new_m', 'new_m_s5_a_im': 'new_m', 'new_m_s5_log_dt': 'new_m', 'new_m_s5_b_re': 'new_m', 'new_m_s5_b_im': 'new_m', 'new_m_s5_c_re': 'new_m', 'new_m_s5_c_im': 'new_m', 'new_m_s5_d': 'new_m', 'new_m_s5_w_glu': 'new_m', 'new_m_s5_b_glu': 'new_m', 'new_m_lru_conv_w': 'new_m', 'new_m_lru_conv_b': 'new_m', 'new_m_lru_w_a': 'new_m', 'new_m_lru_b_a': 'new_m', 'new_m_lru_w_x': 'new_m', 'new_m_lru_b_x': 'new_m', 'new_m_lru_lambda': 'new_m', 'new_m_w_branch': 'new_m', 'new_m_w_out': 'new_m', 'new_m_ln1_g': 'new_m', 'new_m_ln1_b': 'new_m', 'new_m_w_ffn_gate': 'new_m', 'new_m_w_ffn_up': 'new_m', 'new_m_w_ffn_down': 'new_m', 'new_m_ln2_g': 'new_m', 'new_m_ln2_b': 'new_m', 'new_v_w_in': 'new_v', 'new_v_b_f': 'new_v', 'new_v_b_gate': 'new_v', 'new_v_s5_a_re': 'new_v', 'new_v_s5_a_im': 'new_v', 'new_v_s5_log_dt': 'new_v', 'new_v_s5_b_re': 'new_v', 'new_v_s5_b_im': 'new_v', 'new_v_s5_c_re': 'new_v', 'new_v_s5_c_im': 'new_v', 'new_v_s5_d': 'new_v', 'new_v_s5_w_glu': 'new_v', 'new_v_s5_b_glu': 'new_v', 'new_v_lru_conv_w': 'new_v', 'new_v_lru_conv_b': 'new_v', 'new_v_lru_w_a': 'new_v', 'new_v_lru_b_a': 'new_v', 'new_v_lru_w_x': 'new_v', 'new_v_lru_b_x': 'new_v', 'new_v_lru_lambda': 'new_v', 'new_v_w_branch': 'new_v', 'new_v_w_out': 'new_v', 'new_v_ln1_g': 'new_v', 'new_v_ln1_b': 'new_v', 'new_v_w_ffn_gate': 'new_v', 'new_v_w_ffn_up': 'new_v', 'new_v_w_ffn_down': 'new_v', 'new_v_ln2_g': 'new_v', 'new_v_ln2_b': 'new_v'}


def _forward(args):
    return _fwd_reference(*[args[k] for k in FWD_PARAMS])


def _output_shape():
    def fwd():
        inp = _fwd_setup_inputs(0)
        return _fwd_reference(*[inp[k] for k in FWD_PARAMS])
    out = _jax.eval_shape(fwd)
    return out.shape, out.dtype

N_MICROBATCH = 1
ADAM_LR = 0.001
ADAM_B1 = 0.9
ADAM_B2 = 0.999
ADAM_EPS = 1e-08
ADAM_WD = 0.01
ADAM_STEP = 10
PER_EXAMPLE_BATCH_AXIS = {'x': 0, 'loss_target': 0}
SHARED_INPUTS = []
_WEIGHT_DTYPES = {'w_in': _jnp.float32, 'b_f': _jnp.float32, 'b_gate': _jnp.float32, 's5_a_re': _jnp.float32, 's5_a_im': _jnp.float32, 's5_log_dt': _jnp.float32, 's5_b_re': _jnp.float32, 's5_b_im': _jnp.float32, 's5_c_re': _jnp.float32, 's5_c_im': _jnp.float32, 's5_d': _jnp.float32, 's5_w_glu': _jnp.float32, 's5_b_glu': _jnp.float32, 'lru_conv_w': _jnp.float32, 'lru_conv_b': _jnp.float32, 'lru_w_a': _jnp.float32, 'lru_b_a': _jnp.float32, 'lru_w_x': _jnp.float32, 'lru_b_x': _jnp.float32, 'lru_lambda': _jnp.float32, 'w_branch': _jnp.float32, 'w_out': _jnp.float32, 'ln1_g': _jnp.float32, 'ln1_b': _jnp.float32, 'w_ffn_gate': _jnp.float32, 'w_ffn_up': _jnp.float32, 'w_ffn_down': _jnp.float32, 'ln2_g': _jnp.float32, 'ln2_b': _jnp.float32}
MOMENT_SCALE = {'w_in': 9.343315e-03, 'b_f': 5.394401e-02, 'b_gate': 3.890432e-03, 's5_a_re': 6.444800e-04, 's5_a_im': 7.044741e-04, 's5_log_dt': 3.967821e-01, 's5_b_re': 4.295618e-04, 's5_b_im': 4.282872e-04, 's5_c_re': 8.418656e-04, 's5_c_im': 8.566501e-04, 's5_d': 1.484144e-02, 's5_w_glu': 3.619970e-03, 's5_b_glu': 5.987454e-03, 'lru_conv_w': 1.876561e-02, 'lru_conv_b': 1.818071e-01, 'lru_w_a': 6.179743e-03, 'lru_b_a': 5.367019e-03, 'lru_w_x': 1.123985e-02, 'lru_b_x': 6.110264e-03, 'lru_lambda': 9.481854e-03, 'w_branch': 1.026129e-02, 'w_out': 4.198448e-02, 'ln1_g': 5.079069e-01, 'ln1_b': 3.206370e-01, 'w_ffn_gate': 1.659714e-02, 'w_ffn_up': 1.608499e-02, 'w_ffn_down': 6.337731e-02, 'ln2_g': 1.601590e+01, 'ln2_b': 8.893899e-01}


def _to_microbatches(a, axis):
    t = _jnp.moveaxis(a, axis, 0)
    t = t.reshape((N_MICROBATCH, t.shape[0] // N_MICROBATCH) + t.shape[1:])
    return _jnp.moveaxis(t, 1, axis + 1)


def setup_inputs(seed: int = 0) -> dict:
    inp = _fwd_setup_inputs(seed)
    key = _jax.random.fold_in(_jax.random.key(seed), 7919)
    shape, _ = _output_shape()
    out = dict(inp)
    out["loss_target"] = _jax.random.normal(_jax.random.fold_in(key, 0), shape, _jnp.float32)
    for i, name in enumerate(TWIN_WEIGHTS):
        w = inp[name].astype(_jnp.float32)
        if MOMENT_SCALE is None:
            s = _jnp.sqrt(_jnp.mean(_jnp.square(w)) + 1e-30)
        else:
            s = MOMENT_SCALE[name]
        km, kv = _jax.random.split(_jax.random.fold_in(key, i + 1))
        out[name] = w
        out["m_" + name] = s * _jax.random.normal(km, w.shape, _jnp.float32)
        out["v_" + name] = (s * s) * _jax.random.uniform(kv, w.shape, _jnp.float32, 0.5, 1.5)
    if N_MICROBATCH > 1:
        for name, axis in PER_EXAMPLE_BATCH_AXIS.items():
            out[name] = _to_microbatches(out[name], axis)
    return {'x': out['x'], 'w_in': out['w_in'], 'b_f': out['b_f'], 'b_gate': out['b_gate'], 's5_a_re': out['s5_a_re'], 's5_a_im': out['s5_a_im'], 's5_log_dt': out['s5_log_dt'], 's5_b_re': out['s5_b_re'], 's5_b_im': out['s5_b_im'], 's5_c_re': out['s5_c_re'], 's5_c_im': out['s5_c_im'], 's5_d': out['s5_d'], 's5_w_glu': out['s5_w_glu'], 's5_b_glu': out['s5_b_glu'], 'lru_conv_w': out['lru_conv_w'], 'lru_conv_b': out['lru_conv_b'], 'lru_w_a': out['lru_w_a'], 'lru_b_a': out['lru_b_a'], 'lru_w_x': out['lru_w_x'], 'lru_b_x': out['lru_b_x'], 'lru_lambda': out['lru_lambda'], 'w_branch': out['w_branch'], 'w_out': out['w_out'], 'ln1_g': out['ln1_g'], 'ln1_b': out['ln1_b'], 'w_ffn_gate': out['w_ffn_gate'], 'w_ffn_up': out['w_ffn_up'], 'w_ffn_down': out['w_ffn_down'], 'ln2_g': out['ln2_g'], 'ln2_b': out['ln2_b'], 'loss_target': out['loss_target'], 'm_w_in': out['m_w_in'], 'm_b_f': out['m_b_f'], 'm_b_gate': out['m_b_gate'], 'm_s5_a_re': out['m_s5_a_re'], 'm_s5_a_im': out['m_s5_a_im'], 'm_s5_log_dt': out['m_s5_log_dt'], 'm_s5_b_re': out['m_s5_b_re'], 'm_s5_b_im': out['m_s5_b_im'], 'm_s5_c_re': out['m_s5_c_re'], 'm_s5_c_im': out['m_s5_c_im'], 'm_s5_d': out['m_s5_d'], 'm_s5_w_glu': out['m_s5_w_glu'], 'm_s5_b_glu': out['m_s5_b_glu'], 'm_lru_conv_w': out['m_lru_conv_w'], 'm_lru_conv_b': out['m_lru_conv_b'], 'm_lru_w_a': out['m_lru_w_a'], 'm_lru_b_a': out['m_lru_b_a'], 'm_lru_w_x': out['m_lru_w_x'], 'm_lru_b_x': out['m_lru_b_x'], 'm_lru_lambda': out['m_lru_lambda'], 'm_w_branch': out['m_w_branch'], 'm_w_out': out['m_w_out'], 'm_ln1_g': out['m_ln1_g'], 'm_ln1_b': out['m_ln1_b'], 'm_w_ffn_gate': out['m_w_ffn_gate'], 'm_w_ffn_up': out['m_w_ffn_up'], 'm_w_ffn_down': out['m_w_ffn_down'], 'm_ln2_g': out['m_ln2_g'], 'm_ln2_b': out['m_ln2_b'], 'v_w_in': out['v_w_in'], 'v_b_f': out['v_b_f'], 'v_b_gate': out['v_b_gate'], 'v_s5_a_re': out['v_s5_a_re'], 'v_s5_a_im': out['v_s5_a_im'], 'v_s5_log_dt': out['v_s5_log_dt'], 'v_s5_b_re': out['v_s5_b_re'], 'v_s5_b_im': out['v_s5_b_im'], 'v_s5_c_re': out['v_s5_c_re'], 'v_s5_c_im': out['v_s5_c_im'], 'v_s5_d': out['v_s5_d'], 'v_s5_w_glu': out['v_s5_w_glu'], 'v_s5_b_glu': out['v_s5_b_glu'], 'v_lru_conv_w': out['v_lru_conv_w'], 'v_lru_conv_b': out['v_lru_conv_b'], 'v_lru_w_a': out['v_lru_w_a'], 'v_lru_b_a': out['v_lru_b_a'], 'v_lru_w_x': out['v_lru_w_x'], 'v_lru_b_x': out['v_lru_b_x'], 'v_lru_lambda': out['v_lru_lambda'], 'v_w_branch': out['v_w_branch'], 'v_w_out': out['v_w_out'], 'v_ln1_g': out['v_ln1_g'], 'v_ln1_b': out['v_ln1_b'], 'v_w_ffn_gate': out['v_w_ffn_gate'], 'v_w_ffn_up': out['v_w_ffn_up'], 'v_w_ffn_down': out['v_w_ffn_down'], 'v_ln2_g': out['v_ln2_g'], 'v_ln2_b': out['v_ln2_b']}


def _loss(weights, diff, rest, loss_target):
    with _jax.named_scope("forward"):
        args = {**rest, TWIN_DIFF_INPUT: diff, **{k: w.astype(_WEIGHT_DTYPES[k]) for k, w in weights.items()}}
        y = _forward(args)
    with _jax.named_scope("loss_head"):
        err = _jnp.square(y.astype(_jnp.float32) - loss_target)
        return 0.5 * _jnp.sum(_jnp.mean(err, axis=-1)) if err.ndim else 0.5 * err


def _adamw(w, g, m, v):
    m = ADAM_B1 * m + (1.0 - ADAM_B1) * g
    v = ADAM_B2 * v + (1.0 - ADAM_B2) * _jnp.square(g)
    m_hat = m / (1.0 - ADAM_B1 ** ADAM_STEP)
    v_hat = v / (1.0 - ADAM_B2 ** ADAM_STEP)
    delta = -ADAM_LR * (m_hat / (_jnp.sqrt(v_hat) + ADAM_EPS) + ADAM_WD * w)
    return delta, m, v


def reference(x, w_in, b_f, b_gate, s5_a_re, s5_a_im, s5_log_dt, s5_b_re, s5_b_im, s5_c_re, s5_c_im, s5_d, s5_w_glu, s5_b_glu, lru_conv_w, lru_conv_b, lru_w_a, lru_b_a, lru_w_x, lru_b_x, lru_lambda, w_branch, w_out, ln1_g, ln1_b, w_ffn_gate, w_ffn_up, w_ffn_down, ln2_g, ln2_b, loss_target, m_w_in, m_b_f, m_b_gate, m_s5_a_re, m_s5_a_im, m_s5_log_dt, m_s5_b_re, m_s5_b_im, m_s5_c_re, m_s5_c_im, m_s5_d, m_s5_w_glu, m_s5_b_glu, m_lru_conv_w, m_lru_conv_b, m_lru_w_a, m_lru_b_a, m_lru_w_x, m_lru_b_x, m_lru_lambda, m_w_branch, m_w_out, m_ln1_g, m_ln1_b, m_w_ffn_gate, m_w_ffn_up, m_w_ffn_down, m_ln2_g, m_ln2_b, v_w_in, v_b_f, v_b_gate, v_s5_a_re, v_s5_a_im, v_s5_log_dt, v_s5_b_re, v_s5_b_im, v_s5_c_re, v_s5_c_im, v_s5_d, v_s5_w_glu, v_s5_b_glu, v_lru_conv_w, v_lru_conv_b, v_lru_w_a, v_lru_b_a, v_lru_w_x, v_lru_b_x, v_lru_lambda, v_w_branch, v_w_out, v_ln1_g, v_ln1_b, v_w_ffn_gate, v_w_ffn_up, v_w_ffn_down, v_ln2_g, v_ln2_b):
    given = dict(x=x, w_in=w_in, b_f=b_f, b_gate=b_gate, s5_a_re=s5_a_re, s5_a_im=s5_a_im, s5_log_dt=s5_log_dt, s5_b_re=s5_b_re, s5_b_im=s5_b_im, s5_c_re=s5_c_re, s5_c_im=s5_c_im, s5_d=s5_d, s5_w_glu=s5_w_glu, s5_b_glu=s5_b_glu, lru_conv_w=lru_conv_w, lru_conv_b=lru_conv_b, lru_w_a=lru_w_a, lru_b_a=lru_b_a, lru_w_x=lru_w_x, lru_b_x=lru_b_x, lru_lambda=lru_lambda, w_branch=w_branch, w_out=w_out, ln1_g=ln1_g, ln1_b=ln1_b, w_ffn_gate=w_ffn_gate, w_ffn_up=w_ffn_up, w_ffn_down=w_ffn_down, ln2_g=ln2_g, ln2_b=ln2_b, loss_target=loss_target, m_w_in=m_w_in, m_b_f=m_b_f, m_b_gate=m_b_gate, m_s5_a_re=m_s5_a_re, m_s5_a_im=m_s5_a_im, m_s5_log_dt=m_s5_log_dt, m_s5_b_re=m_s5_b_re, m_s5_b_im=m_s5_b_im, m_s5_c_re=m_s5_c_re, m_s5_c_im=m_s5_c_im, m_s5_d=m_s5_d, m_s5_w_glu=m_s5_w_glu, m_s5_b_glu=m_s5_b_glu, m_lru_conv_w=m_lru_conv_w, m_lru_conv_b=m_lru_conv_b, m_lru_w_a=m_lru_w_a, m_lru_b_a=m_lru_b_a, m_lru_w_x=m_lru_w_x, m_lru_b_x=m_lru_b_x, m_lru_lambda=m_lru_lambda, m_w_branch=m_w_branch, m_w_out=m_w_out, m_ln1_g=m_ln1_g, m_ln1_b=m_ln1_b, m_w_ffn_gate=m_w_ffn_gate, m_w_ffn_up=m_w_ffn_up, m_w_ffn_down=m_w_ffn_down, m_ln2_g=m_ln2_g, m_ln2_b=m_ln2_b, v_w_in=v_w_in, v_b_f=v_b_f, v_b_gate=v_b_gate, v_s5_a_re=v_s5_a_re, v_s5_a_im=v_s5_a_im, v_s5_log_dt=v_s5_log_dt, v_s5_b_re=v_s5_b_re, v_s5_b_im=v_s5_b_im, v_s5_c_re=v_s5_c_re, v_s5_c_im=v_s5_c_im, v_s5_d=v_s5_d, v_s5_w_glu=v_s5_w_glu, v_s5_b_glu=v_s5_b_glu, v_lru_conv_w=v_lru_conv_w, v_lru_conv_b=v_lru_conv_b, v_lru_w_a=v_lru_w_a, v_lru_b_a=v_lru_b_a, v_lru_w_x=v_lru_w_x, v_lru_b_x=v_lru_b_x, v_lru_lambda=v_lru_lambda, v_w_branch=v_w_branch, v_w_out=v_w_out, v_ln1_g=v_ln1_g, v_ln1_b=v_ln1_b, v_w_ffn_gate=v_w_ffn_gate, v_w_ffn_up=v_w_ffn_up, v_w_ffn_down=v_w_ffn_down, v_ln2_g=v_ln2_g, v_ln2_b=v_ln2_b)
    weights = {n: given[n] for n in TWIN_WEIGHTS}
    shared = {n: given[n] for n in SHARED_INPUTS}
    per_example = {n: given[n] for n in ['x']}
    grad_fn = _jax.value_and_grad(_loss, argnums=(0, 1))

    def one_microbatch(ex, loss_target):
        ex = dict(ex)
        diff = ex.pop(TWIN_DIFF_INPUT)
        return grad_fn(weights, diff, {**shared, **ex}, loss_target)

    if N_MICROBATCH == 1:
        loss, (grad_w, grad_x) = one_microbatch(per_example, given["loss_target"])
    else:
        def body(carry, xs):
            loss_sum, grad_sum = carry
            l_k, (gw_k, gx_k) = one_microbatch(xs[0], xs[1])
            with _jax.named_scope("update"):
                return (loss_sum + l_k, _jax.tree.map(_jnp.add, grad_sum, gw_k)), gx_k

        init = (_jnp.zeros((), _jnp.float32), _jax.tree.map(_jnp.zeros_like, weights))
        (loss, grad_w), grad_x = _jax.lax.scan(body, init, (per_example, given["loss_target"]))
    with _jax.named_scope("update"):
        delta_w, new_m, new_v = {}, {}, {}
        for n in TWIN_WEIGHTS:
            delta_w[n], new_m[n], new_v[n] = _adamw(weights[n], grad_w[n], given["m_" + n], given["v_" + n])
    return (loss, grad_x, *[grad_w[n] for n in TWIN_WEIGHTS], *[delta_w[n] for n in TWIN_WEIGHTS],
            *[new_m[n] for n in TWIN_WEIGHTS], *[new_v[n] for n in TWIN_WEIGHTS])
```

```python
import functools
import math

import jax
import jax.numpy as jnp
from jax import lax
from jax.experimental import pallas as pl
from jax.experimental.pallas import tpu as pltpu

F32 = jnp.float32
BF16 = jnp.bfloat16

D_MODEL = 1024
DEPTH = 4
BRANCH = 512
S5_GROUPS, S5_GROUP, S5_STATE = 32, 16, 64
S5_N = S5_GROUPS * S5_STATE
LRU_HEADS, LRU_HEAD_DIM = 8, 64
LRU_C = 8.0
CONV_WIDTH = 4
FOX_HEADS, FOX_HEAD_DIM = 8, 64
FFN_HIDDEN = 2816
ALPHA = (2.0 * DEPTH) ** 0.25
LN_EPS = 1e-5
IN_TOTAL = 6 * BRANCH + FOX_HEADS + 3 * D_MODEL
FG_PAD = 128
Z_MAIN = 6 * BRANCH
Z_GATE0 = Z_MAIN
Z_FG0 = Z_MAIN + 3 * D_MODEL
Z_TOTAL = Z_FG0 + FG_PAD
N_DEV = 8
MESH_AXES = ("x", "y", "c")

ADAM_LR, ADAM_B1, ADAM_B2, ADAM_EPS, ADAM_WD, ADAM_STEP = 0.001, 0.9, 0.999, 1e-08, 0.01, 10

VMEM_LIMIT_BYTES = 48 * 1024 * 1024
SUBLANES = 8
LANES = 128
NEG_BIG = -1e30

WEIGHTS = ['w_in', 'b_f', 'b_gate', 's5_a_re', 's5_a_im', 's5_log_dt', 's5_b_re', 's5_b_im', 's5_c_re', 's5_c_im',
           's5_d', 's5_w_glu', 's5_b_glu', 'lru_conv_w', 'lru_conv_b', 'lru_w_a', 'lru_b_a', 'lru_w_x', 'lru_b_x',
           'lru_lambda', 'w_branch', 'w_out', 'ln1_g', 'ln1_b', 'w_ffn_gate', 'w_ffn_up', 'w_ffn_down', 'ln2_g',
           'ln2_b']
SHARDED = ['w_in', 's5_w_glu', 'lru_conv_w', 'w_branch', 'w_out', 'w_ffn_gate', 'w_ffn_up', 'w_ffn_down']
REPLICATED = [n for n in WEIGHTS if n not in SHARDED]


def _params(*sem):
    return pltpu.CompilerParams(dimension_semantics=sem, vmem_limit_bytes=VMEM_LIMIT_BYTES)


def _tile(dim, want):
    if dim % LANES:
        return dim
    t = min(want, dim) // LANES * LANES
    while dim % t:
        t -= LANES
    return t


def _mm(name, a, b, *, ta=False, tb=False, a_c0=0, a_w=None, b_c0=0, b_w=None, add=None, out_dtype=F32,
        tm=512, tn=512, tk=512):
    a_w = a.shape[1] if a_w is None else a_w
    b_w = b.shape[1] if b_w is None else b_w
    m, k = (a_w, a.shape[0]) if ta else (a.shape[0], a_w)
    n = b.shape[0] if tb else b_w
    assert k == (b_w if tb else b.shape[0]), (name, a.shape, b.shape)
    tm, tn, tk = _tile(m, tm), _tile(n, tn), _tile(k, tk)
    nk = k // tk
    a_off = a_c0 // (tm if ta else tk)
    b_off = b_c0 // (tk if tb else tn)
    assert a_c0 % (tm if ta else tk) == 0 and b_c0 % (tk if tb else tn) == 0, name
    dims = (((0 if ta else 1,), (1 if tb else 0,)), ((), ()))

    def body(*refs):
        if add is None:
            a_ref, b_ref, o_ref, acc_ref = refs
        else:
            a_ref, b_ref, add_ref, o_ref, acc_ref = refs
        kk = pl.program_id(2)

        @pl.when(kk == 0)
        def _():
            acc_ref[...] = jnp.zeros_like(acc_ref)

        acc_ref[...] += lax.dot_general(a_ref[...].astype(BF16), b_ref[...].astype(BF16), dims,
                                        preferred_element_type=F32)

        @pl.when(kk == nk - 1)
        def _():
            r = acc_ref[...]
            if add is not None:
                r = r + add_ref[...]
            o_ref[...] = r.astype(o_ref.dtype)

    a_spec = (pl.BlockSpec((tk, tm), lambda i, j, kk: (kk, i + a_off)) if ta
              else pl.BlockSpec((tm, tk), lambda i, j, kk: (i, kk + a_off)))
    b_spec = (pl.BlockSpec((tn, tk), lambda i, j, kk: (j, kk + b_off)) if tb
              else pl.BlockSpec((tk, tn), lambda i, j, kk: (kk, j + b_off)))
    o_spec = pl.BlockSpec((tm, tn), lambda i, j, kk: (i, j))
    ins, in_specs = [a, b], [a_spec, b_spec]
    if add is not None:
        ins.append(add)
        in_specs.append(o_spec)
    return pl.pallas_call(
        body, grid=(m // tm, n // tn, nk), in_specs=in_specs, out_specs=o_spec,
        out_shape=jax.ShapeDtypeStruct((m, n), out_dtype), scratch_shapes=[pltpu.VMEM((tm, tn), F32)],
        compiler_params=_params("parallel", "parallel", "arbitrary"), name=name)(*ins)


def _row_spec(tl, c0, w):
    assert c0 % w == 0
    return pl.BlockSpec((tl, w), lambda i: (i, c0 // w))


def _whole_spec(p):
    return pl.BlockSpec(p.shape, lambda i: (0,) * p.ndim)


def _ew(name, f, rows, prm, out_ws, tl=256):
    nrows, nprm = len(rows), len(prm)
    length = rows[0][0].shape[0]
    tl = min(tl, length)

    def body(*refs):
        vals = [r[...] for r in refs[:nrows + nprm]]
        outs = f(*vals)
        for o_ref, o in zip(refs[nrows + nprm:], outs):
            o_ref[...] = o.astype(o_ref.dtype)

    return pl.pallas_call(
        body, grid=(length // tl,),
        in_specs=[_row_spec(tl, c0, w) for (_, c0, w) in rows] + [_whole_spec(p) for p in prm],
        out_specs=[_row_spec(tl, 0, w) for w in out_ws],
        out_shape=[jax.ShapeDtypeStruct((length, w), F32) for w in out_ws],
        compiler_params=_params("parallel"), name=name)(*[r[0] for r in rows], *prm)


def _ew_bwd(name, f, rows, prm, douts, row_grad, tl=256):
    nrows, nprm, nd = len(rows), len(prm), len(douts)
    length = rows[0][0].shape[0]
    tl = min(tl, length)
    want = [i for i in range(nrows) if row_grad[i]]

    def body(*refs):
        vals = [r[...] for r in refs[:nrows + nprm]]
        cts = tuple(r[...] for r in refs[nrows + nprm:nrows + nprm + nd])
        out_refs = refs[nrows + nprm + nd:]
        _, vjp = jax.vjp(lambda *v: tuple(f(*v)), *vals)
        grads = vjp(cts)
        for o_ref, i in zip(out_refs[:len(want)], want):
            o_ref[...] = grads[i]

        @pl.when(pl.program_id(0) == 0)
        def _():
            for o_ref in out_refs[len(want):]:
                o_ref[...] = jnp.zeros_like(o_ref)

        for o_ref, g in zip(out_refs[len(want):], grads[nrows:]):
            o_ref[...] += g

    return pl.pallas_call(
        body, grid=(length // tl,),
        in_specs=([_row_spec(tl, c0, w) for (_, c0, w) in rows] + [_whole_spec(p) for p in prm]
                  + [_row_spec(tl, 0, d.shape[1]) for d in douts]),
        out_specs=[_row_spec(tl, 0, rows[i][2]) for i in want] + [_whole_spec(p) for p in prm],
        out_shape=([jax.ShapeDtypeStruct((length, rows[i][2]), F32) for i in want]
                   + [jax.ShapeDtypeStruct(p.shape, F32) for p in prm]),
        compiler_params=_params("arbitrary"), name=name)(*[r[0] for r in rows], *prm, *douts)


def _row_ids(width):
    return lax.broadcasted_iota(jnp.int32, (SUBLANES, width), 0)


def _shift_rows(v, d, reverse):
    return pltpu.roll(v, (SUBLANES - d) if reverse else d, 0)


def _scan_real(name, a, b, b2=None, *, reverse=False, bn=256):
    length, n = a.shape
    bn = _tile(n, bn)
    nb = length // SUBLANES
    operands = [a, b] if b2 is None else [a, b, b2]

    def body(*refs):
        a_ref, b_ref, h_ref = refs[0], refs[1], refs[-1]
        rows = _row_ids(bn)

        def step(it, carry):
            i = (nb - 1 - it) if reverse else it
            sl = pl.ds(pl.multiple_of(i * SUBLANES, SUBLANES), SUBLANES)
            av, bv = a_ref[sl, :], b_ref[sl, :]
            if b2 is not None:
                bv = bv + refs[2][sl, :]
            for d in (1, 2, 4):
                live = (rows < SUBLANES - d) if reverse else (rows >= d)
                a_in = jnp.where(live, _shift_rows(av, d, reverse), 1.0)
                b_in = jnp.where(live, _shift_rows(bv, d, reverse), 0.0)
                bv = bv + av * b_in
                av = av * a_in
            hv = bv + av * carry
            h_ref[sl, :] = hv
            edge = hv[0:1, :] if reverse else hv[SUBLANES - 1:SUBLANES, :]
            return jnp.broadcast_to(edge, (SUBLANES, bn))

        lax.fori_loop(0, nb, step, jnp.zeros((SUBLANES, bn), F32))

    spec = pl.BlockSpec((length, bn), lambda j: (0, j))
    return pl.pallas_call(body, grid=(n // bn,), in_specs=[spec] * len(operands), out_specs=spec,
                          out_shape=jax.ShapeDtypeStruct((length, n), F32),
                          compiler_params=_params("parallel"), name=name)(*operands)


def _cmul(ar, ai, br, bi):
    return ar * br - ai * bi, ar * bi + ai * br


def _scan_cplx(name, lam_re, lam_im, x_re, x_im, *, reverse=False, h_re=None, h_im=None, bn=128):
    length, n = x_re.shape
    bn = _tile(n, bn)
    nb = length // SUBLANES
    with_dot = h_re is not None

    def body(*refs):
        if with_dot:
            lr_ref, li_ref, xr_ref, xi_ref, hr_ref, hi_ref, gr_ref, gi_ref, dr_ref, di_ref = refs
        else:
            lr_ref, li_ref, xr_ref, xi_ref, gr_ref, gi_ref = refs
        rows = _row_ids(bn)
        lr = jnp.broadcast_to(lr_ref[...], (SUBLANES, bn))
        li = jnp.broadcast_to(li_ref[...], (SUBLANES, bn))
        powers = [(lr, li)]
        for _ in range(SUBLANES - 1):
            powers.append(_cmul(powers[-1][0], powers[-1][1], lr, li))
        zero = jnp.zeros((SUBLANES, bn), F32)
        steps = []
        for d in (1, 2, 4):
            live = (rows < SUBLANES - d) if reverse else (rows >= d)
            steps.append((d, jnp.where(live, powers[d - 1][0], 0.0), jnp.where(live, powers[d - 1][1], 0.0)))
        cr, ci = zero, zero
        for r in range(SUBLANES):
            e = (SUBLANES - r) if reverse else (r + 1)
            cr = jnp.where(rows == r, powers[e - 1][0], cr)
            ci = jnp.where(rows == r, powers[e - 1][1], ci)

        def step(it, carry):
            i = (nb - 1 - it) if reverse else it
            sl = pl.ds(pl.multiple_of(i * SUBLANES, SUBLANES), SUBLANES)
            vr, vi = xr_ref[sl, :], xi_ref[sl, :]
            for d, pr, pi in steps:
                sr, si = _cmul(pr, pi, _shift_rows(vr, d, reverse), _shift_rows(vi, d, reverse))
                vr, vi = vr + sr, vi + si
            kr, ki = _cmul(cr, ci, carry[0], carry[1])
            vr, vi = vr + kr, vi + ki
            gr_ref[sl, :] = vr
            gi_ref[sl, :] = vi
            er = vr[0:1, :] if reverse else vr[SUBLANES - 1:SUBLANES, :]
            ei = vi[0:1, :] if reverse else vi[SUBLANES - 1:SUBLANES, :]
            new = (jnp.broadcast_to(er, (SUBLANES, bn)), jnp.broadcast_to(ei, (SUBLANES, bn)))
            if not with_dot:
                return new
            prev = pl.ds(pl.multiple_of(jnp.maximum(i - 1, 0) * SUBLANES, SUBLANES), SUBLANES)
            keep = jnp.where(i > 0, 1.0, 0.0)
            pr_ = jnp.broadcast_to(hr_ref[prev, :][SUBLANES - 1:SUBLANES, :], (SUBLANES, bn)) * keep
            pi_ = jnp.broadcast_to(hi_ref[prev, :][SUBLANES - 1:SUBLANES, :], (SUBLANES, bn)) * keep
            hr = jnp.where(rows == 0, pr_, pltpu.roll(hr_ref[sl, :], 1, 0))
            hi = jnp.where(rows == 0, pi_, pltpu.roll(hi_ref[sl, :], 1, 0))
            return new + (carry[2] + vr * hr + vi * hi, carry[3] + vi * hr - vr * hi)

        init = (zero, zero, zero, zero) if with_dot else (zero, zero)
        out = lax.fori_loop(0, nb, step, init)
        if with_dot:
            dr_ref[...] = jnp.sum(out[2], axis=0, keepdims=True)
            di_ref[...] = jnp.sum(out[3], axis=0, keepdims=True)

    col = pl.BlockSpec((length, bn), lambda j: (0, j))
    vec = pl.BlockSpec((1, bn), lambda j: (0, j))
    ins = [lam_re, lam_im, x_re, x_im] + ([h_re, h_im] if with_dot else [])
    in_specs = [vec, vec, col, col] + ([col, col] if with_dot else [])
    out_specs = [col, col] + ([vec, vec] if with_dot else [])
    full = jax.ShapeDtypeStruct((length, n), F32)
    row = jax.ShapeDtypeStruct((1, n), F32)
    out_shape = [full, full] + ([row, row] if with_dot else [])
    return pl.pallas_call(body, grid=(n // bn,), in_specs=in_specs, out_specs=out_specs, out_shape=out_shape,
                          compiler_params=_params("parallel"), name=name)(*ins)


FOX_SCALE = FOX_HEAD_DIM ** -0.5
NT = (((1,), (1,)), ((), ()))
TN = (((0,), (0,)), ((), ()))


def _fox_logits(q, k, cq, ck, qi, ki, t):
    s = lax.dot_general(q.astype(BF16), k.astype(BF16), NT, preferred_element_type=F32) * FOX_SCALE + cq - ck
    row = qi * t + lax.broadcasted_iota(jnp.int32, (t, t), 0)
    col = ki * t + lax.broadcasted_iota(jnp.int32, (t, t), 1)
    return jnp.where(col <= row, s, NEG_BIG)


def _fox_fwd(q, k, v, cq, ck, t=512):
    heads, length, dh = q.shape
    t = min(t, length)
    nt = length // t

    def body(q_ref, k_ref, v_ref, cq_ref, ck_ref, o_ref, lse_ref, m_sc, l_sc, acc_sc):
        qi, ki = pl.program_id(1), pl.program_id(2)

        @pl.when(ki == 0)
        def _():
            m_sc[...] = jnp.full_like(m_sc, NEG_BIG)
            l_sc[...] = jnp.zeros_like(l_sc)
            acc_sc[...] = jnp.zeros_like(acc_sc)

        @pl.when(ki <= qi)
        def _():
            s = _fox_logits(q_ref[...], k_ref[...], cq_ref[...], ck_ref[...], qi, ki, t)
            m_old = m_sc[...]
            m_new = jnp.maximum(m_old, jnp.max(s, axis=1, keepdims=True))
            p = jnp.exp(s - m_new)
            scale = jnp.exp(m_old - m_new)
            l_sc[...] = scale * l_sc[...] + jnp.sum(p, axis=1, keepdims=True)
            acc_sc[...] = scale * acc_sc[...] + jnp.dot(p.astype(BF16), v_ref[...].astype(BF16),
                                                        preferred_element_type=F32)
            m_sc[...] = m_new

        @pl.when(ki == nt - 1)
        def _():
            o_ref[...] = acc_sc[...] / l_sc[...]
            lse_ref[...] = m_sc[...] + jnp.log(l_sc[...])

    qspec = pl.BlockSpec((None, t, dh), lambda h, i, j: (h, i, 0))
    kspec = pl.BlockSpec((None, t, dh), lambda h, i, j: (h, jnp.minimum(i, j), 0))
    colspec = pl.BlockSpec((None, t, 1), lambda h, i, j: (h, i, 0))
    rowspec = pl.BlockSpec((None, 1, t), lambda h, i, j: (h, 0, jnp.minimum(i, j)))
    return pl.pallas_call(
        body, grid=(heads, nt, nt), in_specs=[qspec, kspec, kspec, colspec, rowspec], out_specs=[qspec, colspec],
        out_shape=[jax.ShapeDtypeStruct((heads, length, dh), F32), jax.ShapeDtypeStruct((heads, length, 1), F32)],
        scratch_shapes=[pltpu.VMEM((t, 1), F32), pltpu.VMEM((t, 1), F32), pltpu.VMEM((t, dh), F32)],
        compiler_params=_params("parallel", "parallel", "arbitrary"), name="fox_fwd")(q, k, v, cq, ck)


def _fox_bwd_q(q, k, v, cq, ck, o, lse, do, t=512):
    heads, length, dh = q.shape
    t = min(t, length)
    nt = length // t

    def body(q_ref, k_ref, v_ref, cq_ref, ck_ref, o_ref, lse_ref, do_ref, dq_ref, dc_ref, acc_sc, dc_sc):
        qi, ki = pl.program_id(1), pl.program_id(2)

        @pl.when(ki == 0)
        def _():
            acc_sc[...] = jnp.zeros_like(acc_sc)
            dc_sc[...] = jnp.zeros_like(dc_sc)

        @pl.when(ki <= qi)
        def _():
            s = _fox_logits(q_ref[...], k_ref[...], cq_ref[...], ck_ref[...], qi, ki, t)
            p = jnp.exp(s - lse_ref[...])
            dov = do_ref[...]
            delta = jnp.sum(dov.astype(BF16).astype(F32) * o_ref[...], axis=1, keepdims=True)
            dp = lax.dot_general(dov.astype(BF16), v_ref[...].astype(BF16), NT, preferred_element_type=F32)
            ds = p * (dp - delta)
            acc_sc[...] += jnp.dot(ds.astype(BF16), k_ref[...].astype(BF16), preferred_element_type=F32)
            dc_sc[...] += jnp.sum(ds, axis=1, keepdims=True)

        @pl.when(ki == nt - 1)
        def _():
            dq_ref[...] = acc_sc[...] * FOX_SCALE
            dc_ref[...] = dc_sc[...]

    qspec = pl.BlockSpec((None, t, dh), lambda h, i, j: (h, i, 0))
    kspec = pl.BlockSpec((None, t, dh), lambda h, i, j: (h, jnp.minimum(i, j), 0))
    colspec = pl.BlockSpec((None, t, 1), lambda h, i, j: (h, i, 0))
    rowspec = pl.BlockSpec((None, 1, t), lambda h, i, j: (h, 0, jnp.minimum(i, j)))
    return pl.pallas_call(
        body, grid=(heads, nt, nt), in_specs=[qspec, kspec, kspec, colspec, rowspec, qspec, colspec, qspec],
        out_specs=[qspec, colspec],
        out_shape=[jax.ShapeDtypeStruct((heads, length, dh), F32), jax.ShapeDtypeStruct((heads, length, 1), F32)],
        scratch_shapes=[pltpu.VMEM((t, dh), F32), pltpu.VMEM((t, 1), F32)],
        compiler_params=_params("parallel", "parallel", "arbitrary"), name="fox_bwd_q")(q, k, v, cq, ck, o, lse, do)


def _fox_bwd_kv(q, k, v, cq, ck, o, lse, do, t=512):
    heads, length, dh = q.shape
    t = min(t, length)
    nt = length // t

    def body(q_ref, k_ref, v_ref, cq_ref, ck_ref, o_ref, lse_ref, do_ref, dk_ref, dv_ref, dc_ref, dk_sc, dv_sc, dc_sc):
        ki, qi = pl.program_id(1), pl.program_id(2)

        @pl.when(qi == 0)
        def _():
            dk_sc[...] = jnp.zeros_like(dk_sc)
            dv_sc[...] = jnp.zeros_like(dv_sc)
            dc_sc[...] = jnp.zeros_like(dc_sc)

        @pl.when(qi >= ki)
        def _():
            s = _fox_logits(q_ref[...], k_ref[...], cq_ref[...], ck_ref[...], qi, ki, t)
            p = jnp.exp(s - lse_ref[...])
            dov = do_ref[...]
            delta = jnp.sum(dov.astype(BF16).astype(F32) * o_ref[...], axis=1, keepdims=True)
            dp = lax.dot_general(dov.astype(BF16), v_ref[...].astype(BF16), NT, preferred_element_type=F32)
            ds = p * (dp - delta)
            dv_sc[...] += lax.dot_general(p.astype(BF16), dov.astype(BF16), TN, preferred_element_type=F32)
            dk_sc[...] += lax.dot_general(ds.astype(BF16), q_ref[...].astype(BF16), TN, preferred_element_type=F32)
            dc_sc[...] -= jnp.sum(ds, axis=0, keepdims=True)

        @pl.when(qi == nt - 1)
        def _():
            dk_ref[...] = dk_sc[...] * FOX_SCALE
            dv_ref[...] = dv_sc[...]
            dc_ref[...] = dc_sc[...]

    qspec = pl.BlockSpec((None, t, dh), lambda h, j, i: (h, jnp.maximum(i, j), 0))
    kspec = pl.BlockSpec((None, t, dh), lambda h, j, i: (h, j, 0))
    colspec = pl.BlockSpec((None, t, 1), lambda h, j, i: (h, jnp.maximum(i, j), 0))
    rowspec = pl.BlockSpec((None, 1, t), lambda h, j, i: (h, 0, j))
    big = jax.ShapeDtypeStruct((heads, length, dh), F32)
    return pl.pallas_call(
        body, grid=(heads, nt, nt), in_specs=[qspec, kspec, kspec, colspec, rowspec, qspec, colspec, qspec],
        out_specs=[kspec, kspec, rowspec], out_shape=[big, big, jax.ShapeDtypeStruct((heads, 1, length), F32)],
        scratch_shapes=[pltpu.VMEM((t, dh), F32), pltpu.VMEM((t, dh), F32), pltpu.VMEM((1, t), F32)],
        compiler_params=_params("parallel", "parallel", "arbitrary"), name="fox_bwd_kv")(q, k, v, cq, ck, o, lse, do)


def _softplus(x):
    return jnp.maximum(x, 0.0) + jnp.log1p(jnp.exp(-jnp.abs(x)))


def _f_s5_disc(a_re, a_im, log_dt, b_re, b_im):
    dt = jnp.exp(log_dt)
    mag = jnp.exp(a_re * dt)
    lr, li = mag * jnp.cos(a_im * dt), mag * jnp.sin(a_im * dt)
    den = a_re * a_re + a_im * a_im
    qr = ((lr - 1.0) * a_re + li * a_im) / den
    qi = (li * a_re - (lr - 1.0) * a_im) / den
    return lr, li, qr * b_re - qi * b_im, qr * b_im + qi * b_re


def _f_s5_y1(hc_re, hc_im, u, d):
    return (jax.nn.gelu(hc_re + hc_im + d * u),)


def _f_s5_glu(y1, pre, b):
    return (y1 * jax.nn.sigmoid(pre + b),)


def _f_conv(x0, x1, x2, x3, w0, w1, w2, w3, b):
    return (b + w0 * x0 + w1 * x1 + w2 * x2 + w3 * x3,)


def _f_conv_t(d0, d1, d2, d3, w0, w1, w2, w3):
    return (w0 * d0 + w1 * d1 + w2 * d2 + w3 * d3,)


def _lru_coeffs(xc, pa, px, b_a, b_x, lam):
    r = jax.nn.sigmoid(pa + b_a)
    i = jax.nn.sigmoid(px + b_x)
    log_a = -LRU_C * _softplus(-lam) * r
    a = jnp.exp(log_a)
    mult = jnp.sqrt(-jnp.tanh(log_a) * (a * a + 1.0))
    return a, mult * (i * xc)


def _f_lru_gates(xc, pa, px, b_a, b_x, lam):
    return _lru_coeffs(xc, pa, px, b_a, b_x, lam)


def _f_lru_step(xc, pa, px, h_prev, b_a, b_x, lam):
    a, b = _lru_coeffs(xc, pa, px, b_a, b_x, lam)
    return (a * h_prev + b,)


def _f_lru_out(gate, h):
    return (jax.nn.gelu(gate) * h,)


def _f_logf(zf, bf):
    return (-_softplus(-(zf + bf)),)


def _f_merge(p0, p1, p2, z0, z1, z2, b0, b1, b2):
    return (jax.nn.sigmoid(z0 + b0) * p0 + jax.nn.sigmoid(z1 + b1) * p1 + jax.nn.sigmoid(z2 + b2) * p2,)


def _f_ln(x, r, g, b):
    s = ALPHA * x + r
    mu = jnp.mean(s, axis=-1, keepdims=True)
    var = jnp.mean(jnp.square(s - mu), axis=-1, keepdims=True)
    return ((s - mu) * lax.rsqrt(var + LN_EPS) * g + b,)


def _f_swiglu(hg, hu):
    return (jax.nn.silu(hg) * hu,)


def _full(a):
    return (a, 0, a.shape[1])


def _blockdiag(t):
    g, a, b = t.shape
    eye = jnp.eye(g, dtype=t.dtype)
    return (t[:, :, None, :] * eye[:, None, :, None]).reshape(g * a, g * b)


def _blockdiag_take(d, g, a, b):
    eye = jnp.eye(g, dtype=d.dtype)
    return (d.reshape(g, a, g, b) * eye[:, None, :, None]).sum(axis=2)


def _delay(a, j):
    return a if j == 0 else jnp.pad(a, ((j, 0), (0, 0)))[:a.shape[0]]


def _advance(a, j):
    return a if j == 0 else jnp.pad(a, ((0, j), (0, 0)))[j:]


def _heads(a, c0=0):
    length = a.shape[0]
    return a[:, c0:c0 + BRANCH].reshape(length, FOX_HEADS, FOX_HEAD_DIM).transpose(1, 0, 2)


def _unheads(a):
    return a.transpose(1, 0, 2).reshape(a.shape[1], BRANCH)


def _row(v):
    return v.reshape(1, -1).astype(F32)


def _col(v):
    return v.reshape(-1, 1).astype(F32)


def _prep_layer(w):
    p = {}
    w_in = w['w_in']
    p['wc'] = jnp.concatenate(
        [w_in[:, :Z_MAIN], w_in[:, Z_MAIN + FOX_HEADS:], w_in[:, Z_MAIN:Z_MAIN + FOX_HEADS],
         jnp.zeros((D_MODEL, FG_PAD - FOX_HEADS), w_in.dtype)], axis=1)
    p['b_f'] = jnp.pad(_row(w['b_f']), ((0, 0), (0, FG_PAD - FOX_HEADS)))
    p['b_gate'] = [_row(w['b_gate'][k * D_MODEL:(k + 1) * D_MODEL]) for k in range(3)]
    p['disc_in'] = [_col(w['s5_a_re']), _col(w['s5_a_im']), _col(jnp.repeat(w['s5_log_dt'], S5_STATE)),
                    w['s5_b_re'].reshape(S5_N, S5_GROUP), w['s5_b_im'].reshape(S5_N, S5_GROUP)]
    lam_re, lam_im, bb_re, bb_im = _ew("s5_disc", _f_s5_disc, [_full(a) for a in p['disc_in']], [],
                                       [1, 1, S5_GROUP, S5_GROUP], tl=S5_N)
    p['lam_re'], p['lam_im'] = lam_re.reshape(1, S5_N), lam_im.reshape(1, S5_N)
    to_blk = lambda t: _blockdiag(t.reshape(S5_GROUPS, S5_STATE, S5_GROUP).transpose(0, 2, 1))
    p['s5_bre'], p['s5_bim'] = to_blk(bb_re), to_blk(bb_im)
    p['s5_cre'] = _blockdiag(w['s5_c_re'].transpose(0, 2, 1))
    p['s5_cimn'] = _blockdiag(-w['s5_c_im'].transpose(0, 2, 1))
    p['s5_d'], p['wglu'], p['bglu'] = _row(w['s5_d']), w['s5_w_glu'], _row(w['s5_b_glu'])
    p['conv_w'] = [_row(w['lru_conv_w'][CONV_WIDTH - 1 - j]) for j in range(CONV_WIDTH)]
    p['conv_b'] = _row(w['lru_conv_b'])
    p['wax'] = jnp.concatenate([_blockdiag(w['lru_w_a']), _blockdiag(w['lru_w_x'])], axis=1)
    p['b_a'], p['b_x'], p['lam'] = _row(w['lru_b_a']), _row(w['lru_b_x']), _row(w['lru_lambda'])
    p['wb'] = [w['w_branch'][k] for k in range(3)]
    p['wout'] = w['w_out']
    p['ln1'] = [_row(w['ln1_g']), _row(w['ln1_b'])]
    p['wgu'] = jnp.concatenate([w['w_ffn_gate'], w['w_ffn_up']], axis=1)
    p['wd'] = w['w_ffn_down']
    p['ln2'] = [_row(w['ln2_g']), _row(w['ln2_b'])]
    return p


def _layer_fwd(x, p):
    length = x.shape[0]
    r = {'x': x}
    z = r['z'] = _mm("z_in", x, p['wc'], tn=896)
    bu_re = _mm("s5_bu_re", z, p['s5_bre'], a_w=BRANCH)
    bu_im = _mm("s5_bu_im", z, p['s5_bim'], a_w=BRANCH)
    r['h_re'], r['h_im'] = _scan_cplx("s5_scan", p['lam_re'], p['lam_im'], bu_re, bu_im)
    r['hc_re'] = _mm("s5_hc_re", r['h_re'], p['s5_cre'])
    r['hc_im'] = _mm("s5_hc_im", r['h_im'], p['s5_cimn'])
    (r['y1'],) = _ew("s5_y1", _f_s5_y1, [_full(r['hc_re']), _full(r['hc_im']), (z, 0, BRANCH)], [p['s5_d']], [BRANCH])
    r['pre'] = _mm("s5_glu_pre", r['y1'], p['wglu'])
    (r['ys5'],) = _ew("s5_glu", _f_s5_glu, [_full(r['y1']), _full(r['pre'])], [p['bglu']], [BRANCH])
    xl = z[:, BRANCH:2 * BRANCH]
    r['xd'] = [_delay(xl, j) for j in range(1, CONV_WIDTH)]
    (r['xc'],) = _ew("lru_conv", _f_conv, [(z, BRANCH, BRANCH)] + [_full(a) for a in r['xd']],
                     p['conv_w'] + [p['conv_b']], [BRANCH])
    r['papx'] = _mm("lru_gate_mm", r['xc'], p['wax'])
    r['a'], b = _ew("lru_gates", _f_lru_gates, [_full(r['xc']), (r['papx'], 0, BRANCH), (r['papx'], BRANCH, BRANCH)],
                    [p['b_a'], p['b_x'], p['lam']], [BRANCH, BRANCH])
    r['h'] = _scan_real("lru_scan", r['a'], b)
    (r['ylru'],) = _ew("lru_out", _f_lru_out, [(z, 2 * BRANCH, BRANCH), _full(r['h'])], [], [BRANCH])
    (logf,) = _ew("fox_logf", _f_logf, [(z, Z_FG0, FG_PAD)], [p['b_f']], [FG_PAD])
    cum = _scan_real("fox_cum", jnp.ones((length, FG_PAD), F32), logf)
    cum_t = cum[:, :FOX_HEADS].T
    r['cq'], r['ck'] = cum_t[:, :, None], cum_t[:, None, :]
    r['qh'], r['kh'], r['vh'] = _heads(z, 3 * BRANCH), _heads(z, 4 * BRANCH), _heads(z, 5 * BRANCH)
    r['o'], r['lse'] = _fox_fwd(r['qh'], r['kh'], r['vh'], r['cq'], r['ck'])
    r['yfox'] = _unheads(r['o'])
    ys = [r['ys5'], r['ylru'], r['yfox']]
    r['proj'] = [_mm("proj_%d" % k, ys[k], p['wb'][k]) for k in range(3)]
    gate_rows = [(z, Z_GATE0 + k * D_MODEL, D_MODEL) for k in range(3)]
    (r['mix'],) = _ew("merge", _f_merge, [_full(a) for a in r['proj']] + gate_rows, p['b_gate'], [D_MODEL], tl=128)
    r['mixed'] = _mm("w_out", r['mix'], p['wout'])
    (x1,) = _ew("ln1", _f_ln, [_full(x), _full(r['mixed'])], p['ln1'], [D_MODEL])
    r['x1'] = x1
    r['hgu'] = _mm("ffn_in", x1, p['wgu'])
    (r['hid'],) = _ew("swiglu", _f_swiglu, [(r['hgu'], 0, FFN_HIDDEN), (r['hgu'], FFN_HIDDEN, FFN_HIDDEN)], [],
                      [FFN_HIDDEN], tl=128)
    r['f'] = _mm("ffn_out", r['hid'], p['wd'])
    (x2,) = _ew("ln2", _f_ln, [_full(x1), _full(r['f'])], p['ln2'], [D_MODEL])
    return x2, r


def _layer_bwd(dx2, r, p):
    g = {}
    x, z, x1 = r['x'], r['z'], r['x1']
    dx1_n, df, g['ln2_g'], g['ln2_b'] = _ew_bwd("ln2_bwd", _f_ln, [_full(x1), _full(r['f'])], p['ln2'], [dx2],
                                                [True, True])
    dhid = _mm("ffn_out_dx", df, p['wd'], tb=True)
    g['w_ffn_down'] = _mm("ffn_out_dw", r['hid'], df, ta=True, out_dtype=BF16)
    hgu_rows = [(r['hgu'], 0, FFN_HIDDEN), (r['hgu'], FFN_HIDDEN, FFN_HIDDEN)]
    dhg, dhu = _ew_bwd("swiglu_bwd", _f_swiglu, hgu_rows, [], [dhid], [True, True], tl=128)
    g['w_ffn_gate'] = _mm("ffn_gate_dw", x1, dhg, ta=True, out_dtype=BF16)
    g['w_ffn_up'] = _mm("ffn_up_dw", x1, dhu, ta=True, out_dtype=BF16)
    dx1 = _mm("ffn_gate_dx", dhg, p['wgu'], tb=True, b_w=FFN_HIDDEN, add=dx1_n)
    dx1 = _mm("ffn_up_dx", dhu, p['wgu'], tb=True, b_c0=FFN_HIDDEN, b_w=FFN_HIDDEN, add=dx1)
    dx_n, dmixed, g['ln1_g'], g['ln1_b'] = _ew_bwd("ln1_bwd", _f_ln, [_full(x), _full(r['mixed'])], p['ln1'], [dx1],
                                                   [True, True])
    dmix = _mm("w_out_dx", dmixed, p['wout'], tb=True)
    g['w_out'] = _mm("w_out_dw", r['mix'], dmixed, ta=True, out_dtype=BF16)
    gate_rows = [(z, Z_GATE0 + k * D_MODEL, D_MODEL) for k in range(3)]
    mg = _ew_bwd("merge_bwd", _f_merge, [_full(a) for a in r['proj']] + gate_rows, p['b_gate'], [dmix], [True] * 6,
                 tl=128)
    dproj, dzg = mg[0:3], mg[3:6]
    g['b_gate'] = jnp.concatenate([b.reshape(-1) for b in mg[6:9]])
    ys = [r['ys5'], r['ylru'], r['yfox']]
    dys = [_mm("proj_%d_dx" % k, dproj[k], p['wb'][k], tb=True) for k in range(3)]
    g['w_branch'] = jnp.stack([_mm("proj_%d_dw" % k, ys[k], dproj[k], ta=True, out_dtype=BF16) for k in range(3)])
    do = _heads(dys[2])
    dqh, dcq = _fox_bwd_q(r['qh'], r['kh'], r['vh'], r['cq'], r['ck'], r['o'], r['lse'], do)
    dkh, dvh, dck = _fox_bwd_kv(r['qh'], r['kh'], r['vh'], r['cq'], r['ck'], r['o'], r['lse'], do)
    pad_heads = lambda a: jnp.pad(a.T, ((0, 0), (0, FG_PAD - FOX_HEADS)))
    dlogf = _scan_real("fox_cum_bwd", jnp.ones((dck.shape[2], FG_PAD), F32), pad_heads(dcq[:, :, 0]),
                       pad_heads(dck[:, 0, :]), reverse=True)
    dzf, dbf = _ew_bwd("fox_logf_bwd", _f_logf, [(z, Z_FG0, FG_PAD)], [p['b_f']], [dlogf], [True])
    g['b_f'] = dbf[0, :FOX_HEADS]
    dgate, dh = _ew_bwd("lru_out_bwd", _f_lru_out, [(z, 2 * BRANCH, BRANCH), _full(r['h'])], [], [dys[1]], [True, True])
    db = _scan_real("lru_scan_bwd", _advance(r['a'], 1), dh, reverse=True)
    gates_rows = [_full(r['xc']), (r['papx'], 0, BRANCH), (r['papx'], BRANCH, BRANCH), _full(_delay(r['h'], 1))]
    dxc, dpa, dpx, db_a, db_x, dlam = _ew_bwd("lru_gates_bwd", _f_lru_step, gates_rows, [p['b_a'], p['b_x'], p['lam']],
                                              [db], [True, True, True, False])
    dxc = _mm("lru_a_dx", dpa, p['wax'], tb=True, b_w=BRANCH, add=dxc)
    dxc = _mm("lru_x_dx", dpx, p['wax'], tb=True, b_c0=BRANCH, b_w=BRANCH, add=dxc)
    take_heads = lambda d: _blockdiag_take(d, LRU_HEADS, LRU_HEAD_DIM, LRU_HEAD_DIM)
    g['lru_w_a'] = take_heads(_mm("lru_a_dw", r['xc'], dpa, ta=True))
    g['lru_w_x'] = take_heads(_mm("lru_x_dw", r['xc'], dpx, ta=True))
    g['lru_b_a'] = db_a.reshape(LRU_HEADS, LRU_HEAD_DIM)
    g['lru_b_x'] = db_x.reshape(LRU_HEADS, LRU_HEAD_DIM)
    g['lru_lambda'] = dlam.reshape(-1)
    conv_rows = [(z, BRANCH, BRANCH)] + [_full(a) for a in r['xd']]
    cw = _ew_bwd("lru_conv_dw", _f_conv, conv_rows, p['conv_w'] + [p['conv_b']], [dxc], [False] * CONV_WIDTH)
    g['lru_conv_w'] = jnp.concatenate([cw[CONV_WIDTH - 1 - k] for k in range(CONV_WIDTH)], axis=0)
    g['lru_conv_b'] = cw[CONV_WIDTH].reshape(-1)
    (dxl,) = _ew("lru_conv_dx", _f_conv_t, [_full(_advance(dxc, j)) for j in range(CONV_WIDTH)], p['conv_w'], [BRANCH])
    dy1, dpre, dbglu = _ew_bwd("s5_glu_bwd", _f_s5_glu, [_full(r['y1']), _full(r['pre'])], [p['bglu']], [dys[0]],
                               [True, True])
    g['s5_b_glu'] = dbglu.reshape(-1)
    g['s5_w_glu'] = _mm("s5_glu_dw", r['y1'], dpre, ta=True, out_dtype=BF16)
    dy1 = _mm("s5_glu_dx", dpre, p['wglu'], tb=True, add=dy1)
    dy0, du, dd = _ew_bwd("s5_y1_bwd", _f_s5_y1, [_full(r['hc_re']), _full(r['hc_im']), (z, 0, BRANCH)], [p['s5_d']],
                          [dy1], [True, False, True])
    g['s5_d'] = dd.reshape(-1)
    dh_re = _mm("s5_hc_re_dx", dy0, p['s5_cre'], tb=True)
    dh_im = _mm("s5_hc_im_dx", dy0, p['s5_cimn'], tb=True)
    take_c = lambda d: _blockdiag_take(d, S5_GROUPS, S5_STATE, S5_GROUP).transpose(0, 2, 1)
    g['s5_c_re'] = take_c(_mm("s5_hc_re_dw", r['h_re'], dy0, ta=True))
    g['s5_c_im'] = -take_c(_mm("s5_hc_im_dw", r['h_im'], dy0, ta=True))
    gb_re, gb_im, dl_re, dl_im = _scan_cplx("s5_scan_bwd", p['lam_re'], -p['lam_im'], dh_re, dh_im, reverse=True,
                                            h_re=r['h_re'], h_im=r['h_im'])
    du = _mm("s5_bu_re_dx", gb_re, p['s5_bre'], tb=True, add=du)
    du = _mm("s5_bu_im_dx", gb_im, p['s5_bim'], tb=True, add=du)
    take_b = lambda d: _blockdiag_take(d, S5_GROUPS, S5_GROUP, S5_STATE).transpose(0, 2, 1).reshape(S5_N, S5_GROUP)
    dbb_re = take_b(_mm("s5_bu_re_dw", z, gb_re, ta=True, a_w=BRANCH))
    dbb_im = take_b(_mm("s5_bu_im_dw", z, gb_im, ta=True, a_w=BRANCH))
    disc = _ew_bwd("s5_disc_bwd", _f_s5_disc, [_full(a) for a in p['disc_in']], [],
                   [dl_re.reshape(S5_N, 1), dl_im.reshape(S5_N, 1), dbb_re, dbb_im], [True] * 5, tl=S5_N)
    grp = (S5_GROUPS, S5_STATE)
    g['s5_a_re'], g['s5_a_im'] = disc[0].reshape(grp), disc[1].reshape(grp)
    g['s5_log_dt'] = disc[2].reshape(grp).sum(axis=1)
    g['s5_b_re'], g['s5_b_im'] = disc[3].reshape(grp + (S5_GROUP,)), disc[4].reshape(grp + (S5_GROUP,))
    dz = jnp.concatenate([du, dxl, dgate, _unheads(dqh), _unheads(dkh), _unheads(dvh)] + list(dzg) + [dzf], axis=1)
    dwc = _mm("z_in_dw", x, dz, ta=True, out_dtype=BF16, tn=896)
    g['w_in'] = jnp.concatenate([dwc[:, :Z_MAIN], dwc[:, Z_FG0:Z_FG0 + FOX_HEADS], dwc[:, Z_GATE0:Z_FG0]], axis=1)
    dx = _mm("z_in_dx", dz, p['wc'], tb=True, add=dx_n, tk=896)
    return dx, g


def _loss_head(y, target, tl=256):
    length, width = y.shape
    tl = min(tl, length)
    nt = length // tl

    def body(y_ref, t_ref, dy_ref, loss_ref, acc_sc):
        i = pl.program_id(0)

        @pl.when(i == 0)
        def _():
            acc_sc[...] = jnp.zeros_like(acc_sc)

        err = y_ref[...] - t_ref[...]
        dy_ref[...] = err / width
        acc_sc[...] += jnp.sum(jnp.square(err), axis=0, keepdims=True)

        @pl.when(i == nt - 1)
        def _():
            total = jnp.sum(acc_sc[...], axis=1, keepdims=True) * (0.5 / width)
            loss_ref[...] = jnp.broadcast_to(total, loss_ref.shape)

    spec = pl.BlockSpec((tl, width), lambda i: (i, 0))
    dy, loss = pl.pallas_call(
        body, grid=(nt,), in_specs=[spec, spec], out_specs=[spec, pl.BlockSpec((1, LANES), lambda i: (0, 0))],
        out_shape=[jax.ShapeDtypeStruct((length, width), F32), jax.ShapeDtypeStruct((1, LANES), F32)],
        scratch_shapes=[pltpu.VMEM((1, width), F32)], compiler_params=_params("arbitrary"), name="loss_head")(y, target)
    return loss[0, 0], dy


def _adamw(name, w, parts, m, v):
    rows, cols = w.shape
    tr = rows
    for cand in (256, 128, 64, 32, 16):
        if rows % cand == 0:
            tr = cand
            break

    def body(w_ref, p_ref, m_ref, v_ref, g_ref, d_ref, m2_ref, v2_ref):
        grad = p_ref[0].astype(F32)
        for dev in range(1, N_DEV):
            grad = grad + p_ref[dev].astype(F32)
        m2 = ADAM_B1 * m_ref[...] + (1.0 - ADAM_B1) * grad
        v2 = ADAM_B2 * v_ref[...] + (1.0 - ADAM_B2) * jnp.square(grad)
        m_hat = m2 / (1.0 - ADAM_B1 ** ADAM_STEP)
        v_hat = v2 / (1.0 - ADAM_B2 ** ADAM_STEP)
        g_ref[...] = grad
        d_ref[...] = -ADAM_LR * (m_hat / (jnp.sqrt(v_hat) + ADAM_EPS) + ADAM_WD * w_ref[...])
        m2_ref[...] = m2
        v2_ref[...] = v2

    spec = pl.BlockSpec((tr, cols), lambda i: (i, 0))
    pspec = pl.BlockSpec((N_DEV, tr, cols), lambda i: (0, i, 0))
    shape = jax.ShapeDtypeStruct((rows, cols), F32)
    return pl.pallas_call(body, grid=(rows // tr,), in_specs=[spec, pspec, spec, spec], out_specs=[spec] * 4,
                          out_shape=[shape] * 4, compiler_params=_params("parallel"), name=name)(w, parts, m, v)


def _forward_backward(x, target, layers):
    prepared, saved = [], []
    for w in layers:
        p = _prep_layer(w)
        x, r = _layer_fwd(x, p)
        prepared.append(p)
        saved.append(r)
    loss, dx = _loss_head(x, target)
    grads = [None] * len(layers)
    for l in reversed(range(len(layers))):
        dx, grads[l] = _layer_bwd(dx, saved[l], prepared[l])
    return loss, dx, grads


def _exchange(name, arrays, scatter):
    n = len(arrays)
    hbm = pl.BlockSpec(memory_space=pltpu.HBM)

    def body(*refs):
        ins, outs = refs[:n], refs[n:2 * n]
        send_sems, recv_sems, local_sems = refs[2 * n:]
        x, y, c = lax.axis_index("x"), lax.axis_index("y"), lax.axis_index("c")
        me = 4 * x + 2 * y + c

        def peer(k):
            px = 1 - x if k & 4 else x
            py = 1 - y if k & 2 else y
            pc = 1 - c if k & 1 else c
            return (px, py, pc), 4 * px + 2 * py + pc

        local = []
        for a in range(n):
            cp = pltpu.make_async_copy(ins[a].at[me] if scatter else ins[a], outs[a].at[me], local_sems.at[a])
            cp.start()
            local.append(cp)
        sends = []
        for k in range(1, N_DEV):
            dev, idx = peer(k)
            for a in range(n):
                s = a * (N_DEV - 1) + k - 1
                cp = pltpu.make_async_remote_copy(
                    src_ref=ins[a].at[idx] if scatter else ins[a], dst_ref=outs[a].at[me],
                    send_sem=send_sems.at[s], recv_sem=recv_sems.at[s], device_id=dev,
                    device_id_type=pl.DeviceIdType.MESH)
                cp.start()
                sends.append(cp)
        for cp in local:
            cp.wait()
        for cp in sends:
            cp.wait_send()
        for k in range(1, N_DEV):
            dev, idx = peer(k)
            for a in range(n):
                s = a * (N_DEV - 1) + k - 1
                pltpu.make_async_remote_copy(
                    src_ref=ins[a].at[idx] if scatter else ins[a], dst_ref=outs[a].at[idx],
                    send_sem=send_sems.at[s], recv_sem=recv_sems.at[s], device_id=dev,
                    device_id_type=pl.DeviceIdType.MESH).wait_recv()

    out_shape = [jax.ShapeDtypeStruct(a.shape if scatter else (N_DEV,) + a.shape, a.dtype) for a in arrays]
    nsem = n * (N_DEV - 1)
    return pl.pallas_call(
        body, in_specs=[hbm] * n, out_specs=[hbm] * n, out_shape=out_shape,
        scratch_shapes=[pltpu.SemaphoreType.DMA((nsem,)), pltpu.SemaphoreType.DMA((nsem,)),
                        pltpu.SemaphoreType.DMA((n,))],
        name=name)(*arrays)


def _shard_2d(a):
    return a.reshape(-1, a.shape[-1])


def _full_layer_weight(name, gathered, l):
    t = gathered[:, l]
    if name in ('s5_w_glu', 'w_out', 'w_ffn_down'):
        return t.reshape(-1, t.shape[-1])
    if name == 'w_branch':
        return t.transpose(1, 2, 0, 3).reshape(3, BRANCH, D_MODEL)
    return t.transpose(1, 0, 2).reshape(t.shape[1], -1)


def _split_layer_grad(name, g):
    if name in ('s5_w_glu', 'w_out', 'w_ffn_down'):
        return g.reshape(N_DEV, g.shape[0] // N_DEV, g.shape[1])
    if name == 'w_branch':
        return g.reshape(3, BRANCH, N_DEV, D_MODEL // N_DEV).transpose(2, 0, 1, 3)
    return g.reshape(g.shape[0], N_DEV, g.shape[1] // N_DEV).transpose(1, 0, 2)


def _pack(arrays, rows):
    flat = jnp.concatenate([a.reshape(-1).astype(F32) for a in arrays])
    return jnp.pad(flat, (0, rows * LANES - flat.shape[0])).reshape(rows, LANES)


def _unpack(packed, shapes):
    flat, out, at = packed.reshape(-1), [], 0
    for s in shapes:
        size = math.prod(s)
        out.append(flat[at:at + size].reshape(s))
        at += size
    return out


def kernel(x, w_in, b_f, b_gate, s5_a_re, s5_a_im, s5_log_dt, s5_b_re, s5_b_im, s5_c_re, s5_c_im, s5_d, s5_w_glu, s5_b_glu, lru_conv_w, lru_conv_b, lru_w_a, lru_b_a, lru_w_x, lru_b_x, lru_lambda, w_branch, w_out, ln1_g, ln1_b, w_ffn_gate, w_ffn_up, w_ffn_down, ln2_g, ln2_b, loss_target, m_w_in, m_b_f, m_b_gate, m_s5_a_re, m_s5_a_im, m_s5_log_dt, m_s5_b_re, m_s5_b_im, m_s5_c_re, m_s5_c_im, m_s5_d, m_s5_w_glu, m_s5_b_glu, m_lru_conv_w, m_lru_conv_b, m_lru_w_a, m_lru_b_a, m_lru_w_x, m_lru_b_x, m_lru_lambda, m_w_branch, m_w_out, m_ln1_g, m_ln1_b, m_w_ffn_gate, m_w_ffn_up, m_w_ffn_down, m_ln2_g, m_ln2_b, v_w_in, v_b_f, v_b_gate, v_s5_a_re, v_s5_a_im, v_s5_log_dt, v_s5_b_re, v_s5_b_im, v_s5_c_re, v_s5_c_im, v_s5_d, v_s5_w_glu, v_s5_b_glu, v_lru_conv_w, v_lru_conv_b, v_lru_w_a, v_lru_b_a, v_lru_w_x, v_lru_b_x, v_lru_lambda, v_w_branch, v_w_out, v_ln1_g, v_ln1_b, v_w_ffn_gate, v_w_ffn_up, v_w_ffn_down, v_ln2_g, v_ln2_b):
    given = dict(locals())
    weights = {n: given[n] for n in WEIGHTS}
    moments_m = {n: given['m_' + n] for n in WEIGHTS}
    moments_v = {n: given['v_' + n] for n in WEIGHTS}

    wire = {n: (F32 if n == 'lru_conv_w' else BF16) for n in SHARDED}
    gathered = _exchange("gather_weights", [_shard_2d(weights[n]).astype(wire[n]) for n in SHARDED], scatter=False)
    gathered = {n: g.reshape((N_DEV,) + weights[n].shape) for n, g in zip(SHARDED, gathered)}
    layers = []
    for l in range(DEPTH):
        w = {n: weights[n][l] for n in REPLICATED}
        w.update({n: _full_layer_weight(n, gathered[n], l) for n in SHARDED})
        layers.append(w)

    loss_local, dx, grads = _forward_backward(x[0], loss_target[0], layers)
    loss = lax.psum(loss_local, MESH_AXES)

    outgoing = []
    for n in SHARDED:
        per_layer = jnp.stack([_split_layer_grad(n, grads[l][n]) for l in range(DEPTH)], axis=1)
        outgoing.append(per_layer.reshape((N_DEV,) + _shard_2d(weights[n]).shape).astype(wire[n]))
    incoming = _exchange("scatter_grads", outgoing, scatter=True)
    new = {}
    for n, parts in zip(SHARDED, incoming):
        res = _adamw("adamw_" + n, _shard_2d(weights[n]), parts, _shard_2d(moments_m[n]), _shard_2d(moments_v[n]))
        new[n] = [t.reshape(weights[n].shape) for t in res]

    shapes = [weights[n].shape for n in REPLICATED]
    total = sum(math.prod(s) for s in shapes)
    rows = -(-total // (LANES * 256)) * 256
    partial = _pack([jnp.stack([grads[l][n] for l in range(DEPTH)]) for n in REPLICATED], rows)
    (parts,) = _exchange("gather_small_grads", [partial], scatter=False)
    res = _adamw("adamw_replicated", _pack([weights[n] for n in REPLICATED], rows), parts,
                 _pack([moments_m[n] for n in REPLICATED], rows), _pack([moments_v[n] for n in REPLICATED], rows))
    for n, vals in zip(REPLICATED, zip(*[_unpack(t, shapes) for t in res])):
        new[n] = list(vals)

    return (loss, dx[None], *[new[n][0] for n in WEIGHTS], *[new[n][1] for n in WEIGHTS],
            *[new[n][2] for n in WEIGHTS], *[new[n][3] for n in WEIGHTS])
```

```python
import functools
import math

import jax
import jax.numpy as jnp
from jax import lax
from jax.experimental import pallas as pl
from jax.experimental.pallas import tpu as pltpu

F32 = jnp.float32
BF16 = jnp.bfloat16

D_MODEL = 1024
DEPTH = 4
BRANCH = 512
S5_GROUPS, S5_GROUP, S5_STATE = 32, 16, 64
S5_N = S5_GROUPS * S5_STATE
LRU_HEADS, LRU_HEAD_DIM = 8, 64
LRU_C = 8.0
CONV_WIDTH = 4
FOX_HEADS, FOX_HEAD_DIM = 8, 64
FFN_HIDDEN = 2816
ALPHA = (2.0 * DEPTH) ** 0.25
LN_EPS = 1e-5
IN_TOTAL = 6 * BRANCH + FOX_HEADS + 3 * D_MODEL
FG_PAD = 128
Z_MAIN = 6 * BRANCH
Z_GATE0 = Z_MAIN
Z_FG0 = Z_MAIN + 3 * D_MODEL
Z_TOTAL = Z_FG0 + FG_PAD
N_DEV = 8
MESH_AXES = ("x", "y", "c")

ADAM_LR, ADAM_B1, ADAM_B2, ADAM_EPS, ADAM_WD, ADAM_STEP = 0.001, 0.9, 0.999, 1e-08, 0.01, 10

VMEM_LIMIT_BYTES = 48 * 1024 * 1024
SUBLANES = 8
LANES = 128
NEG_BIG = -1e30

WEIGHTS = ['w_in', 'b_f', 'b_gate', 's5_a_re', 's5_a_im', 's5_log_dt', 's5_b_re', 's5_b_im', 's5_c_re', 's5_c_im',
           's5_d', 's5_w_glu', 's5_b_glu', 'lru_conv_w', 'lru_conv_b', 'lru_w_a', 'lru_b_a', 'lru_w_x', 'lru_b_x',
           'lru_lambda', 'w_branch', 'w_out', 'ln1_g', 'ln1_b', 'w_ffn_gate', 'w_ffn_up', 'w_ffn_down', 'ln2_g',
           'ln2_b']
SHARDED = ['w_in', 's5_w_glu', 'lru_conv_w', 'w_branch', 'w_out', 'w_ffn_gate', 'w_ffn_up', 'w_ffn_down']
REPLICATED = [n for n in WEIGHTS if n not in SHARDED]


def _params(*sem):
    return pltpu.CompilerParams(dimension_semantics=sem, vmem_limit_bytes=VMEM_LIMIT_BYTES)


def _tile(dim, want):
    if dim % LANES:
        return dim
    t = min(want, dim) // LANES * LANES
    while dim % t:
        t -= LANES
    return t


MM_VMEM_BUDGET_BYTES = 30 * 1024 * 1024
MM_MAX_TILE = 1024


def _divisor_tiles(dim, cap, must_divide=0):
    if dim % LANES:
        return [dim]
    out = [t for t in range(min(cap, dim) // LANES * LANES, 0, -LANES) if dim % t == 0 and must_divide % t == 0]
    return out or [dim]


def _mm_tiles(m, n, k, a_bytes, b_bytes, o_bytes, has_add, m_c0, n_c0, k_c0):
    for tk in _divisor_tiles(k, k, k_c0):
        best = None
        for tm in _divisor_tiles(m, MM_MAX_TILE, m_c0):
            for tn in _divisor_tiles(n, MM_MAX_TILE, n_c0):
                used = 2 * (tm * tk * a_bytes + tk * tn * b_bytes + tm * tn * o_bytes) + tm * tn * 4
                used += tm * tn * 4 if tk < k else 0
                used += 2 * tm * tn * 4 if has_add else 0
                if used <= MM_VMEM_BUDGET_BYTES and (best is None or tm * tn / (tm + tn) > best[0]):
                    best = (tm * tn / (tm + tn), tm, tn)
        if best is not None and (min(best[1], best[2]) >= 256 or tk <= 512):
            return best[1], best[2], tk
    raise ValueError("no matmul tiling fits VMEM")


def _mm(name, a, b, *, ta=False, tb=False, a_c0=0, a_w=None, b_c0=0, b_w=None, add=None, out_dtype=F32):
    a_w = a.shape[1] if a_w is None else a_w
    b_w = b.shape[1] if b_w is None else b_w
    m, k = (a_w, a.shape[0]) if ta else (a.shape[0], a_w)
    n = b.shape[0] if tb else b_w
    assert k == (b_w if tb else b.shape[0]), (name, a.shape, b.shape)
    tm, tn, tk = _mm_tiles(m, n, k, a.dtype.itemsize, b.dtype.itemsize, jnp.dtype(out_dtype).itemsize,
                           add is not None, a_c0 if ta else 0, 0 if tb else b_c0,
                           math.gcd(0 if ta else a_c0, b_c0 if tb else 0))
    nk = k // tk
    a_off = a_c0 // (tm if ta else tk)
    b_off = b_c0 // (tk if tb else tn)
    assert a_c0 % (tm if ta else tk) == 0 and b_c0 % (tk if tb else tn) == 0, name
    dims = (((0 if ta else 1,), (1 if tb else 0,)), ((), ()))
    a_total, b_total = m * k * a.dtype.itemsize, n * k * b.dtype.itemsize
    a_stays = a_total + b_total * (m // tm) <= b_total + a_total * (n // tn)
    if nk > 1:
        a_stays = True

    def mn(o, i):
        return (o, i) if a_stays else (i, o)

    def body(*refs):
        a_ref, b_ref = refs[0], refs[1]
        add_ref = refs[2] if add is not None else None
        o_ref = refs[3] if add is not None else refs[2]
        part = lax.dot_general(a_ref[...].astype(BF16), b_ref[...].astype(BF16), dims, preferred_element_type=F32)

        def finish(r):
            if add is not None:
                r = r + add_ref[...]
            o_ref[...] = r.astype(o_ref.dtype)

        if nk == 1:
            finish(part)
            return
        acc_ref = refs[-1]
        kk = pl.program_id(2)

        @pl.when(kk == 0)
        def _():
            acc_ref[...] = part

        @pl.when(kk > 0)
        def _():
            acc_ref[...] += part

        @pl.when(kk == nk - 1)
        def _():
            finish(acc_ref[...])

    def a_map(o, i, kk):
        im = mn(o, i)[0]
        return (kk, im + a_off) if ta else (im, kk + a_off)

    def b_map(o, i, kk):
        jn = mn(o, i)[1]
        return (jn, kk + b_off) if tb else (kk, jn + b_off)

    a_spec = pl.BlockSpec((tk, tm) if ta else (tm, tk), a_map)
    b_spec = pl.BlockSpec((tn, tk) if tb else (tk, tn), b_map)
    o_spec = pl.BlockSpec((tm, tn), lambda o, i, kk: mn(o, i))
    ins, in_specs = [a, b], [a_spec, b_spec]
    if add is not None:
        ins.append(add)
        in_specs.append(o_spec)
    grid = (m // tm, n // tn, nk) if a_stays else (n // tn, m // tm, nk)
    return pl.pallas_call(
        body, grid=grid, in_specs=in_specs, out_specs=o_spec, out_shape=jax.ShapeDtypeStruct((m, n), out_dtype),
        scratch_shapes=[pltpu.VMEM((tm, tn), F32)] if nk > 1 else [],
        compiler_params=_params("parallel", "parallel", "arbitrary"), name=name)(*ins)


def _row_spec(tl, c0, w):
    assert c0 % w == 0
    return pl.BlockSpec((tl, w), lambda i: (i, c0 // w))


def _whole_spec(p):
    return pl.BlockSpec(p.shape, lambda i: (0,) * p.ndim)


def _ew(name, f, rows, prm, out_ws, tl=256, out_dtypes=None):
    out_dtypes = out_dtypes or [F32] * len(out_ws)
    nrows, nprm = len(rows), len(prm)
    length = rows[0][0].shape[0]
    tl = min(tl, length)

    def body(*refs):
        vals = [r[...] for r in refs[:nrows + nprm]]
        outs = f(*vals)
        for o_ref, o in zip(refs[nrows + nprm:], outs):
            o_ref[...] = o.astype(o_ref.dtype)

    return pl.pallas_call(
        body, grid=(length // tl,),
        in_specs=[_row_spec(tl, c0, w) for (_, c0, w) in rows] + [_whole_spec(p) for p in prm],
        out_specs=[_row_spec(tl, 0, w) for w in out_ws],
        out_shape=[jax.ShapeDtypeStruct((length, w), dt) for w, dt in zip(out_ws, out_dtypes)],
        compiler_params=_params("parallel"), name=name)(*[r[0] for r in rows], *prm)


def _ew_bwd(name, f, rows, prm, douts, row_grad, tl=256):
    nrows, nprm, nd = len(rows), len(prm), len(douts)
    length = rows[0][0].shape[0]
    tl = min(tl, length)
    want = [i for i in range(nrows) if row_grad[i]]

    def body(*refs):
        vals = [r[...] for r in refs[:nrows + nprm]]
        cts = tuple(r[...] for r in refs[nrows + nprm:nrows + nprm + nd])
        out_refs = refs[nrows + nprm + nd:]
        _, vjp = jax.vjp(lambda *v: tuple(f(*v)), *vals)
        grads = vjp(cts)
        for o_ref, i in zip(out_refs[:len(want)], want):
            o_ref[...] = grads[i].astype(o_ref.dtype)

        @pl.when(pl.program_id(0) == 0)
        def _():
            for o_ref in out_refs[len(want):]:
                o_ref[...] = jnp.zeros_like(o_ref)

        for o_ref, g in zip(out_refs[len(want):], grads[nrows:]):
            o_ref[...] += g

    return pl.pallas_call(
        body, grid=(length // tl,),
        in_specs=([_row_spec(tl, c0, w) for (_, c0, w) in rows] + [_whole_spec(p) for p in prm]
                  + [_row_spec(tl, 0, d.shape[1]) for d in douts]),
        out_specs=[_row_spec(tl, 0, rows[i][2]) for i in want] + [_whole_spec(p) for p in prm],
        out_shape=([jax.ShapeDtypeStruct((length, rows[i][2]), F32 if row_grad[i] is True else row_grad[i])
                    for i in want]
                   + [jax.ShapeDtypeStruct(p.shape, F32) for p in prm]),
        compiler_params=_params("arbitrary"), name=name)(*[r[0] for r in rows], *prm, *douts)


def _row_ids(width):
    return lax.broadcasted_iota(jnp.int32, (SUBLANES, width), 0)


def _shift_rows(v, d, reverse):
    return pltpu.roll(v, (SUBLANES - d) if reverse else d, 0)


def _scan_real(name, a, b, b2=None, *, reverse=False, bn=256):
    length, n = a.shape
    bn = _tile(n, bn)
    nb = length // SUBLANES
    operands = [a, b] if b2 is None else [a, b, b2]

    def body(*refs):
        a_ref, b_ref, h_ref = refs[0], refs[1], refs[-1]
        rows = _row_ids(bn)

        def step(it, carry):
            i = (nb - 1 - it) if reverse else it
            sl = pl.ds(pl.multiple_of(i * SUBLANES, SUBLANES), SUBLANES)
            av, bv = a_ref[sl, :], b_ref[sl, :]
            if b2 is not None:
                bv = bv + refs[2][sl, :]
            for d in (1, 2, 4):
                live = (rows < SUBLANES - d) if reverse else (rows >= d)
                a_in = jnp.where(live, _shift_rows(av, d, reverse), 1.0)
                b_in = jnp.where(live, _shift_rows(bv, d, reverse), 0.0)
                bv = bv + av * b_in
                av = av * a_in
            hv = bv + av * carry
            h_ref[sl, :] = hv
            edge = hv[0:1, :] if reverse else hv[SUBLANES - 1:SUBLANES, :]
            return jnp.broadcast_to(edge, (SUBLANES, bn))

        lax.fori_loop(0, nb, step, jnp.zeros((SUBLANES, bn), F32))

    spec = pl.BlockSpec((length, bn), lambda j: (0, j))
    return pl.pallas_call(body, grid=(n // bn,), in_specs=[spec] * len(operands), out_specs=spec,
                          out_shape=jax.ShapeDtypeStruct((length, n), F32),
                          compiler_params=_params("parallel"), name=name)(*operands)


def _cmul(ar, ai, br, bi):
    return ar * br - ai * bi, ar * bi + ai * br


def _scan_cplx(name, lam_re, lam_im, x_re, x_im, *, reverse=False, h_re=None, h_im=None, bn=128):
    length, n = x_re.shape
    bn = _tile(n, bn)
    nb = length // SUBLANES
    with_dot = h_re is not None

    def body(*refs):
        if with_dot:
            lr_ref, li_ref, xr_ref, xi_ref, hr_ref, hi_ref, gr_ref, gi_ref, dr_ref, di_ref = refs
        else:
            lr_ref, li_ref, xr_ref, xi_ref, gr_ref, gi_ref = refs
        rows = _row_ids(bn)
        lr = jnp.broadcast_to(lr_ref[...], (SUBLANES, bn))
        li = jnp.broadcast_to(li_ref[...], (SUBLANES, bn))
        powers = [(lr, li)]
        for _ in range(SUBLANES - 1):
            powers.append(_cmul(powers[-1][0], powers[-1][1], lr, li))
        zero = jnp.zeros((SUBLANES, bn), F32)
        steps = []
        for d in (1, 2, 4):
            live = (rows < SUBLANES - d) if reverse else (rows >= d)
            steps.append((d, jnp.where(live, powers[d - 1][0], 0.0), jnp.where(live, powers[d - 1][1], 0.0)))
        cr, ci = zero, zero
        for r in range(SUBLANES):
            e = (SUBLANES - r) if reverse else (r + 1)
            cr = jnp.where(rows == r, powers[e - 1][0], cr)
            ci = jnp.where(rows == r, powers[e - 1][1], ci)

        def step(it, carry):
            i = (nb - 1 - it) if reverse else it
            sl = pl.ds(pl.multiple_of(i * SUBLANES, SUBLANES), SUBLANES)
            vr, vi = xr_ref[sl, :], xi_ref[sl, :]
            for d, pr, pi in steps:
                sr, si = _cmul(pr, pi, _shift_rows(vr, d, reverse), _shift_rows(vi, d, reverse))
                vr, vi = vr + sr, vi + si
            kr, ki = _cmul(cr, ci, carry[0], carry[1])
            vr, vi = vr + kr, vi + ki
            gr_ref[sl, :] = vr
            gi_ref[sl, :] = vi
            er = vr[0:1, :] if reverse else vr[SUBLANES - 1:SUBLANES, :]
            ei = vi[0:1, :] if reverse else vi[SUBLANES - 1:SUBLANES, :]
            new = (jnp.broadcast_to(er, (SUBLANES, bn)), jnp.broadcast_to(ei, (SUBLANES, bn)))
            if not with_dot:
                return new
            prev = pl.ds(pl.multiple_of(jnp.maximum(i - 1, 0) * SUBLANES, SUBLANES), SUBLANES)
            keep = jnp.where(i > 0, 1.0, 0.0)
            pr_ = jnp.broadcast_to(hr_ref[prev, :][SUBLANES - 1:SUBLANES, :], (SUBLANES, bn)) * keep
            pi_ = jnp.broadcast_to(hi_ref[prev, :][SUBLANES - 1:SUBLANES, :], (SUBLANES, bn)) * keep
            hr = jnp.where(rows == 0, pr_, pltpu.roll(hr_ref[sl, :], 1, 0))
            hi = jnp.where(rows == 0, pi_, pltpu.roll(hi_ref[sl, :], 1, 0))
            return new + (carry[2] + vr * hr + vi * hi, carry[3] + vi * hr - vr * hi)

        init = (zero, zero, zero, zero) if with_dot else (zero, zero)
        out = lax.fori_loop(0, nb, step, init)
        if with_dot:
            dr_ref[...] = jnp.sum(out[2], axis=0, keepdims=True)
            di_ref[...] = jnp.sum(out[3], axis=0, keepdims=True)

    col = pl.BlockSpec((length, bn), lambda j: (0, j))
    vec = pl.BlockSpec((1, bn), lambda j: (0, j))
    ins = [lam_re, lam_im, x_re, x_im] + ([h_re, h_im] if with_dot else [])
    in_specs = [vec, vec, col, col] + ([col, col] if with_dot else [])
    out_specs = [col, col] + ([vec, vec] if with_dot else [])
    full = jax.ShapeDtypeStruct((length, n), F32)
    row = jax.ShapeDtypeStruct((1, n), F32)
    out_shape = [full, full] + ([row, row] if with_dot else [])
    return pl.pallas_call(body, grid=(n // bn,), in_specs=in_specs, out_specs=out_specs, out_shape=out_shape,
                          compiler_params=_params("parallel"), name=name)(*ins)


FOX_SCALE = FOX_HEAD_DIM ** -0.5
FOX_AUG = 128
FOX_CQ0 = FOX_HEAD_DIM
FOX_CK0 = FOX_HEAD_DIM + 3
NT = (((1,), (1,)), ((), ()))


def _fox_logits_t(ka, qa, on_diagonal):
    st = lax.dot_general(ka, qa, NT, preferred_element_type=F32)
    if on_diagonal:
        key = lax.broadcasted_iota(jnp.int32, st.shape, 0)
        query = lax.broadcasted_iota(jnp.int32, st.shape, 1)
        st = jnp.where(key <= query, st, NEG_BIG)
    return st


def _fox_specs(t, q_first):
    def q_idx(h, a, b):
        i, j = (a, b) if q_first else (b, a)
        return i if q_first else jnp.maximum(i, j)

    def k_idx(h, a, b):
        i, j = (a, b) if q_first else (b, a)
        return jnp.minimum(i, j) if q_first else j

    rows = lambda idx, w: pl.BlockSpec((None, t, w), lambda h, a, b: (h, idx(h, a, b), 0))
    cols = lambda idx, w: pl.BlockSpec((None, w, t), lambda h, a, b: (h, 0, idx(h, a, b)))
    return rows, cols, q_idx, k_idx


def _fox_fwd(qa, ka, vt, t=512):
    heads, length, _ = qa.shape
    dh = vt.shape[1]
    t = min(t, length)
    nt = length // t

    def body(qa_ref, ka_ref, vt_ref, o_ref, lse_ref, m_sc, l_sc, acc_sc):
        qi, ki = pl.program_id(1), pl.program_id(2)

        @pl.when(ki == 0)
        def _():
            m_sc[...] = jnp.full_like(m_sc, NEG_BIG)
            l_sc[...] = jnp.zeros_like(l_sc)
            acc_sc[...] = jnp.zeros_like(acc_sc)

        def step(on_diagonal):
            st = _fox_logits_t(ka_ref[...], qa_ref[...], on_diagonal)
            m_old = m_sc[...]
            m_new = jnp.maximum(m_old, jnp.max(st, axis=0, keepdims=True))
            pt = jnp.exp(st - m_new)
            scale = jnp.exp(m_old - m_new)
            l_sc[...] = scale * l_sc[...] + jnp.sum(pt, axis=0, keepdims=True)
            acc_sc[...] = scale * acc_sc[...] + jnp.dot(vt_ref[...], pt.astype(BF16), preferred_element_type=F32)
            m_sc[...] = m_new

        pl.when(ki < qi)(functools.partial(step, False))
        pl.when(ki == qi)(functools.partial(step, True))

        @pl.when(ki == nt - 1)
        def _():
            o_ref[...] = acc_sc[...] / l_sc[...]
            lse_ref[...] = m_sc[...] + jnp.log(l_sc[...])

    rows, cols, q_idx, k_idx = _fox_specs(t, True)
    return pl.pallas_call(
        body, grid=(heads, nt, nt), in_specs=[rows(q_idx, FOX_AUG), rows(k_idx, FOX_AUG), cols(k_idx, dh)],
        out_specs=[cols(q_idx, dh), cols(q_idx, 1)],
        out_shape=[jax.ShapeDtypeStruct((heads, dh, length), F32), jax.ShapeDtypeStruct((heads, 1, length), F32)],
        scratch_shapes=[pltpu.VMEM((1, t), F32), pltpu.VMEM((1, t), F32), pltpu.VMEM((dh, t), F32)],
        compiler_params=_params("parallel", "parallel", "arbitrary"), name="fox_fwd")(qa, ka, vt)


def _fox_ds_t(qa_ref, ka_ref, v_ref, dot_ref, ot_ref, lse_ref, on_diagonal):
    pt = jnp.exp(_fox_logits_t(ka_ref[...], qa_ref[...], on_diagonal) - lse_ref[...])
    dpt = jnp.dot(v_ref[...], dot_ref[...], preferred_element_type=F32)
    delta = jnp.sum(dot_ref[...].astype(F32) * ot_ref[...], axis=0, keepdims=True)
    return pt, pt * (dpt - delta)


def _fox_bwd_q(qa, ka, kat, v, dot, ot, lse, t=512):
    heads, length, _ = qa.shape
    dh = v.shape[2]
    t = min(t, length)
    nt = length // t

    def body(qa_ref, ka_ref, kat_ref, v_ref, dot_ref, ot_ref, lse_ref, dqa_ref, acc_sc):
        qi, ki = pl.program_id(1), pl.program_id(2)

        @pl.when(ki == 0)
        def _():
            acc_sc[...] = jnp.zeros_like(acc_sc)

        def step(on_diagonal):
            _, dst = _fox_ds_t(qa_ref, ka_ref, v_ref, dot_ref, ot_ref, lse_ref, on_diagonal)
            acc_sc[...] += jnp.dot(kat_ref[...], dst.astype(BF16), preferred_element_type=F32)

        pl.when(ki < qi)(functools.partial(step, False))
        pl.when(ki == qi)(functools.partial(step, True))

        @pl.when(ki == nt - 1)
        def _():
            dqa_ref[...] = acc_sc[...]

    rows, cols, q_idx, k_idx = _fox_specs(t, True)
    return pl.pallas_call(
        body, grid=(heads, nt, nt),
        in_specs=[rows(q_idx, FOX_AUG), rows(k_idx, FOX_AUG), cols(k_idx, FOX_AUG), rows(k_idx, dh), cols(q_idx, dh),
                  cols(q_idx, dh), cols(q_idx, 1)],
        out_specs=cols(q_idx, FOX_AUG), out_shape=jax.ShapeDtypeStruct((heads, FOX_AUG, length), F32),
        scratch_shapes=[pltpu.VMEM((FOX_AUG, t), F32)],
        compiler_params=_params("parallel", "parallel", "arbitrary"), name="fox_bwd_q")(qa, ka, kat, v, dot, ot, lse)


def _fox_bwd_kv(qa, ka, v, do, dot, ot, lse, t=512):
    heads, length, _ = qa.shape
    dh = v.shape[2]
    t = min(t, length)
    nt = length // t

    def body(qa_ref, ka_ref, v_ref, do_ref, dot_ref, ot_ref, lse_ref, dka_ref, dv_ref, dka_sc, dv_sc):
        ki, qi = pl.program_id(1), pl.program_id(2)

        @pl.when(qi == 0)
        def _():
            dka_sc[...] = jnp.zeros_like(dka_sc)
            dv_sc[...] = jnp.zeros_like(dv_sc)

        def step(on_diagonal):
            pt, dst = _fox_ds_t(qa_ref, ka_ref, v_ref, dot_ref, ot_ref, lse_ref, on_diagonal)
            dv_sc[...] += jnp.dot(pt.astype(BF16), do_ref[...], preferred_element_type=F32)
            dka_sc[...] += jnp.dot(dst.astype(BF16), qa_ref[...], preferred_element_type=F32)

        pl.when(qi > ki)(functools.partial(step, False))
        pl.when(qi == ki)(functools.partial(step, True))

        @pl.when(qi == nt - 1)
        def _():
            dka_ref[...] = dka_sc[...]
            dv_ref[...] = dv_sc[...]

    rows, cols, q_idx, k_idx = _fox_specs(t, False)
    return pl.pallas_call(
        body, grid=(heads, nt, nt),
        in_specs=[rows(q_idx, FOX_AUG), rows(k_idx, FOX_AUG), rows(k_idx, dh), rows(q_idx, dh), cols(q_idx, dh),
                  cols(q_idx, dh), cols(q_idx, 1)],
        out_specs=[rows(k_idx, FOX_AUG), rows(k_idx, dh)],
        out_shape=[jax.ShapeDtypeStruct((heads, length, FOX_AUG), F32), jax.ShapeDtypeStruct((heads, length, dh), F32)],
        scratch_shapes=[pltpu.VMEM((t, FOX_AUG), F32), pltpu.VMEM((t, dh), F32)],
        compiler_params=_params("parallel", "parallel", "arbitrary"), name="fox_bwd_kv")(qa, ka, v, do, dot, ot, lse)


def _split3(x):
    hi = lax.reduce_precision(x, 8, 7)
    mid = lax.reduce_precision(x - hi, 8, 7)
    return [hi, mid, lax.reduce_precision(x - hi - mid, 8, 7)]


def _fox_operands(z, cum):
    length = z.shape[0]
    parts = jnp.stack(_split3(cum[:, :FOX_HEADS].T), axis=-1)
    ones = jnp.ones_like(parts)
    pad = jnp.zeros((FOX_HEADS, length, FOX_AUG - FOX_HEAD_DIM - 6), F32)
    qa = jnp.concatenate([_heads(z, 3 * BRANCH) * FOX_SCALE, parts, ones, pad], axis=-1).astype(BF16)
    ka = jnp.concatenate([_heads(z, 4 * BRANCH), ones, -parts, pad], axis=-1).astype(BF16)
    v = _heads(z, 5 * BRANCH).astype(BF16)
    return qa, ka, ka.transpose(0, 2, 1), v, v.transpose(0, 2, 1)


def _softplus(x):
    return jnp.maximum(x, 0.0) + jnp.log1p(jnp.exp(-jnp.abs(x)))


def _f_s5_disc(a_re, a_im, log_dt, b_re, b_im):
    dt = jnp.exp(log_dt)
    mag = jnp.exp(a_re * dt)
    lr, li = mag * jnp.cos(a_im * dt), mag * jnp.sin(a_im * dt)
    den = a_re * a_re + a_im * a_im
    qr = ((lr - 1.0) * a_re + li * a_im) / den
    qi = (li * a_re - (lr - 1.0) * a_im) / den
    return lr, li, qr * b_re - qi * b_im, qr * b_im + qi * b_re


def _f_s5_y1(hc_re, hc_im, u, d):
    return (jax.nn.gelu(hc_re + hc_im + d * u),)


def _f_s5_glu(y1, pre, b):
    return (y1 * jax.nn.sigmoid(pre + b),)


def _f_conv(x0, x1, x2, x3, w0, w1, w2, w3, b):
    return (b + w0 * x0 + w1 * x1 + w2 * x2 + w3 * x3,)


def _f_conv_t(d0, d1, d2, d3, w0, w1, w2, w3):
    return (w0 * d0 + w1 * d1 + w2 * d2 + w3 * d3,)


def _lru_coeffs(xc, pa, px, b_a, b_x, lam):
    r = jax.nn.sigmoid(pa + b_a)
    i = jax.nn.sigmoid(px + b_x)
    log_a = -LRU_C * _softplus(-lam) * r
    a = jnp.exp(log_a)
    mult = jnp.sqrt(-jnp.tanh(log_a) * (a * a + 1.0))
    return a, mult * (i * xc)


def _f_lru_gates(xc, pa, px, b_a, b_x, lam):
    return _lru_coeffs(xc, pa, px, b_a, b_x, lam)


def _f_lru_step(xc, pa, px, h_prev, b_a, b_x, lam):
    a, b = _lru_coeffs(xc, pa, px, b_a, b_x, lam)
    return (a * h_prev + b,)


def _f_lru_out(gate, h):
    return (jax.nn.gelu(gate) * h,)


def _f_logf(zf, bf):
    return (-_softplus(-(zf + bf)),)


def _f_merge(p0, p1, p2, z0, z1, z2, b0, b1, b2):
    return (jax.nn.sigmoid(z0 + b0) * p0 + jax.nn.sigmoid(z1 + b1) * p1 + jax.nn.sigmoid(z2 + b2) * p2,)


def _f_ln(x, r, g, b):
    s = ALPHA * x + r
    mu = jnp.mean(s, axis=-1, keepdims=True)
    var = jnp.mean(jnp.square(s - mu), axis=-1, keepdims=True)
    return ((s - mu) * lax.rsqrt(var + LN_EPS) * g + b,)


def _f_swiglu(hg, hu):
    return (jax.nn.silu(hg) * hu,)


def _full(a):
    return (a, 0, a.shape[1])


def _blockdiag(t):
    g, a, b = t.shape
    eye = jnp.eye(g, dtype=t.dtype)
    return (t[:, :, None, :] * eye[:, None, :, None]).reshape(g * a, g * b)


def _blockdiag_take(d, g, a, b):
    eye = jnp.eye(g, dtype=d.dtype)
    return (d.reshape(g, a, g, b) * eye[:, None, :, None]).sum(axis=2)


def _delay(a, j):
    return a if j == 0 else jnp.pad(a, ((j, 0), (0, 0)))[:a.shape[0]]


def _advance(a, j):
    return a if j == 0 else jnp.pad(a, ((0, j), (0, 0)))[j:]


def _heads(a, c0=0):
    length = a.shape[0]
    return a[:, c0:c0 + BRANCH].reshape(length, FOX_HEADS, FOX_HEAD_DIM).transpose(1, 0, 2)


def _unheads(a):
    return a.transpose(1, 0, 2).reshape(a.shape[1], BRANCH)


def _row(v):
    return v.reshape(1, -1).astype(F32)


def _col(v):
    return v.reshape(-1, 1).astype(F32)


def _prep_layer(w):
    p = {}
    w_in = w['w_in']
    p['wc'] = jnp.concatenate(
        [w_in[:, :Z_MAIN], w_in[:, Z_MAIN + FOX_HEADS:], w_in[:, Z_MAIN:Z_MAIN + FOX_HEADS],
         jnp.zeros((D_MODEL, FG_PAD - FOX_HEADS), w_in.dtype)], axis=1)
    p['b_f'] = jnp.pad(_row(w['b_f']), ((0, 0), (0, FG_PAD - FOX_HEADS)))
    p['b_gate'] = [_row(w['b_gate'][k * D_MODEL:(k + 1) * D_MODEL]) for k in range(3)]
    p['disc_in'] = [_col(w['s5_a_re']), _col(w['s5_a_im']), _col(jnp.repeat(w['s5_log_dt'], S5_STATE)),
                    w['s5_b_re'].reshape(S5_N, S5_GROUP), w['s5_b_im'].reshape(S5_N, S5_GROUP)]
    lam_re, lam_im, bb_re, bb_im = _ew("s5_disc", _f_s5_disc, [_full(a) for a in p['disc_in']], [],
                                       [1, 1, S5_GROUP, S5_GROUP], tl=S5_N)
    p['lam_re'], p['lam_im'] = lam_re.reshape(1, S5_N), lam_im.reshape(1, S5_N)
    to_blk = lambda t: _blockdiag(t.reshape(S5_GROUPS, S5_STATE, S5_GROUP).transpose(0, 2, 1)).astype(BF16)
    p['s5_bre'], p['s5_bim'] = to_blk(bb_re), to_blk(bb_im)
    p['s5_cre'] = _blockdiag(w['s5_c_re'].transpose(0, 2, 1)).astype(BF16)
    p['s5_cimn'] = _blockdiag(-w['s5_c_im'].transpose(0, 2, 1)).astype(BF16)
    p['s5_d'], p['wglu'], p['bglu'] = _row(w['s5_d']), w['s5_w_glu'], _row(w['s5_b_glu'])
    p['conv_w'] = [_row(w['lru_conv_w'][CONV_WIDTH - 1 - j]) for j in range(CONV_WIDTH)]
    p['conv_b'] = _row(w['lru_conv_b'])
    p['wax'] = jnp.concatenate([_blockdiag(w['lru_w_a']), _blockdiag(w['lru_w_x'])], axis=1).astype(BF16)
    p['b_a'], p['b_x'], p['lam'] = _row(w['lru_b_a']), _row(w['lru_b_x']), _row(w['lru_lambda'])
    p['wb'] = [w['w_branch'][k] for k in range(3)]
    p['wout'] = w['w_out']
    p['ln1'] = [_row(w['ln1_g']), _row(w['ln1_b'])]
    p['wgu'] = jnp.concatenate([w['w_ffn_gate'], w['w_ffn_up']], axis=1)
    p['wd'] = w['w_ffn_down']
    p['ln2'] = [_row(w['ln2_g']), _row(w['ln2_b'])]
    return p


def _with_copy(f):
    def g(*args):
        (y,) = f(*args)
        return y, y
    return g


def _layer_fwd(x, x_bf, p):
    length = x.shape[0]
    r = {'x': x, 'x_bf': x_bf}
    z = r['z'] = _mm("z_in", x_bf, p['wc'])
    bu_re = _mm("s5_bu_re", z, p['s5_bre'], a_w=BRANCH)
    bu_im = _mm("s5_bu_im", z, p['s5_bim'], a_w=BRANCH)
    r['h_re'], r['h_im'] = _scan_cplx("s5_scan", p['lam_re'], p['lam_im'], bu_re, bu_im)
    r['hc_re'] = _mm("s5_hc_re", r['h_re'], p['s5_cre'])
    r['hc_im'] = _mm("s5_hc_im", r['h_im'], p['s5_cimn'])
    r['y1'], r['y1_bf'] = _ew("s5_y1", _with_copy(_f_s5_y1), [_full(r['hc_re']), _full(r['hc_im']), (z, 0, BRANCH)],
                              [p['s5_d']], [BRANCH, BRANCH], out_dtypes=[F32, BF16])
    r['pre'] = _mm("s5_glu_pre", r['y1_bf'], p['wglu'])
    (r['ys5'],) = _ew("s5_glu", _f_s5_glu, [_full(r['y1']), _full(r['pre'])], [p['bglu']], [BRANCH], out_dtypes=[BF16])
    xl = z[:, BRANCH:2 * BRANCH]
    r['xd'] = [_delay(xl, j) for j in range(1, CONV_WIDTH)]
    r['xc'], r['xc_bf'] = _ew("lru_conv", _with_copy(_f_conv), [(z, BRANCH, BRANCH)] + [_full(a) for a in r['xd']],
                              p['conv_w'] + [p['conv_b']], [BRANCH, BRANCH], out_dtypes=[F32, BF16])
    r['papx'] = _mm("lru_gate_mm", r['xc_bf'], p['wax'])
    r['a'], b = _ew("lru_gates", _f_lru_gates, [_full(r['xc']), (r['papx'], 0, BRANCH), (r['papx'], BRANCH, BRANCH)],
                    [p['b_a'], p['b_x'], p['lam']], [BRANCH, BRANCH])
    r['h'] = _scan_real("lru_scan", r['a'], b)
    (r['ylru'],) = _ew("lru_out", _f_lru_out, [(z, 2 * BRANCH, BRANCH), _full(r['h'])], [], [BRANCH], out_dtypes=[BF16])
    (logf,) = _ew("fox_logf", _f_logf, [(z, Z_FG0, FG_PAD)], [p['b_f']], [FG_PAD])
    cum = _scan_real("fox_cum", jnp.ones((length, FG_PAD), F32), logf)
    qa, ka, kat, v, vt = _fox_operands(z, cum)
    r['fox'] = (qa, ka, kat, v)
    r['ot'], r['lse'] = _fox_fwd(qa, ka, vt)
    r['yfox'] = r['ot'].reshape(BRANCH, length).T.astype(BF16)
    ys = [r['ys5'], r['ylru'], r['yfox']]
    r['proj'] = [_mm("proj_%d" % k, ys[k], p['wb'][k]) for k in range(3)]
    gate_rows = [(z, Z_GATE0 + k * D_MODEL, D_MODEL) for k in range(3)]
    (r['mix'],) = _ew("merge", _f_merge, [_full(a) for a in r['proj']] + gate_rows, p['b_gate'], [D_MODEL], tl=128,
                      out_dtypes=[BF16])
    r['mixed'] = _mm("w_out", r['mix'], p['wout'])
    two = dict(out_ws=[D_MODEL, D_MODEL], out_dtypes=[F32, BF16])
    x1, r['x1_bf'] = _ew("ln1", _with_copy(_f_ln), [_full(x), _full(r['mixed'])], p['ln1'], **two)
    r['x1'] = x1
    r['hgu'] = _mm("ffn_in", r['x1_bf'], p['wgu'])
    (r['hid'],) = _ew("swiglu", _f_swiglu, [(r['hgu'], 0, FFN_HIDDEN), (r['hgu'], FFN_HIDDEN, FFN_HIDDEN)], [],
                      [FFN_HIDDEN], tl=128, out_dtypes=[BF16])
    r['f'] = _mm("ffn_out", r['hid'], p['wd'])
    x2, x2_bf = _ew("ln2", _with_copy(_f_ln), [_full(x1), _full(r['f'])], p['ln2'], **two)
    return x2, x2_bf, r


def _layer_bwd(dx2, r, p):
    g = {}
    x, z, x1 = r['x'], r['z'], r['x1']
    dx1_n, df, g['ln2_g'], g['ln2_b'] = _ew_bwd("ln2_bwd", _f_ln, [_full(x1), _full(r['f'])], p['ln2'], [dx2],
                                                [True, BF16])
    dhid = _mm("ffn_out_dx", df, p['wd'], tb=True)
    g['w_ffn_down'] = _mm("ffn_out_dw", r['hid'], df, ta=True, out_dtype=BF16)
    hgu_rows = [(r['hgu'], 0, FFN_HIDDEN), (r['hgu'], FFN_HIDDEN, FFN_HIDDEN)]
    dhg, dhu = _ew_bwd("swiglu_bwd", _f_swiglu, hgu_rows, [], [dhid], [BF16, BF16], tl=128)
    g['w_ffn_gate'] = _mm("ffn_gate_dw", r['x1_bf'], dhg, ta=True, out_dtype=BF16)
    g['w_ffn_up'] = _mm("ffn_up_dw", r['x1_bf'], dhu, ta=True, out_dtype=BF16)
    dx1 = _mm("ffn_gate_dx", dhg, p['wgu'], tb=True, b_w=FFN_HIDDEN, add=dx1_n)
    dx1 = _mm("ffn_up_dx", dhu, p['wgu'], tb=True, b_c0=FFN_HIDDEN, b_w=FFN_HIDDEN, add=dx1)
    dx_n, dmixed, g['ln1_g'], g['ln1_b'] = _ew_bwd("ln1_bwd", _f_ln, [_full(x), _full(r['mixed'])], p['ln1'], [dx1],
                                                   [True, BF16])
    dmix = _mm("w_out_dx", dmixed, p['wout'], tb=True)
    g['w_out'] = _mm("w_out_dw", r['mix'], dmixed, ta=True, out_dtype=BF16)
    gate_rows = [(z, Z_GATE0 + k * D_MODEL, D_MODEL) for k in range(3)]
    mg = _ew_bwd("merge_bwd", _f_merge, [_full(a) for a in r['proj']] + gate_rows, p['b_gate'], [dmix], [BF16] * 6,
                 tl=128)
    dproj, dzg = mg[0:3], mg[3:6]
    g['b_gate'] = jnp.concatenate([b.reshape(-1) for b in mg[6:9]])
    ys = [r['ys5'], r['ylru'], r['yfox']]
    dys = [_mm("proj_%d_dx" % k, dproj[k], p['wb'][k], tb=True) for k in range(3)]
    g['w_branch'] = jnp.stack([_mm("proj_%d_dw" % k, ys[k], dproj[k], ta=True, out_dtype=BF16) for k in range(3)])
    qa, ka, kat, v = r['fox']
    do = _heads(dys[2]).astype(BF16)
    dot = do.transpose(0, 2, 1)
    dqat = _fox_bwd_q(qa, ka, kat, v, dot, r['ot'], r['lse'])
    dka, dvh = _fox_bwd_kv(qa, ka, v, do, dot, r['ot'], r['lse'])
    pad_heads = lambda a: jnp.pad(a.T, ((0, 0), (0, FG_PAD - FOX_HEADS)))
    dlogf = _scan_real("fox_cum_bwd", jnp.ones((x.shape[0], FG_PAD), F32), pad_heads(dqat[:, FOX_CQ0, :]),
                       pad_heads(-dka[:, :, FOX_CK0]), reverse=True)
    dq = (dqat[:, :FOX_HEAD_DIM, :].reshape(BRANCH, x.shape[0]).T * FOX_SCALE).astype(BF16)
    dqkv = [dq, _unheads(dka[:, :, :FOX_HEAD_DIM]).astype(BF16), _unheads(dvh).astype(BF16)]
    dzf, dbf = _ew_bwd("fox_logf_bwd", _f_logf, [(z, Z_FG0, FG_PAD)], [p['b_f']], [dlogf], [BF16])
    g['b_f'] = dbf[0, :FOX_HEADS]
    dgate, dh = _ew_bwd("lru_out_bwd", _f_lru_out, [(z, 2 * BRANCH, BRANCH), _full(r['h'])], [], [dys[1]], [BF16, True])
    db = _scan_real("lru_scan_bwd", _advance(r['a'], 1), dh, reverse=True)
    gates_rows = [_full(r['xc']), (r['papx'], 0, BRANCH), (r['papx'], BRANCH, BRANCH), _full(_delay(r['h'], 1))]
    dxc, dpa, dpx, db_a, db_x, dlam = _ew_bwd("lru_gates_bwd", _f_lru_step, gates_rows, [p['b_a'], p['b_x'], p['lam']],
                                              [db], [True, BF16, BF16, False])
    dxc = _mm("lru_a_dx", dpa, p['wax'], tb=True, b_w=BRANCH, add=dxc)
    dxc = _mm("lru_x_dx", dpx, p['wax'], tb=True, b_c0=BRANCH, b_w=BRANCH, add=dxc)
    take_heads = lambda d: _blockdiag_take(d, LRU_HEADS, LRU_HEAD_DIM, LRU_HEAD_DIM)
    g['lru_w_a'] = take_heads(_mm("lru_a_dw", r['xc_bf'], dpa, ta=True))
    g['lru_w_x'] = take_heads(_mm("lru_x_dw", r['xc_bf'], dpx, ta=True))
    g['lru_b_a'] = db_a.reshape(LRU_HEADS, LRU_HEAD_DIM)
    g['lru_b_x'] = db_x.reshape(LRU_HEADS, LRU_HEAD_DIM)
    g['lru_lambda'] = dlam.reshape(-1)
    conv_rows = [(z, BRANCH, BRANCH)] + [_full(a) for a in r['xd']]
    cw = _ew_bwd("lru_conv_dw", _f_conv, conv_rows, p['conv_w'] + [p['conv_b']], [dxc], [False] * CONV_WIDTH)
    g['lru_conv_w'] = jnp.concatenate([cw[CONV_WIDTH - 1 - k] for k in range(CONV_WIDTH)], axis=0)
    g['lru_conv_b'] = cw[CONV_WIDTH].reshape(-1)
    (dxl,) = _ew("lru_conv_dx", _f_conv_t, [_full(_advance(dxc, j)) for j in range(CONV_WIDTH)], p['conv_w'], [BRANCH],
                 out_dtypes=[BF16])
    dy1, dpre, dbglu = _ew_bwd("s5_glu_bwd", _f_s5_glu, [_full(r['y1']), _full(r['pre'])], [p['bglu']], [dys[0]],
                               [True, BF16])
    g['s5_b_glu'] = dbglu.reshape(-1)
    g['s5_w_glu'] = _mm("s5_glu_dw", r['y1_bf'], dpre, ta=True, out_dtype=BF16)
    dy1 = _mm("s5_glu_dx", dpre, p['wglu'], tb=True, add=dy1)
    dy0, du, dd = _ew_bwd("s5_y1_bwd", _f_s5_y1, [_full(r['hc_re']), _full(r['hc_im']), (z, 0, BRANCH)], [p['s5_d']],
                          [dy1], [BF16, False, True])
    g['s5_d'] = dd.reshape(-1)
    dh_re = _mm("s5_hc_re_dx", dy0, p['s5_cre'], tb=True)
    dh_im = _mm("s5_hc_im_dx", dy0, p['s5_cimn'], tb=True)
    take_c = lambda d: _blockdiag_take(d, S5_GROUPS, S5_STATE, S5_GROUP).transpose(0, 2, 1)
    g['s5_c_re'] = take_c(_mm("s5_hc_re_dw", r['h_re'], dy0, ta=True))
    g['s5_c_im'] = -take_c(_mm("s5_hc_im_dw", r['h_im'], dy0, ta=True))
    gb_re, gb_im, dl_re, dl_im = _scan_cplx("s5_scan_bwd", p['lam_re'], -p['lam_im'], dh_re, dh_im, reverse=True,
                                            h_re=r['h_re'], h_im=r['h_im'])
    du = _mm("s5_bu_re_dx", gb_re, p['s5_bre'], tb=True, add=du)
    du = _mm("s5_bu_im_dx", gb_im, p['s5_bim'], tb=True, add=du, out_dtype=BF16)
    take_b = lambda d: _blockdiag_take(d, S5_GROUPS, S5_GROUP, S5_STATE).transpose(0, 2, 1).reshape(S5_N, S5_GROUP)
    dbb_re = take_b(_mm("s5_bu_re_dw", z, gb_re, ta=True, a_w=BRANCH))
    dbb_im = take_b(_mm("s5_bu_im_dw", z, gb_im, ta=True, a_w=BRANCH))
    disc = _ew_bwd("s5_disc_bwd", _f_s5_disc, [_full(a) for a in p['disc_in']], [],
                   [dl_re.reshape(S5_N, 1), dl_im.reshape(S5_N, 1), dbb_re, dbb_im], [True] * 5, tl=S5_N)
    grp = (S5_GROUPS, S5_STATE)
    g['s5_a_re'], g['s5_a_im'] = disc[0].reshape(grp), disc[1].reshape(grp)
    g['s5_log_dt'] = disc[2].reshape(grp).sum(axis=1)
    g['s5_b_re'], g['s5_b_im'] = disc[3].reshape(grp + (S5_GROUP,)), disc[4].reshape(grp + (S5_GROUP,))
    dz = jnp.concatenate([du, dxl, dgate] + dqkv + list(dzg) + [dzf], axis=1)
    dwc = _mm("z_in_dw", r['x_bf'], dz, ta=True, out_dtype=BF16)
    g['w_in'] = jnp.concatenate([dwc[:, :Z_MAIN], dwc[:, Z_FG0:Z_FG0 + FOX_HEADS], dwc[:, Z_GATE0:Z_FG0]], axis=1)
    dx = _mm("z_in_dx", dz, p['wc'], tb=True, add=dx_n)
    return dx, g


def _loss_head(y, target, tl=256):
    length, width = y.shape
    tl = min(tl, length)
    nt = length // tl

    def body(y_ref, t_ref, dy_ref, loss_ref, acc_sc):
        i = pl.program_id(0)

        @pl.when(i == 0)
        def _():
            acc_sc[...] = jnp.zeros_like(acc_sc)

        err = y_ref[...] - t_ref[...]
        dy_ref[...] = err / width
        acc_sc[...] += jnp.sum(jnp.square(err), axis=0, keepdims=True)

        @pl.when(i == nt - 1)
        def _():
            total = jnp.sum(acc_sc[...], axis=1, keepdims=True) * (0.5 / width)
            loss_ref[...] = jnp.broadcast_to(total, loss_ref.shape)

    spec = pl.BlockSpec((tl, width), lambda i: (i, 0))
    dy, loss = pl.pallas_call(
        body, grid=(nt,), in_specs=[spec, spec], out_specs=[spec, pl.BlockSpec((1, LANES), lambda i: (0, 0))],
        out_shape=[jax.ShapeDtypeStruct((length, width), F32), jax.ShapeDtypeStruct((1, LANES), F32)],
        scratch_shapes=[pltpu.VMEM((1, width), F32)], compiler_params=_params("arbitrary"), name="loss_head")(y, target)
    return loss[0, 0], dy


def _adamw(name, w, parts, m, v):
    rows, cols = w.shape
    tr = rows
    for cand in (256, 128, 64, 32, 16):
        if rows % cand == 0:
            tr = cand
            break

    def body(w_ref, p_ref, m_ref, v_ref, g_ref, d_ref, m2_ref, v2_ref):
        grad = p_ref[0].astype(F32)
        for dev in range(1, N_DEV):
            grad = grad + p_ref[dev].astype(F32)
        m2 = ADAM_B1 * m_ref[...] + (1.0 - ADAM_B1) * grad
        v2 = ADAM_B2 * v_ref[...] + (1.0 - ADAM_B2) * jnp.square(grad)
        m_hat = m2 / (1.0 - ADAM_B1 ** ADAM_STEP)
        v_hat = v2 / (1.0 - ADAM_B2 ** ADAM_STEP)
        g_ref[...] = grad
        d_ref[...] = -ADAM_LR * (m_hat / (jnp.sqrt(v_hat) + ADAM_EPS) + ADAM_WD * w_ref[...])
        m2_ref[...] = m2
        v2_ref[...] = v2

    spec = pl.BlockSpec((tr, cols), lambda i: (i, 0))
    pspec = pl.BlockSpec((N_DEV, tr, cols), lambda i: (0, i, 0))
    shape = jax.ShapeDtypeStruct((rows, cols), F32)
    return pl.pallas_call(body, grid=(rows // tr,), in_specs=[spec, pspec, spec, spec], out_specs=[spec] * 4,
                          out_shape=[shape] * 4, compiler_params=_params("parallel"), name=name)(w, parts, m, v)


def _forward_backward(x, target, layers):
    prepared, saved = [], []
    x_bf = x.astype(BF16)
    for w in layers:
        p = _prep_layer(w)
        x, x_bf, r = _layer_fwd(x, x_bf, p)
        prepared.append(p)
        saved.append(r)
    loss, dx = _loss_head(x, target)
    grads = [None] * len(layers)
    for l in reversed(range(len(layers))):
        dx, grads[l] = _layer_bwd(dx, saved[l], prepared[l])
    return loss, dx, grads


def _exchange(name, arrays, scatter):
    n = len(arrays)
    hbm = pl.BlockSpec(memory_space=pltpu.HBM)

    def body(*refs):
        ins, outs = refs[:n], refs[n:2 * n]
        send_sems, recv_sems, local_sems = refs[2 * n:]
        x, y, c = lax.axis_index("x"), lax.axis_index("y"), lax.axis_index("c")
        me = 4 * x + 2 * y + c

        def peer(k):
            px = 1 - x if k & 4 else x
            py = 1 - y if k & 2 else y
            pc = 1 - c if k & 1 else c
            return (px, py, pc), 4 * px + 2 * py + pc

        local = []
        for a in range(n):
            cp = pltpu.make_async_copy(ins[a].at[me] if scatter else ins[a], outs[a].at[me], local_sems.at[a])
            cp.start()
            local.append(cp)
        sends = []
        for k in range(1, N_DEV):
            dev, idx = peer(k)
            for a in range(n):
                s = a * (N_DEV - 1) + k - 1
                cp = pltpu.make_async_remote_copy(
                    src_ref=ins[a].at[idx] if scatter else ins[a], dst_ref=outs[a].at[me],
                    send_sem=send_sems.at[s], recv_sem=recv_sems.at[s], device_id=dev,
                    device_id_type=pl.DeviceIdType.MESH)
                cp.start()
                sends.append(cp)
        for cp in local:
            cp.wait()
        for cp in sends:
            cp.wait_send()
        for k in range(1, N_DEV):
            dev, idx = peer(k)
            for a in range(n):
                s = a * (N_DEV - 1) + k - 1
                pltpu.make_async_remote_copy(
                    src_ref=ins[a].at[idx] if scatter else ins[a], dst_ref=outs[a].at[idx],
                    send_sem=send_sems.at[s], recv_sem=recv_sems.at[s], device_id=dev,
                    device_id_type=pl.DeviceIdType.MESH).wait_recv()

    out_shape = [jax.ShapeDtypeStruct(a.shape if scatter else (N_DEV,) + a.shape, a.dtype) for a in arrays]
    nsem = n * (N_DEV - 1)
    return pl.pallas_call(
        body, in_specs=[hbm] * n, out_specs=[hbm] * n, out_shape=out_shape,
        scratch_shapes=[pltpu.SemaphoreType.DMA((nsem,)), pltpu.SemaphoreType.DMA((nsem,)),
                        pltpu.SemaphoreType.DMA((n,))],
        name=name)(*arrays)


def _shard_2d(a):
    return a.reshape(-1, a.shape[-1])


def _full_layer_weight(name, gathered, l):
    t = gathered[:, l]
    if name in ('s5_w_glu', 'w_out', 'w_ffn_down'):
        return t.reshape(-1, t.shape[-1])
    if name == 'w_branch':
        return t.transpose(1, 2, 0, 3).reshape(3, BRANCH, D_MODEL)
    return t.transpose(1, 0, 2).reshape(t.shape[1], -1)


def _split_layer_grad(name, g):
    if name in ('s5_w_glu', 'w_out', 'w_ffn_down'):
        return g.reshape(N_DEV, g.shape[0] // N_DEV, g.shape[1])
    if name == 'w_branch':
        return g.reshape(3, BRANCH, N_DEV, D_MODEL // N_DEV).transpose(2, 0, 1, 3)
    return g.reshape(g.shape[0], N_DEV, g.shape[1] // N_DEV).transpose(1, 0, 2)


def _pack(arrays, rows):
    flat = jnp.concatenate([a.reshape(-1).astype(F32) for a in arrays])
    return jnp.pad(flat, (0, rows * LANES - flat.shape[0])).reshape(rows, LANES)


def _unpack(packed, shapes):
    flat, out, at = packed.reshape(-1), [], 0
    for s in shapes:
        size = math.prod(s)
        out.append(flat[at:at + size].reshape(s))
        at += size
    return out


def kernel(x, w_in, b_f, b_gate, s5_a_re, s5_a_im, s5_log_dt, s5_b_re, s5_b_im, s5_c_re, s5_c_im, s5_d, s5_w_glu, s5_b_glu, lru_conv_w, lru_conv_b, lru_w_a, lru_b_a, lru_w_x, lru_b_x, lru_lambda, w_branch, w_out, ln1_g, ln1_b, w_ffn_gate, w_ffn_up, w_ffn_down, ln2_g, ln2_b, loss_target, m_w_in, m_b_f, m_b_gate, m_s5_a_re, m_s5_a_im, m_s5_log_dt, m_s5_b_re, m_s5_b_im, m_s5_c_re, m_s5_c_im, m_s5_d, m_s5_w_glu, m_s5_b_glu, m_lru_conv_w, m_lru_conv_b, m_lru_w_a, m_lru_b_a, m_lru_w_x, m_lru_b_x, m_lru_lambda, m_w_branch, m_w_out, m_ln1_g, m_ln1_b, m_w_ffn_gate, m_w_ffn_up, m_w_ffn_down, m_ln2_g, m_ln2_b, v_w_in, v_b_f, v_b_gate, v_s5_a_re, v_s5_a_im, v_s5_log_dt, v_s5_b_re, v_s5_b_im, v_s5_c_re, v_s5_c_im, v_s5_d, v_s5_w_glu, v_s5_b_glu, v_lru_conv_w, v_lru_conv_b, v_lru_w_a, v_lru_b_a, v_lru_w_x, v_lru_b_x, v_lru_lambda, v_w_branch, v_w_out, v_ln1_g, v_ln1_b, v_w_ffn_gate, v_w_ffn_up, v_w_ffn_down, v_ln2_g, v_ln2_b):
    given = dict(locals())
    weights = {n: given[n] for n in WEIGHTS}
    moments_m = {n: given['m_' + n] for n in WEIGHTS}
    moments_v = {n: given['v_' + n] for n in WEIGHTS}

    wire = {n: (F32 if n == 'lru_conv_w' else BF16) for n in SHARDED}
    gathered = _exchange("gather_weights", [_shard_2d(weights[n]).astype(wire[n]) for n in SHARDED], scatter=False)
    gathered = {n: g.reshape((N_DEV,) + weights[n].shape) for n, g in zip(SHARDED, gathered)}
    layers = []
    for l in range(DEPTH):
        w = {n: weights[n][l] for n in REPLICATED}
        w.update({n: _full_layer_weight(n, gathered[n], l) for n in SHARDED})
        layers.append(w)

    loss_local, dx, grads = _forward_backward(x[0], loss_target[0], layers)
    loss = lax.psum(loss_local, MESH_AXES)

    outgoing = []
    for n in SHARDED:
        per_layer = jnp.stack([_split_layer_grad(n, grads[l][n]) for l in range(DEPTH)], axis=1)
        outgoing.append(per_layer.reshape((N_DEV,) + _shard_2d(weights[n]).shape).astype(wire[n]))
    incoming = _exchange("scatter_grads", outgoing, scatter=True)
    new = {}
    for n, parts in zip(SHARDED, incoming):
        res = _adamw("adamw_" + n, _shard_2d(weights[n]), parts, _shard_2d(moments_m[n]), _shard_2d(moments_v[n]))
        new[n] = [t.reshape(weights[n].shape) for t in res]

    shapes = [weights[n].shape for n in REPLICATED]
    total = sum(math.prod(s) for s in shapes)
    rows = -(-total // (LANES * 256)) * 256
    partial = _pack([jnp.stack([grads[l][n] for l in range(DEPTH)]) for n in REPLICATED], rows)
    (parts,) = _exchange("gather_small_grads", [partial], scatter=False)
    res = _adamw("adamw_replicated", _pack([weights[n] for n in REPLICATED], rows), parts,
                 _pack([moments_m[n] for n in REPLICATED], rows), _pack([moments_v[n] for n in REPLICATED], rows))
    for n, vals in zip(REPLICATED, zip(*[_unpack(t, shapes) for t in res])):
        new[n] = list(vals)

    return (loss, dx[None], *[new[n][0] for n in WEIGHTS], *[new[n][1] for n in WEIGHTS],
            *[new[n][2] for n in WEIGHTS], *[new[n][3] for n in WEIGHTS])
```

```python
import functools
import math

import jax
import jax.numpy as jnp
from jax import lax
from jax.experimental import pallas as pl
from jax.experimental.pallas import tpu as pltpu

F32 = jnp.float32
BF16 = jnp.bfloat16

D_MODEL = 1024
DEPTH = 4
BRANCH = 512
S5_GROUPS, S5_GROUP, S5_STATE = 32, 16, 64
S5_N = S5_GROUPS * S5_STATE
LRU_HEADS, LRU_HEAD_DIM = 8, 64
LRU_C = 8.0
CONV_WIDTH = 4
FOX_HEADS, FOX_HEAD_DIM = 8, 64
FFN_HIDDEN = 2816
ALPHA = (2.0 * DEPTH) ** 0.25
LN_EPS = 1e-5
IN_TOTAL = 6 * BRANCH + FOX_HEADS + 3 * D_MODEL
FG_PAD = 128
Z_MAIN = 6 * BRANCH
Z_GATE0 = Z_MAIN
Z_FG0 = Z_MAIN + 3 * D_MODEL
Z_TOTAL = Z_FG0 + FG_PAD
N_DEV = 8
MESH_AXES = ("x", "y", "c")

ADAM_LR, ADAM_B1, ADAM_B2, ADAM_EPS, ADAM_WD, ADAM_STEP = 0.001, 0.9, 0.999, 1e-08, 0.01, 10

VMEM_LIMIT_BYTES = 48 * 1024 * 1024
SUBLANES = 8
LANES = 128
NEG_BIG = -1e30

WEIGHTS = ['w_in', 'b_f', 'b_gate', 's5_a_re', 's5_a_im', 's5_log_dt', 's5_b_re', 's5_b_im', 's5_c_re', 's5_c_im',
           's5_d', 's5_w_glu', 's5_b_glu', 'lru_conv_w', 'lru_conv_b', 'lru_w_a', 'lru_b_a', 'lru_w_x', 'lru_b_x',
           'lru_lambda', 'w_branch', 'w_out', 'ln1_g', 'ln1_b', 'w_ffn_gate', 'w_ffn_up', 'w_ffn_down', 'ln2_g',
           'ln2_b']
SHARDED = ['w_in', 's5_w_glu', 'lru_conv_w', 'w_branch', 'w_out', 'w_ffn_gate', 'w_ffn_up', 'w_ffn_down']
REPLICATED = [n for n in WEIGHTS if n not in SHARDED]


def _params(*sem):
    return pltpu.CompilerParams(dimension_semantics=sem, vmem_limit_bytes=VMEM_LIMIT_BYTES)


def _tile(dim, want):
    if dim % LANES:
        return dim
    t = min(want, dim) // LANES * LANES
    while dim % t:
        t -= LANES
    return t


MM_VMEM_BUDGET_BYTES = 30 * 1024 * 1024
MM_MAX_TILE = 1024


def _divisor_tiles(dim, cap, must_divide=0):
    if dim % LANES:
        return [dim]
    out = [t for t in range(min(cap, dim) // LANES * LANES, 0, -LANES) if dim % t == 0 and must_divide % t == 0]
    return out or [dim]


def _mm_tiles(m, n, k, a_bytes, b_bytes, o_bytes, has_add, m_c0, n_c0, k_c0):
    for tk in _divisor_tiles(k, k, k_c0):
        best = None
        for tm in _divisor_tiles(m, MM_MAX_TILE, m_c0):
            for tn in _divisor_tiles(n, MM_MAX_TILE, n_c0):
                used = 2 * (tm * tk * a_bytes + tk * tn * b_bytes + tm * tn * o_bytes) + tm * tn * 4
                used += tm * tn * 4 if tk < k else 0
                used += 2 * tm * tn * 4 if has_add else 0
                if used <= MM_VMEM_BUDGET_BYTES and (best is None or tm * tn / (tm + tn) > best[0]):
                    best = (tm * tn / (tm + tn), tm, tn)
        if best is not None and (min(best[1], best[2]) >= 256 or tk <= 512):
            return best[1], best[2], tk
    raise ValueError("no matmul tiling fits VMEM")


def _mm(name, a, b, *, ta=False, tb=False, a_c0=0, a_w=None, b_c0=0, b_w=None, add=None, out_dtype=F32):
    a_w = a.shape[1] if a_w is None else a_w
    b_w = b.shape[1] if b_w is None else b_w
    m, k = (a_w, a.shape[0]) if ta else (a.shape[0], a_w)
    n = b.shape[0] if tb else b_w
    assert k == (b_w if tb else b.shape[0]), (name, a.shape, b.shape)
    tm, tn, tk = _mm_tiles(m, n, k, a.dtype.itemsize, b.dtype.itemsize, jnp.dtype(out_dtype).itemsize,
                           add is not None, a_c0 if ta else 0, 0 if tb else b_c0,
                           math.gcd(0 if ta else a_c0, b_c0 if tb else 0))
    nk = k // tk
    a_off = a_c0 // (tm if ta else tk)
    b_off = b_c0 // (tk if tb else tn)
    assert a_c0 % (tm if ta else tk) == 0 and b_c0 % (tk if tb else tn) == 0, name
    dims = (((0 if ta else 1,), (1 if tb else 0,)), ((), ()))
    a_total, b_total = m * k * a.dtype.itemsize, n * k * b.dtype.itemsize
    a_stays = a_total + b_total * (m // tm) <= b_total + a_total * (n // tn)
    if nk > 1:
        a_stays = True

    def mn(o, i):
        return (o, i) if a_stays else (i, o)

    def body(*refs):
        a_ref, b_ref = refs[0], refs[1]
        add_ref = refs[2] if add is not None else None
        o_ref = refs[3] if add is not None else refs[2]
        part = lax.dot_general(a_ref[...].astype(BF16), b_ref[...].astype(BF16), dims, preferred_element_type=F32)

        def finish(r):
            if add is not None:
                r = r + add_ref[...]
            o_ref[...] = r.astype(o_ref.dtype)

        if nk == 1:
            finish(part)
            return
        acc_ref = refs[-1]
        kk = pl.program_id(2)

        @pl.when(kk == 0)
        def _():
            acc_ref[...] = part

        @pl.when(kk > 0)
        def _():
            acc_ref[...] += part

        @pl.when(kk == nk - 1)
        def _():
            finish(acc_ref[...])

    def a_map(o, i, kk):
        im = mn(o, i)[0]
        return (kk, im + a_off) if ta else (im, kk + a_off)

    def b_map(o, i, kk):
        jn = mn(o, i)[1]
        return (jn, kk + b_off) if tb else (kk, jn + b_off)

    a_spec = pl.BlockSpec((tk, tm) if ta else (tm, tk), a_map)
    b_spec = pl.BlockSpec((tn, tk) if tb else (tk, tn), b_map)
    o_spec = pl.BlockSpec((tm, tn), lambda o, i, kk: mn(o, i))
    ins, in_specs = [a, b], [a_spec, b_spec]
    if add is not None:
        ins.append(add)
        in_specs.append(o_spec)
    grid = (m // tm, n // tn, nk) if a_stays else (n // tn, m // tm, nk)
    return pl.pallas_call(
        body, grid=grid, in_specs=in_specs, out_specs=o_spec, out_shape=jax.ShapeDtypeStruct((m, n), out_dtype),
        scratch_shapes=[pltpu.VMEM((tm, tn), F32)] if nk > 1 else [],
        compiler_params=_params("parallel", "parallel", "arbitrary"), name=name)(*ins)


def _row_spec(tl, c0, w):
    assert c0 % w == 0
    return pl.BlockSpec((tl, w), lambda i: (i, c0 // w))


def _whole_spec(p):
    return pl.BlockSpec(p.shape, lambda i: (0,) * p.ndim)


def _ew(name, f, rows, prm, out_ws, tl=256, out_dtypes=None):
    out_dtypes = out_dtypes or [F32] * len(out_ws)
    nrows, nprm = len(rows), len(prm)
    length = rows[0][0].shape[0]
    tl = min(tl, length)

    def body(*refs):
        vals = [r[...] for r in refs[:nrows + nprm]]
        outs = f(*vals)
        for o_ref, o in zip(refs[nrows + nprm:], outs):
            o_ref[...] = o.astype(o_ref.dtype)

    return pl.pallas_call(
        body, grid=(length // tl,),
        in_specs=[_row_spec(tl, c0, w) for (_, c0, w) in rows] + [_whole_spec(p) for p in prm],
        out_specs=[_row_spec(tl, 0, w) for w in out_ws],
        out_shape=[jax.ShapeDtypeStruct((length, w), dt) for w, dt in zip(out_ws, out_dtypes)],
        compiler_params=_params("parallel"), name=name)(*[r[0] for r in rows], *prm)


def _ew_bwd(name, f, rows, prm, douts, row_grad, tl=256):
    nrows, nprm, nd = len(rows), len(prm), len(douts)
    length = rows[0][0].shape[0]
    tl = min(tl, length)
    want = [i for i in range(nrows) if row_grad[i]]

    def body(*refs):
        vals = [r[...] for r in refs[:nrows + nprm]]
        cts = tuple(r[...] for r in refs[nrows + nprm:nrows + nprm + nd])
        out_refs = refs[nrows + nprm + nd:]
        _, vjp = jax.vjp(lambda *v: tuple(f(*v)), *vals)
        grads = vjp(cts)
        for o_ref, i in zip(out_refs[:len(want)], want):
            o_ref[...] = grads[i].astype(o_ref.dtype)

        @pl.when(pl.program_id(0) == 0)
        def _():
            for o_ref in out_refs[len(want):]:
                o_ref[...] = jnp.zeros_like(o_ref)

        for o_ref, g in zip(out_refs[len(want):], grads[nrows:]):
            o_ref[...] += g

    return pl.pallas_call(
        body, grid=(length // tl,),
        in_specs=([_row_spec(tl, c0, w) for (_, c0, w) in rows] + [_whole_spec(p) for p in prm]
                  + [_row_spec(tl, 0, d.shape[1]) for d in douts]),
        out_specs=[_row_spec(tl, 0, rows[i][2]) for i in want] + [_whole_spec(p) for p in prm],
        out_shape=([jax.ShapeDtypeStruct((length, rows[i][2]), F32 if row_grad[i] is True else row_grad[i])
                    for i in want]
                   + [jax.ShapeDtypeStruct(p.shape, F32) for p in prm]),
        compiler_params=_params("arbitrary"), name=name)(*[r[0] for r in rows], *prm, *douts)


def _row_ids(width):
    return lax.broadcasted_iota(jnp.int32, (SUBLANES, width), 0)


def _shift_rows(v, d, reverse):
    return pltpu.roll(v, (SUBLANES - d) if reverse else d, 0)


def _scan_real(name, a, b, b2=None, *, reverse=False, bn=256):
    length, n = a.shape
    bn = _tile(n, bn)
    nb = length // SUBLANES
    operands = [a, b] if b2 is None else [a, b, b2]

    def body(*refs):
        a_ref, b_ref, h_ref = refs[0], refs[1], refs[-1]
        rows = _row_ids(bn)

        def step(it, carry):
            i = (nb - 1 - it) if reverse else it
            sl = pl.ds(pl.multiple_of(i * SUBLANES, SUBLANES), SUBLANES)
            av, bv = a_ref[sl, :], b_ref[sl, :]
            if b2 is not None:
                bv = bv + refs[2][sl, :]
            for d in (1, 2, 4):
                live = (rows < SUBLANES - d) if reverse else (rows >= d)
                a_in = jnp.where(live, _shift_rows(av, d, reverse), 1.0)
                b_in = jnp.where(live, _shift_rows(bv, d, reverse), 0.0)
                bv = bv + av * b_in
                av = av * a_in
            hv = bv + av * carry
            h_ref[sl, :] = hv
            edge = hv[0:1, :] if reverse else hv[SUBLANES - 1:SUBLANES, :]
            return jnp.broadcast_to(edge, (SUBLANES, bn))

        lax.fori_loop(0, nb, step, jnp.zeros((SUBLANES, bn), F32))

    spec = pl.BlockSpec((length, bn), lambda j: (0, j))
    return pl.pallas_call(body, grid=(n // bn,), in_specs=[spec] * len(operands), out_specs=spec,
                          out_shape=jax.ShapeDtypeStruct((length, n), F32),
                          compiler_params=_params("parallel"), name=name)(*operands)


def _cmul(ar, ai, br, bi):
    return ar * br - ai * bi, ar * bi + ai * br


def _scan_cplx(name, lam_re, lam_im, x_re, x_im, *, reverse=False, h_re=None, h_im=None, bn=128):
    length, n = x_re.shape
    bn = _tile(n, bn)
    nb = length // SUBLANES
    with_dot = h_re is not None

    def body(*refs):
        if with_dot:
            lr_ref, li_ref, xr_ref, xi_ref, hr_ref, hi_ref, gr_ref, gi_ref, dr_ref, di_ref = refs
        else:
            lr_ref, li_ref, xr_ref, xi_ref, gr_ref, gi_ref = refs
        rows = _row_ids(bn)
        lr = jnp.broadcast_to(lr_ref[...], (SUBLANES, bn))
        li = jnp.broadcast_to(li_ref[...], (SUBLANES, bn))
        powers = [(lr, li)]
        for _ in range(SUBLANES - 1):
            powers.append(_cmul(powers[-1][0], powers[-1][1], lr, li))
        zero = jnp.zeros((SUBLANES, bn), F32)
        steps = []
        for d in (1, 2, 4):
            live = (rows < SUBLANES - d) if reverse else (rows >= d)
            steps.append((d, jnp.where(live, powers[d - 1][0], 0.0), jnp.where(live, powers[d - 1][1], 0.0)))
        cr, ci = zero, zero
        for r in range(SUBLANES):
            e = (SUBLANES - r) if reverse else (r + 1)
            cr = jnp.where(rows == r, powers[e - 1][0], cr)
            ci = jnp.where(rows == r, powers[e - 1][1], ci)

        def step(it, carry):
            i = (nb - 1 - it) if reverse else it
            sl = pl.ds(pl.multiple_of(i * SUBLANES, SUBLANES), SUBLANES)
            vr, vi = xr_ref[sl, :], xi_ref[sl, :]
            for d, pr, pi in steps:
                sr, si = _cmul(pr, pi, _shift_rows(vr, d, reverse), _shift_rows(vi, d, reverse))
                vr, vi = vr + sr, vi + si
            kr, ki = _cmul(cr, ci, carry[0], carry[1])
            vr, vi = vr + kr, vi + ki
            gr_ref[sl, :] = vr
            gi_ref[sl, :] = vi
            er = vr[0:1, :] if reverse else vr[SUBLANES - 1:SUBLANES, :]
            ei = vi[0:1, :] if reverse else vi[SUBLANES - 1:SUBLANES, :]
            new = (jnp.broadcast_to(er, (SUBLANES, bn)), jnp.broadcast_to(ei, (SUBLANES, bn)))
            if not with_dot:
                return new
            prev = pl.ds(pl.multiple_of(jnp.maximum(i - 1, 0) * SUBLANES, SUBLANES), SUBLANES)
            keep = jnp.where(i > 0, 1.0, 0.0)
            pr_ = jnp.broadcast_to(hr_ref[prev, :][SUBLANES - 1:SUBLANES, :], (SUBLANES, bn)) * keep
            pi_ = jnp.broadcast_to(hi_ref[prev, :][SUBLANES - 1:SUBLANES, :], (SUBLANES, bn)) * keep
            hr = jnp.where(rows == 0, pr_, pltpu.roll(hr_ref[sl, :], 1, 0))
            hi = jnp.where(rows == 0, pi_, pltpu.roll(hi_ref[sl, :], 1, 0))
            return new + (carry[2] + vr * hr + vi * hi, carry[3] + vi * hr - vr * hi)

        init = (zero, zero, zero, zero) if with_dot else (zero, zero)
        out = lax.fori_loop(0, nb, step, init)
        if with_dot:
            dr_ref[...] = jnp.sum(out[2], axis=0, keepdims=True)
            di_ref[...] = jnp.sum(out[3], axis=0, keepdims=True)

    col = pl.BlockSpec((length, bn), lambda j: (0, j))
    vec = pl.BlockSpec((1, bn), lambda j: (0, j))
    ins = [lam_re, lam_im, x_re, x_im] + ([h_re, h_im] if with_dot else [])
    in_specs = [vec, vec, col, col] + ([col, col] if with_dot else [])
    out_specs = [col, col] + ([vec, vec] if with_dot else [])
    full = jax.ShapeDtypeStruct((length, n), F32)
    row = jax.ShapeDtypeStruct((1, n), F32)
    out_shape = [full, full] + ([row, row] if with_dot else [])
    return pl.pallas_call(body, grid=(n // bn,), in_specs=in_specs, out_specs=out_specs, out_shape=out_shape,
                          compiler_params=_params("parallel"), name=name)(*ins)


FOX_SCALE = FOX_HEAD_DIM ** -0.5
FOX_AUG = 128
FOX_CQ0 = FOX_HEAD_DIM
FOX_CK0 = FOX_HEAD_DIM + 3
NT = (((1,), (1,)), ((), ()))


def _fox_logits_t(ka, qa, on_diagonal):
    st = lax.dot_general(ka, qa, NT, preferred_element_type=F32)
    if on_diagonal:
        key = lax.broadcasted_iota(jnp.int32, st.shape, 0)
        query = lax.broadcasted_iota(jnp.int32, st.shape, 1)
        st = jnp.where(key <= query, st, NEG_BIG)
    return st


def _fox_specs(t, q_first):
    def q_idx(h, a, b):
        i, j = (a, b) if q_first else (b, a)
        return i if q_first else jnp.maximum(i, j)

    def k_idx(h, a, b):
        i, j = (a, b) if q_first else (b, a)
        return jnp.minimum(i, j) if q_first else j

    rows = lambda idx, w: pl.BlockSpec((None, t, w), lambda h, a, b: (h, idx(h, a, b), 0))
    cols = lambda idx, w: pl.BlockSpec((None, w, t), lambda h, a, b: (h, 0, idx(h, a, b)))
    return rows, cols, q_idx, k_idx


def _fox_fwd(qa, ka, vt, t=512, rider=None):
    heads, length, _ = qa.shape
    dh = vt.shape[1]
    t = min(t, length)
    nt = length // t

    def body(in_refs, out_refs, scratch_refs):
        (qa_ref, ka_ref, vt_ref), (o_ref, lse_ref), (m_sc, l_sc, acc_sc) = in_refs, out_refs, scratch_refs
        qi, ki = pl.program_id(1), pl.program_id(2)

        @pl.when(ki == 0)
        def _():
            m_sc[...] = jnp.full_like(m_sc, NEG_BIG)
            l_sc[...] = jnp.zeros_like(l_sc)
            acc_sc[...] = jnp.zeros_like(acc_sc)

        def step(on_diagonal):
            st = _fox_logits_t(ka_ref[...], qa_ref[...], on_diagonal)
            m_old = m_sc[...]
            m_new = jnp.maximum(m_old, jnp.max(st, axis=0, keepdims=True))
            pt = jnp.exp(st - m_new)
            scale = jnp.exp(m_old - m_new)
            l_sc[...] = scale * l_sc[...] + jnp.sum(pt, axis=0, keepdims=True)
            acc_sc[...] = scale * acc_sc[...] + jnp.dot(vt_ref[...], pt.astype(BF16), preferred_element_type=F32)
            m_sc[...] = m_new

        pl.when(ki < qi)(functools.partial(step, False))
        pl.when(ki == qi)(functools.partial(step, True))

        @pl.when(ki == nt - 1)
        def _():
            o_ref[...] = acc_sc[...] / l_sc[...]
            lse_ref[...] = m_sc[...] + jnp.log(l_sc[...])

    rows, cols, q_idx, k_idx = _fox_specs(t, True)
    (ot, lse), carried = _call_carrying(
        "fox_fwd" if rider is None else "fox_fwd_carrying", body, (heads, nt, nt), [qa, ka, vt],
        [rows(q_idx, FOX_AUG), rows(k_idx, FOX_AUG), cols(k_idx, dh)],
        [jax.ShapeDtypeStruct((heads, dh, length), F32), jax.ShapeDtypeStruct((heads, 1, length), F32)],
        [cols(q_idx, dh), cols(q_idx, 1)],
        [pltpu.VMEM((1, t), F32), pltpu.VMEM((1, t), F32), pltpu.VMEM((dh, t), F32)],
        ("parallel", "parallel", "arbitrary"), rider)
    return ot, lse, carried


def _fox_ds_t(qa_ref, ka_ref, v_ref, dot_ref, ot_ref, lse_ref, on_diagonal):
    pt = jnp.exp(_fox_logits_t(ka_ref[...], qa_ref[...], on_diagonal) - lse_ref[...])
    dpt = jnp.dot(v_ref[...], dot_ref[...], preferred_element_type=F32)
    delta = jnp.sum(dot_ref[...].astype(F32) * ot_ref[...], axis=0, keepdims=True)
    return pt, pt * (dpt - delta)


def _fox_bwd_q(qa, ka, kat, v, dot, ot, lse, t=512, rider=None):
    heads, length, _ = qa.shape
    dh = v.shape[2]
    t = min(t, length)
    nt = length // t

    def body(in_refs, out_refs, scratch_refs):
        qa_ref, ka_ref, kat_ref, v_ref, dot_ref, ot_ref, lse_ref = in_refs
        (dqa_ref,), (acc_sc,) = out_refs, scratch_refs
        qi, ki = pl.program_id(1), pl.program_id(2)

        @pl.when(ki == 0)
        def _():
            acc_sc[...] = jnp.zeros_like(acc_sc)

        def step(on_diagonal):
            _, dst = _fox_ds_t(qa_ref, ka_ref, v_ref, dot_ref, ot_ref, lse_ref, on_diagonal)
            acc_sc[...] += jnp.dot(kat_ref[...], dst.astype(BF16), preferred_element_type=F32)

        pl.when(ki < qi)(functools.partial(step, False))
        pl.when(ki == qi)(functools.partial(step, True))

        @pl.when(ki == nt - 1)
        def _():
            dqa_ref[...] = acc_sc[...]

    rows, cols, q_idx, k_idx = _fox_specs(t, True)
    (dqat,), carried = _call_carrying(
        "fox_bwd_q" if rider is None else "fox_bwd_q_carrying", body, (heads, nt, nt), [qa, ka, kat, v, dot, ot, lse],
        [rows(q_idx, FOX_AUG), rows(k_idx, FOX_AUG), cols(k_idx, FOX_AUG), rows(k_idx, dh), cols(q_idx, dh),
         cols(q_idx, dh), cols(q_idx, 1)],
        [jax.ShapeDtypeStruct((heads, FOX_AUG, length), F32)], [cols(q_idx, FOX_AUG)],
        [pltpu.VMEM((FOX_AUG, t), F32)], ("parallel", "parallel", "arbitrary"), rider)
    return dqat, carried


def _fox_bwd_kv(qa, ka, v, do, dot, ot, lse, t=512, rider=None):
    heads, length, _ = qa.shape
    dh = v.shape[2]
    t = min(t, length)
    nt = length // t

    def body(in_refs, out_refs, scratch_refs):
        qa_ref, ka_ref, v_ref, do_ref, dot_ref, ot_ref, lse_ref = in_refs
        (dk_ref, dv_ref, dc_ref), (dka_sc, dv_sc) = out_refs, scratch_refs
        ki, qi = pl.program_id(1), pl.program_id(2)

        @pl.when(qi == 0)
        def _():
            dka_sc[...] = jnp.zeros_like(dka_sc)
            dv_sc[...] = jnp.zeros_like(dv_sc)

        def step(on_diagonal):
            pt, dst = _fox_ds_t(qa_ref, ka_ref, v_ref, dot_ref, ot_ref, lse_ref, on_diagonal)
            dv_sc[...] += jnp.dot(pt.astype(BF16), do_ref[...], preferred_element_type=F32)
            dka_sc[...] += jnp.dot(dst.astype(BF16), qa_ref[...], preferred_element_type=F32)

        pl.when(qi > ki)(functools.partial(step, False))
        pl.when(qi == ki)(functools.partial(step, True))

        @pl.when(qi == nt - 1)
        def _():
            dka = dka_sc[...]
            lane = lax.broadcasted_iota(jnp.int32, dka.shape, 1)
            dk_ref[...] = dka_sc[:, :dh]
            dc_ref[...] = jnp.sum(jnp.where(lane == FOX_CK0, dka, 0.0), axis=1, keepdims=True)
            dv_ref[...] = dv_sc[...]

    rows, cols, q_idx, k_idx = _fox_specs(t, False)
    big = jax.ShapeDtypeStruct((heads, length, dh), F32)
    (dk, dv, dc), carried = _call_carrying(
        "fox_bwd_kv" if rider is None else "fox_bwd_kv_carrying", body, (heads, nt, nt), [qa, ka, v, do, dot, ot, lse],
        [rows(q_idx, FOX_AUG), rows(k_idx, FOX_AUG), rows(k_idx, dh), rows(q_idx, dh), cols(q_idx, dh),
         cols(q_idx, dh), cols(q_idx, 1)],
        [big, big, jax.ShapeDtypeStruct((heads, length, 1), F32)], [rows(k_idx, dh), rows(k_idx, dh), rows(k_idx, 1)],
        [pltpu.VMEM((t, FOX_AUG), F32), pltpu.VMEM((t, dh), F32)], ("parallel", "parallel", "arbitrary"), rider)
    return dk, dv, dc, carried


def _split3(x):
    hi = lax.reduce_precision(x, 8, 7)
    mid = lax.reduce_precision(x - hi, 8, 7)
    return [hi, mid, lax.reduce_precision(x - hi - mid, 8, 7)]


def _fox_operands(z, cum):
    length = z.shape[0]
    parts = jnp.stack(_split3(cum[:, :FOX_HEADS].T), axis=-1)
    ones = jnp.ones_like(parts)
    pad = jnp.zeros((FOX_HEADS, length, FOX_AUG - FOX_HEAD_DIM - 6), F32)
    qa = jnp.concatenate([_heads(z, 3 * BRANCH) * FOX_SCALE, parts, ones, pad], axis=-1).astype(BF16)
    ka = jnp.concatenate([_heads(z, 4 * BRANCH), ones, -parts, pad], axis=-1).astype(BF16)
    v = _heads(z, 5 * BRANCH).astype(BF16)
    return qa, ka, ka.transpose(0, 2, 1), v, v.transpose(0, 2, 1)


def _softplus(x):
    return jnp.maximum(x, 0.0) + jnp.log1p(jnp.exp(-jnp.abs(x)))


def _f_s5_disc(a_re, a_im, log_dt, b_re, b_im):
    dt = jnp.exp(log_dt)
    mag = jnp.exp(a_re * dt)
    lr, li = mag * jnp.cos(a_im * dt), mag * jnp.sin(a_im * dt)
    den = a_re * a_re + a_im * a_im
    qr = ((lr - 1.0) * a_re + li * a_im) / den
    qi = (li * a_re - (lr - 1.0) * a_im) / den
    return lr, li, qr * b_re - qi * b_im, qr * b_im + qi * b_re


def _f_s5_y1(hc_re, hc_im, u, d):
    return (jax.nn.gelu(hc_re + hc_im + d * u),)


def _f_s5_glu(y1, pre, b):
    return (y1 * jax.nn.sigmoid(pre + b),)


def _f_conv(x0, x1, x2, x3, w0, w1, w2, w3, b):
    return (b + w0 * x0 + w1 * x1 + w2 * x2 + w3 * x3,)


def _f_conv_t(d0, d1, d2, d3, w0, w1, w2, w3):
    return (w0 * d0 + w1 * d1 + w2 * d2 + w3 * d3,)


def _lru_coeffs(xc, pa, px, b_a, b_x, lam):
    r = jax.nn.sigmoid(pa + b_a)
    i = jax.nn.sigmoid(px + b_x)
    log_a = -LRU_C * _softplus(-lam) * r
    a = jnp.exp(log_a)
    mult = jnp.sqrt(-jnp.tanh(log_a) * (a * a + 1.0))
    return a, mult * (i * xc)


def _f_lru_gates(xc, pa, px, b_a, b_x, lam):
    return _lru_coeffs(xc, pa, px, b_a, b_x, lam)


def _f_lru_step(xc, pa, px, h_prev, b_a, b_x, lam):
    a, b = _lru_coeffs(xc, pa, px, b_a, b_x, lam)
    return (a * h_prev + b,)


def _f_lru_out(gate, h):
    return (jax.nn.gelu(gate) * h,)


def _f_logf(zf, bf):
    return (-_softplus(-(zf + bf)),)


def _f_merge(p0, p1, p2, z0, z1, z2, b0, b1, b2):
    return (jax.nn.sigmoid(z0 + b0) * p0 + jax.nn.sigmoid(z1 + b1) * p1 + jax.nn.sigmoid(z2 + b2) * p2,)


def _f_ln(x, r, g, b):
    s = ALPHA * x + r
    mu = jnp.mean(s, axis=-1, keepdims=True)
    var = jnp.mean(jnp.square(s - mu), axis=-1, keepdims=True)
    return ((s - mu) * lax.rsqrt(var + LN_EPS) * g + b,)


def _f_swiglu(hg, hu):
    return (jax.nn.silu(hg) * hu,)


def _full(a):
    return (a, 0, a.shape[1])


def _blockdiag(t):
    g, a, b = t.shape
    eye = jnp.eye(g, dtype=t.dtype)
    return (t[:, :, None, :] * eye[:, None, :, None]).reshape(g * a, g * b)


def _blockdiag_take(d, g, a, b):
    eye = jnp.eye(g, dtype=d.dtype)
    return (d.reshape(g, a, g, b) * eye[:, None, :, None]).sum(axis=2)


def _delay(a, j):
    return a if j == 0 else jnp.pad(a, ((j, 0), (0, 0)))[:a.shape[0]]


def _advance(a, j):
    return a if j == 0 else jnp.pad(a, ((0, j), (0, 0)))[j:]


def _heads(a, c0=0):
    length = a.shape[0]
    return a[:, c0:c0 + BRANCH].reshape(length, FOX_HEADS, FOX_HEAD_DIM).transpose(1, 0, 2)


def _unheads(a):
    return a.transpose(1, 0, 2).reshape(a.shape[1], BRANCH)


def _row(v):
    return v.reshape(1, -1).astype(F32)


def _col(v):
    return v.reshape(-1, 1).astype(F32)


def _prep_layer(w):
    p = {}
    w_in = w['w_in']
    p['wc'] = jnp.concatenate(
        [w_in[:, :Z_MAIN], w_in[:, Z_MAIN + FOX_HEADS:], w_in[:, Z_MAIN:Z_MAIN + FOX_HEADS],
         jnp.zeros((D_MODEL, FG_PAD - FOX_HEADS), w_in.dtype)], axis=1)
    p['b_f'] = jnp.pad(_row(w['b_f']), ((0, 0), (0, FG_PAD - FOX_HEADS)))
    p['b_gate'] = [_row(w['b_gate'][k * D_MODEL:(k + 1) * D_MODEL]) for k in range(3)]
    p['disc_in'] = [_col(w['s5_a_re']), _col(w['s5_a_im']), _col(jnp.repeat(w['s5_log_dt'], S5_STATE)),
                    w['s5_b_re'].reshape(S5_N, S5_GROUP), w['s5_b_im'].reshape(S5_N, S5_GROUP)]
    lam_re, lam_im, bb_re, bb_im = _ew("s5_disc", _f_s5_disc, [_full(a) for a in p['disc_in']], [],
                                       [1, 1, S5_GROUP, S5_GROUP], tl=S5_N)
    p['lam_re'], p['lam_im'] = lam_re.reshape(1, S5_N), lam_im.reshape(1, S5_N)
    to_blk = lambda t: _blockdiag(t.reshape(S5_GROUPS, S5_STATE, S5_GROUP).transpose(0, 2, 1)).astype(BF16)
    p['s5_bre'], p['s5_bim'] = to_blk(bb_re), to_blk(bb_im)
    p['s5_cre'] = _blockdiag(w['s5_c_re'].transpose(0, 2, 1)).astype(BF16)
    p['s5_cimn'] = _blockdiag(-w['s5_c_im'].transpose(0, 2, 1)).astype(BF16)
    p['s5_d'], p['wglu'], p['bglu'] = _row(w['s5_d']), w['s5_w_glu'], _row(w['s5_b_glu'])
    p['conv_w'] = [_row(w['lru_conv_w'][CONV_WIDTH - 1 - j]) for j in range(CONV_WIDTH)]
    p['conv_b'] = _row(w['lru_conv_b'])
    p['wax'] = jnp.concatenate([_blockdiag(w['lru_w_a']), _blockdiag(w['lru_w_x'])], axis=1).astype(BF16)
    p['b_a'], p['b_x'], p['lam'] = _row(w['lru_b_a']), _row(w['lru_b_x']), _row(w['lru_lambda'])
    p['wb'] = [w['w_branch'][k] for k in range(3)]
    p['wout'] = w['w_out']
    p['ln1'] = [_row(w['ln1_g']), _row(w['ln1_b'])]
    p['wgu'] = jnp.concatenate([w['w_ffn_gate'], w['w_ffn_up']], axis=1)
    p['wd'] = w['w_ffn_down']
    p['ln2'] = [_row(w['ln2_g']), _row(w['ln2_b'])]
    return p


def _with_copy(f):
    def g(*args):
        (y,) = f(*args)
        return y, y
    return g


def _layer_fwd(x, x_bf, p, rider=None):
    length = x.shape[0]
    r = {'x': x, 'x_bf': x_bf}
    z = r['z'] = _mm("z_in", x_bf, p['wc'])
    bu_re = _mm("s5_bu_re", z, p['s5_bre'], a_w=BRANCH)
    bu_im = _mm("s5_bu_im", z, p['s5_bim'], a_w=BRANCH)
    r['h_re'], r['h_im'] = _scan_cplx("s5_scan", p['lam_re'], p['lam_im'], bu_re, bu_im, bn=256)
    r['hc_re'] = _mm("s5_hc_re", r['h_re'], p['s5_cre'])
    r['hc_im'] = _mm("s5_hc_im", r['h_im'], p['s5_cimn'])
    r['y1'], r['y1_bf'] = _ew("s5_y1", _with_copy(_f_s5_y1), [_full(r['hc_re']), _full(r['hc_im']), (z, 0, BRANCH)],
                              [p['s5_d']], [BRANCH, BRANCH], out_dtypes=[F32, BF16])
    r['pre'] = _mm("s5_glu_pre", r['y1_bf'], p['wglu'])
    (r['ys5'],) = _ew("s5_glu", _f_s5_glu, [_full(r['y1']), _full(r['pre'])], [p['bglu']], [BRANCH], out_dtypes=[BF16])
    xl = z[:, BRANCH:2 * BRANCH]
    r['xd'] = [_delay(xl, j) for j in range(1, CONV_WIDTH)]
    r['xc'], r['xc_bf'] = _ew("lru_conv", _with_copy(_f_conv), [(z, BRANCH, BRANCH)] + [_full(a) for a in r['xd']],
                              p['conv_w'] + [p['conv_b']], [BRANCH, BRANCH], out_dtypes=[F32, BF16])
    r['papx'] = _mm("lru_gate_mm", r['xc_bf'], p['wax'])
    r['a'], b = _ew("lru_gates", _f_lru_gates, [_full(r['xc']), (r['papx'], 0, BRANCH), (r['papx'], BRANCH, BRANCH)],
                    [p['b_a'], p['b_x'], p['lam']], [BRANCH, BRANCH])
    r['h'] = _scan_real("lru_scan", r['a'], b)
    (r['ylru'],) = _ew("lru_out", _f_lru_out, [(z, 2 * BRANCH, BRANCH), _full(r['h'])], [], [BRANCH], out_dtypes=[BF16])
    (logf,) = _ew("fox_logf", _f_logf, [(z, Z_FG0, FG_PAD)], [p['b_f']], [FG_PAD])
    cum = _scan_real("fox_cum", jnp.ones((length, FG_PAD), F32), logf)
    qa, ka, kat, v, vt = _fox_operands(z, cum)
    r['fox'] = (qa, ka, kat, v)
    r['ot'], r['lse'], carried = _fox_fwd(qa, ka, vt, rider=rider)
    r['yfox'] = r['ot'].reshape(BRANCH, length).T.astype(BF16)
    ys = [r['ys5'], r['ylru'], r['yfox']]
    r['proj'] = [_mm("proj_%d" % k, ys[k], p['wb'][k]) for k in range(3)]
    gate_rows = [(z, Z_GATE0 + k * D_MODEL, D_MODEL) for k in range(3)]
    (r['mix'],) = _ew("merge", _f_merge, [_full(a) for a in r['proj']] + gate_rows, p['b_gate'], [D_MODEL], tl=128,
                      out_dtypes=[BF16])
    r['mixed'] = _mm("w_out", r['mix'], p['wout'])
    two = dict(out_ws=[D_MODEL, D_MODEL], out_dtypes=[F32, BF16])
    x1, r['x1_bf'] = _ew("ln1", _with_copy(_f_ln), [_full(x), _full(r['mixed'])], p['ln1'], **two)
    r['x1'] = x1
    r['hgu'] = _mm("ffn_in", r['x1_bf'], p['wgu'])
    (r['hid'],) = _ew("swiglu", _f_swiglu, [(r['hgu'], 0, FFN_HIDDEN), (r['hgu'], FFN_HIDDEN, FFN_HIDDEN)], [],
                      [FFN_HIDDEN], tl=128, out_dtypes=[BF16])
    r['f'] = _mm("ffn_out", r['hid'], p['wd'])
    x2, x2_bf = _ew("ln2", _with_copy(_f_ln), [_full(x1), _full(r['f'])], p['ln2'], **two)
    return x2, x2_bf, r, carried


def _layer_bwd(dx2, r, p, riders=(None, None)):
    g = {}
    x, z, x1 = r['x'], r['z'], r['x1']
    dx1_n, df, g['ln2_g'], g['ln2_b'] = _ew_bwd("ln2_bwd", _f_ln, [_full(x1), _full(r['f'])], p['ln2'], [dx2],
                                                [True, BF16])
    dhid = _mm("ffn_out_dx", df, p['wd'], tb=True)
    g['w_ffn_down'] = _mm("ffn_out_dw", r['hid'], df, ta=True, out_dtype=BF16)
    hgu_rows = [(r['hgu'], 0, FFN_HIDDEN), (r['hgu'], FFN_HIDDEN, FFN_HIDDEN)]
    dhg, dhu = _ew_bwd("swiglu_bwd", _f_swiglu, hgu_rows, [], [dhid], [BF16, BF16], tl=128)
    g['w_ffn_gate'] = _mm("ffn_gate_dw", r['x1_bf'], dhg, ta=True, out_dtype=BF16)
    g['w_ffn_up'] = _mm("ffn_up_dw", r['x1_bf'], dhu, ta=True, out_dtype=BF16)
    dx1 = _mm("ffn_gate_dx", dhg, p['wgu'], tb=True, b_w=FFN_HIDDEN, add=dx1_n)
    dx1 = _mm("ffn_up_dx", dhu, p['wgu'], tb=True, b_c0=FFN_HIDDEN, b_w=FFN_HIDDEN, add=dx1)
    dx_n, dmixed, g['ln1_g'], g['ln1_b'] = _ew_bwd("ln1_bwd", _f_ln, [_full(x), _full(r['mixed'])], p['ln1'], [dx1],
                                                   [True, BF16])
    dmix = _mm("w_out_dx", dmixed, p['wout'], tb=True)
    g['w_out'] = _mm("w_out_dw", r['mix'], dmixed, ta=True, out_dtype=BF16)
    gate_rows = [(z, Z_GATE0 + k * D_MODEL, D_MODEL) for k in range(3)]
    mg = _ew_bwd("merge_bwd", _f_merge, [_full(a) for a in r['proj']] + gate_rows, p['b_gate'], [dmix], [BF16] * 6,
                 tl=128)
    dproj, dzg = mg[0:3], mg[3:6]
    g['b_gate'] = jnp.concatenate([b.reshape(-1) for b in mg[6:9]])
    ys = [r['ys5'], r['ylru'], r['yfox']]
    dys = [_mm("proj_%d_dx" % k, dproj[k], p['wb'][k], tb=True) for k in range(3)]
    g['w_branch'] = jnp.stack([_mm("proj_%d_dw" % k, ys[k], dproj[k], ta=True, out_dtype=BF16) for k in range(3)])
    qa, ka, kat, v = r['fox']
    do = _heads(dys[2]).astype(BF16)
    dot = do.transpose(0, 2, 1)
    dqat, carried_q = _fox_bwd_q(qa, ka, kat, v, dot, r['ot'], r['lse'], rider=riders[0])
    dkh, dvh, dck, carried_kv = _fox_bwd_kv(qa, ka, v, do, dot, r['ot'], r['lse'], rider=riders[1])
    pad_heads = lambda a: jnp.pad(a.T, ((0, 0), (0, FG_PAD - FOX_HEADS)))
    dlogf = _scan_real("fox_cum_bwd", jnp.ones((x.shape[0], FG_PAD), F32), pad_heads(dqat[:, FOX_CQ0, :]),
                       pad_heads(-dck[:, :, 0]), reverse=True)
    dq = (dqat[:, :FOX_HEAD_DIM, :].reshape(BRANCH, x.shape[0]).T * FOX_SCALE).astype(BF16)
    dqkv = [dq, _unheads(dkh).astype(BF16), _unheads(dvh).astype(BF16)]
    dzf, dbf = _ew_bwd("fox_logf_bwd", _f_logf, [(z, Z_FG0, FG_PAD)], [p['b_f']], [dlogf], [BF16])
    g['b_f'] = dbf[0, :FOX_HEADS]
    dgate, dh = _ew_bwd("lru_out_bwd", _f_lru_out, [(z, 2 * BRANCH, BRANCH), _full(r['h'])], [], [dys[1]], [BF16, True])
    db = _scan_real("lru_scan_bwd", _advance(r['a'], 1), dh, reverse=True)
    gates_rows = [_full(r['xc']), (r['papx'], 0, BRANCH), (r['papx'], BRANCH, BRANCH), _full(_delay(r['h'], 1))]
    dxc, dpa, dpx, db_a, db_x, dlam = _ew_bwd("lru_gates_bwd", _f_lru_step, gates_rows, [p['b_a'], p['b_x'], p['lam']],
                                              [db], [True, BF16, BF16, False])
    dxc = _mm("lru_a_dx", dpa, p['wax'], tb=True, b_w=BRANCH, add=dxc)
    dxc = _mm("lru_x_dx", dpx, p['wax'], tb=True, b_c0=BRANCH, b_w=BRANCH, add=dxc)
    take_heads = lambda d: _blockdiag_take(d, LRU_HEADS, LRU_HEAD_DIM, LRU_HEAD_DIM)
    g['lru_w_a'] = take_heads(_mm("lru_a_dw", r['xc_bf'], dpa, ta=True))
    g['lru_w_x'] = take_heads(_mm("lru_x_dw", r['xc_bf'], dpx, ta=True))
    g['lru_b_a'] = db_a.reshape(LRU_HEADS, LRU_HEAD_DIM)
    g['lru_b_x'] = db_x.reshape(LRU_HEADS, LRU_HEAD_DIM)
    g['lru_lambda'] = dlam.reshape(-1)
    conv_rows = [(z, BRANCH, BRANCH)] + [_full(a) for a in r['xd']]
    cw = _ew_bwd("lru_conv_dw", _f_conv, conv_rows, p['conv_w'] + [p['conv_b']], [dxc], [False] * CONV_WIDTH)
    g['lru_conv_w'] = jnp.concatenate([cw[CONV_WIDTH - 1 - k] for k in range(CONV_WIDTH)], axis=0)
    g['lru_conv_b'] = cw[CONV_WIDTH].reshape(-1)
    (dxl,) = _ew("lru_conv_dx", _f_conv_t, [_full(_advance(dxc, j)) for j in range(CONV_WIDTH)], p['conv_w'], [BRANCH],
                 out_dtypes=[BF16])
    dy1, dpre, dbglu = _ew_bwd("s5_glu_bwd", _f_s5_glu, [_full(r['y1']), _full(r['pre'])], [p['bglu']], [dys[0]],
                               [True, BF16])
    g['s5_b_glu'] = dbglu.reshape(-1)
    g['s5_w_glu'] = _mm("s5_glu_dw", r['y1_bf'], dpre, ta=True, out_dtype=BF16)
    dy1 = _mm("s5_glu_dx", dpre, p['wglu'], tb=True, add=dy1)
    dy0, du, dd = _ew_bwd("s5_y1_bwd", _f_s5_y1, [_full(r['hc_re']), _full(r['hc_im']), (z, 0, BRANCH)], [p['s5_d']],
                          [dy1], [BF16, False, True])
    g['s5_d'] = dd.reshape(-1)
    dh_re = _mm("s5_hc_re_dx", dy0, p['s5_cre'], tb=True)
    dh_im = _mm("s5_hc_im_dx", dy0, p['s5_cimn'], tb=True)
    take_c = lambda d: _blockdiag_take(d, S5_GROUPS, S5_STATE, S5_GROUP).transpose(0, 2, 1)
    g['s5_c_re'] = take_c(_mm("s5_hc_re_dw", r['h_re'], dy0, ta=True))
    g['s5_c_im'] = -take_c(_mm("s5_hc_im_dw", r['h_im'], dy0, ta=True))
    gb_re, gb_im, dl_re, dl_im = _scan_cplx("s5_scan_bwd", p['lam_re'], -p['lam_im'], dh_re, dh_im, reverse=True,
                                            h_re=r['h_re'], h_im=r['h_im'])
    du = _mm("s5_bu_re_dx", gb_re, p['s5_bre'], tb=True, add=du)
    du = _mm("s5_bu_im_dx", gb_im, p['s5_bim'], tb=True, add=du, out_dtype=BF16)
    take_b = lambda d: _blockdiag_take(d, S5_GROUPS, S5_GROUP, S5_STATE).transpose(0, 2, 1).reshape(S5_N, S5_GROUP)
    dbb_re = take_b(_mm("s5_bu_re_dw", z, gb_re, ta=True, a_w=BRANCH))
    dbb_im = take_b(_mm("s5_bu_im_dw", z, gb_im, ta=True, a_w=BRANCH))
    disc = _ew_bwd("s5_disc_bwd", _f_s5_disc, [_full(a) for a in p['disc_in']], [],
                   [dl_re.reshape(S5_N, 1), dl_im.reshape(S5_N, 1), dbb_re, dbb_im], [True] * 5, tl=S5_N)
    grp = (S5_GROUPS, S5_STATE)
    g['s5_a_re'], g['s5_a_im'] = disc[0].reshape(grp), disc[1].reshape(grp)
    g['s5_log_dt'] = disc[2].reshape(grp).sum(axis=1)
    g['s5_b_re'], g['s5_b_im'] = disc[3].reshape(grp + (S5_GROUP,)), disc[4].reshape(grp + (S5_GROUP,))
    dz = jnp.concatenate([du, dxl, dgate] + dqkv + list(dzg) + [dzf], axis=1)
    dwc = _mm("z_in_dw", r['x_bf'], dz, ta=True, out_dtype=BF16)
    g['w_in'] = jnp.concatenate([dwc[:, :Z_MAIN], dwc[:, Z_FG0:Z_FG0 + FOX_HEADS], dwc[:, Z_GATE0:Z_FG0]], axis=1)
    dx = _mm("z_in_dx", dz, p['wc'], tb=True, add=dx_n)
    return dx, g, list(carried_q) + list(carried_kv)


def _loss_head(y, target, tl=256):
    length, width = y.shape
    tl = min(tl, length)
    nt = length // tl

    def body(y_ref, t_ref, dy_ref, loss_ref, acc_sc):
        i = pl.program_id(0)

        @pl.when(i == 0)
        def _():
            acc_sc[...] = jnp.zeros_like(acc_sc)

        err = y_ref[...] - t_ref[...]
        dy_ref[...] = err / width
        acc_sc[...] += jnp.sum(jnp.square(err), axis=0, keepdims=True)

        @pl.when(i == nt - 1)
        def _():
            total = jnp.sum(acc_sc[...], axis=1, keepdims=True) * (0.5 / width)
            loss_ref[...] = jnp.broadcast_to(total, loss_ref.shape)

    spec = pl.BlockSpec((tl, width), lambda i: (i, 0))
    dy, loss = pl.pallas_call(
        body, grid=(nt,), in_specs=[spec, spec], out_specs=[spec, pl.BlockSpec((1, LANES), lambda i: (0, 0))],
        out_shape=[jax.ShapeDtypeStruct((length, width), F32), jax.ShapeDtypeStruct((1, LANES), F32)],
        scratch_shapes=[pltpu.VMEM((1, width), F32)], compiler_params=_params("arbitrary"), name="loss_head")(y, target)
    return loss[0, 0], dy


def _adamw(name, w, parts, m, v):
    rows, cols = w.shape
    span = rows // len(parts)
    tr = span
    for cand in (256, 128, 64, 32, 16):
        if span % cand == 0:
            tr = cand
            break
    per_span = span // tr

    def body(*refs):
        w_ref, p_refs = refs[0], refs[1:1 + len(parts)]
        m_ref, v_ref, g_ref, d_ref, m2_ref, v2_ref = refs[1 + len(parts):]
        step = pl.program_id(0)
        grad = None
        for j, p_ref in enumerate(p_refs):
            total = p_ref[0].astype(F32)
            for dev in range(1, N_DEV):
                total = total + p_ref[dev].astype(F32)
            grad = total if grad is None else jnp.where(step >= j * per_span, total, grad)
        m2 = ADAM_B1 * m_ref[...] + (1.0 - ADAM_B1) * grad
        v2 = ADAM_B2 * v_ref[...] + (1.0 - ADAM_B2) * jnp.square(grad)
        m_hat = m2 / (1.0 - ADAM_B1 ** ADAM_STEP)
        v_hat = v2 / (1.0 - ADAM_B2 ** ADAM_STEP)
        g_ref[...] = grad
        d_ref[...] = -ADAM_LR * (m_hat / (jnp.sqrt(v_hat) + ADAM_EPS) + ADAM_WD * w_ref[...])
        m2_ref[...] = m2
        v2_ref[...] = v2

    spec = pl.BlockSpec((tr, cols), lambda i: (i, 0))
    pspecs = [pl.BlockSpec((N_DEV, tr, cols),
                           lambda i, j=j: (0, jnp.minimum(jnp.maximum(i - j * per_span, 0), per_span - 1), 0))
              for j in range(len(parts))]
    shape = jax.ShapeDtypeStruct((rows, cols), F32)
    return pl.pallas_call(body, grid=(rows // tr,), in_specs=[spec] + pspecs + [spec, spec], out_specs=[spec] * 4,
                          out_shape=[shape] * 4, compiler_params=_params("arbitrary"), name=name)(w, *parts, m, v)


class _NoExchange:
    def __init__(self, layers):
        self.layers = layers

    def weights(self, l, carried):
        return self.layers[l]

    def forward_rider(self, l):
        return None

    def backward_riders(self, l):
        return (None, None)

    def collect(self, l, grads, carried):
        pass


def _forward_backward(x, target, hooks):
    prepared, saved, carried = [], [], None
    x_bf = x.astype(BF16)
    for l in range(DEPTH):
        p = _prep_layer(hooks.weights(l, carried))
        x, x_bf, r, carried = _layer_fwd(x, x_bf, p, hooks.forward_rider(l))
        prepared.append(p)
        saved.append(r)
    loss, dx = _loss_head(x, target)
    grads = [None] * DEPTH
    for l in reversed(range(DEPTH)):
        dx, grads[l], carried = _layer_bwd(dx, saved[l], prepared[l], hooks.backward_riders(l))
        hooks.collect(l, grads[l], carried)
    return loss, dx, grads


def _exchange_copies(ins, outs, sems, scatter, with_arrivals):
    send_sems, recv_sems, local_sems = sems
    x, y, c = lax.axis_index("x"), lax.axis_index("y"), lax.axis_index("c")
    me = 4 * x + 2 * y + c
    local, sends, arrivals = [], [], []
    for a in range(len(ins)):
        local.append(pltpu.make_async_copy(ins[a].at[me] if scatter else ins[a], outs[a].at[me], local_sems.at[a]))
    for k in range(1, N_DEV):
        px = 1 - x if k & 4 else x
        py = 1 - y if k & 2 else y
        pc = 1 - c if k & 1 else c
        idx = 4 * px + 2 * py + pc
        for a in range(len(ins)):
            s = a * (N_DEV - 1) + k - 1
            src = ins[a].at[idx] if scatter else ins[a]
            common = dict(src_ref=src, send_sem=send_sems.at[s], recv_sem=recv_sems.at[s], device_id=(px, py, pc),
                          device_id_type=pl.DeviceIdType.MESH)
            sends.append(pltpu.make_async_remote_copy(dst_ref=outs[a].at[me], **common))
            if with_arrivals:
                arrivals.append(pltpu.make_async_remote_copy(dst_ref=outs[a].at[idx], **common))
    return local, sends, arrivals


def _exchange_start(ins, outs, sems, scatter):
    local, sends, _ = _exchange_copies(ins, outs, sems, scatter, False)
    for cp in local + sends:
        cp.start()


def _exchange_wait(ins, outs, sems, scatter):
    local, sends, arrivals = _exchange_copies(ins, outs, sems, scatter, True)
    for cp in local:
        cp.wait()
    for cp in sends:
        cp.wait_send()
    for cp in arrivals:
        cp.wait_recv()


def _exchange_parts(arrays, scatter):
    n = len(arrays)
    hbm = [pl.BlockSpec(memory_space=pltpu.HBM)] * n
    out_shape = [jax.ShapeDtypeStruct(a.shape if scatter else (N_DEV,) + a.shape, a.dtype) for a in arrays]
    nsem = n * (N_DEV - 1)
    sems = [pltpu.SemaphoreType.DMA((nsem,)), pltpu.SemaphoreType.DMA((nsem,)), pltpu.SemaphoreType.DMA((n,))]
    return hbm, out_shape, sems


def _exchange(name, arrays, scatter):
    n = len(arrays)
    hbm, out_shape, sems = _exchange_parts(arrays, scatter)

    def body(*refs):
        ins, outs, sem_refs = refs[:n], refs[n:2 * n], refs[2 * n:]
        _exchange_start(ins, outs, sem_refs, scatter)
        _exchange_wait(ins, outs, sem_refs, scatter)

    return pl.pallas_call(body, in_specs=hbm, out_specs=hbm, out_shape=out_shape, scratch_shapes=sems,
                          name=name)(*arrays)


def _call_carrying(name, body, grid, ins, in_specs, out_shape, out_specs, scratch, semantics, rider):
    if rider is None:
        r_arrays, r_hbm, r_shape, r_sems = [], [], [], []
    else:
        r_arrays, scatter = rider
        r_hbm, r_shape, r_sems = _exchange_parts(r_arrays, scatter)
        semantics = ("arbitrary",) * len(grid)
    n_in, n_out, n_scr, n_r = len(ins), len(out_shape), len(scratch), len(r_arrays)

    def full_body(*refs):
        at = [0]

        def take(count):
            at[0] += count
            return refs[at[0] - count:at[0]]

        in_refs, r_in, out_refs, r_out, scr, r_scr = take(n_in), take(n_r), take(n_out), take(n_r), take(n_scr), take(3)
        ids = [pl.program_id(d) for d in range(len(grid))]
        if rider is not None:
            first = functools.reduce(jnp.logical_and, [i == 0 for i in ids])
            pl.when(first)(functools.partial(_exchange_start, r_in, r_out, r_scr, scatter))
        body(in_refs, out_refs, scr)
        if rider is not None:
            last = functools.reduce(jnp.logical_and, [i == g - 1 for i, g in zip(ids, grid)])
            pl.when(last)(functools.partial(_exchange_wait, r_in, r_out, r_scr, scatter))

    res = pl.pallas_call(
        full_body, grid=grid, in_specs=list(in_specs) + r_hbm, out_specs=list(out_specs) + r_hbm,
        out_shape=list(out_shape) + r_shape, scratch_shapes=list(scratch) + r_sems,
        compiler_params=_params(*semantics), name=name)(*ins, *r_arrays)
    return res[:n_out], res[n_out:]


def _shard_2d(a):
    return a.reshape(-1, a.shape[-1])


def _full_layer_weight(name, t):
    if name in ('s5_w_glu', 'w_out', 'w_ffn_down'):
        return t.reshape(-1, t.shape[-1])
    if name == 'w_branch':
        return t.transpose(1, 2, 0, 3).reshape(3, BRANCH, D_MODEL)
    return t.transpose(1, 0, 2).reshape(t.shape[1], -1)


def _split_layer_grad(name, g):
    if name in ('s5_w_glu', 'w_out', 'w_ffn_down'):
        return g.reshape(N_DEV, g.shape[0] // N_DEV, g.shape[1])
    if name == 'w_branch':
        return g.reshape(3, BRANCH, N_DEV, D_MODEL // N_DEV).transpose(2, 0, 1, 3)
    return g.reshape(g.shape[0], N_DEV, g.shape[1] // N_DEV).transpose(1, 0, 2)


RIDING = [n for n in SHARDED if n != 'lru_conv_w']
CARRIED_BY_DQ = ['w_in', 's5_w_glu', 'w_branch', 'w_out']
CARRIED_BY_DKV = ['w_ffn_gate', 'w_ffn_up', 'w_ffn_down']


class _Fsdp:
    def __init__(self, weights):
        self.weights_in = weights
        self.shard = {n: _shard_2d(weights[n]).astype(BF16) for n in RIDING}
        self.rows = {n: self.shard[n].shape[0] // DEPTH for n in RIDING}
        first = _exchange("gather_first_layer", [self.layer_shard(n, 0) for n in RIDING]
                          + [_shard_2d(weights['lru_conv_w'])], scatter=False)
        self.first = first[:-1]
        self.conv = first[-1].reshape((N_DEV,) + weights['lru_conv_w'].shape)
        self.outgoing = None
        self.incoming = {n: [None] * DEPTH for n in RIDING}

    def layer_shard(self, n, l):
        return self.shard[n][l * self.rows[n]:(l + 1) * self.rows[n]]

    def weights(self, l, carried):
        w = {n: self.weights_in[n][l] for n in REPLICATED}
        for n, t in zip(RIDING, self.first if l == 0 else carried):
            w[n] = _full_layer_weight(n, t.reshape((N_DEV,) + self.weights_in[n].shape[1:]))
        w['lru_conv_w'] = _full_layer_weight('lru_conv_w', self.conv[:, l])
        return w

    def forward_rider(self, l):
        return ([self.layer_shard(n, l + 1) for n in RIDING], False) if l + 1 < DEPTH else None

    def backward_riders(self, l):
        if self.outgoing is None:
            return (None, None)
        return ([self.outgoing[n] for n in CARRIED_BY_DQ], True), ([self.outgoing[n] for n in CARRIED_BY_DKV], True)

    def collect(self, l, grads, carried):
        for n, t in zip(CARRIED_BY_DQ + CARRIED_BY_DKV, carried):
            self.incoming[n][l + 1] = t
        self.outgoing = {n: _split_layer_grad(n, grads[n]).reshape(N_DEV, self.rows[n], -1).astype(BF16)
                         for n in RIDING}

    def finish(self, grads):
        conv = jnp.stack([_split_layer_grad('lru_conv_w', grads[l]['lru_conv_w']) for l in range(DEPTH)], axis=1)
        conv = conv.reshape(N_DEV, -1, conv.shape[-1]).astype(F32)
        last = _exchange("scatter_last_layer", [self.outgoing[n] for n in RIDING] + [conv], scatter=True)
        for n, t in zip(RIDING, last[:-1]):
            self.incoming[n][0] = t
        return {**self.incoming, 'lru_conv_w': [last[-1]]}


def kernel(x, w_in, b_f, b_gate, s5_a_re, s5_a_im, s5_log_dt, s5_b_re, s5_b_im, s5_c_re, s5_c_im, s5_d, s5_w_glu, s5_b_glu, lru_conv_w, lru_conv_b, lru_w_a, lru_b_a, lru_w_x, lru_b_x, lru_lambda, w_branch, w_out, ln1_g, ln1_b, w_ffn_gate, w_ffn_up, w_ffn_down, ln2_g, ln2_b, loss_target, m_w_in, m_b_f, m_b_gate, m_s5_a_re, m_s5_a_im, m_s5_log_dt, m_s5_b_re, m_s5_b_im, m_s5_c_re, m_s5_c_im, m_s5_d, m_s5_w_glu, m_s5_b_glu, m_lru_conv_w, m_lru_conv_b, m_lru_w_a, m_lru_b_a, m_lru_w_x, m_lru_b_x, m_lru_lambda, m_w_branch, m_w_out, m_ln1_g, m_ln1_b, m_w_ffn_gate, m_w_ffn_up, m_w_ffn_down, m_ln2_g, m_ln2_b, v_w_in, v_b_f, v_b_gate, v_s5_a_re, v_s5_a_im, v_s5_log_dt, v_s5_b_re, v_s5_b_im, v_s5_c_re, v_s5_c_im, v_s5_d, v_s5_w_glu, v_s5_b_glu, v_lru_conv_w, v_lru_conv_b, v_lru_w_a, v_lru_b_a, v_lru_w_x, v_lru_b_x, v_lru_lambda, v_w_branch, v_w_out, v_ln1_g, v_ln1_b, v_w_ffn_gate, v_w_ffn_up, v_w_ffn_down, v_ln2_g, v_ln2_b):
    given = dict(locals())
    weights = {n: given[n] for n in WEIGHTS}
    moments_m = {n: given['m_' + n] for n in WEIGHTS}
    moments_v = {n: given['v_' + n] for n in WEIGHTS}

    hooks = _Fsdp(weights)
    loss_local, dx, grads = _forward_backward(x[0], loss_target[0], hooks)
    loss = lax.psum(loss_local, MESH_AXES)
    incoming = hooks.finish(grads)

    new = {}
    for n in SHARDED:
        res = _adamw("adamw_" + n, _shard_2d(weights[n]), incoming[n], _shard_2d(moments_m[n]), _shard_2d(moments_v[n]))
        new[n] = [t.reshape(weights[n].shape) for t in res]

    partial = [jnp.stack([grads[l][n] for l in range(DEPTH)]).astype(F32).reshape(_shard_2d(weights[n]).shape)
               for n in REPLICATED]
    arrived = _exchange("gather_small_grads", partial, scatter=False)
    for n, parts in zip(REPLICATED, arrived):
        res = _adamw("adamw_" + n, _shard_2d(weights[n]), [parts], _shard_2d(moments_m[n]), _shard_2d(moments_v[n]))
        new[n] = [t.reshape(weights[n].shape) for t in res]

    return (loss, dx[None], *[new[n][0] for n in WEIGHTS], *[new[n][1] for n in WEIGHTS],
            *[new[n][2] for n in WEIGHTS], *[new[n][3] for n in WEIGHTS])
```

```python
import functools
import math

import jax
import jax.numpy as jnp
from jax import lax
from jax.experimental import pallas as pl
from jax.experimental.pallas import tpu as pltpu

F32 = jnp.float32
BF16 = jnp.bfloat16

D_MODEL = 1024
DEPTH = 4
BRANCH = 512
S5_GROUPS, S5_GROUP, S5_STATE = 32, 16, 64
S5_N = S5_GROUPS * S5_STATE
LRU_HEADS, LRU_HEAD_DIM = 8, 64
LRU_C = 8.0
CONV_WIDTH = 4
FOX_HEADS, FOX_HEAD_DIM = 8, 64
FFN_HIDDEN = 2816
ALPHA = (2.0 * DEPTH) ** 0.25
LN_EPS = 1e-5
IN_TOTAL = 6 * BRANCH + FOX_HEADS + 3 * D_MODEL
FG_PAD = 128
Z_MAIN = 6 * BRANCH
Z_GATE0 = Z_MAIN
Z_FG0 = Z_MAIN + 3 * D_MODEL
Z_TOTAL = Z_FG0 + FG_PAD
N_DEV = 8
MESH_AXES = ("x", "y", "c")

ADAM_LR, ADAM_B1, ADAM_B2, ADAM_EPS, ADAM_WD, ADAM_STEP = 0.001, 0.9, 0.999, 1e-08, 0.01, 10

VMEM_LIMIT_BYTES = 48 * 1024 * 1024
SCAN_SLACK_BYTES = 6 * 1024 * 1024
SUBLANES = 8
LANES = 128
NEG_BIG = -1e30

WEIGHTS = ['w_in', 'b_f', 'b_gate', 's5_a_re', 's5_a_im', 's5_log_dt', 's5_b_re', 's5_b_im', 's5_c_re', 's5_c_im',
           's5_d', 's5_w_glu', 's5_b_glu', 'lru_conv_w', 'lru_conv_b', 'lru_w_a', 'lru_b_a', 'lru_w_x', 'lru_b_x',
           'lru_lambda', 'w_branch', 'w_out', 'ln1_g', 'ln1_b', 'w_ffn_gate', 'w_ffn_up', 'w_ffn_down', 'ln2_g',
           'ln2_b']
SHARDED = ['w_in', 's5_w_glu', 'lru_conv_w', 'w_branch', 'w_out', 'w_ffn_gate', 'w_ffn_up', 'w_ffn_down']
REPLICATED = [n for n in WEIGHTS if n not in SHARDED]


def _params(*sem, vmem=VMEM_LIMIT_BYTES):
    return pltpu.CompilerParams(dimension_semantics=sem, vmem_limit_bytes=vmem)


def _tile(dim, want):
    if dim % LANES:
        return dim
    t = min(want, dim) // LANES * LANES
    while dim % t:
        t -= LANES
    return t


MM_VMEM_BUDGET_BYTES = 30 * 1024 * 1024
MM_MAX_TILE = 1024


def _divisor_tiles(dim, cap, must_divide=0):
    if dim % LANES:
        return [dim]
    out = [t for t in range(min(cap, dim) // LANES * LANES, 0, -LANES) if dim % t == 0 and must_divide % t == 0]
    return out or [dim]


def _mm_tiles(m, n, k, a_bytes, b_bytes, o_bytes, has_add, m_c0, n_c0, k_c0):
    for tk in _divisor_tiles(k, k, k_c0):
        best = None
        for tm in _divisor_tiles(m, MM_MAX_TILE, m_c0):
            for tn in _divisor_tiles(n, MM_MAX_TILE, n_c0):
                used = 2 * (tm * tk * a_bytes + tk * tn * b_bytes + tm * tn * o_bytes) + tm * tn * 4
                used += tm * tn * 4 if tk < k else 0
                used += 2 * tm * tn * 4 if has_add else 0
                if used <= MM_VMEM_BUDGET_BYTES and (best is None or tm * tn / (tm + tn) > best[0]):
                    best = (tm * tn / (tm + tn), tm, tn)
        if best is not None and (min(best[1], best[2]) >= 256 or tk <= 512):
            return best[1], best[2], tk
    raise ValueError("no matmul tiling fits VMEM")


def _mm(name, a, b, *, ta=False, tb=False, a_c0=0, a_w=None, b_c0=0, b_w=None, add=None, out_dtype=F32):
    a_w = a.shape[1] if a_w is None else a_w
    b_w = b.shape[1] if b_w is None else b_w
    m, k = (a_w, a.shape[0]) if ta else (a.shape[0], a_w)
    n = b.shape[0] if tb else b_w
    assert k == (b_w if tb else b.shape[0]), (name, a.shape, b.shape)
    tm, tn, tk = _mm_tiles(m, n, k, a.dtype.itemsize, b.dtype.itemsize, jnp.dtype(out_dtype).itemsize,
                           add is not None, a_c0 if ta else 0, 0 if tb else b_c0,
                           math.gcd(0 if ta else a_c0, b_c0 if tb else 0))
    nk = k // tk
    a_off = a_c0 // (tm if ta else tk)
    b_off = b_c0 // (tk if tb else tn)
    assert a_c0 % (tm if ta else tk) == 0 and b_c0 % (tk if tb else tn) == 0, name
    dims = (((0 if ta else 1,), (1 if tb else 0,)), ((), ()))
    a_total, b_total = m * k * a.dtype.itemsize, n * k * b.dtype.itemsize
    a_stays = a_total + b_total * (m // tm) <= b_total + a_total * (n // tn)
    if nk > 1:
        a_stays = True

    def mn(o, i):
        return (o, i) if a_stays else (i, o)

    def body(*refs):
        a_ref, b_ref = refs[0], refs[1]
        add_ref = refs[2] if add is not None else None
        o_ref = refs[3] if add is not None else refs[2]
        part = lax.dot_general(a_ref[...].astype(BF16), b_ref[...].astype(BF16), dims, preferred_element_type=F32)

        def finish(r):
            if add is not None:
                r = r + add_ref[...]
            o_ref[...] = r.astype(o_ref.dtype)

        if nk == 1:
            finish(part)
            return
        acc_ref = refs[-1]
        kk = pl.program_id(2)

        @pl.when(kk == 0)
        def _():
            acc_ref[...] = part

        @pl.when(kk > 0)
        def _():
            acc_ref[...] += part

        @pl.when(kk == nk - 1)
        def _():
            finish(acc_ref[...])

    def a_map(o, i, kk):
        im = mn(o, i)[0]
        return (kk, im + a_off) if ta else (im, kk + a_off)

    def b_map(o, i, kk):
        jn = mn(o, i)[1]
        return (jn, kk + b_off) if tb else (kk, jn + b_off)

    a_spec = pl.BlockSpec((tk, tm) if ta else (tm, tk), a_map)
    b_spec = pl.BlockSpec((tn, tk) if tb else (tk, tn), b_map)
    o_spec = pl.BlockSpec((tm, tn), lambda o, i, kk: mn(o, i))
    ins, in_specs = [a, b], [a_spec, b_spec]
    if add is not None:
        ins.append(add)
        in_specs.append(o_spec)
    grid = (m // tm, n // tn, nk) if a_stays else (n // tn, m // tm, nk)
    return pl.pallas_call(
        body, grid=grid, in_specs=in_specs, out_specs=o_spec, out_shape=jax.ShapeDtypeStruct((m, n), out_dtype),
        scratch_shapes=[pltpu.VMEM((tm, tn), F32)] if nk > 1 else [],
        compiler_params=_params("parallel", "parallel", "arbitrary"), name=name)(*ins)


def _row_spec(tl, c0, w):
    assert c0 % w == 0
    return pl.BlockSpec((tl, w), lambda i: (i, c0 // w))


def _whole_spec(p):
    return pl.BlockSpec(p.shape, lambda i: (0,) * p.ndim)


def _ew(name, f, rows, prm, out_ws, tl=256, out_dtypes=None):
    out_dtypes = out_dtypes or [F32] * len(out_ws)
    nrows, nprm = len(rows), len(prm)
    length = rows[0][0].shape[0]
    tl = min(tl, length)

    def body(*refs):
        vals = [r[...] for r in refs[:nrows + nprm]]
        outs = f(*vals)
        for o_ref, o in zip(refs[nrows + nprm:], outs):
            o_ref[...] = o.astype(o_ref.dtype)

    return pl.pallas_call(
        body, grid=(length // tl,),
        in_specs=[_row_spec(tl, c0, w) for (_, c0, w) in rows] + [_whole_spec(p) for p in prm],
        out_specs=[_row_spec(tl, 0, w) for w in out_ws],
        out_shape=[jax.ShapeDtypeStruct((length, w), dt) for w, dt in zip(out_ws, out_dtypes)],
        compiler_params=_params("parallel"), name=name)(*[r[0] for r in rows], *prm)


def _ew_bwd(name, f, rows, prm, douts, row_grad, tl=256):
    nrows, nprm, nd = len(rows), len(prm), len(douts)
    length = rows[0][0].shape[0]
    tl = min(tl, length)
    want = [i for i in range(nrows) if row_grad[i]]

    def body(*refs):
        vals = [r[...] for r in refs[:nrows + nprm]]
        cts = tuple(r[...] for r in refs[nrows + nprm:nrows + nprm + nd])
        out_refs = refs[nrows + nprm + nd:]
        _, vjp = jax.vjp(lambda *v: tuple(f(*v)), *vals)
        grads = vjp(cts)
        for o_ref, i in zip(out_refs[:len(want)], want):
            o_ref[...] = grads[i].astype(o_ref.dtype)

        @pl.when(pl.program_id(0) == 0)
        def _():
            for o_ref in out_refs[len(want):]:
                o_ref[...] = jnp.zeros_like(o_ref)

        for o_ref, g in zip(out_refs[len(want):], grads[nrows:]):
            o_ref[...] += g

    return pl.pallas_call(
        body, grid=(length // tl,),
        in_specs=([_row_spec(tl, c0, w) for (_, c0, w) in rows] + [_whole_spec(p) for p in prm]
                  + [_row_spec(tl, 0, d.shape[1]) for d in douts]),
        out_specs=[_row_spec(tl, 0, rows[i][2]) for i in want] + [_whole_spec(p) for p in prm],
        out_shape=([jax.ShapeDtypeStruct((length, rows[i][2]), F32 if row_grad[i] is True else row_grad[i])
                    for i in want]
                   + [jax.ShapeDtypeStruct(p.shape, F32) for p in prm]),
        compiler_params=_params("arbitrary"), name=name)(*[r[0] for r in rows], *prm, *douts)


def _row_ids(width):
    return lax.broadcasted_iota(jnp.int32, (SUBLANES, width), 0)


def _shift_rows(v, d, reverse):
    return pltpu.roll(v, (SUBLANES - d) if reverse else d, 0)


def _scan_real(name, a, b, b2=None, *, reverse=False, bn=256):
    length, n = a.shape
    bn = _tile(n, bn)
    nb = length // SUBLANES
    operands = [a, b] if b2 is None else [a, b, b2]

    def body(*refs):
        a_ref, b_ref, h_ref = refs[0], refs[1], refs[-1]
        rows = _row_ids(bn)

        def step(it, carry):
            i = (nb - 1 - it) if reverse else it
            sl = pl.ds(pl.multiple_of(i * SUBLANES, SUBLANES), SUBLANES)
            av, bv = a_ref[sl, :], b_ref[sl, :]
            if b2 is not None:
                bv = bv + refs[2][sl, :]
            for d in (1, 2, 4):
                live = (rows < SUBLANES - d) if reverse else (rows >= d)
                a_in = jnp.where(live, _shift_rows(av, d, reverse), 1.0)
                b_in = jnp.where(live, _shift_rows(bv, d, reverse), 0.0)
                bv = bv + av * b_in
                av = av * a_in
            hv = bv + av * carry
            h_ref[sl, :] = hv
            edge = hv[0:1, :] if reverse else hv[SUBLANES - 1:SUBLANES, :]
            return jnp.broadcast_to(edge, (SUBLANES, bn))

        lax.fori_loop(0, nb, step, jnp.zeros((SUBLANES, bn), F32))

    spec = pl.BlockSpec((length, bn), lambda j: (0, j))
    return pl.pallas_call(body, grid=(n // bn,), in_specs=[spec] * len(operands), out_specs=spec,
                          out_shape=jax.ShapeDtypeStruct((length, n), F32),
                          compiler_params=_params("parallel"), name=name)(*operands)


def _cmul(ar, ai, br, bi):
    return ar * br - ai * bi, ar * bi + ai * br


def _scan_cplx(name, lam_re, lam_im, x_re, x_im, *, reverse=False, h_re=None, h_im=None, bn=128):
    length, n = x_re.shape
    bn = _tile(n, bn)
    nb = length // SUBLANES
    with_dot = h_re is not None

    def body(*refs):
        if with_dot:
            lr_ref, li_ref, xr_ref, xi_ref, hr_ref, hi_ref, gr_ref, gi_ref, dr_ref, di_ref = refs
        else:
            lr_ref, li_ref, xr_ref, xi_ref, gr_ref, gi_ref = refs
        rows = _row_ids(bn)
        lr = jnp.broadcast_to(lr_ref[...], (SUBLANES, bn))
        li = jnp.broadcast_to(li_ref[...], (SUBLANES, bn))
        powers = [(lr, li)]
        for _ in range(SUBLANES - 1):
            powers.append(_cmul(powers[-1][0], powers[-1][1], lr, li))
        zero = jnp.zeros((SUBLANES, bn), F32)
        steps = []
        for d in (1, 2, 4):
            live = (rows < SUBLANES - d) if reverse else (rows >= d)
            steps.append((d, jnp.where(live, powers[d - 1][0], 0.0), jnp.where(live, powers[d - 1][1], 0.0)))
        cr, ci = zero, zero
        for r in range(SUBLANES):
            e = (SUBLANES - r) if reverse else (r + 1)
            cr = jnp.where(rows == r, powers[e - 1][0], cr)
            ci = jnp.where(rows == r, powers[e - 1][1], ci)

        def step(it, carry):
            i = (nb - 1 - it) if reverse else it
            sl = pl.ds(pl.multiple_of(i * SUBLANES, SUBLANES), SUBLANES)
            vr, vi = xr_ref[sl, :], xi_ref[sl, :]
            for d, pr, pi in steps:
                sr, si = _cmul(pr, pi, _shift_rows(vr, d, reverse), _shift_rows(vi, d, reverse))
                vr, vi = vr + sr, vi + si
            kr, ki = _cmul(cr, ci, carry[0], carry[1])
            vr, vi = vr + kr, vi + ki
            gr_ref[sl, :] = vr
            gi_ref[sl, :] = vi
            er = vr[0:1, :] if reverse else vr[SUBLANES - 1:SUBLANES, :]
            ei = vi[0:1, :] if reverse else vi[SUBLANES - 1:SUBLANES, :]
            new = (jnp.broadcast_to(er, (SUBLANES, bn)), jnp.broadcast_to(ei, (SUBLANES, bn)))
            if not with_dot:
                return new
            prev = pl.ds(pl.multiple_of(jnp.maximum(i - 1, 0) * SUBLANES, SUBLANES), SUBLANES)
            keep = jnp.where(i > 0, 1.0, 0.0)
            pr_ = jnp.broadcast_to(hr_ref[prev, :][SUBLANES - 1:SUBLANES, :], (SUBLANES, bn)) * keep
            pi_ = jnp.broadcast_to(hi_ref[prev, :][SUBLANES - 1:SUBLANES, :], (SUBLANES, bn)) * keep
            hr = jnp.where(rows == 0, pr_, pltpu.roll(hr_ref[sl, :], 1, 0))
            hi = jnp.where(rows == 0, pi_, pltpu.roll(hi_ref[sl, :], 1, 0))
            return new + (carry[2] + vr * hr + vi * hi, carry[3] + vi * hr - vr * hi)

        init = (zero, zero, zero, zero) if with_dot else (zero, zero)
        out = lax.fori_loop(0, nb, step, init)
        if with_dot:
            dr_ref[...] = jnp.sum(out[2], axis=0, keepdims=True)
            di_ref[...] = jnp.sum(out[3], axis=0, keepdims=True)

    col = pl.BlockSpec((length, bn), lambda j: (0, j))
    vec = pl.BlockSpec((1, bn), lambda j: (0, j))
    ins = [lam_re, lam_im, x_re, x_im] + ([h_re, h_im] if with_dot else [])
    in_specs = [vec, vec, col, col] + ([col, col] if with_dot else [])
    out_specs = [col, col] + ([vec, vec] if with_dot else [])
    full = jax.ShapeDtypeStruct((length, n), F32)
    row = jax.ShapeDtypeStruct((1, n), F32)
    out_shape = [full, full] + ([row, row] if with_dot else [])
    columns = 6 if with_dot else 4
    return pl.pallas_call(body, grid=(n // bn,), in_specs=in_specs, out_specs=out_specs, out_shape=out_shape,
                          compiler_params=_params("parallel", vmem=2 * columns * length * bn * 4 + SCAN_SLACK_BYTES),
                          name=name)(*ins)


FOX_SCALE = FOX_HEAD_DIM ** -0.5
FOX_AUG = 128
FOX_CQ0 = FOX_HEAD_DIM
FOX_CK0 = FOX_HEAD_DIM + 3
NT = (((1,), (1,)), ((), ()))


def _fox_logits_t(ka, qa, on_diagonal):
    st = lax.dot_general(ka, qa, NT, preferred_element_type=F32)
    if on_diagonal:
        key = lax.broadcasted_iota(jnp.int32, st.shape, 0)
        query = lax.broadcasted_iota(jnp.int32, st.shape, 1)
        st = jnp.where(key <= query, st, NEG_BIG)
    return st


def _fox_specs(t, q_first):
    def q_idx(h, a, b):
        i, j = (a, b) if q_first else (b, a)
        return i if q_first else jnp.maximum(i, j)

    def k_idx(h, a, b):
        i, j = (a, b) if q_first else (b, a)
        return jnp.minimum(i, j) if q_first else j

    rows = lambda idx, w: pl.BlockSpec((None, t, w), lambda h, a, b: (h, idx(h, a, b), 0))
    cols = lambda idx, w: pl.BlockSpec((None, w, t), lambda h, a, b: (h, 0, idx(h, a, b)))
    return rows, cols, q_idx, k_idx


def _fox_fwd(qa, ka, vt, t=512, rider=None):
    heads, length, _ = qa.shape
    dh = vt.shape[1]
    t = min(t, length)
    nt = length // t

    def body(in_refs, out_refs, scratch_refs):
        (qa_ref, ka_ref, vt_ref), (o_ref, lse_ref), (m_sc, l_sc, acc_sc) = in_refs, out_refs, scratch_refs
        qi, ki = pl.program_id(1), pl.program_id(2)

        @pl.when(ki == 0)
        def _():
            m_sc[...] = jnp.full_like(m_sc, NEG_BIG)
            l_sc[...] = jnp.zeros_like(l_sc)
            acc_sc[...] = jnp.zeros_like(acc_sc)

        def step(on_diagonal):
            st = _fox_logits_t(ka_ref[...], qa_ref[...], on_diagonal)
            m_old = m_sc[...]
            m_new = jnp.maximum(m_old, jnp.max(st, axis=0, keepdims=True))
            pt = jnp.exp(st - m_new)
            scale = jnp.exp(m_old - m_new)
            l_sc[...] = scale * l_sc[...] + jnp.sum(pt, axis=0, keepdims=True)
            acc_sc[...] = scale * acc_sc[...] + jnp.dot(vt_ref[...], pt.astype(BF16), preferred_element_type=F32)
            m_sc[...] = m_new

        pl.when(ki < qi)(functools.partial(step, False))
        pl.when(ki == qi)(functools.partial(step, True))

        @pl.when(ki == nt - 1)
        def _():
            o_ref[...] = acc_sc[...] / l_sc[...]
            lse_ref[...] = m_sc[...] + jnp.log(l_sc[...])

    rows, cols, q_idx, k_idx = _fox_specs(t, True)
    (ot, lse), carried = _call_carrying(
        "fox_fwd" if rider is None else "fox_fwd_carrying", body, (heads, nt, nt), [qa, ka, vt],
        [rows(q_idx, FOX_AUG), rows(k_idx, FOX_AUG), cols(k_idx, dh)],
        [jax.ShapeDtypeStruct((heads, dh, length), F32), jax.ShapeDtypeStruct((heads, 1, length), F32)],
        [cols(q_idx, dh), cols(q_idx, 1)],
        [pltpu.VMEM((1, t), F32), pltpu.VMEM((1, t), F32), pltpu.VMEM((dh, t), F32)],
        ("parallel", "parallel", "arbitrary"), rider)
    return ot, lse, carried


def _fox_ds_t(qa_ref, ka_ref, v_ref, dot_ref, ot_ref, lse_ref, on_diagonal):
    pt = jnp.exp(_fox_logits_t(ka_ref[...], qa_ref[...], on_diagonal) - lse_ref[...])
    dpt = jnp.dot(v_ref[...], dot_ref[...], preferred_element_type=F32)
    delta = jnp.sum(dot_ref[...].astype(F32) * ot_ref[...], axis=0, keepdims=True)
    return pt, pt * (dpt - delta)


def _fox_bwd_q(qa, ka, kat, v, dot, ot, lse, t=512, rider=None):
    heads, length, _ = qa.shape
    dh = v.shape[2]
    t = min(t, length)
    nt = length // t

    def body(in_refs, out_refs, scratch_refs):
        qa_ref, ka_ref, kat_ref, v_ref, dot_ref, ot_ref, lse_ref = in_refs
        (dqa_ref,), (acc_sc,) = out_refs, scratch_refs
        qi, ki = pl.program_id(1), pl.program_id(2)

        @pl.when(ki == 0)
        def _():
            acc_sc[...] = jnp.zeros_like(acc_sc)

        def step(on_diagonal):
            _, dst = _fox_ds_t(qa_ref, ka_ref, v_ref, dot_ref, ot_ref, lse_ref, on_diagonal)
            acc_sc[...] += jnp.dot(kat_ref[...], dst.astype(BF16), preferred_element_type=F32)

        pl.when(ki < qi)(functools.partial(step, False))
        pl.when(ki == qi)(functools.partial(step, True))

        @pl.when(ki == nt - 1)
        def _():
            dqa_ref[...] = acc_sc[...]

    rows, cols, q_idx, k_idx = _fox_specs(t, True)
    (dqat,), carried = _call_carrying(
        "fox_bwd_q" if rider is None else "fox_bwd_q_carrying", body, (heads, nt, nt), [qa, ka, kat, v, dot, ot, lse],
        [rows(q_idx, FOX_AUG), rows(k_idx, FOX_AUG), cols(k_idx, FOX_AUG), rows(k_idx, dh), cols(q_idx, dh),
         cols(q_idx, dh), cols(q_idx, 1)],
        [jax.ShapeDtypeStruct((heads, FOX_AUG, length), F32)], [cols(q_idx, FOX_AUG)],
        [pltpu.VMEM((FOX_AUG, t), F32)], ("parallel", "parallel", "arbitrary"), rider)
    return dqat, carried


def _fox_bwd_kv(qa, ka, v, do, dot, ot, lse, t=512, rider=None):
    heads, length, _ = qa.shape
    dh = v.shape[2]
    t = min(t, length)
    nt = length // t

    def body(in_refs, out_refs, scratch_refs):
        qa_ref, ka_ref, v_ref, do_ref, dot_ref, ot_ref, lse_ref = in_refs
        (dk_ref, dv_ref, dc_ref), (dka_sc, dv_sc) = out_refs, scratch_refs
        ki, qi = pl.program_id(1), pl.program_id(2)

        @pl.when(qi == 0)
        def _():
            dka_sc[...] = jnp.zeros_like(dka_sc)
            dv_sc[...] = jnp.zeros_like(dv_sc)

        def step(on_diagonal):
            pt, dst = _fox_ds_t(qa_ref, ka_ref, v_ref, dot_ref, ot_ref, lse_ref, on_diagonal)
            dv_sc[...] += jnp.dot(pt.astype(BF16), do_ref[...], preferred_element_type=F32)
            dka_sc[...] += jnp.dot(dst.astype(BF16), qa_ref[...], preferred_element_type=F32)

        pl.when(qi > ki)(functools.partial(step, False))
        pl.when(qi == ki)(functools.partial(step, True))

        @pl.when(qi == nt - 1)
        def _():
            dka = dka_sc[...]
            lane = lax.broadcasted_iota(jnp.int32, dka.shape, 1)
            dk_ref[...] = dka_sc[:, :dh]
            dc_ref[...] = jnp.sum(jnp.where(lane == FOX_CK0, dka, 0.0), axis=1, keepdims=True)
            dv_ref[...] = dv_sc[...]

    rows, cols, q_idx, k_idx = _fox_specs(t, False)
    big = jax.ShapeDtypeStruct((heads, length, dh), F32)
    (dk, dv, dc), carried = _call_carrying(
        "fox_bwd_kv" if rider is None else "fox_bwd_kv_carrying", body, (heads, nt, nt), [qa, ka, v, do, dot, ot, lse],
        [rows(q_idx, FOX_AUG), rows(k_idx, FOX_AUG), rows(k_idx, dh), rows(q_idx, dh), cols(q_idx, dh),
         cols(q_idx, dh), cols(q_idx, 1)],
        [big, big, jax.ShapeDtypeStruct((heads, length, 1), F32)], [rows(k_idx, dh), rows(k_idx, dh), rows(k_idx, 1)],
        [pltpu.VMEM((t, FOX_AUG), F32), pltpu.VMEM((t, dh), F32)], ("parallel", "parallel", "arbitrary"), rider)
    return dk, dv, dc, carried


def _split3(x):
    hi = lax.reduce_precision(x, 8, 7)
    mid = lax.reduce_precision(x - hi, 8, 7)
    return [hi, mid, lax.reduce_precision(x - hi - mid, 8, 7)]


def _fox_operands(z, cum):
    length = z.shape[0]
    parts = jnp.stack(_split3(cum[:, :FOX_HEADS].T), axis=-1)
    ones = jnp.ones_like(parts)
    pad = jnp.zeros((FOX_HEADS, length, FOX_AUG - FOX_HEAD_DIM - 6), F32)
    qa = jnp.concatenate([_heads(z, 3 * BRANCH) * FOX_SCALE, parts, ones, pad], axis=-1).astype(BF16)
    ka = jnp.concatenate([_heads(z, 4 * BRANCH), ones, -parts, pad], axis=-1).astype(BF16)
    v = _heads(z, 5 * BRANCH).astype(BF16)
    return qa, ka, ka.transpose(0, 2, 1), v, v.transpose(0, 2, 1)


def _softplus(x):
    return jnp.maximum(x, 0.0) + jnp.log1p(jnp.exp(-jnp.abs(x)))


def _f_s5_disc(a_re, a_im, log_dt, b_re, b_im):
    dt = jnp.exp(log_dt)
    mag = jnp.exp(a_re * dt)
    lr, li = mag * jnp.cos(a_im * dt), mag * jnp.sin(a_im * dt)
    den = a_re * a_re + a_im * a_im
    qr = ((lr - 1.0) * a_re + li * a_im) / den
    qi = (li * a_re - (lr - 1.0) * a_im) / den
    return lr, li, qr * b_re - qi * b_im, qr * b_im + qi * b_re


def _f_s5_y1(hc_re, hc_im, u, d):
    return (jax.nn.gelu(hc_re + hc_im + d * u),)


def _f_s5_glu(y1, pre, b):
    return (y1 * jax.nn.sigmoid(pre + b),)


def _f_conv(x0, x1, x2, x3, w0, w1, w2, w3, b):
    return (b + w0 * x0 + w1 * x1 + w2 * x2 + w3 * x3,)


def _f_conv_t(d0, d1, d2, d3, w0, w1, w2, w3):
    return (w0 * d0 + w1 * d1 + w2 * d2 + w3 * d3,)


def _lru_coeffs(xc, pa, px, b_a, b_x, lam):
    r = jax.nn.sigmoid(pa + b_a)
    i = jax.nn.sigmoid(px + b_x)
    log_a = -LRU_C * _softplus(-lam) * r
    a = jnp.exp(log_a)
    mult = jnp.sqrt(-jnp.tanh(log_a) * (a * a + 1.0))
    return a, mult * (i * xc)


def _f_lru_gates(xc, pa, px, b_a, b_x, lam):
    return _lru_coeffs(xc, pa, px, b_a, b_x, lam)


def _f_lru_step(xc, pa, px, h_prev, b_a, b_x, lam):
    a, b = _lru_coeffs(xc, pa, px, b_a, b_x, lam)
    return (a * h_prev + b,)


def _f_lru_out(gate, h):
    return (jax.nn.gelu(gate) * h,)


def _f_logf(zf, bf):
    return (-_softplus(-(zf + bf)),)


def _f_merge(p0, p1, p2, z0, z1, z2, b0, b1, b2):
    return (jax.nn.sigmoid(z0 + b0) * p0 + jax.nn.sigmoid(z1 + b1) * p1 + jax.nn.sigmoid(z2 + b2) * p2,)


def _f_ln(x, r, g, b):
    s = ALPHA * x + r
    mu = jnp.mean(s, axis=-1, keepdims=True)
    var = jnp.mean(jnp.square(s - mu), axis=-1, keepdims=True)
    return ((s - mu) * lax.rsqrt(var + LN_EPS) * g + b,)


def _f_swiglu(hg, hu):
    return (jax.nn.silu(hg) * hu,)


def _full(a):
    return (a, 0, a.shape[1])


def _blockdiag(t):
    g, a, b = t.shape
    eye = jnp.eye(g, dtype=t.dtype)
    return (t[:, :, None, :] * eye[:, None, :, None]).reshape(g * a, g * b)


def _blockdiag_take(d, g, a, b):
    eye = jnp.eye(g, dtype=d.dtype)
    return (d.reshape(g, a, g, b) * eye[:, None, :, None]).sum(axis=2)


def _delay(a, j):
    return a if j == 0 else jnp.pad(a, ((j, 0), (0, 0)))[:a.shape[0]]


def _advance(a, j):
    return a if j == 0 else jnp.pad(a, ((0, j), (0, 0)))[j:]


def _heads(a, c0=0):
    length = a.shape[0]
    return a[:, c0:c0 + BRANCH].reshape(length, FOX_HEADS, FOX_HEAD_DIM).transpose(1, 0, 2)


def _unheads(a):
    return a.transpose(1, 0, 2).reshape(a.shape[1], BRANCH)


def _row(v):
    return v.reshape(1, -1).astype(F32)


def _col(v):
    return v.reshape(-1, 1).astype(F32)


def _prep_layer(w):
    p = {}
    w_in = w['w_in']
    p['wc'] = jnp.concatenate(
        [w_in[:, :Z_MAIN], w_in[:, Z_MAIN + FOX_HEADS:], w_in[:, Z_MAIN:Z_MAIN + FOX_HEADS],
         jnp.zeros((D_MODEL, FG_PAD - FOX_HEADS), w_in.dtype)], axis=1)
    p['b_f'] = jnp.pad(_row(w['b_f']), ((0, 0), (0, FG_PAD - FOX_HEADS)))
    p['b_gate'] = [_row(w['b_gate'][k * D_MODEL:(k + 1) * D_MODEL]) for k in range(3)]
    p['disc_in'] = [_col(w['s5_a_re']), _col(w['s5_a_im']), _col(jnp.repeat(w['s5_log_dt'], S5_STATE)),
                    w['s5_b_re'].reshape(S5_N, S5_GROUP), w['s5_b_im'].reshape(S5_N, S5_GROUP)]
    lam_re, lam_im, bb_re, bb_im = _ew("s5_disc", _f_s5_disc, [_full(a) for a in p['disc_in']], [],
                                       [1, 1, S5_GROUP, S5_GROUP], tl=S5_N)
    p['lam_re'], p['lam_im'] = lam_re.reshape(1, S5_N), lam_im.reshape(1, S5_N)
    to_blk = lambda t: _blockdiag(t.reshape(S5_GROUPS, S5_STATE, S5_GROUP).transpose(0, 2, 1)).astype(BF16)
    p['s5_bre'], p['s5_bim'] = to_blk(bb_re), to_blk(bb_im)
    p['s5_cre'] = _blockdiag(w['s5_c_re'].transpose(0, 2, 1)).astype(BF16)
    p['s5_cimn'] = _blockdiag(-w['s5_c_im'].transpose(0, 2, 1)).astype(BF16)
    p['s5_d'], p['wglu'], p['bglu'] = _row(w['s5_d']), w['s5_w_glu'], _row(w['s5_b_glu'])
    p['conv_w'] = [_row(w['lru_conv_w'][CONV_WIDTH - 1 - j]) for j in range(CONV_WIDTH)]
    p['conv_b'] = _row(w['lru_conv_b'])
    p['wax'] = jnp.concatenate([_blockdiag(w['lru_w_a']), _blockdiag(w['lru_w_x'])], axis=1).astype(BF16)
    p['b_a'], p['b_x'], p['lam'] = _row(w['lru_b_a']), _row(w['lru_b_x']), _row(w['lru_lambda'])
    p['wb'] = [w['w_branch'][k] for k in range(3)]
    p['wout'] = w['w_out']
    p['ln1'] = [_row(w['ln1_g']), _row(w['ln1_b'])]
    p['wgu'] = jnp.concatenate([w['w_ffn_gate'], w['w_ffn_up']], axis=1)
    p['wd'] = w['w_ffn_down']
    p['ln2'] = [_row(w['ln2_g']), _row(w['ln2_b'])]
    return p


def _with_copy(f):
    def g(*args):
        (y,) = f(*args)
        return y, y
    return g


def _layer_fwd(x, x_bf, p, rider=None):
    length = x.shape[0]
    r = {'x': x, 'x_bf': x_bf}
    z = r['z'] = _mm("z_in", x_bf, p['wc'])
    bu_re = _mm("s5_bu_re", z, p['s5_bre'], a_w=BRANCH)
    bu_im = _mm("s5_bu_im", z, p['s5_bim'], a_w=BRANCH)
    r['h_re'], r['h_im'] = _scan_cplx("s5_scan", p['lam_re'], p['lam_im'], bu_re, bu_im, bn=256)
    r['hc_re'] = _mm("s5_hc_re", r['h_re'], p['s5_cre'])
    r['hc_im'] = _mm("s5_hc_im", r['h_im'], p['s5_cimn'])
    r['y1'], r['y1_bf'] = _ew("s5_y1", _with_copy(_f_s5_y1), [_full(r['hc_re']), _full(r['hc_im']), (z, 0, BRANCH)],
                              [p['s5_d']], [BRANCH, BRANCH], out_dtypes=[F32, BF16])
    r['pre'] = _mm("s5_glu_pre", r['y1_bf'], p['wglu'])
    (r['ys5'],) = _ew("s5_glu", _f_s5_glu, [_full(r['y1']), _full(r['pre'])], [p['bglu']], [BRANCH], out_dtypes=[BF16])
    xl = z[:, BRANCH:2 * BRANCH]
    r['xd'] = [_delay(xl, j) for j in range(1, CONV_WIDTH)]
    r['xc'], r['xc_bf'] = _ew("lru_conv", _with_copy(_f_conv), [(z, BRANCH, BRANCH)] + [_full(a) for a in r['xd']],
                              p['conv_w'] + [p['conv_b']], [BRANCH, BRANCH], out_dtypes=[F32, BF16])
    r['papx'] = _mm("lru_gate_mm", r['xc_bf'], p['wax'])
    r['a'], b = _ew("lru_gates", _f_lru_gates, [_full(r['xc']), (r['papx'], 0, BRANCH), (r['papx'], BRANCH, BRANCH)],
                    [p['b_a'], p['b_x'], p['lam']], [BRANCH, BRANCH])
    r['h'] = _scan_real("lru_scan", r['a'], b)
    (r['ylru'],) = _ew("lru_out", _f_lru_out, [(z, 2 * BRANCH, BRANCH), _full(r['h'])], [], [BRANCH], out_dtypes=[BF16])
    (logf,) = _ew("fox_logf", _f_logf, [(z, Z_FG0, FG_PAD)], [p['b_f']], [FG_PAD])
    cum = _scan_real("fox_cum", jnp.ones((length, FG_PAD), F32), logf)
    qa, ka, kat, v, vt = _fox_operands(z, cum)
    r['fox'] = (qa, ka, kat, v)
    r['ot'], r['lse'], carried = _fox_fwd(qa, ka, vt, rider=rider)
    r['yfox'] = r['ot'].reshape(BRANCH, length).T.astype(BF16)
    ys = [r['ys5'], r['ylru'], r['yfox']]
    r['proj'] = [_mm("proj_%d" % k, ys[k], p['wb'][k]) for k in range(3)]
    gate_rows = [(z, Z_GATE0 + k * D_MODEL, D_MODEL) for k in range(3)]
    (r['mix'],) = _ew("merge", _f_merge, [_full(a) for a in r['proj']] + gate_rows, p['b_gate'], [D_MODEL], tl=128,
                      out_dtypes=[BF16])
    r['mixed'] = _mm("w_out", r['mix'], p['wout'])
    two = dict(out_ws=[D_MODEL, D_MODEL], out_dtypes=[F32, BF16])
    x1, r['x1_bf'] = _ew("ln1", _with_copy(_f_ln), [_full(x), _full(r['mixed'])], p['ln1'], **two)
    r['x1'] = x1
    r['hgu'] = _mm("ffn_in", r['x1_bf'], p['wgu'])
    (r['hid'],) = _ew("swiglu", _f_swiglu, [(r['hgu'], 0, FFN_HIDDEN), (r['hgu'], FFN_HIDDEN, FFN_HIDDEN)], [],
                      [FFN_HIDDEN], tl=128, out_dtypes=[BF16])
    r['f'] = _mm("ffn_out", r['hid'], p['wd'])
    x2, x2_bf = _ew("ln2", _with_copy(_f_ln), [_full(x1), _full(r['f'])], p['ln2'], **two)
    return x2, x2_bf, r, carried


def _layer_bwd(dx2, r, p, riders=(None, lambda grads: None)):
    g = {}
    x, z, x1 = r['x'], r['z'], r['x1']
    dx1_n, df, g['ln2_g'], g['ln2_b'] = _ew_bwd("ln2_bwd", _f_ln, [_full(x1), _full(r['f'])], p['ln2'], [dx2],
                                                [True, BF16])
    dhid = _mm("ffn_out_dx", df, p['wd'], tb=True)
    g['w_ffn_down'] = _mm("ffn_out_dw", r['hid'], df, ta=True, out_dtype=BF16)
    hgu_rows = [(r['hgu'], 0, FFN_HIDDEN), (r['hgu'], FFN_HIDDEN, FFN_HIDDEN)]
    dhg, dhu = _ew_bwd("swiglu_bwd", _f_swiglu, hgu_rows, [], [dhid], [BF16, BF16], tl=128)
    g['w_ffn_gate'] = _mm("ffn_gate_dw", r['x1_bf'], dhg, ta=True, out_dtype=BF16)
    g['w_ffn_up'] = _mm("ffn_up_dw", r['x1_bf'], dhu, ta=True, out_dtype=BF16)
    dx1 = _mm("ffn_gate_dx", dhg, p['wgu'], tb=True, b_w=FFN_HIDDEN, add=dx1_n)
    dx1 = _mm("ffn_up_dx", dhu, p['wgu'], tb=True, b_c0=FFN_HIDDEN, b_w=FFN_HIDDEN, add=dx1)
    dx_n, dmixed, g['ln1_g'], g['ln1_b'] = _ew_bwd("ln1_bwd", _f_ln, [_full(x), _full(r['mixed'])], p['ln1'], [dx1],
                                                   [True, BF16])
    dmix = _mm("w_out_dx", dmixed, p['wout'], tb=True)
    g['w_out'] = _mm("w_out_dw", r['mix'], dmixed, ta=True, out_dtype=BF16)
    gate_rows = [(z, Z_GATE0 + k * D_MODEL, D_MODEL) for k in range(3)]
    mg = _ew_bwd("merge_bwd", _f_merge, [_full(a) for a in r['proj']] + gate_rows, p['b_gate'], [dmix], [BF16] * 6,
                 tl=128)
    dproj, dzg = mg[0:3], mg[3:6]
    g['b_gate'] = jnp.concatenate([b.reshape(-1) for b in mg[6:9]])
    ys = [r['ys5'], r['ylru'], r['yfox']]
    dys = [_mm("proj_%d_dx" % k, dproj[k], p['wb'][k], tb=True) for k in range(3)]
    g['w_branch'] = jnp.stack([_mm("proj_%d_dw" % k, ys[k], dproj[k], ta=True, out_dtype=BF16) for k in range(3)])
    qa, ka, kat, v = r['fox']
    do = _heads(dys[2]).astype(BF16)
    dot = do.transpose(0, 2, 1)
    dqat, carried_q = _fox_bwd_q(qa, ka, kat, v, dot, r['ot'], r['lse'], rider=riders[0])
    dkh, dvh, dck, carried_kv = _fox_bwd_kv(qa, ka, v, do, dot, r['ot'], r['lse'], rider=riders[1](g))
    pad_heads = lambda a: jnp.pad(a.T, ((0, 0), (0, FG_PAD - FOX_HEADS)))
    dlogf = _scan_real("fox_cum_bwd", jnp.ones((x.shape[0], FG_PAD), F32), pad_heads(dqat[:, FOX_CQ0, :]),
                       pad_heads(-dck[:, :, 0]), reverse=True)
    dq = (dqat[:, :FOX_HEAD_DIM, :].reshape(BRANCH, x.shape[0]).T * FOX_SCALE).astype(BF16)
    dqkv = [dq, _unheads(dkh).astype(BF16), _unheads(dvh).astype(BF16)]
    dzf, dbf = _ew_bwd("fox_logf_bwd", _f_logf, [(z, Z_FG0, FG_PAD)], [p['b_f']], [dlogf], [BF16])
    g['b_f'] = dbf[0, :FOX_HEADS]
    dgate, dh = _ew_bwd("lru_out_bwd", _f_lru_out, [(z, 2 * BRANCH, BRANCH), _full(r['h'])], [], [dys[1]], [BF16, True])
    db = _scan_real("lru_scan_bwd", _advance(r['a'], 1), dh, reverse=True)
    gates_rows = [_full(r['xc']), (r['papx'], 0, BRANCH), (r['papx'], BRANCH, BRANCH), _full(_delay(r['h'], 1))]
    dxc, dpa, dpx, db_a, db_x, dlam = _ew_bwd("lru_gates_bwd", _f_lru_step, gates_rows, [p['b_a'], p['b_x'], p['lam']],
                                              [db], [True, BF16, BF16, False])
    dxc = _mm("lru_a_dx", dpa, p['wax'], tb=True, b_w=BRANCH, add=dxc)
    dxc = _mm("lru_x_dx", dpx, p['wax'], tb=True, b_c0=BRANCH, b_w=BRANCH, add=dxc)
    take_heads = lambda d: _blockdiag_take(d, LRU_HEADS, LRU_HEAD_DIM, LRU_HEAD_DIM)
    g['lru_w_a'] = take_heads(_mm("lru_a_dw", r['xc_bf'], dpa, ta=True))
    g['lru_w_x'] = take_heads(_mm("lru_x_dw", r['xc_bf'], dpx, ta=True))
    g['lru_b_a'] = db_a.reshape(LRU_HEADS, LRU_HEAD_DIM)
    g['lru_b_x'] = db_x.reshape(LRU_HEADS, LRU_HEAD_DIM)
    g['lru_lambda'] = dlam.reshape(-1)
    conv_rows = [(z, BRANCH, BRANCH)] + [_full(a) for a in r['xd']]
    cw = _ew_bwd("lru_conv_dw", _f_conv, conv_rows, p['conv_w'] + [p['conv_b']], [dxc], [False] * CONV_WIDTH)
    g['lru_conv_w'] = jnp.concatenate([cw[CONV_WIDTH - 1 - k] for k in range(CONV_WIDTH)], axis=0)
    g['lru_conv_b'] = cw[CONV_WIDTH].reshape(-1)
    (dxl,) = _ew("lru_conv_dx", _f_conv_t, [_full(_advance(dxc, j)) for j in range(CONV_WIDTH)], p['conv_w'], [BRANCH],
                 out_dtypes=[BF16])
    dy1, dpre, dbglu = _ew_bwd("s5_glu_bwd", _f_s5_glu, [_full(r['y1']), _full(r['pre'])], [p['bglu']], [dys[0]],
                               [True, BF16])
    g['s5_b_glu'] = dbglu.reshape(-1)
    g['s5_w_glu'] = _mm("s5_glu_dw", r['y1_bf'], dpre, ta=True, out_dtype=BF16)
    dy1 = _mm("s5_glu_dx", dpre, p['wglu'], tb=True, add=dy1)
    dy0, du, dd = _ew_bwd("s5_y1_bwd", _f_s5_y1, [_full(r['hc_re']), _full(r['hc_im']), (z, 0, BRANCH)], [p['s5_d']],
                          [dy1], [BF16, False, True])
    g['s5_d'] = dd.reshape(-1)
    dh_re = _mm("s5_hc_re_dx", dy0, p['s5_cre'], tb=True)
    dh_im = _mm("s5_hc_im_dx", dy0, p['s5_cimn'], tb=True)
    take_c = lambda d: _blockdiag_take(d, S5_GROUPS, S5_STATE, S5_GROUP).transpose(0, 2, 1)
    g['s5_c_re'] = take_c(_mm("s5_hc_re_dw", r['h_re'], dy0, ta=True))
    g['s5_c_im'] = -take_c(_mm("s5_hc_im_dw", r['h_im'], dy0, ta=True))
    gb_re, gb_im, dl_re, dl_im = _scan_cplx("s5_scan_bwd", p['lam_re'], -p['lam_im'], dh_re, dh_im, reverse=True,
                                            h_re=r['h_re'], h_im=r['h_im'], bn=256)
    du = _mm("s5_bu_re_dx", gb_re, p['s5_bre'], tb=True, add=du)
    du = _mm("s5_bu_im_dx", gb_im, p['s5_bim'], tb=True, add=du, out_dtype=BF16)
    take_b = lambda d: _blockdiag_take(d, S5_GROUPS, S5_GROUP, S5_STATE).transpose(0, 2, 1).reshape(S5_N, S5_GROUP)
    dbb_re = take_b(_mm("s5_bu_re_dw", z, gb_re, ta=True, a_w=BRANCH))
    dbb_im = take_b(_mm("s5_bu_im_dw", z, gb_im, ta=True, a_w=BRANCH))
    disc = _ew_bwd("s5_disc_bwd", _f_s5_disc, [_full(a) for a in p['disc_in']], [],
                   [dl_re.reshape(S5_N, 1), dl_im.reshape(S5_N, 1), dbb_re, dbb_im], [True] * 5, tl=S5_N)
    grp = (S5_GROUPS, S5_STATE)
    g['s5_a_re'], g['s5_a_im'] = disc[0].reshape(grp), disc[1].reshape(grp)
    g['s5_log_dt'] = disc[2].reshape(grp).sum(axis=1)
    g['s5_b_re'], g['s5_b_im'] = disc[3].reshape(grp + (S5_GROUP,)), disc[4].reshape(grp + (S5_GROUP,))
    dz = jnp.concatenate([du, dxl, dgate] + dqkv + list(dzg) + [dzf], axis=1)
    dwc = _mm("z_in_dw", r['x_bf'], dz, ta=True, out_dtype=BF16)
    g['w_in'] = jnp.concatenate([dwc[:, :Z_MAIN], dwc[:, Z_FG0:Z_FG0 + FOX_HEADS], dwc[:, Z_GATE0:Z_FG0]], axis=1)
    dx = _mm("z_in_dx", dz, p['wc'], tb=True, add=dx_n)
    return dx, g, carried_q, carried_kv


def _loss_head(y, target, tl=256):
    length, width = y.shape
    tl = min(tl, length)
    nt = length // tl

    def body(y_ref, t_ref, dy_ref, loss_ref, acc_sc):
        i = pl.program_id(0)

        @pl.when(i == 0)
        def _():
            acc_sc[...] = jnp.zeros_like(acc_sc)

        err = y_ref[...] - t_ref[...]
        dy_ref[...] = err / width
        acc_sc[...] += jnp.sum(jnp.square(err), axis=0, keepdims=True)

        @pl.when(i == nt - 1)
        def _():
            total = jnp.sum(acc_sc[...], axis=1, keepdims=True) * (0.5 / width)
            loss_ref[...] = jnp.broadcast_to(total, loss_ref.shape)

    spec = pl.BlockSpec((tl, width), lambda i: (i, 0))
    dy, loss = pl.pallas_call(
        body, grid=(nt,), in_specs=[spec, spec], out_specs=[spec, pl.BlockSpec((1, LANES), lambda i: (0, 0))],
        out_shape=[jax.ShapeDtypeStruct((length, width), F32), jax.ShapeDtypeStruct((1, LANES), F32)],
        scratch_shapes=[pltpu.VMEM((1, width), F32)], compiler_params=_params("arbitrary"), name="loss_head")(y, target)
    return loss[0, 0], dy


def _sum_parts(name, parts):
    count, rows, cols = parts.shape
    tr = 256

    def body(p_ref, o_ref):
        total = p_ref[0]
        for dev in range(1, count):
            total = total + p_ref[dev]
        o_ref[...] = total

    return pl.pallas_call(body, grid=(rows // tr,), in_specs=[pl.BlockSpec((count, tr, cols), lambda i: (0, i, 0))],
                          out_specs=pl.BlockSpec((tr, cols), lambda i: (i, 0)),
                          out_shape=jax.ShapeDtypeStruct((rows, cols), F32), compiler_params=_params("parallel"),
                          name=name)(parts)


def _adamw(name, w, parts, m, v):
    rows, cols = w.shape
    count = parts[0].shape[0]
    span = rows // len(parts)
    tr = span
    for cand in (256, 128, 64, 32, 16):
        if span % cand == 0:
            tr = cand
            break
    per_span = span // tr

    def body(*refs):
        w_ref, p_refs = refs[0], refs[1:1 + len(parts)]
        m_ref, v_ref, g_ref, d_ref, m2_ref, v2_ref = refs[1 + len(parts):]
        step = pl.program_id(0)
        grad = None
        for j, p_ref in enumerate(p_refs):
            total = p_ref[0].astype(F32)
            for dev in range(1, count):
                total = total + p_ref[dev].astype(F32)
            grad = total if grad is None else jnp.where(step >= j * per_span, total, grad)
        m2 = ADAM_B1 * m_ref[...] + (1.0 - ADAM_B1) * grad
        v2 = ADAM_B2 * v_ref[...] + (1.0 - ADAM_B2) * jnp.square(grad)
        m_hat = m2 / (1.0 - ADAM_B1 ** ADAM_STEP)
        v_hat = v2 / (1.0 - ADAM_B2 ** ADAM_STEP)
        g_ref[...] = grad
        d_ref[...] = -ADAM_LR * (m_hat / (jnp.sqrt(v_hat) + ADAM_EPS) + ADAM_WD * w_ref[...])
        m2_ref[...] = m2
        v2_ref[...] = v2

    spec = pl.BlockSpec((tr, cols), lambda i: (i, 0))
    pspecs = [pl.BlockSpec((count, tr, cols),
                           lambda i, j=j: (0, jnp.minimum(jnp.maximum(i - j * per_span, 0), per_span - 1), 0))
              for j in range(len(parts))]
    shape = jax.ShapeDtypeStruct((rows, cols), F32)
    return pl.pallas_call(body, grid=(rows // tr,), in_specs=[spec] + pspecs + [spec, spec], out_specs=[spec] * 4,
                          out_shape=[shape] * 4, compiler_params=_params("arbitrary"), name=name)(w, *parts, m, v)


class _NoExchange:
    def __init__(self, layers):
        self.layers = layers

    def weights(self, l, carried):
        return self.layers[l]

    def forward_rider(self, l):
        return None

    def backward_riders(self, l):
        return None, lambda grads: None

    def collect(self, l, grads, carried_q, carried_kv):
        pass


def _forward_backward(x, target, hooks):
    prepared, saved, carried = [], [], None
    x_bf = x.astype(BF16)
    for l in range(DEPTH):
        p = _prep_layer(hooks.weights(l, carried))
        x, x_bf, r, carried = _layer_fwd(x, x_bf, p, hooks.forward_rider(l))
        prepared.append(p)
        saved.append(r)
    loss, dx = _loss_head(x, target)
    grads = [None] * DEPTH
    for l in reversed(range(DEPTH)):
        dx, grads[l], carried_q, carried_kv = _layer_bwd(dx, saved[l], prepared[l], hooks.backward_riders(l))
        hooks.collect(l, grads[l], carried_q, carried_kv)
    return loss, dx, grads


def _exchange_copies(ins, outs, sems, scatter, with_arrivals):
    send_sems, recv_sems, local_sems = sems
    x, y, c = lax.axis_index("x"), lax.axis_index("y"), lax.axis_index("c")
    me = 4 * x + 2 * y + c
    local, sends, arrivals = [], [], []
    for a in range(len(ins)):
        local.append(pltpu.make_async_copy(ins[a].at[me] if scatter else ins[a], outs[a].at[me], local_sems.at[a]))
    for k in range(1, N_DEV):
        px = 1 - x if k & 4 else x
        py = 1 - y if k & 2 else y
        pc = 1 - c if k & 1 else c
        idx = 4 * px + 2 * py + pc
        for a in range(len(ins)):
            s = a * (N_DEV - 1) + k - 1
            src = ins[a].at[idx] if scatter else ins[a]
            common = dict(src_ref=src, send_sem=send_sems.at[s], recv_sem=recv_sems.at[s], device_id=(px, py, pc),
                          device_id_type=pl.DeviceIdType.MESH)
            sends.append(pltpu.make_async_remote_copy(dst_ref=outs[a].at[me], **common))
            if with_arrivals:
                arrivals.append(pltpu.make_async_remote_copy(dst_ref=outs[a].at[idx], **common))
    return local, sends, arrivals


def _exchange_start(ins, outs, sems, scatter):
    local, sends, _ = _exchange_copies(ins, outs, sems, scatter, False)
    for cp in local + sends:
        cp.start()


def _exchange_wait(ins, outs, sems, scatter):
    local, sends, arrivals = _exchange_copies(ins, outs, sems, scatter, True)
    for cp in local:
        cp.wait()
    for cp in sends:
        cp.wait_send()
    for cp in arrivals:
        cp.wait_recv()


def _exchange_parts(arrays, scatter):
    n = len(arrays)
    hbm = [pl.BlockSpec(memory_space=pltpu.HBM)] * n
    out_shape = [jax.ShapeDtypeStruct(a.shape if scatter else (N_DEV,) + a.shape, a.dtype) for a in arrays]
    nsem = n * (N_DEV - 1)
    sems = [pltpu.SemaphoreType.DMA((nsem,)), pltpu.SemaphoreType.DMA((nsem,)), pltpu.SemaphoreType.DMA((n,))]
    return hbm, out_shape, sems


def _exchange(name, arrays, scatter):
    n = len(arrays)
    hbm, out_shape, sems = _exchange_parts(arrays, scatter)

    def body(*refs):
        ins, outs, sem_refs = refs[:n], refs[n:2 * n], refs[2 * n:]
        _exchange_start(ins, outs, sem_refs, scatter)
        _exchange_wait(ins, outs, sem_refs, scatter)

    return pl.pallas_call(body, in_specs=hbm, out_specs=hbm, out_shape=out_shape, scratch_shapes=sems,
                          name=name)(*arrays)


def _call_carrying(name, body, grid, ins, in_specs, out_shape, out_specs, scratch, semantics, rider):
    if rider is None:
        r_arrays, r_hbm, r_shape, r_sems = [], [], [], []
    else:
        r_arrays, scatter = rider
        r_hbm, r_shape, r_sems = _exchange_parts(r_arrays, scatter)
        semantics = ("arbitrary",) * len(grid)
    n_in, n_out, n_scr, n_r = len(ins), len(out_shape), len(scratch), len(r_arrays)

    def full_body(*refs):
        at = [0]

        def take(count):
            at[0] += count
            return refs[at[0] - count:at[0]]

        in_refs, r_in, out_refs, r_out, scr, r_scr = take(n_in), take(n_r), take(n_out), take(n_r), take(n_scr), take(3)
        ids = [pl.program_id(d) for d in range(len(grid))]
        if rider is not None:
            first = functools.reduce(jnp.logical_and, [i == 0 for i in ids])
            pl.when(first)(functools.partial(_exchange_start, r_in, r_out, r_scr, scatter))
        body(in_refs, out_refs, scr)
        if rider is not None:
            last = functools.reduce(jnp.logical_and, [i == g - 1 for i, g in zip(ids, grid)])
            pl.when(last)(functools.partial(_exchange_wait, r_in, r_out, r_scr, scatter))

    res = pl.pallas_call(
        full_body, grid=grid, in_specs=list(in_specs) + r_hbm, out_specs=list(out_specs) + r_hbm,
        out_shape=list(out_shape) + r_shape, scratch_shapes=list(scratch) + r_sems,
        compiler_params=_params(*semantics), name=name)(*ins, *r_arrays)
    return res[:n_out], res[n_out:]


def _shard_2d(a):
    return a.reshape(-1, a.shape[-1])


def _full_layer_weight(name, t):
    if name in ('s5_w_glu', 'w_out', 'w_ffn_down'):
        return t.reshape(-1, t.shape[-1])
    if name == 'w_branch':
        return t.transpose(1, 2, 0, 3).reshape(3, BRANCH, D_MODEL)
    return t.transpose(1, 0, 2).reshape(t.shape[1], -1)


def _split_layer_grad(name, g):
    if name in ('s5_w_glu', 'w_out', 'w_ffn_down'):
        return g.reshape(N_DEV, g.shape[0] // N_DEV, g.shape[1])
    if name == 'w_branch':
        return g.reshape(3, BRANCH, N_DEV, D_MODEL // N_DEV).transpose(2, 0, 1, 3)
    return g.reshape(g.shape[0], N_DEV, g.shape[1] // N_DEV).transpose(1, 0, 2)


RIDING = [n for n in SHARDED if n != 'lru_conv_w']
CARRIED_BY_DQ = ['w_in', 's5_w_glu', 'w_branch', 'w_out']
CARRIED_BY_DKV = ['w_ffn_gate', 'w_ffn_up', 'w_ffn_down']


class _Fsdp:
    def __init__(self, weights):
        self.weights_in = weights
        self.shard = {n: _shard_2d(weights[n]).astype(BF16) for n in RIDING}
        self.rows = {n: self.shard[n].shape[0] // DEPTH for n in RIDING}
        first = _exchange("gather_first_layer", [self.layer_shard(n, 0) for n in RIDING]
                          + [_shard_2d(weights['lru_conv_w'])], scatter=False)
        self.first = first[:-1]
        self.conv = first[-1].reshape((N_DEV,) + weights['lru_conv_w'].shape)
        self.outgoing = None
        self.incoming = {n: [None] * DEPTH for n in RIDING}

    def layer_shard(self, n, l):
        return self.shard[n][l * self.rows[n]:(l + 1) * self.rows[n]]

    def weights(self, l, carried):
        w = {n: self.weights_in[n][l] for n in REPLICATED}
        for n, t in zip(RIDING, self.first if l == 0 else carried):
            w[n] = _full_layer_weight(n, t.reshape((N_DEV,) + self.weights_in[n].shape[1:]))
        w['lru_conv_w'] = _full_layer_weight('lru_conv_w', self.conv[:, l])
        return w

    def forward_rider(self, l):
        return ([self.layer_shard(n, l + 1) for n in RIDING], False) if l + 1 < DEPTH else None

    def blocks(self, grads, names):
        return [_split_layer_grad(n, grads[n]).reshape(N_DEV, self.rows[n], -1).astype(BF16) for n in names]

    def backward_riders(self, l):
        first = None if self.outgoing is None else (self.outgoing, True)
        return first, lambda grads: (self.blocks(grads, CARRIED_BY_DKV), True)

    def collect(self, l, grads, carried_q, carried_kv):
        for n, t in zip(CARRIED_BY_DQ, carried_q):
            self.incoming[n][l + 1] = t
        for n, t in zip(CARRIED_BY_DKV, carried_kv):
            self.incoming[n][l] = t
        self.outgoing = self.blocks(grads, CARRIED_BY_DQ)

    def finish(self, grads):
        conv = jnp.stack([_split_layer_grad('lru_conv_w', grads[l]['lru_conv_w']) for l in range(DEPTH)], axis=1)
        conv = conv.reshape(N_DEV, -1, conv.shape[-1]).astype(F32)
        last = _exchange("scatter_last_layer", self.outgoing + [conv], scatter=True)
        for n, t in zip(CARRIED_BY_DQ, last[:-1]):
            self.incoming[n][0] = t
        return {**self.incoming, 'lru_conv_w': [last[-1]]}


def kernel(x, w_in, b_f, b_gate, s5_a_re, s5_a_im, s5_log_dt, s5_b_re, s5_b_im, s5_c_re, s5_c_im, s5_d, s5_w_glu, s5_b_glu, lru_conv_w, lru_conv_b, lru_w_a, lru_b_a, lru_w_x, lru_b_x, lru_lambda, w_branch, w_out, ln1_g, ln1_b, w_ffn_gate, w_ffn_up, w_ffn_down, ln2_g, ln2_b, loss_target, m_w_in, m_b_f, m_b_gate, m_s5_a_re, m_s5_a_im, m_s5_log_dt, m_s5_b_re, m_s5_b_im, m_s5_c_re, m_s5_c_im, m_s5_d, m_s5_w_glu, m_s5_b_glu, m_lru_conv_w, m_lru_conv_b, m_lru_w_a, m_lru_b_a, m_lru_w_x, m_lru_b_x, m_lru_lambda, m_w_branch, m_w_out, m_ln1_g, m_ln1_b, m_w_ffn_gate, m_w_ffn_up, m_w_ffn_down, m_ln2_g, m_ln2_b, v_w_in, v_b_f, v_b_gate, v_s5_a_re, v_s5_a_im, v_s5_log_dt, v_s5_b_re, v_s5_b_im, v_s5_c_re, v_s5_c_im, v_s5_d, v_s5_w_glu, v_s5_b_glu, v_lru_conv_w, v_lru_conv_b, v_lru_w_a, v_lru_b_a, v_lru_w_x, v_lru_b_x, v_lru_lambda, v_w_branch, v_w_out, v_ln1_g, v_ln1_b, v_w_ffn_gate, v_w_ffn_up, v_w_ffn_down, v_ln2_g, v_ln2_b):
    given = dict(locals())
    weights = {n: given[n] for n in WEIGHTS}
    moments_m = {n: given['m_' + n] for n in WEIGHTS}
    moments_v = {n: given['v_' + n] for n in WEIGHTS}

    hooks = _Fsdp(weights)
    loss_local, dx, grads = _forward_backward(x[0], loss_target[0], hooks)
    loss = lax.psum(loss_local, MESH_AXES)
    incoming = hooks.finish(grads)

    new = {}
    for n in SHARDED:
        res = _adamw("adamw_" + n, _shard_2d(weights[n]), incoming[n], _shard_2d(moments_m[n]), _shard_2d(moments_v[n]))
        new[n] = [t.reshape(weights[n].shape) for t in res]

    flat = jnp.concatenate([jnp.stack([grads[l][n] for l in range(DEPTH)]).astype(F32).reshape(-1) for n in REPLICATED])
    rows = -(-flat.shape[0] // (LANES * 256)) * 256
    packed = jnp.pad(flat, (0, rows * LANES - flat.shape[0])).reshape(rows, LANES)
    (arrived,) = _exchange("gather_small_grads", [packed], scatter=False)
    total, at = _sum_parts("sum_small_grads", arrived).reshape(-1), 0
    for n in REPLICATED:
        w2 = _shard_2d(weights[n])
        grad = total[at:at + w2.size].reshape((1,) + w2.shape)
        at += w2.size
        res = _adamw("adamw_" + n, w2, [grad], _shard_2d(moments_m[n]), _shard_2d(moments_v[n]))
        new[n] = [t.reshape(weights[n].shape) for t in res]

    return (loss, dx[None], *[new[n][0] for n in WEIGHTS], *[new[n][1] for n in WEIGHTS],
            *[new[n][2] for n in WEIGHTS], *[new[n][3] for n in WEIGHTS])
```

```python
import functools
import math

import jax
import jax.numpy as jnp
from jax import lax
from jax.experimental import pallas as pl
from jax.experimental.pallas import tpu as pltpu

F32 = jnp.float32
BF16 = jnp.bfloat16

D_MODEL = 1024
DEPTH = 4
BRANCH = 512
S5_GROUPS, S5_GROUP, S5_STATE = 32, 16, 64
S5_N = S5_GROUPS * S5_STATE
LRU_HEADS, LRU_HEAD_DIM = 8, 64
LRU_C = 8.0
CONV_WIDTH = 4
FOX_HEADS, FOX_HEAD_DIM = 8, 64
FFN_HIDDEN = 2816
ALPHA = (2.0 * DEPTH) ** 0.25
LN_EPS = 1e-5
IN_TOTAL = 6 * BRANCH + FOX_HEADS + 3 * D_MODEL
FG_PAD = 128
Z_MAIN = 6 * BRANCH
Z_GATE0 = Z_MAIN
Z_FG0 = Z_MAIN + 3 * D_MODEL
Z_TOTAL = Z_FG0 + FG_PAD
N_DEV = 8
MESH_AXES = ("x", "y", "c")

ADAM_LR, ADAM_B1, ADAM_B2, ADAM_EPS, ADAM_WD, ADAM_STEP = 0.001, 0.9, 0.999, 1e-08, 0.01, 10

VMEM_LIMIT_BYTES = 48 * 1024 * 1024
SCAN_SLACK_BYTES = 6 * 1024 * 1024
SUBLANES = 8
LANES = 128
NEG_BIG = -1e30

WEIGHTS = ['w_in', 'b_f', 'b_gate', 's5_a_re', 's5_a_im', 's5_log_dt', 's5_b_re', 's5_b_im', 's5_c_re', 's5_c_im',
           's5_d', 's5_w_glu', 's5_b_glu', 'lru_conv_w', 'lru_conv_b', 'lru_w_a', 'lru_b_a', 'lru_w_x', 'lru_b_x',
           'lru_lambda', 'w_branch', 'w_out', 'ln1_g', 'ln1_b', 'w_ffn_gate', 'w_ffn_up', 'w_ffn_down', 'ln2_g',
           'ln2_b']
SHARDED = ['w_in', 's5_w_glu', 'lru_conv_w', 'w_branch', 'w_out', 'w_ffn_gate', 'w_ffn_up', 'w_ffn_down']
REPLICATED = [n for n in WEIGHTS if n not in SHARDED]


def _params(*sem, vmem=VMEM_LIMIT_BYTES):
    return pltpu.CompilerParams(dimension_semantics=sem, vmem_limit_bytes=vmem)


def _tile(dim, want):
    if dim % LANES:
        return dim
    t = min(want, dim) // LANES * LANES
    while dim % t:
        t -= LANES
    return t


MM_VMEM_BUDGET_BYTES = 30 * 1024 * 1024
MM_MAX_TILE = 1024


def _divisor_tiles(dim, cap, must_divide=0):
    if dim % LANES:
        return [dim]
    out = [t for t in range(min(cap, dim) // LANES * LANES, 0, -LANES) if dim % t == 0 and must_divide % t == 0]
    return out or [dim]


def _mm_tiles(m, n, k, a_bytes, b_bytes, o_bytes, has_add, m_c0, n_c0, k_c0):
    for tk in _divisor_tiles(k, k, k_c0):
        best = None
        for tm in _divisor_tiles(m, MM_MAX_TILE, m_c0):
            for tn in _divisor_tiles(n, MM_MAX_TILE, n_c0):
                used = 2 * (tm * tk * a_bytes + tk * tn * b_bytes + tm * tn * o_bytes) + tm * tn * 4
                used += tm * tn * 4 if tk < k else 0
                used += 2 * tm * tn * 4 if has_add else 0
                if used <= MM_VMEM_BUDGET_BYTES and (best is None or tm * tn / (tm + tn) > best[0]):
                    best = (tm * tn / (tm + tn), tm, tn)
        if best is not None and (min(best[1], best[2]) >= 256 or tk <= 512):
            return best[1], best[2], tk
    raise ValueError("no matmul tiling fits VMEM")


def _mm(name, a, b, *, ta=False, tb=False, a_c0=0, a_w=None, b_c0=0, b_w=None, add=None, out_dtype=F32):
    a_w = a.shape[1] if a_w is None else a_w
    b_w = b.shape[1] if b_w is None else b_w
    m, k = (a_w, a.shape[0]) if ta else (a.shape[0], a_w)
    n = b.shape[0] if tb else b_w
    assert k == (b_w if tb else b.shape[0]), (name, a.shape, b.shape)
    tm, tn, tk = _mm_tiles(m, n, k, a.dtype.itemsize, b.dtype.itemsize, jnp.dtype(out_dtype).itemsize,
                           add is not None, a_c0 if ta else 0, 0 if tb else b_c0,
                           math.gcd(0 if ta else a_c0, b_c0 if tb else 0))
    nk = k // tk
    a_off = a_c0 // (tm if ta else tk)
    b_off = b_c0 // (tk if tb else tn)
    assert a_c0 % (tm if ta else tk) == 0 and b_c0 % (tk if tb else tn) == 0, name
    dims = (((0 if ta else 1,), (1 if tb else 0,)), ((), ()))
    a_total, b_total = m * k * a.dtype.itemsize, n * k * b.dtype.itemsize
    a_stays = a_total + b_total * (m // tm) <= b_total + a_total * (n // tn)
    if nk > 1:
        a_stays = True

    def mn(o, i):
        return (o, i) if a_stays else (i, o)

    def body(*refs):
        a_ref, b_ref = refs[0], refs[1]
        add_ref = refs[2] if add is not None else None
        o_ref = refs[3] if add is not None else refs[2]
        part = lax.dot_general(a_ref[...].astype(BF16), b_ref[...].astype(BF16), dims, preferred_element_type=F32)

        def finish(r):
            if add is not None:
                r = r + add_ref[...]
            o_ref[...] = r.astype(o_ref.dtype)

        if nk == 1:
            finish(part)
            return
        acc_ref = refs[-1]
        kk = pl.program_id(2)

        @pl.when(kk == 0)
        def _():
            acc_ref[...] = part

        @pl.when(kk > 0)
        def _():
            acc_ref[...] += part

        @pl.when(kk == nk - 1)
        def _():
            finish(acc_ref[...])

    def a_map(o, i, kk):
        im = mn(o, i)[0]
        return (kk, im + a_off) if ta else (im, kk + a_off)

    def b_map(o, i, kk):
        jn = mn(o, i)[1]
        return (jn, kk + b_off) if tb else (kk, jn + b_off)

    a_spec = pl.BlockSpec((tk, tm) if ta else (tm, tk), a_map)
    b_spec = pl.BlockSpec((tn, tk) if tb else (tk, tn), b_map)
    o_spec = pl.BlockSpec((tm, tn), lambda o, i, kk: mn(o, i))
    ins, in_specs = [a, b], [a_spec, b_spec]
    if add is not None:
        ins.append(add)
        in_specs.append(o_spec)
    grid = (m // tm, n // tn, nk) if a_stays else (n // tn, m // tm, nk)
    return pl.pallas_call(
        body, grid=grid, in_specs=in_specs, out_specs=o_spec, out_shape=jax.ShapeDtypeStruct((m, n), out_dtype),
        scratch_shapes=[pltpu.VMEM((tm, tn), F32)] if nk > 1 else [],
        compiler_params=_params("parallel", "parallel", "arbitrary"), name=name)(*ins)


def _row_spec(tl, c0, w):
    assert c0 % w == 0
    return pl.BlockSpec((tl, w), lambda i: (i, c0 // w))


def _whole_spec(p):
    return pl.BlockSpec(p.shape, lambda i: (0,) * p.ndim)


def _ew(name, f, rows, prm, out_ws, tl=256, out_dtypes=None):
    out_dtypes = out_dtypes or [F32] * len(out_ws)
    nrows, nprm = len(rows), len(prm)
    length = rows[0][0].shape[0]
    tl = min(tl, length)

    def body(*refs):
        vals = [r[...] for r in refs[:nrows + nprm]]
        outs = f(*vals)
        for o_ref, o in zip(refs[nrows + nprm:], outs):
            o_ref[...] = o.astype(o_ref.dtype)

    return pl.pallas_call(
        body, grid=(length // tl,),
        in_specs=[_row_spec(tl, c0, w) for (_, c0, w) in rows] + [_whole_spec(p) for p in prm],
        out_specs=[_row_spec(tl, 0, w) for w in out_ws],
        out_shape=[jax.ShapeDtypeStruct((length, w), dt) for w, dt in zip(out_ws, out_dtypes)],
        compiler_params=_params("parallel"), name=name)(*[r[0] for r in rows], *prm)


def _ew_bwd(name, f, rows, prm, douts, row_grad, tl=256):
    nrows, nprm, nd = len(rows), len(prm), len(douts)
    length = rows[0][0].shape[0]
    tl = min(tl, length)
    want = [i for i in range(nrows) if row_grad[i]]

    def body(*refs):
        vals = [r[...] for r in refs[:nrows + nprm]]
        cts = tuple(r[...] for r in refs[nrows + nprm:nrows + nprm + nd])
        out_refs = refs[nrows + nprm + nd:]
        _, vjp = jax.vjp(lambda *v: tuple(f(*v)), *vals)
        grads = vjp(cts)
        for o_ref, i in zip(out_refs[:len(want)], want):
            o_ref[...] = grads[i].astype(o_ref.dtype)

        @pl.when(pl.program_id(0) == 0)
        def _():
            for o_ref in out_refs[len(want):]:
                o_ref[...] = jnp.zeros_like(o_ref)

        for o_ref, g in zip(out_refs[len(want):], grads[nrows:]):
            o_ref[...] += g

    return pl.pallas_call(
        body, grid=(length // tl,),
        in_specs=([_row_spec(tl, c0, w) for (_, c0, w) in rows] + [_whole_spec(p) for p in prm]
                  + [_row_spec(tl, 0, d.shape[1]) for d in douts]),
        out_specs=[_row_spec(tl, 0, rows[i][2]) for i in want] + [_whole_spec(p) for p in prm],
        out_shape=([jax.ShapeDtypeStruct((length, rows[i][2]), F32 if row_grad[i] is True else row_grad[i])
                    for i in want]
                   + [jax.ShapeDtypeStruct(p.shape, F32) for p in prm]),
        compiler_params=_params("arbitrary"), name=name)(*[r[0] for r in rows], *prm, *douts)


def _row_ids(width):
    return lax.broadcasted_iota(jnp.int32, (SUBLANES, width), 0)


def _shift_rows(v, d, reverse):
    return pltpu.roll(v, (SUBLANES - d) if reverse else d, 0)


def _scan_real(name, a, b, b2=None, *, reverse=False, bn=256):
    length, n = a.shape
    bn = _tile(n, bn)
    nb = length // SUBLANES
    operands = [a, b] if b2 is None else [a, b, b2]

    def body(*refs):
        a_ref, b_ref, h_ref = refs[0], refs[1], refs[-1]
        rows = _row_ids(bn)

        def step(it, carry):
            i = (nb - 1 - it) if reverse else it
            sl = pl.ds(pl.multiple_of(i * SUBLANES, SUBLANES), SUBLANES)
            av, bv = a_ref[sl, :], b_ref[sl, :]
            if b2 is not None:
                bv = bv + refs[2][sl, :]
            for d in (1, 2, 4):
                live = (rows < SUBLANES - d) if reverse else (rows >= d)
                a_in = jnp.where(live, _shift_rows(av, d, reverse), 1.0)
                b_in = jnp.where(live, _shift_rows(bv, d, reverse), 0.0)
                bv = bv + av * b_in
                av = av * a_in
            hv = bv + av * carry
            h_ref[sl, :] = hv
            edge = hv[0:1, :] if reverse else hv[SUBLANES - 1:SUBLANES, :]
            return jnp.broadcast_to(edge, (SUBLANES, bn))

        lax.fori_loop(0, nb, step, jnp.zeros((SUBLANES, bn), F32))

    spec = pl.BlockSpec((length, bn), lambda j: (0, j))
    return pl.pallas_call(body, grid=(n // bn,), in_specs=[spec] * len(operands), out_specs=spec,
                          out_shape=jax.ShapeDtypeStruct((length, n), F32),
                          compiler_params=_params("parallel"), name=name)(*operands)


def _cmul(ar, ai, br, bi):
    return ar * br - ai * bi, ar * bi + ai * br


def _scan_cplx(name, lam_re, lam_im, x_re, x_im, *, reverse=False, h_re=None, h_im=None, bn=128):
    length, n = x_re.shape
    bn = _tile(n, bn)
    nb = length // SUBLANES
    with_dot = h_re is not None

    def body(*refs):
        if with_dot:
            lr_ref, li_ref, xr_ref, xi_ref, hr_ref, hi_ref, gr_ref, gi_ref, dr_ref, di_ref = refs
        else:
            lr_ref, li_ref, xr_ref, xi_ref, gr_ref, gi_ref = refs
        rows = _row_ids(bn)
        lr = jnp.broadcast_to(lr_ref[...], (SUBLANES, bn))
        li = jnp.broadcast_to(li_ref[...], (SUBLANES, bn))
        powers = [(lr, li)]
        for _ in range(SUBLANES - 1):
            powers.append(_cmul(powers[-1][0], powers[-1][1], lr, li))
        zero = jnp.zeros((SUBLANES, bn), F32)
        steps = []
        for d in (1, 2, 4):
            live = (rows < SUBLANES - d) if reverse else (rows >= d)
            steps.append((d, jnp.where(live, powers[d - 1][0], 0.0), jnp.where(live, powers[d - 1][1], 0.0)))
        cr, ci = zero, zero
        for r in range(SUBLANES):
            e = (SUBLANES - r) if reverse else (r + 1)
            cr = jnp.where(rows == r, powers[e - 1][0], cr)
            ci = jnp.where(rows == r, powers[e - 1][1], ci)

        def step(it, carry):
            i = (nb - 1 - it) if reverse else it
            sl = pl.ds(pl.multiple_of(i * SUBLANES, SUBLANES), SUBLANES)
            vr, vi = xr_ref[sl, :], xi_ref[sl, :]
            for d, pr, pi in steps:
                sr, si = _cmul(pr, pi, _shift_rows(vr, d, reverse), _shift_rows(vi, d, reverse))
                vr, vi = vr + sr, vi + si
            kr, ki = _cmul(cr, ci, carry[0], carry[1])
            vr, vi = vr + kr, vi + ki
            gr_ref[sl, :] = vr
            gi_ref[sl, :] = vi
            er = vr[0:1, :] if reverse else vr[SUBLANES - 1:SUBLANES, :]
            ei = vi[0:1, :] if reverse else vi[SUBLANES - 1:SUBLANES, :]
            new = (jnp.broadcast_to(er, (SUBLANES, bn)), jnp.broadcast_to(ei, (SUBLANES, bn)))
            if not with_dot:
                return new
            prev = pl.ds(pl.multiple_of(jnp.maximum(i - 1, 0) * SUBLANES, SUBLANES), SUBLANES)
            keep = jnp.where(i > 0, 1.0, 0.0)
            pr_ = jnp.broadcast_to(hr_ref[prev, :][SUBLANES - 1:SUBLANES, :], (SUBLANES, bn)) * keep
            pi_ = jnp.broadcast_to(hi_ref[prev, :][SUBLANES - 1:SUBLANES, :], (SUBLANES, bn)) * keep
            hr = jnp.where(rows == 0, pr_, pltpu.roll(hr_ref[sl, :], 1, 0))
            hi = jnp.where(rows == 0, pi_, pltpu.roll(hi_ref[sl, :], 1, 0))
            return new + (carry[2] + vr * hr + vi * hi, carry[3] + vi * hr - vr * hi)

        init = (zero, zero, zero, zero) if with_dot else (zero, zero)
        out = lax.fori_loop(0, nb, step, init)
        if with_dot:
            dr_ref[...] = jnp.sum(out[2], axis=0, keepdims=True)
            di_ref[...] = jnp.sum(out[3], axis=0, keepdims=True)

    col = pl.BlockSpec((length, bn), lambda j: (0, j))
    vec = pl.BlockSpec((1, bn), lambda j: (0, j))
    ins = [lam_re, lam_im, x_re, x_im] + ([h_re, h_im] if with_dot else [])
    in_specs = [vec, vec, col, col] + ([col, col] if with_dot else [])
    out_specs = [col, col] + ([vec, vec] if with_dot else [])
    full = jax.ShapeDtypeStruct((length, n), F32)
    row = jax.ShapeDtypeStruct((1, n), F32)
    out_shape = [full, full] + ([row, row] if with_dot else [])
    columns = 6 if with_dot else 4
    return pl.pallas_call(body, grid=(n // bn,), in_specs=in_specs, out_specs=out_specs, out_shape=out_shape,
                          compiler_params=_params("parallel", vmem=2 * columns * length * bn * 4 + SCAN_SLACK_BYTES),
                          name=name)(*ins)


FOX_SCALE = FOX_HEAD_DIM ** -0.5
FOX_AUG = 128
FOX_CQ0 = FOX_HEAD_DIM
FOX_CK0 = FOX_HEAD_DIM + 3
NT = (((1,), (1,)), ((), ()))


def _fox_logits_t(ka, qa, on_diagonal):
    st = lax.dot_general(ka, qa, NT, preferred_element_type=F32)
    if on_diagonal:
        key = lax.broadcasted_iota(jnp.int32, st.shape, 0)
        query = lax.broadcasted_iota(jnp.int32, st.shape, 1)
        st = jnp.where(key <= query, st, NEG_BIG)
    return st


def _fox_pair(s, nt, q_first):
    if q_first:
        qi = sum((s >= (m * (m + 1)) // 2).astype(jnp.int32) for m in range(1, nt))
        return qi, s - ((qi * (qi + 1)) >> 1)
    ki = sum((s >= m * nt - (m * (m - 1)) // 2).astype(jnp.int32) for m in range(1, nt))
    return ki + s - (ki * nt - ((ki * (ki - 1)) >> 1)), ki


def _fox_specs(t, nt, q_first):
    q_idx = lambda s: _fox_pair(s, nt, q_first)[0]
    k_idx = lambda s: _fox_pair(s, nt, q_first)[1]
    rows = lambda idx, w: pl.BlockSpec((None, t, w), lambda h, s: (h, idx(s), 0))
    cols = lambda idx, w: pl.BlockSpec((None, w, t), lambda h, s: (h, 0, idx(s)))
    return rows, cols, q_idx, k_idx, (nt * (nt + 1)) // 2


def _fox_fwd(qa, ka, vt, t=512, rider=None):
    heads, length, _ = qa.shape
    dh = vt.shape[1]
    t = min(t, length)
    nt = length // t

    def body(in_refs, out_refs, scratch_refs):
        (qa_ref, ka_ref, vt_ref), (o_ref, lse_ref), (m_sc, l_sc, acc_sc) = in_refs, out_refs, scratch_refs
        qi, ki = _fox_pair(pl.program_id(1), nt, True)

        @pl.when(ki == 0)
        def _():
            m_sc[...] = jnp.full_like(m_sc, NEG_BIG)
            l_sc[...] = jnp.zeros_like(l_sc)
            acc_sc[...] = jnp.zeros_like(acc_sc)

        def step(on_diagonal):
            st = _fox_logits_t(ka_ref[...], qa_ref[...], on_diagonal)
            m_old = m_sc[...]
            m_new = jnp.maximum(m_old, jnp.max(st, axis=0, keepdims=True))
            pt = jnp.exp(st - m_new)
            scale = jnp.exp(m_old - m_new)
            l_sc[...] = scale * l_sc[...] + jnp.sum(pt, axis=0, keepdims=True)
            acc_sc[...] = scale * acc_sc[...] + jnp.dot(vt_ref[...], pt.astype(BF16), preferred_element_type=F32)
            m_sc[...] = m_new

        pl.when(ki < qi)(functools.partial(step, False))
        pl.when(ki == qi)(functools.partial(step, True))

        @pl.when(ki == qi)
        def _():
            o_ref[...] = acc_sc[...] / l_sc[...]
            lse_ref[...] = m_sc[...] + jnp.log(l_sc[...])

    rows, cols, q_idx, k_idx, pairs = _fox_specs(t, nt, True)
    (ot, lse), carried = _call_carrying(
        "fox_fwd" if rider is None else "fox_fwd_carrying", body, (heads, pairs), [qa, ka, vt],
        [rows(q_idx, FOX_AUG), rows(k_idx, FOX_AUG), cols(k_idx, dh)],
        [jax.ShapeDtypeStruct((heads, dh, length), F32), jax.ShapeDtypeStruct((heads, 1, length), F32)],
        [cols(q_idx, dh), cols(q_idx, 1)],
        [pltpu.VMEM((1, t), F32), pltpu.VMEM((1, t), F32), pltpu.VMEM((dh, t), F32)],
        ("parallel", "arbitrary"), rider)
    return ot, lse, carried


def _fox_ds_t(qa_ref, ka_ref, v_ref, dot_ref, ot_ref, lse_ref, on_diagonal):
    pt = jnp.exp(_fox_logits_t(ka_ref[...], qa_ref[...], on_diagonal) - lse_ref[...])
    dpt = jnp.dot(v_ref[...], dot_ref[...], preferred_element_type=F32)
    delta = jnp.sum(dot_ref[...].astype(F32) * ot_ref[...], axis=0, keepdims=True)
    return pt, pt * (dpt - delta)


def _fox_bwd_q(qa, ka, kat, v, dot, ot, lse, t=512, rider=None):
    heads, length, _ = qa.shape
    dh = v.shape[2]
    t = min(t, length)
    nt = length // t

    def body(in_refs, out_refs, scratch_refs):
        qa_ref, ka_ref, kat_ref, v_ref, dot_ref, ot_ref, lse_ref = in_refs
        (dqa_ref,), (acc_sc,) = out_refs, scratch_refs
        qi, ki = _fox_pair(pl.program_id(1), nt, True)

        @pl.when(ki == 0)
        def _():
            acc_sc[...] = jnp.zeros_like(acc_sc)

        def step(on_diagonal):
            _, dst = _fox_ds_t(qa_ref, ka_ref, v_ref, dot_ref, ot_ref, lse_ref, on_diagonal)
            acc_sc[...] += jnp.dot(kat_ref[...], dst.astype(BF16), preferred_element_type=F32)

        pl.when(ki < qi)(functools.partial(step, False))
        pl.when(ki == qi)(functools.partial(step, True))

        @pl.when(ki == qi)
        def _():
            dqa_ref[...] = acc_sc[...]

    rows, cols, q_idx, k_idx, pairs = _fox_specs(t, nt, True)
    (dqat,), carried = _call_carrying(
        "fox_bwd_q" if rider is None else "fox_bwd_q_carrying", body, (heads, pairs), [qa, ka, kat, v, dot, ot, lse],
        [rows(q_idx, FOX_AUG), rows(k_idx, FOX_AUG), cols(k_idx, FOX_AUG), rows(k_idx, dh), cols(q_idx, dh),
         cols(q_idx, dh), cols(q_idx, 1)],
        [jax.ShapeDtypeStruct((heads, FOX_AUG, length), F32)], [cols(q_idx, FOX_AUG)],
        [pltpu.VMEM((FOX_AUG, t), F32)], ("parallel", "arbitrary"), rider)
    return dqat, carried


def _fox_bwd_kv(qa, ka, v, do, dot, ot, lse, t=512, rider=None):
    heads, length, _ = qa.shape
    dh = v.shape[2]
    t = min(t, length)
    nt = length // t

    def body(in_refs, out_refs, scratch_refs):
        qa_ref, ka_ref, v_ref, do_ref, dot_ref, ot_ref, lse_ref = in_refs
        (dk_ref, dv_ref, dc_ref), (dka_sc, dv_sc) = out_refs, scratch_refs
        qi, ki = _fox_pair(pl.program_id(1), nt, False)

        @pl.when(qi == ki)
        def _():
            dka_sc[...] = jnp.zeros_like(dka_sc)
            dv_sc[...] = jnp.zeros_like(dv_sc)

        def step(on_diagonal):
            pt, dst = _fox_ds_t(qa_ref, ka_ref, v_ref, dot_ref, ot_ref, lse_ref, on_diagonal)
            dv_sc[...] += jnp.dot(pt.astype(BF16), do_ref[...], preferred_element_type=F32)
            dka_sc[...] += jnp.dot(dst.astype(BF16), qa_ref[...], preferred_element_type=F32)

        pl.when(qi > ki)(functools.partial(step, False))
        pl.when(qi == ki)(functools.partial(step, True))

        @pl.when(qi == nt - 1)
        def _():
            dka = dka_sc[...]
            lane = lax.broadcasted_iota(jnp.int32, dka.shape, 1)
            dk_ref[...] = dka_sc[:, :dh]
            dc_ref[...] = jnp.sum(jnp.where(lane == FOX_CK0, dka, 0.0), axis=1, keepdims=True)
            dv_ref[...] = dv_sc[...]

    rows, cols, q_idx, k_idx, pairs = _fox_specs(t, nt, False)
    big = jax.ShapeDtypeStruct((heads, length, dh), F32)
    (dk, dv, dc), carried = _call_carrying(
        "fox_bwd_kv" if rider is None else "fox_bwd_kv_carrying", body, (heads, pairs), [qa, ka, v, do, dot, ot, lse],
        [rows(q_idx, FOX_AUG), rows(k_idx, FOX_AUG), rows(k_idx, dh), rows(q_idx, dh), cols(q_idx, dh),
         cols(q_idx, dh), cols(q_idx, 1)],
        [big, big, jax.ShapeDtypeStruct((heads, length, 1), F32)], [rows(k_idx, dh), rows(k_idx, dh), rows(k_idx, 1)],
        [pltpu.VMEM((t, FOX_AUG), F32), pltpu.VMEM((t, dh), F32)], ("parallel", "arbitrary"), rider)
    return dk, dv, dc, carried


def _split3(x):
    hi = lax.reduce_precision(x, 8, 7)
    mid = lax.reduce_precision(x - hi, 8, 7)
    return [hi, mid, lax.reduce_precision(x - hi - mid, 8, 7)]


def _fox_operands(z, cum):
    length = z.shape[0]
    parts = jnp.stack(_split3(cum[:, :FOX_HEADS].T), axis=-1)
    ones = jnp.ones_like(parts)
    pad = jnp.zeros((FOX_HEADS, length, FOX_AUG - FOX_HEAD_DIM - 6), F32)
    qa = jnp.concatenate([_heads(z, 3 * BRANCH) * FOX_SCALE, parts, ones, pad], axis=-1).astype(BF16)
    ka = jnp.concatenate([_heads(z, 4 * BRANCH), ones, -parts, pad], axis=-1).astype(BF16)
    v = _heads(z, 5 * BRANCH).astype(BF16)
    return qa, ka, ka.transpose(0, 2, 1), v, v.transpose(0, 2, 1)


def _softplus(x):
    return jnp.maximum(x, 0.0) + jnp.log1p(jnp.exp(-jnp.abs(x)))


def _f_s5_disc(a_re, a_im, log_dt, b_re, b_im):
    dt = jnp.exp(log_dt)
    mag = jnp.exp(a_re * dt)
    lr, li = mag * jnp.cos(a_im * dt), mag * jnp.sin(a_im * dt)
    den = a_re * a_re + a_im * a_im
    qr = ((lr - 1.0) * a_re + li * a_im) / den
    qi = (li * a_re - (lr - 1.0) * a_im) / den
    return lr, li, qr * b_re - qi * b_im, qr * b_im + qi * b_re


def _f_s5_y1(hc_re, hc_im, u, d):
    return (jax.nn.gelu(hc_re + hc_im + d * u),)


def _f_s5_glu(y1, pre, b):
    return (y1 * jax.nn.sigmoid(pre + b),)


def _f_conv(x0, x1, x2, x3, w0, w1, w2, w3, b):
    return (b + w0 * x0 + w1 * x1 + w2 * x2 + w3 * x3,)


def _f_conv_t(d0, d1, d2, d3, w0, w1, w2, w3):
    return (w0 * d0 + w1 * d1 + w2 * d2 + w3 * d3,)


def _lru_coeffs(xc, pa, px, b_a, b_x, lam):
    r = jax.nn.sigmoid(pa + b_a)
    i = jax.nn.sigmoid(px + b_x)
    log_a = -LRU_C * _softplus(-lam) * r
    a = jnp.exp(log_a)
    mult = jnp.sqrt(-jnp.tanh(log_a) * (a * a + 1.0))
    return a, mult * (i * xc)


def _f_lru_gates(xc, pa, px, b_a, b_x, lam):
    return _lru_coeffs(xc, pa, px, b_a, b_x, lam)


def _f_lru_step(xc, pa, px, h_prev, b_a, b_x, lam):
    a, b = _lru_coeffs(xc, pa, px, b_a, b_x, lam)
    return (a * h_prev + b,)


def _f_lru_out(gate, h):
    return (jax.nn.gelu(gate) * h,)


def _f_logf(zf, bf):
    return (-_softplus(-(zf + bf)),)


def _f_merge(p0, p1, p2, z0, z1, z2, b0, b1, b2):
    return (jax.nn.sigmoid(z0 + b0) * p0 + jax.nn.sigmoid(z1 + b1) * p1 + jax.nn.sigmoid(z2 + b2) * p2,)


def _f_ln(x, r, g, b):
    s = ALPHA * x + r
    mu = jnp.mean(s, axis=-1, keepdims=True)
    var = jnp.mean(jnp.square(s - mu), axis=-1, keepdims=True)
    return ((s - mu) * lax.rsqrt(var + LN_EPS) * g + b,)


def _f_swiglu(hg, hu):
    return (jax.nn.silu(hg) * hu,)


def _full(a):
    return (a, 0, a.shape[1])


def _blockdiag(t):
    g, a, b = t.shape
    eye = jnp.eye(g, dtype=t.dtype)
    return (t[:, :, None, :] * eye[:, None, :, None]).reshape(g * a, g * b)


def _blockdiag_take(d, g, a, b):
    eye = jnp.eye(g, dtype=d.dtype)
    return (d.reshape(g, a, g, b) * eye[:, None, :, None]).sum(axis=2)


def _delay(a, j):
    return a if j == 0 else jnp.pad(a, ((j, 0), (0, 0)))[:a.shape[0]]


def _advance(a, j):
    return a if j == 0 else jnp.pad(a, ((0, j), (0, 0)))[j:]


def _heads(a, c0=0):
    length = a.shape[0]
    return a[:, c0:c0 + BRANCH].reshape(length, FOX_HEADS, FOX_HEAD_DIM).transpose(1, 0, 2)


def _unheads(a):
    return a.transpose(1, 0, 2).reshape(a.shape[1], BRANCH)


def _row(v):
    return v.reshape(1, -1).astype(F32)


def _col(v):
    return v.reshape(-1, 1).astype(F32)


def _prep_layer(w):
    p = {}
    w_in = w['w_in']
    p['wc'] = jnp.concatenate(
        [w_in[:, :Z_MAIN], w_in[:, Z_MAIN + FOX_HEADS:], w_in[:, Z_MAIN:Z_MAIN + FOX_HEADS],
         jnp.zeros((D_MODEL, FG_PAD - FOX_HEADS), w_in.dtype)], axis=1)
    p['b_f'] = jnp.pad(_row(w['b_f']), ((0, 0), (0, FG_PAD - FOX_HEADS)))
    p['b_gate'] = [_row(w['b_gate'][k * D_MODEL:(k + 1) * D_MODEL]) for k in range(3)]
    p['disc_in'] = [_col(w['s5_a_re']), _col(w['s5_a_im']), _col(jnp.repeat(w['s5_log_dt'], S5_STATE)),
                    w['s5_b_re'].reshape(S5_N, S5_GROUP), w['s5_b_im'].reshape(S5_N, S5_GROUP)]
    lam_re, lam_im, bb_re, bb_im = _ew("s5_disc", _f_s5_disc, [_full(a) for a in p['disc_in']], [],
                                       [1, 1, S5_GROUP, S5_GROUP], tl=S5_N)
    p['lam_re'], p['lam_im'] = lam_re.reshape(1, S5_N), lam_im.reshape(1, S5_N)
    to_blk = lambda t: _blockdiag(t.reshape(S5_GROUPS, S5_STATE, S5_GROUP).transpose(0, 2, 1)).astype(BF16)
    p['s5_bre'], p['s5_bim'] = to_blk(bb_re), to_blk(bb_im)
    p['s5_cre'] = _blockdiag(w['s5_c_re'].transpose(0, 2, 1)).astype(BF16)
    p['s5_cimn'] = _blockdiag(-w['s5_c_im'].transpose(0, 2, 1)).astype(BF16)
    p['s5_d'], p['wglu'], p['bglu'] = _row(w['s5_d']), w['s5_w_glu'], _row(w['s5_b_glu'])
    p['conv_w'] = [_row(w['lru_conv_w'][CONV_WIDTH - 1 - j]) for j in range(CONV_WIDTH)]
    p['conv_b'] = _row(w['lru_conv_b'])
    p['wax'] = jnp.concatenate([_blockdiag(w['lru_w_a']), _blockdiag(w['lru_w_x'])], axis=1).astype(BF16)
    p['b_a'], p['b_x'], p['lam'] = _row(w['lru_b_a']), _row(w['lru_b_x']), _row(w['lru_lambda'])
    p['wb'] = [w['w_branch'][k] for k in range(3)]
    p['wout'] = w['w_out']
    p['ln1'] = [_row(w['ln1_g']), _row(w['ln1_b'])]
    p['wgu'] = jnp.concatenate([w['w_ffn_gate'], w['w_ffn_up']], axis=1)
    p['wd'] = w['w_ffn_down']
    p['ln2'] = [_row(w['ln2_g']), _row(w['ln2_b'])]
    return p


def _with_copy(f):
    def g(*args):
        (y,) = f(*args)
        return y, y
    return g


def _layer_fwd(x, x_bf, p, rider=None):
    length = x.shape[0]
    r = {'x': x, 'x_bf': x_bf}
    z = r['z'] = _mm("z_in", x_bf, p['wc'])
    bu_re = _mm("s5_bu_re", z, p['s5_bre'], a_w=BRANCH)
    bu_im = _mm("s5_bu_im", z, p['s5_bim'], a_w=BRANCH)
    r['h_re'], r['h_im'] = _scan_cplx("s5_scan", p['lam_re'], p['lam_im'], bu_re, bu_im, bn=256)
    r['hc_re'] = _mm("s5_hc_re", r['h_re'], p['s5_cre'])
    r['hc_im'] = _mm("s5_hc_im", r['h_im'], p['s5_cimn'])
    r['y1'], r['y1_bf'] = _ew("s5_y1", _with_copy(_f_s5_y1), [_full(r['hc_re']), _full(r['hc_im']), (z, 0, BRANCH)],
                              [p['s5_d']], [BRANCH, BRANCH], out_dtypes=[F32, BF16])
    r['pre'] = _mm("s5_glu_pre", r['y1_bf'], p['wglu'])
    (r['ys5'],) = _ew("s5_glu", _f_s5_glu, [_full(r['y1']), _full(r['pre'])], [p['bglu']], [BRANCH], out_dtypes=[BF16])
    xl = z[:, BRANCH:2 * BRANCH]
    r['xd'] = [_delay(xl, j) for j in range(1, CONV_WIDTH)]
    r['xc'], r['xc_bf'] = _ew("lru_conv", _with_copy(_f_conv), [(z, BRANCH, BRANCH)] + [_full(a) for a in r['xd']],
                              p['conv_w'] + [p['conv_b']], [BRANCH, BRANCH], out_dtypes=[F32, BF16])
    r['papx'] = _mm("lru_gate_mm", r['xc_bf'], p['wax'])
    r['a'], b = _ew("lru_gates", _f_lru_gates, [_full(r['xc']), (r['papx'], 0, BRANCH), (r['papx'], BRANCH, BRANCH)],
                    [p['b_a'], p['b_x'], p['lam']], [BRANCH, BRANCH])
    r['h'] = _scan_real("lru_scan", r['a'], b)
    (r['ylru'],) = _ew("lru_out", _f_lru_out, [(z, 2 * BRANCH, BRANCH), _full(r['h'])], [], [BRANCH], out_dtypes=[BF16])
    (logf,) = _ew("fox_logf", _f_logf, [(z, Z_FG0, FG_PAD)], [p['b_f']], [FG_PAD])
    cum = _scan_real("fox_cum", jnp.ones((length, FG_PAD), F32), logf)
    qa, ka, kat, v, vt = _fox_operands(z, cum)
    r['fox'] = (qa, ka, kat, v)
    r['ot'], r['lse'], carried = _fox_fwd(qa, ka, vt, rider=rider)
    r['yfox'] = r['ot'].reshape(BRANCH, length).T.astype(BF16)
    ys = [r['ys5'], r['ylru'], r['yfox']]
    r['proj'] = [_mm("proj_%d" % k, ys[k], p['wb'][k]) for k in range(3)]
    gate_rows = [(z, Z_GATE0 + k * D_MODEL, D_MODEL) for k in range(3)]
    (r['mix'],) = _ew("merge", _f_merge, [_full(a) for a in r['proj']] + gate_rows, p['b_gate'], [D_MODEL], tl=128,
                      out_dtypes=[BF16])
    r['mixed'] = _mm("w_out", r['mix'], p['wout'])
    two = dict(out_ws=[D_MODEL, D_MODEL], out_dtypes=[F32, BF16])
    x1, r['x1_bf'] = _ew("ln1", _with_copy(_f_ln), [_full(x), _full(r['mixed'])], p['ln1'], **two)
    r['x1'] = x1
    r['hgu'] = _mm("ffn_in", r['x1_bf'], p['wgu'])
    (r['hid'],) = _ew("swiglu", _f_swiglu, [(r['hgu'], 0, FFN_HIDDEN), (r['hgu'], FFN_HIDDEN, FFN_HIDDEN)], [],
                      [FFN_HIDDEN], tl=128, out_dtypes=[BF16])
    r['f'] = _mm("ffn_out", r['hid'], p['wd'])
    x2, x2_bf = _ew("ln2", _with_copy(_f_ln), [_full(x1), _full(r['f'])], p['ln2'], **two)
    return x2, x2_bf, r, carried


def _layer_bwd(dx2, r, p, riders=(None, lambda grads: None)):
    g = {}
    x, z, x1 = r['x'], r['z'], r['x1']
    dx1_n, df, g['ln2_g'], g['ln2_b'] = _ew_bwd("ln2_bwd", _f_ln, [_full(x1), _full(r['f'])], p['ln2'], [dx2],
                                                [True, BF16])
    dhid = _mm("ffn_out_dx", df, p['wd'], tb=True)
    g['w_ffn_down'] = _mm("ffn_out_dw", r['hid'], df, ta=True, out_dtype=BF16)
    hgu_rows = [(r['hgu'], 0, FFN_HIDDEN), (r['hgu'], FFN_HIDDEN, FFN_HIDDEN)]
    dhg, dhu = _ew_bwd("swiglu_bwd", _f_swiglu, hgu_rows, [], [dhid], [BF16, BF16], tl=128)
    g['w_ffn_gate'] = _mm("ffn_gate_dw", r['x1_bf'], dhg, ta=True, out_dtype=BF16)
    g['w_ffn_up'] = _mm("ffn_up_dw", r['x1_bf'], dhu, ta=True, out_dtype=BF16)
    dx1 = _mm("ffn_gate_dx", dhg, p['wgu'], tb=True, b_w=FFN_HIDDEN, add=dx1_n)
    dx1 = _mm("ffn_up_dx", dhu, p['wgu'], tb=True, b_c0=FFN_HIDDEN, b_w=FFN_HIDDEN, add=dx1)
    dx_n, dmixed, g['ln1_g'], g['ln1_b'] = _ew_bwd("ln1_bwd", _f_ln, [_full(x), _full(r['mixed'])], p['ln1'], [dx1],
                                                   [True, BF16])
    dmix = _mm("w_out_dx", dmixed, p['wout'], tb=True)
    g['w_out'] = _mm("w_out_dw", r['mix'], dmixed, ta=True, out_dtype=BF16)
    gate_rows = [(z, Z_GATE0 + k * D_MODEL, D_MODEL) for k in range(3)]
    mg = _ew_bwd("merge_bwd", _f_merge, [_full(a) for a in r['proj']] + gate_rows, p['b_gate'], [dmix], [BF16] * 6,
                 tl=128)
    dproj, dzg = mg[0:3], mg[3:6]
    g['b_gate'] = jnp.concatenate([b.reshape(-1) for b in mg[6:9]])
    ys = [r['ys5'], r['ylru'], r['yfox']]
    dys = [_mm("proj_%d_dx" % k, dproj[k], p['wb'][k], tb=True) for k in range(3)]
    g['w_branch'] = jnp.stack([_mm("proj_%d_dw" % k, ys[k], dproj[k], ta=True, out_dtype=BF16) for k in range(3)])
    qa, ka, kat, v = r['fox']
    do = _heads(dys[2]).astype(BF16)
    dot = do.transpose(0, 2, 1)
    dqat, carried_q = _fox_bwd_q(qa, ka, kat, v, dot, r['ot'], r['lse'], rider=riders[0])
    dkh, dvh, dck, carried_kv = _fox_bwd_kv(qa, ka, v, do, dot, r['ot'], r['lse'], rider=riders[1](g))
    pad_heads = lambda a: jnp.pad(a.T, ((0, 0), (0, FG_PAD - FOX_HEADS)))
    dlogf = _scan_real("fox_cum_bwd", jnp.ones((x.shape[0], FG_PAD), F32), pad_heads(dqat[:, FOX_CQ0, :]),
                       pad_heads(-dck[:, :, 0]), reverse=True)
    dq = (dqat[:, :FOX_HEAD_DIM, :].reshape(BRANCH, x.shape[0]).T * FOX_SCALE).astype(BF16)
    dqkv = [dq, _unheads(dkh).astype(BF16), _unheads(dvh).astype(BF16)]
    dzf, dbf = _ew_bwd("fox_logf_bwd", _f_logf, [(z, Z_FG0, FG_PAD)], [p['b_f']], [dlogf], [BF16])
    g['b_f'] = dbf[0, :FOX_HEADS]
    dgate, dh = _ew_bwd("lru_out_bwd", _f_lru_out, [(z, 2 * BRANCH, BRANCH), _full(r['h'])], [], [dys[1]], [BF16, True])
    db = _scan_real("lru_scan_bwd", _advance(r['a'], 1), dh, reverse=True)
    gates_rows = [_full(r['xc']), (r['papx'], 0, BRANCH), (r['papx'], BRANCH, BRANCH), _full(_delay(r['h'], 1))]
    dxc, dpa, dpx, db_a, db_x, dlam = _ew_bwd("lru_gates_bwd", _f_lru_step, gates_rows, [p['b_a'], p['b_x'], p['lam']],
                                              [db], [True, BF16, BF16, False])
    dxc = _mm("lru_a_dx", dpa, p['wax'], tb=True, b_w=BRANCH, add=dxc)
    dxc = _mm("lru_x_dx", dpx, p['wax'], tb=True, b_c0=BRANCH, b_w=BRANCH, add=dxc)
    take_heads = lambda d: _blockdiag_take(d, LRU_HEADS, LRU_HEAD_DIM, LRU_HEAD_DIM)
    g['lru_w_a'] = take_heads(_mm("lru_a_dw", r['xc_bf'], dpa, ta=True))
    g['lru_w_x'] = take_heads(_mm("lru_x_dw", r['xc_bf'], dpx, ta=True))
    g['lru_b_a'] = db_a.reshape(LRU_HEADS, LRU_HEAD_DIM)
    g['lru_b_x'] = db_x.reshape(LRU_HEADS, LRU_HEAD_DIM)
    g['lru_lambda'] = dlam.reshape(-1)
    conv_rows = [(z, BRANCH, BRANCH)] + [_full(a) for a in r['xd']]
    cw = _ew_bwd("lru_conv_dw", _f_conv, conv_rows, p['conv_w'] + [p['conv_b']], [dxc], [False] * CONV_WIDTH)
    g['lru_conv_w'] = jnp.concatenate([cw[CONV_WIDTH - 1 - k] for k in range(CONV_WIDTH)], axis=0)
    g['lru_conv_b'] = cw[CONV_WIDTH].reshape(-1)
    (dxl,) = _ew("lru_conv_dx", _f_conv_t, [_full(_advance(dxc, j)) for j in range(CONV_WIDTH)], p['conv_w'], [BRANCH],
                 out_dtypes=[BF16])
    dy1, dpre, dbglu = _ew_bwd("s5_glu_bwd", _f_s5_glu, [_full(r['y1']), _full(r['pre'])], [p['bglu']], [dys[0]],
                               [True, BF16])
    g['s5_b_glu'] = dbglu.reshape(-1)
    g['s5_w_glu'] = _mm("s5_glu_dw", r['y1_bf'], dpre, ta=True, out_dtype=BF16)
    dy1 = _mm("s5_glu_dx", dpre, p['wglu'], tb=True, add=dy1)
    dy0, du, dd = _ew_bwd("s5_y1_bwd", _f_s5_y1, [_full(r['hc_re']), _full(r['hc_im']), (z, 0, BRANCH)], [p['s5_d']],
                          [dy1], [BF16, False, True])
    g['s5_d'] = dd.reshape(-1)
    dh_re = _mm("s5_hc_re_dx", dy0, p['s5_cre'], tb=True)
    dh_im = _mm("s5_hc_im_dx", dy0, p['s5_cimn'], tb=True)
    take_c = lambda d: _blockdiag_take(d, S5_GROUPS, S5_STATE, S5_GROUP).transpose(0, 2, 1)
    g['s5_c_re'] = take_c(_mm("s5_hc_re_dw", r['h_re'], dy0, ta=True))
    g['s5_c_im'] = -take_c(_mm("s5_hc_im_dw", r['h_im'], dy0, ta=True))
    gb_re, gb_im, dl_re, dl_im = _scan_cplx("s5_scan_bwd", p['lam_re'], -p['lam_im'], dh_re, dh_im, reverse=True,
                                            h_re=r['h_re'], h_im=r['h_im'], bn=256)
    du = _mm("s5_bu_re_dx", gb_re, p['s5_bre'], tb=True, add=du)
    du = _mm("s5_bu_im_dx", gb_im, p['s5_bim'], tb=True, add=du, out_dtype=BF16)
    take_b = lambda d: _blockdiag_take(d, S5_GROUPS, S5_GROUP, S5_STATE).transpose(0, 2, 1).reshape(S5_N, S5_GROUP)
    dbb_re = take_b(_mm("s5_bu_re_dw", z, gb_re, ta=True, a_w=BRANCH))
    dbb_im = take_b(_mm("s5_bu_im_dw", z, gb_im, ta=True, a_w=BRANCH))
    disc = _ew_bwd("s5_disc_bwd", _f_s5_disc, [_full(a) for a in p['disc_in']], [],
                   [dl_re.reshape(S5_N, 1), dl_im.reshape(S5_N, 1), dbb_re, dbb_im], [True] * 5, tl=S5_N)
    grp = (S5_GROUPS, S5_STATE)
    g['s5_a_re'], g['s5_a_im'] = disc[0].reshape(grp), disc[1].reshape(grp)
    g['s5_log_dt'] = disc[2].reshape(grp).sum(axis=1)
    g['s5_b_re'], g['s5_b_im'] = disc[3].reshape(grp + (S5_GROUP,)), disc[4].reshape(grp + (S5_GROUP,))
    dz = jnp.concatenate([du, dxl, dgate] + dqkv + list(dzg) + [dzf], axis=1)
    dwc = _mm("z_in_dw", r['x_bf'], dz, ta=True, out_dtype=BF16)
    g['w_in'] = jnp.concatenate([dwc[:, :Z_MAIN], dwc[:, Z_FG0:Z_FG0 + FOX_HEADS], dwc[:, Z_GATE0:Z_FG0]], axis=1)
    dx = _mm("z_in_dx", dz, p['wc'], tb=True, add=dx_n)
    return dx, g, carried_q, carried_kv


def _loss_head(y, target, tl=256):
    length, width = y.shape
    tl = min(tl, length)
    nt = length // tl

    def body(y_ref, t_ref, dy_ref, loss_ref, acc_sc):
        i = pl.program_id(0)

        @pl.when(i == 0)
        def _():
            acc_sc[...] = jnp.zeros_like(acc_sc)

        err = y_ref[...] - t_ref[...]
        dy_ref[...] = err / width
        acc_sc[...] += jnp.sum(jnp.square(err), axis=0, keepdims=True)

        @pl.when(i == nt - 1)
        def _():
            total = jnp.sum(acc_sc[...], axis=1, keepdims=True) * (0.5 / width)
            loss_ref[...] = jnp.broadcast_to(total, loss_ref.shape)

    spec = pl.BlockSpec((tl, width), lambda i: (i, 0))
    dy, loss = pl.pallas_call(
        body, grid=(nt,), in_specs=[spec, spec], out_specs=[spec, pl.BlockSpec((1, LANES), lambda i: (0, 0))],
        out_shape=[jax.ShapeDtypeStruct((length, width), F32), jax.ShapeDtypeStruct((1, LANES), F32)],
        scratch_shapes=[pltpu.VMEM((1, width), F32)], compiler_params=_params("arbitrary"), name="loss_head")(y, target)
    return loss[0, 0], dy


def _sum_parts(name, parts):
    count, rows, cols = parts.shape
    tr = 256

    def body(p_ref, o_ref):
        total = p_ref[0]
        for dev in range(1, count):
            total = total + p_ref[dev]
        o_ref[...] = total

    return pl.pallas_call(body, grid=(rows // tr,), in_specs=[pl.BlockSpec((count, tr, cols), lambda i: (0, i, 0))],
                          out_specs=pl.BlockSpec((tr, cols), lambda i: (i, 0)),
                          out_shape=jax.ShapeDtypeStruct((rows, cols), F32), compiler_params=_params("parallel"),
                          name=name)(parts)


def _adamw(name, w, parts, m, v):
    rows, cols = w.shape
    count = parts[0].shape[0]
    span = rows // len(parts)
    tr = span
    for cand in (256, 128, 64, 32, 16):
        if span % cand == 0:
            tr = cand
            break
    per_span = span // tr

    def body(*refs):
        w_ref, p_refs = refs[0], refs[1:1 + len(parts)]
        m_ref, v_ref, g_ref, d_ref, m2_ref, v2_ref = refs[1 + len(parts):]
        step = pl.program_id(0)
        grad = None
        for j, p_ref in enumerate(p_refs):
            total = p_ref[0].astype(F32)
            for dev in range(1, count):
                total = total + p_ref[dev].astype(F32)
            grad = total if grad is None else jnp.where(step >= j * per_span, total, grad)
        m2 = ADAM_B1 * m_ref[...] + (1.0 - ADAM_B1) * grad
        v2 = ADAM_B2 * v_ref[...] + (1.0 - ADAM_B2) * jnp.square(grad)
        m_hat = m2 / (1.0 - ADAM_B1 ** ADAM_STEP)
        v_hat = v2 / (1.0 - ADAM_B2 ** ADAM_STEP)
        g_ref[...] = grad
        d_ref[...] = -ADAM_LR * (m_hat / (jnp.sqrt(v_hat) + ADAM_EPS) + ADAM_WD * w_ref[...])
        m2_ref[...] = m2
        v2_ref[...] = v2

    spec = pl.BlockSpec((tr, cols), lambda i: (i, 0))
    pspecs = [pl.BlockSpec((count, tr, cols),
                           lambda i, j=j: (0, jnp.minimum(jnp.maximum(i - j * per_span, 0), per_span - 1), 0))
              for j in range(len(parts))]
    shape = jax.ShapeDtypeStruct((rows, cols), F32)
    return pl.pallas_call(body, grid=(rows // tr,), in_specs=[spec] + pspecs + [spec, spec], out_specs=[spec] * 4,
                          out_shape=[shape] * 4, compiler_params=_params("arbitrary"), name=name)(w, *parts, m, v)


class _NoExchange:
    def __init__(self, layers):
        self.layers = layers

    def weights(self, l, carried):
        return self.layers[l]

    def forward_rider(self, l):
        return None

    def backward_riders(self, l):
        return None, lambda grads: None

    def collect(self, l, grads, carried_q, carried_kv):
        pass


def _forward_backward(x, target, hooks):
    prepared, saved, carried = [], [], None
    x_bf = x.astype(BF16)
    for l in range(DEPTH):
        p = _prep_layer(hooks.weights(l, carried))
        x, x_bf, r, carried = _layer_fwd(x, x_bf, p, hooks.forward_rider(l))
        prepared.append(p)
        saved.append(r)
    loss, dx = _loss_head(x, target)
    grads = [None] * DEPTH
    for l in reversed(range(DEPTH)):
        dx, grads[l], carried_q, carried_kv = _layer_bwd(dx, saved[l], prepared[l], hooks.backward_riders(l))
        hooks.collect(l, grads[l], carried_q, carried_kv)
    return loss, dx, grads


def _exchange_copies(ins, outs, sems, scatter, with_arrivals):
    send_sems, recv_sems, local_sems = sems
    x, y, c = lax.axis_index("x"), lax.axis_index("y"), lax.axis_index("c")
    me = 4 * x + 2 * y + c
    local, sends, arrivals = [], [], []
    for a in range(len(ins)):
        local.append(pltpu.make_async_copy(ins[a].at[me] if scatter else ins[a], outs[a].at[me], local_sems.at[a]))
    for k in range(1, N_DEV):
        px = 1 - x if k & 4 else x
        py = 1 - y if k & 2 else y
        pc = 1 - c if k & 1 else c
        idx = 4 * px + 2 * py + pc
        for a in range(len(ins)):
            s = a * (N_DEV - 1) + k - 1
            src = ins[a].at[idx] if scatter else ins[a]
            common = dict(src_ref=src, send_sem=send_sems.at[s], recv_sem=recv_sems.at[s], device_id=(px, py, pc),
                          device_id_type=pl.DeviceIdType.MESH)
            sends.append(pltpu.make_async_remote_copy(dst_ref=outs[a].at[me], **common))
            if with_arrivals:
                arrivals.append(pltpu.make_async_remote_copy(dst_ref=outs[a].at[idx], **common))
    return local, sends, arrivals


def _exchange_start(ins, outs, sems, scatter):
    local, sends, _ = _exchange_copies(ins, outs, sems, scatter, False)
    for cp in local + sends:
        cp.start()


def _exchange_wait(ins, outs, sems, scatter):
    local, sends, arrivals = _exchange_copies(ins, outs, sems, scatter, True)
    for cp in local:
        cp.wait()
    for cp in sends:
        cp.wait_send()
    for cp in arrivals:
        cp.wait_recv()


def _exchange_parts(arrays, scatter):
    n = len(arrays)
    hbm = [pl.BlockSpec(memory_space=pltpu.HBM)] * n
    out_shape = [jax.ShapeDtypeStruct(a.shape if scatter else (N_DEV,) + a.shape, a.dtype) for a in arrays]
    nsem = n * (N_DEV - 1)
    sems = [pltpu.SemaphoreType.DMA((nsem,)), pltpu.SemaphoreType.DMA((nsem,)), pltpu.SemaphoreType.DMA((n,))]
    return hbm, out_shape, sems


def _exchange(name, arrays, scatter):
    n = len(arrays)
    hbm, out_shape, sems = _exchange_parts(arrays, scatter)

    def body(*refs):
        ins, outs, sem_refs = refs[:n], refs[n:2 * n], refs[2 * n:]
        _exchange_start(ins, outs, sem_refs, scatter)
        _exchange_wait(ins, outs, sem_refs, scatter)

    return pl.pallas_call(body, in_specs=hbm, out_specs=hbm, out_shape=out_shape, scratch_shapes=sems,
                          name=name)(*arrays)


def _call_carrying(name, body, grid, ins, in_specs, out_shape, out_specs, scratch, semantics, rider):
    if rider is None:
        r_arrays, r_hbm, r_shape, r_sems = [], [], [], []
    else:
        r_arrays, scatter = rider
        r_hbm, r_shape, r_sems = _exchange_parts(r_arrays, scatter)
        semantics = ("arbitrary",) * len(grid)
    n_in, n_out, n_scr, n_r = len(ins), len(out_shape), len(scratch), len(r_arrays)

    def full_body(*refs):
        at = [0]

        def take(count):
            at[0] += count
            return refs[at[0] - count:at[0]]

        in_refs, r_in, out_refs, r_out, scr, r_scr = take(n_in), take(n_r), take(n_out), take(n_r), take(n_scr), take(3)
        ids = [pl.program_id(d) for d in range(len(grid))]
        if rider is not None:
            first = functools.reduce(jnp.logical_and, [i == 0 for i in ids])
            pl.when(first)(functools.partial(_exchange_start, r_in, r_out, r_scr, scatter))
        body(in_refs, out_refs, scr)
        if rider is not None:
            last = functools.reduce(jnp.logical_and, [i == g - 1 for i, g in zip(ids, grid)])
            pl.when(last)(functools.partial(_exchange_wait, r_in, r_out, r_scr, scatter))

    res = pl.pallas_call(
        full_body, grid=grid, in_specs=list(in_specs) + r_hbm, out_specs=list(out_specs) + r_hbm,
        out_shape=list(out_shape) + r_shape, scratch_shapes=list(scratch) + r_sems,
        compiler_params=_params(*semantics), name=name)(*ins, *r_arrays)
    return res[:n_out], res[n_out:]


def _shard_2d(a):
    return a.reshape(-1, a.shape[-1])


def _full_layer_weight(name, t):
    if name in ('s5_w_glu', 'w_out', 'w_ffn_down'):
        return t.reshape(-1, t.shape[-1])
    if name == 'w_branch':
        return t.transpose(1, 2, 0, 3).reshape(3, BRANCH, D_MODEL)
    return t.transpose(1, 0, 2).reshape(t.shape[1], -1)


def _split_layer_grad(name, g):
    if name in ('s5_w_glu', 'w_out', 'w_ffn_down'):
        return g.reshape(N_DEV, g.shape[0] // N_DEV, g.shape[1])
    if name == 'w_branch':
        return g.reshape(3, BRANCH, N_DEV, D_MODEL // N_DEV).transpose(2, 0, 1, 3)
    return g.reshape(g.shape[0], N_DEV, g.shape[1] // N_DEV).transpose(1, 0, 2)


RIDING = [n for n in SHARDED if n != 'lru_conv_w']
CARRIED_BY_DQ = ['w_in', 's5_w_glu', 'w_branch', 'w_out']
CARRIED_BY_DKV = ['w_ffn_gate', 'w_ffn_up', 'w_ffn_down']


class _Fsdp:
    def __init__(self, weights):
        self.weights_in = weights
        self.shard = {n: _shard_2d(weights[n]).astype(BF16) for n in RIDING}
        self.rows = {n: self.shard[n].shape[0] // DEPTH for n in RIDING}
        first = _exchange("gather_first_layer", [self.layer_shard(n, 0) for n in RIDING]
                          + [_shard_2d(weights['lru_conv_w'])], scatter=False)
        self.first = first[:-1]
        self.conv = first[-1].reshape((N_DEV,) + weights['lru_conv_w'].shape)
        self.outgoing = None
        self.incoming = {n: [None] * DEPTH for n in RIDING}

    def layer_shard(self, n, l):
        return self.shard[n][l * self.rows[n]:(l + 1) * self.rows[n]]

    def weights(self, l, carried):
        w = {n: self.weights_in[n][l] for n in REPLICATED}
        for n, t in zip(RIDING, self.first if l == 0 else carried):
            w[n] = _full_layer_weight(n, t.reshape((N_DEV,) + self.weights_in[n].shape[1:]))
        w['lru_conv_w'] = _full_layer_weight('lru_conv_w', self.conv[:, l])
        return w

    def forward_rider(self, l):
        return ([self.layer_shard(n, l + 1) for n in RIDING], False) if l + 1 < DEPTH else None

    def blocks(self, grads, names):
        return [_split_layer_grad(n, grads[n]).reshape(N_DEV, self.rows[n], -1).astype(BF16) for n in names]

    def backward_riders(self, l):
        first = None if self.outgoing is None else (self.outgoing, True)
        return first, lambda grads: (self.blocks(grads, CARRIED_BY_DKV), True)

    def collect(self, l, grads, carried_q, carried_kv):
        for n, t in zip(CARRIED_BY_DQ, carried_q):
            self.incoming[n][l + 1] = t
        for n, t in zip(CARRIED_BY_DKV, carried_kv):
            self.incoming[n][l] = t
        self.outgoing = self.blocks(grads, CARRIED_BY_DQ)

    def finish(self, grads):
        conv = jnp.stack([_split_layer_grad('lru_conv_w', grads[l]['lru_conv_w']) for l in range(DEPTH)], axis=1)
        conv = conv.reshape(N_DEV, -1, conv.shape[-1]).astype(F32)
        last = _exchange("scatter_last_layer", self.outgoing + [conv], scatter=True)
        for n, t in zip(CARRIED_BY_DQ, last[:-1]):
            self.incoming[n][0] = t
        return {**self.incoming, 'lru_conv_w': [last[-1]]}


def kernel(x, w_in, b_f, b_gate, s5_a_re, s5_a_im, s5_log_dt, s5_b_re, s5_b_im, s5_c_re, s5_c_im, s5_d, s5_w_glu, s5_b_glu, lru_conv_w, lru_conv_b, lru_w_a, lru_b_a, lru_w_x, lru_b_x, lru_lambda, w_branch, w_out, ln1_g, ln1_b, w_ffn_gate, w_ffn_up, w_ffn_down, ln2_g, ln2_b, loss_target, m_w_in, m_b_f, m_b_gate, m_s5_a_re, m_s5_a_im, m_s5_log_dt, m_s5_b_re, m_s5_b_im, m_s5_c_re, m_s5_c_im, m_s5_d, m_s5_w_glu, m_s5_b_glu, m_lru_conv_w, m_lru_conv_b, m_lru_w_a, m_lru_b_a, m_lru_w_x, m_lru_b_x, m_lru_lambda, m_w_branch, m_w_out, m_ln1_g, m_ln1_b, m_w_ffn_gate, m_w_ffn_up, m_w_ffn_down, m_ln2_g, m_ln2_b, v_w_in, v_b_f, v_b_gate, v_s5_a_re, v_s5_a_im, v_s5_log_dt, v_s5_b_re, v_s5_b_im, v_s5_c_re, v_s5_c_im, v_s5_d, v_s5_w_glu, v_s5_b_glu, v_lru_conv_w, v_lru_conv_b, v_lru_w_a, v_lru_b_a, v_lru_w_x, v_lru_b_x, v_lru_lambda, v_w_branch, v_w_out, v_ln1_g, v_ln1_b, v_w_ffn_gate, v_w_ffn_up, v_w_ffn_down, v_ln2_g, v_ln2_b):
    given = dict(locals())
    weights = {n: given[n] for n in WEIGHTS}
    moments_m = {n: given['m_' + n] for n in WEIGHTS}
    moments_v = {n: given['v_' + n] for n in WEIGHTS}

    hooks = _Fsdp(weights)
    loss_local, dx, grads = _forward_backward(x[0], loss_target[0], hooks)
    loss = lax.psum(loss_local, MESH_AXES)
    incoming = hooks.finish(grads)

    new = {}
    for n in SHARDED:
        res = _adamw("adamw_" + n, _shard_2d(weights[n]), incoming[n], _shard_2d(moments_m[n]), _shard_2d(moments_v[n]))
        new[n] = [t.reshape(weights[n].shape) for t in res]

    flat = jnp.concatenate([jnp.stack([grads[l][n] for l in range(DEPTH)]).astype(F32).reshape(-1) for n in REPLICATED])
    rows = -(-flat.shape[0] // (LANES * 256)) * 256
    packed = jnp.pad(flat, (0, rows * LANES - flat.shape[0])).reshape(rows, LANES)
    (arrived,) = _exchange("gather_small_grads", [packed], scatter=False)
    total, at = _sum_parts("sum_small_grads", arrived).reshape(-1), 0
    for n in REPLICATED:
        w2 = _shard_2d(weights[n])
        grad = total[at:at + w2.size].reshape((1,) + w2.shape)
        at += w2.size
        res = _adamw("adamw_" + n, w2, [grad], _shard_2d(moments_m[n]), _shard_2d(moments_v[n]))
        new[n] = [t.reshape(weights[n].shape) for t in res]

    return (loss, dx[None], *[new[n][0] for n in WEIGHTS], *[new[n][1] for n in WEIGHTS],
            *[new[n][2] for n in WEIGHTS], *[new[n][3] for n in WEIGHTS])
```

```python
import functools
import math

import jax
import jax.numpy as jnp
from jax import lax
from jax.experimental import pallas as pl
from jax.experimental.pallas import tpu as pltpu

F32 = jnp.float32
BF16 = jnp.bfloat16

D_MODEL = 1024
DEPTH = 4
BRANCH = 512
S5_GROUPS, S5_GROUP, S5_STATE = 32, 16, 64
S5_N = S5_GROUPS * S5_STATE
LRU_HEADS, LRU_HEAD_DIM = 8, 64
LRU_C = 8.0
CONV_WIDTH = 4
FOX_HEADS, FOX_HEAD_DIM = 8, 64
FFN_HIDDEN = 2816
ALPHA = (2.0 * DEPTH) ** 0.25
LN_EPS = 1e-5
IN_TOTAL = 6 * BRANCH + FOX_HEADS + 3 * D_MODEL
FG_PAD = 128
Z_MAIN = 6 * BRANCH
Z_GATE0 = Z_MAIN
Z_FG0 = Z_MAIN + 3 * D_MODEL
Z_TOTAL = Z_FG0 + FG_PAD
N_DEV = 8
MESH_AXES = ("x", "y", "c")

ADAM_LR, ADAM_B1, ADAM_B2, ADAM_EPS, ADAM_WD, ADAM_STEP = 0.001, 0.9, 0.999, 1e-08, 0.01, 10

VMEM_LIMIT_BYTES = 48 * 1024 * 1024
SCAN_SLACK_BYTES = 6 * 1024 * 1024
SUBLANES = 8
LANES = 128
NEG_BIG = -1e30

WEIGHTS = ['w_in', 'b_f', 'b_gate', 's5_a_re', 's5_a_im', 's5_log_dt', 's5_b_re', 's5_b_im', 's5_c_re', 's5_c_im',
           's5_d', 's5_w_glu', 's5_b_glu', 'lru_conv_w', 'lru_conv_b', 'lru_w_a', 'lru_b_a', 'lru_w_x', 'lru_b_x',
           'lru_lambda', 'w_branch', 'w_out', 'ln1_g', 'ln1_b', 'w_ffn_gate', 'w_ffn_up', 'w_ffn_down', 'ln2_g',
           'ln2_b']
SHARDED = ['w_in', 's5_w_glu', 'lru_conv_w', 'w_branch', 'w_out', 'w_ffn_gate', 'w_ffn_up', 'w_ffn_down']
REPLICATED = [n for n in WEIGHTS if n not in SHARDED]


def _params(*sem, vmem=VMEM_LIMIT_BYTES):
    return pltpu.CompilerParams(dimension_semantics=sem, vmem_limit_bytes=vmem)


def _tile(dim, want):
    if dim % LANES:
        return dim
    t = min(want, dim) // LANES * LANES
    while dim % t:
        t -= LANES
    return t


MM_VMEM_BUDGET_BYTES = 30 * 1024 * 1024
MM_MAX_TILE = 1024


def _divisor_tiles(dim, cap, must_divide=0):
    if dim % LANES:
        return [dim]
    out = [t for t in range(min(cap, dim) // LANES * LANES, 0, -LANES) if dim % t == 0 and must_divide % t == 0]
    return out or [dim]


def _mm_tiles(m, n, k, a_bytes, b_bytes, o_bytes, has_add, m_c0, n_c0, k_c0):
    for tk in _divisor_tiles(k, k, k_c0):
        best = None
        for tm in _divisor_tiles(m, MM_MAX_TILE, m_c0):
            for tn in _divisor_tiles(n, MM_MAX_TILE, n_c0):
                used = 2 * (tm * tk * a_bytes + tk * tn * b_bytes + tm * tn * o_bytes) + tm * tn * 4
                used += tm * tn * 4 if tk < k else 0
                used += 2 * tm * tn * 4 if has_add else 0
                if used <= MM_VMEM_BUDGET_BYTES and (best is None or tm * tn / (tm + tn) > best[0]):
                    best = (tm * tn / (tm + tn), tm, tn)
        if best is not None and (min(best[1], best[2]) >= 256 or tk <= 512):
            return best[1], best[2], tk
    raise ValueError("no matmul tiling fits VMEM")


def _mm(name, a, b, *, ta=False, tb=False, a_c0=0, a_w=None, b_c0=0, b_w=None, add=None, out_dtype=F32, rider=None):
    a_w = a.shape[1] if a_w is None else a_w
    b_w = b.shape[1] if b_w is None else b_w
    m, k = (a_w, a.shape[0]) if ta else (a.shape[0], a_w)
    n = b.shape[0] if tb else b_w
    assert k == (b_w if tb else b.shape[0]), (name, a.shape, b.shape)
    tm, tn, tk = _mm_tiles(m, n, k, a.dtype.itemsize, b.dtype.itemsize, jnp.dtype(out_dtype).itemsize,
                           add is not None, a_c0 if ta else 0, 0 if tb else b_c0,
                           math.gcd(0 if ta else a_c0, b_c0 if tb else 0))
    nk = k // tk
    a_off = a_c0 // (tm if ta else tk)
    b_off = b_c0 // (tk if tb else tn)
    assert a_c0 % (tm if ta else tk) == 0 and b_c0 % (tk if tb else tn) == 0, name
    dims = (((0 if ta else 1,), (1 if tb else 0,)), ((), ()))
    a_total, b_total = m * k * a.dtype.itemsize, n * k * b.dtype.itemsize
    a_stays = a_total + b_total * (m // tm) <= b_total + a_total * (n // tn)
    if nk > 1:
        a_stays = True

    def mn(o, i):
        return (o, i) if a_stays else (i, o)

    def body(in_refs, out_refs, scratch_refs):
        a_ref, b_ref = in_refs[0], in_refs[1]
        add_ref = in_refs[2] if add is not None else None
        (o_ref,) = out_refs
        part = lax.dot_general(a_ref[...].astype(BF16), b_ref[...].astype(BF16), dims, preferred_element_type=F32)

        def finish(r):
            if add is not None:
                r = r + add_ref[...]
            o_ref[...] = r.astype(o_ref.dtype)

        if nk == 1:
            finish(part)
            return
        (acc_ref,) = scratch_refs
        kk = pl.program_id(2)

        @pl.when(kk == 0)
        def _():
            acc_ref[...] = part

        @pl.when(kk > 0)
        def _():
            acc_ref[...] += part

        @pl.when(kk == nk - 1)
        def _():
            finish(acc_ref[...])

    def a_map(o, i, kk):
        im = mn(o, i)[0]
        return (kk, im + a_off) if ta else (im, kk + a_off)

    def b_map(o, i, kk):
        jn = mn(o, i)[1]
        return (jn, kk + b_off) if tb else (kk, jn + b_off)

    a_spec = pl.BlockSpec((tk, tm) if ta else (tm, tk), a_map)
    b_spec = pl.BlockSpec((tn, tk) if tb else (tk, tn), b_map)
    o_spec = pl.BlockSpec((tm, tn), lambda o, i, kk: mn(o, i))
    ins, in_specs = [a, b], [a_spec, b_spec]
    if add is not None:
        ins.append(add)
        in_specs.append(o_spec)
    grid = (m // tm, n // tn, nk) if a_stays else (n // tn, m // tm, nk)
    (out,), carried = _call_carrying(
        name, body, grid, ins, in_specs, [jax.ShapeDtypeStruct((m, n), out_dtype)], [o_spec],
        [pltpu.VMEM((tm, tn), F32)] if nk > 1 else [], ("parallel", "parallel", "arbitrary"), rider)
    return out if rider is None else (out, carried)


def _row_spec(tl, c0, w):
    assert c0 % w == 0
    return pl.BlockSpec((tl, w), lambda i: (i, c0 // w))


def _whole_spec(p):
    return pl.BlockSpec(p.shape, lambda i: (0,) * p.ndim)


def _ew(name, f, rows, prm, out_ws, tl=256, out_dtypes=None):
    out_dtypes = out_dtypes or [F32] * len(out_ws)
    nrows, nprm = len(rows), len(prm)
    length = rows[0][0].shape[0]
    tl = min(tl, length)

    def body(*refs):
        vals = [r[...] for r in refs[:nrows + nprm]]
        outs = f(*vals)
        for o_ref, o in zip(refs[nrows + nprm:], outs):
            o_ref[...] = o.astype(o_ref.dtype)

    return pl.pallas_call(
        body, grid=(length // tl,),
        in_specs=[_row_spec(tl, c0, w) for (_, c0, w) in rows] + [_whole_spec(p) for p in prm],
        out_specs=[_row_spec(tl, 0, w) for w in out_ws],
        out_shape=[jax.ShapeDtypeStruct((length, w), dt) for w, dt in zip(out_ws, out_dtypes)],
        compiler_params=_params("parallel"), name=name)(*[r[0] for r in rows], *prm)


def _ew_bwd(name, f, rows, prm, douts, row_grad, tl=256):
    nrows, nprm, nd = len(rows), len(prm), len(douts)
    length = rows[0][0].shape[0]
    tl = min(tl, length)
    want = [i for i in range(nrows) if row_grad[i]]

    def body(*refs):
        vals = [r[...] for r in refs[:nrows + nprm]]
        cts = tuple(r[...] for r in refs[nrows + nprm:nrows + nprm + nd])
        out_refs = refs[nrows + nprm + nd:]
        _, vjp = jax.vjp(lambda *v: tuple(f(*v)), *vals)
        grads = vjp(cts)
        for o_ref, i in zip(out_refs[:len(want)], want):
            o_ref[...] = grads[i].astype(o_ref.dtype)

        @pl.when(pl.program_id(0) == 0)
        def _():
            for o_ref in out_refs[len(want):]:
                o_ref[...] = jnp.zeros_like(o_ref)

        for o_ref, g in zip(out_refs[len(want):], grads[nrows:]):
            o_ref[...] += g

    return pl.pallas_call(
        body, grid=(length // tl,),
        in_specs=([_row_spec(tl, c0, w) for (_, c0, w) in rows] + [_whole_spec(p) for p in prm]
                  + [_row_spec(tl, 0, d.shape[1]) for d in douts]),
        out_specs=[_row_spec(tl, 0, rows[i][2]) for i in want] + [_whole_spec(p) for p in prm],
        out_shape=([jax.ShapeDtypeStruct((length, rows[i][2]), F32 if row_grad[i] is True else row_grad[i])
                    for i in want]
                   + [jax.ShapeDtypeStruct(p.shape, F32) for p in prm]),
        compiler_params=_params("arbitrary"), name=name)(*[r[0] for r in rows], *prm, *douts)


def _row_ids(width):
    return lax.broadcasted_iota(jnp.int32, (SUBLANES, width), 0)


def _shift_rows(v, d, reverse):
    return pltpu.roll(v, (SUBLANES - d) if reverse else d, 0)


def _scan_real(name, a, b, b2=None, *, reverse=False, bn=256):
    length, n = a.shape
    bn = _tile(n, bn)
    nb = length // SUBLANES
    operands = [a, b] if b2 is None else [a, b, b2]

    def body(*refs):
        a_ref, b_ref, h_ref = refs[0], refs[1], refs[-1]
        rows = _row_ids(bn)

        def step(it, carry):
            i = (nb - 1 - it) if reverse else it
            sl = pl.ds(pl.multiple_of(i * SUBLANES, SUBLANES), SUBLANES)
            av, bv = a_ref[sl, :], b_ref[sl, :]
            if b2 is not None:
                bv = bv + refs[2][sl, :]
            for d in (1, 2, 4):
                live = (rows < SUBLANES - d) if reverse else (rows >= d)
                a_in = jnp.where(live, _shift_rows(av, d, reverse), 1.0)
                b_in = jnp.where(live, _shift_rows(bv, d, reverse), 0.0)
                bv = bv + av * b_in
                av = av * a_in
            hv = bv + av * carry
            h_ref[sl, :] = hv
            edge = hv[0:1, :] if reverse else hv[SUBLANES - 1:SUBLANES, :]
            return jnp.broadcast_to(edge, (SUBLANES, bn))

        lax.fori_loop(0, nb, step, jnp.zeros((SUBLANES, bn), F32))

    spec = pl.BlockSpec((length, bn), lambda j: (0, j))
    return pl.pallas_call(body, grid=(n // bn,), in_specs=[spec] * len(operands), out_specs=spec,
                          out_shape=jax.ShapeDtypeStruct((length, n), F32),
                          compiler_params=_params("parallel"), name=name)(*operands)


def _cmul(ar, ai, br, bi):
    return ar * br - ai * bi, ar * bi + ai * br


def _scan_cplx(name, lam_re, lam_im, x_re, x_im, *, reverse=False, h_re=None, h_im=None, bn=128, rider=None):
    length, n = x_re.shape
    bn = _tile(n, bn)
    nb = length // SUBLANES
    with_dot = h_re is not None

    def body(in_refs, out_refs, scratch_refs):
        refs = tuple(in_refs) + tuple(out_refs)
        if with_dot:
            lr_ref, li_ref, xr_ref, xi_ref, hr_ref, hi_ref, gr_ref, gi_ref, dr_ref, di_ref = refs
        else:
            lr_ref, li_ref, xr_ref, xi_ref, gr_ref, gi_ref = refs
        rows = _row_ids(bn)
        lr = jnp.broadcast_to(lr_ref[...], (SUBLANES, bn))
        li = jnp.broadcast_to(li_ref[...], (SUBLANES, bn))
        powers = [(lr, li)]
        for _ in range(SUBLANES - 1):
            powers.append(_cmul(powers[-1][0], powers[-1][1], lr, li))
        zero = jnp.zeros((SUBLANES, bn), F32)
        steps = []
        for d in (1, 2, 4):
            live = (rows < SUBLANES - d) if reverse else (rows >= d)
            steps.append((d, jnp.where(live, powers[d - 1][0], 0.0), jnp.where(live, powers[d - 1][1], 0.0)))
        cr, ci = zero, zero
        for r in range(SUBLANES):
            e = (SUBLANES - r) if reverse else (r + 1)
            cr = jnp.where(rows == r, powers[e - 1][0], cr)
            ci = jnp.where(rows == r, powers[e - 1][1], ci)

        def step(it, carry):
            i = (nb - 1 - it) if reverse else it
            sl = pl.ds(pl.multiple_of(i * SUBLANES, SUBLANES), SUBLANES)
            vr, vi = xr_ref[sl, :], xi_ref[sl, :]
            for d, pr, pi in steps:
                sr, si = _cmul(pr, pi, _shift_rows(vr, d, reverse), _shift_rows(vi, d, reverse))
                vr, vi = vr + sr, vi + si
            kr, ki = _cmul(cr, ci, carry[0], carry[1])
            vr, vi = vr + kr, vi + ki
            gr_ref[sl, :] = vr
            gi_ref[sl, :] = vi
            er = vr[0:1, :] if reverse else vr[SUBLANES - 1:SUBLANES, :]
            ei = vi[0:1, :] if reverse else vi[SUBLANES - 1:SUBLANES, :]
            new = (jnp.broadcast_to(er, (SUBLANES, bn)), jnp.broadcast_to(ei, (SUBLANES, bn)))
            if not with_dot:
                return new
            prev = pl.ds(pl.multiple_of(jnp.maximum(i - 1, 0) * SUBLANES, SUBLANES), SUBLANES)
            keep = jnp.where(i > 0, 1.0, 0.0)
            pr_ = jnp.broadcast_to(hr_ref[prev, :][SUBLANES - 1:SUBLANES, :], (SUBLANES, bn)) * keep
            pi_ = jnp.broadcast_to(hi_ref[prev, :][SUBLANES - 1:SUBLANES, :], (SUBLANES, bn)) * keep
            hr = jnp.where(rows == 0, pr_, pltpu.roll(hr_ref[sl, :], 1, 0))
            hi = jnp.where(rows == 0, pi_, pltpu.roll(hi_ref[sl, :], 1, 0))
            return new + (carry[2] + vr * hr + vi * hi, carry[3] + vi * hr - vr * hi)

        init = (zero, zero, zero, zero) if with_dot else (zero, zero)
        out = lax.fori_loop(0, nb, step, init)
        if with_dot:
            dr_ref[...] = jnp.sum(out[2], axis=0, keepdims=True)
            di_ref[...] = jnp.sum(out[3], axis=0, keepdims=True)

    col = pl.BlockSpec((length, bn), lambda j: (0, j))
    vec = pl.BlockSpec((1, bn), lambda j: (0, j))
    ins = [lam_re, lam_im, x_re, x_im] + ([h_re, h_im] if with_dot else [])
    in_specs = [vec, vec, col, col] + ([col, col] if with_dot else [])
    out_specs = [col, col] + ([vec, vec] if with_dot else [])
    full = jax.ShapeDtypeStruct((length, n), F32)
    row = jax.ShapeDtypeStruct((1, n), F32)
    out_shape = [full, full] + ([row, row] if with_dot else [])
    columns = 6 if with_dot else 4
    outs, carried = _call_carrying(name, body, (n // bn,), ins, in_specs, out_shape, out_specs, [], ("parallel",), rider,
                                   vmem=2 * columns * length * bn * 4 + SCAN_SLACK_BYTES)
    return list(outs) if rider is None else list(outs) + [carried]


FOX_SCALE = FOX_HEAD_DIM ** -0.5
FOX_AUG = 128
FOX_CQ0 = FOX_HEAD_DIM
FOX_CK0 = FOX_HEAD_DIM + 3
NT = (((1,), (1,)), ((), ()))


def _fox_logits_t(ka, qa, on_diagonal):
    st = lax.dot_general(ka, qa, NT, preferred_element_type=F32)
    if on_diagonal:
        key = lax.broadcasted_iota(jnp.int32, st.shape, 0)
        query = lax.broadcasted_iota(jnp.int32, st.shape, 1)
        st = jnp.where(key <= query, st, NEG_BIG)
    return st


def _fox_pair(s, nt, q_first):
    if q_first:
        qi = sum((s >= (m * (m + 1)) // 2).astype(jnp.int32) for m in range(1, nt))
        return qi, s - ((qi * (qi + 1)) >> 1)
    ki = sum((s >= m * nt - (m * (m - 1)) // 2).astype(jnp.int32) for m in range(1, nt))
    return ki + s - (ki * nt - ((ki * (ki - 1)) >> 1)), ki


def _fox_specs(t, nt, q_first):
    q_idx = lambda s: _fox_pair(s, nt, q_first)[0]
    k_idx = lambda s: _fox_pair(s, nt, q_first)[1]
    rows = lambda idx, w: pl.BlockSpec((None, t, w), lambda h, s: (h, idx(s), 0))
    cols = lambda idx, w: pl.BlockSpec((None, w, t), lambda h, s: (h, 0, idx(s)))
    return rows, cols, q_idx, k_idx, (nt * (nt + 1)) // 2


def _fox_fwd(qa, ka, vt, t=512, rider=None):
    heads, length, _ = qa.shape
    dh = vt.shape[1]
    t = min(t, length)
    nt = length // t

    def body(in_refs, out_refs, scratch_refs):
        (qa_ref, ka_ref, vt_ref), (o_ref, lse_ref), (m_sc, l_sc, acc_sc) = in_refs, out_refs, scratch_refs
        qi, ki = _fox_pair(pl.program_id(1), nt, True)

        @pl.when(ki == 0)
        def _():
            m_sc[...] = jnp.full_like(m_sc, NEG_BIG)
            l_sc[...] = jnp.zeros_like(l_sc)
            acc_sc[...] = jnp.zeros_like(acc_sc)

        def step(on_diagonal):
            st = _fox_logits_t(ka_ref[...], qa_ref[...], on_diagonal)
            m_old = m_sc[...]
            m_new = jnp.maximum(m_old, jnp.max(st, axis=0, keepdims=True))
            pt = jnp.exp(st - m_new)
            scale = jnp.exp(m_old - m_new)
            l_sc[...] = scale * l_sc[...] + jnp.sum(pt, axis=0, keepdims=True)
            acc_sc[...] = scale * acc_sc[...] + jnp.dot(vt_ref[...], pt.astype(BF16), preferred_element_type=F32)
            m_sc[...] = m_new

        pl.when(ki < qi)(functools.partial(step, False))
        pl.when(ki == qi)(functools.partial(step, True))

        @pl.when(ki == qi)
        def _():
            o_ref[...] = acc_sc[...] / l_sc[...]
            lse_ref[...] = m_sc[...] + jnp.log(l_sc[...])

    rows, cols, q_idx, k_idx, pairs = _fox_specs(t, nt, True)
    (ot, lse), carried = _call_carrying(
        "fox_fwd", body, (heads, pairs), [qa, ka, vt],
        [rows(q_idx, FOX_AUG), rows(k_idx, FOX_AUG), cols(k_idx, dh)],
        [jax.ShapeDtypeStruct((heads, dh, length), F32), jax.ShapeDtypeStruct((heads, 1, length), F32)],
        [cols(q_idx, dh), cols(q_idx, 1)],
        [pltpu.VMEM((1, t), F32), pltpu.VMEM((1, t), F32), pltpu.VMEM((dh, t), F32)],
        ("parallel", "arbitrary"), rider)
    return ot, lse, carried


def _fox_ds_t(qa_ref, ka_ref, v_ref, dot_ref, ot_ref, lse_ref, on_diagonal):
    pt = jnp.exp(_fox_logits_t(ka_ref[...], qa_ref[...], on_diagonal) - lse_ref[...])
    dpt = jnp.dot(v_ref[...], dot_ref[...], preferred_element_type=F32)
    delta = jnp.sum(dot_ref[...].astype(F32) * ot_ref[...], axis=0, keepdims=True)
    return pt, pt * (dpt - delta)


def _fox_bwd_q(qa, ka, kat, v, dot, ot, lse, t=512, rider=None):
    heads, length, _ = qa.shape
    dh = v.shape[2]
    t = min(t, length)
    nt = length // t

    def body(in_refs, out_refs, scratch_refs):
        qa_ref, ka_ref, kat_ref, v_ref, dot_ref, ot_ref, lse_ref = in_refs
        (dqa_ref,), (acc_sc,) = out_refs, scratch_refs
        qi, ki = _fox_pair(pl.program_id(1), nt, True)

        @pl.when(ki == 0)
        def _():
            acc_sc[...] = jnp.zeros_like(acc_sc)

        def step(on_diagonal):
            _, dst = _fox_ds_t(qa_ref, ka_ref, v_ref, dot_ref, ot_ref, lse_ref, on_diagonal)
            acc_sc[...] += jnp.dot(kat_ref[...], dst.astype(BF16), preferred_element_type=F32)

        pl.when(ki < qi)(functools.partial(step, False))
        pl.when(ki == qi)(functools.partial(step, True))

        @pl.when(ki == qi)
        def _():
            dqa_ref[...] = acc_sc[...]

    rows, cols, q_idx, k_idx, pairs = _fox_specs(t, nt, True)
    (dqat,), carried = _call_carrying(
        "fox_bwd_q", body, (heads, pairs), [qa, ka, kat, v, dot, ot, lse],
        [rows(q_idx, FOX_AUG), rows(k_idx, FOX_AUG), cols(k_idx, FOX_AUG), rows(k_idx, dh), cols(q_idx, dh),
         cols(q_idx, dh), cols(q_idx, 1)],
        [jax.ShapeDtypeStruct((heads, FOX_AUG, length), F32)], [cols(q_idx, FOX_AUG)],
        [pltpu.VMEM((FOX_AUG, t), F32)], ("parallel", "arbitrary"), rider)
    return dqat, carried


def _fox_bwd_kv(qa, ka, v, do, dot, ot, lse, t=512, rider=None):
    heads, length, _ = qa.shape
    dh = v.shape[2]
    t = min(t, length)
    nt = length // t

    def body(in_refs, out_refs, scratch_refs):
        qa_ref, ka_ref, v_ref, do_ref, dot_ref, ot_ref, lse_ref = in_refs
        (dk_ref, dv_ref, dc_ref), (dka_sc, dv_sc) = out_refs, scratch_refs
        qi, ki = _fox_pair(pl.program_id(1), nt, False)

        @pl.when(qi == ki)
        def _():
            dka_sc[...] = jnp.zeros_like(dka_sc)
            dv_sc[...] = jnp.zeros_like(dv_sc)

        def step(on_diagonal):
            pt, dst = _fox_ds_t(qa_ref, ka_ref, v_ref, dot_ref, ot_ref, lse_ref, on_diagonal)
            dv_sc[...] += jnp.dot(pt.astype(BF16), do_ref[...], preferred_element_type=F32)
            dka_sc[...] += jnp.dot(dst.astype(BF16), qa_ref[...], preferred_element_type=F32)

        pl.when(qi > ki)(functools.partial(step, False))
        pl.when(qi == ki)(functools.partial(step, True))

        @pl.when(qi == nt - 1)
        def _():
            dka = dka_sc[...]
            lane = lax.broadcasted_iota(jnp.int32, dka.shape, 1)
            dk_ref[...] = dka_sc[:, :dh]
            dc_ref[...] = jnp.sum(jnp.where(lane == FOX_CK0, dka, 0.0), axis=1, keepdims=True)
            dv_ref[...] = dv_sc[...]

    rows, cols, q_idx, k_idx, pairs = _fox_specs(t, nt, False)
    big = jax.ShapeDtypeStruct((heads, length, dh), F32)
    (dk, dv, dc), carried = _call_carrying(
        "fox_bwd_kv", body, (heads, pairs), [qa, ka, v, do, dot, ot, lse],
        [rows(q_idx, FOX_AUG), rows(k_idx, FOX_AUG), rows(k_idx, dh), rows(q_idx, dh), cols(q_idx, dh),
         cols(q_idx, dh), cols(q_idx, 1)],
        [big, big, jax.ShapeDtypeStruct((heads, length, 1), F32)], [rows(k_idx, dh), rows(k_idx, dh), rows(k_idx, 1)],
        [pltpu.VMEM((t, FOX_AUG), F32), pltpu.VMEM((t, dh), F32)], ("parallel", "arbitrary"), rider)
    return dk, dv, dc, carried


def _split3(x):
    hi = lax.reduce_precision(x, 8, 7)
    mid = lax.reduce_precision(x - hi, 8, 7)
    return [hi, mid, lax.reduce_precision(x - hi - mid, 8, 7)]


def _fox_operands(z, cum):
    length = z.shape[0]
    parts = jnp.stack(_split3(cum[:, :FOX_HEADS].T), axis=-1)
    ones = jnp.ones_like(parts)
    pad = jnp.zeros((FOX_HEADS, length, FOX_AUG - FOX_HEAD_DIM - 6), F32)
    qa = jnp.concatenate([_heads(z, 3 * BRANCH) * FOX_SCALE, parts, ones, pad], axis=-1).astype(BF16)
    ka = jnp.concatenate([_heads(z, 4 * BRANCH), ones, -parts, pad], axis=-1).astype(BF16)
    v = _heads(z, 5 * BRANCH).astype(BF16)
    return qa, ka, ka.transpose(0, 2, 1), v, v.transpose(0, 2, 1)


def _softplus(x):
    return jnp.maximum(x, 0.0) + jnp.log1p(jnp.exp(-jnp.abs(x)))


def _f_s5_disc(a_re, a_im, log_dt, b_re, b_im):
    dt = jnp.exp(log_dt)
    mag = jnp.exp(a_re * dt)
    lr, li = mag * jnp.cos(a_im * dt), mag * jnp.sin(a_im * dt)
    den = a_re * a_re + a_im * a_im
    qr = ((lr - 1.0) * a_re + li * a_im) / den
    qi = (li * a_re - (lr - 1.0) * a_im) / den
    return lr, li, qr * b_re - qi * b_im, qr * b_im + qi * b_re


def _f_s5_y1(hc_re, hc_im, u, d):
    return (jax.nn.gelu(hc_re + hc_im + d * u),)


def _f_s5_glu(y1, pre, b):
    return (y1 * jax.nn.sigmoid(pre + b),)


def _f_conv(x0, x1, x2, x3, w0, w1, w2, w3, b):
    return (b + w0 * x0 + w1 * x1 + w2 * x2 + w3 * x3,)


def _f_conv_t(d0, d1, d2, d3, w0, w1, w2, w3):
    return (w0 * d0 + w1 * d1 + w2 * d2 + w3 * d3,)


def _lru_coeffs(xc, pa, px, b_a, b_x, lam):
    r = jax.nn.sigmoid(pa + b_a)
    i = jax.nn.sigmoid(px + b_x)
    log_a = -LRU_C * _softplus(-lam) * r
    a = jnp.exp(log_a)
    mult = jnp.sqrt(-jnp.tanh(log_a) * (a * a + 1.0))
    return a, mult * (i * xc)


def _f_lru_gates(xc, pa, px, b_a, b_x, lam):
    return _lru_coeffs(xc, pa, px, b_a, b_x, lam)


def _f_lru_step(xc, pa, px, h_prev, b_a, b_x, lam):
    a, b = _lru_coeffs(xc, pa, px, b_a, b_x, lam)
    return (a * h_prev + b,)


def _f_lru_out(gate, h):
    return (jax.nn.gelu(gate) * h,)


def _f_logf(zf, bf):
    return (-_softplus(-(zf + bf)),)


def _f_merge(p0, p1, p2, z0, z1, z2, b0, b1, b2):
    return (jax.nn.sigmoid(z0 + b0) * p0 + jax.nn.sigmoid(z1 + b1) * p1 + jax.nn.sigmoid(z2 + b2) * p2,)


def _f_ln(x, r, g, b):
    s = ALPHA * x + r
    mu = jnp.mean(s, axis=-1, keepdims=True)
    var = jnp.mean(jnp.square(s - mu), axis=-1, keepdims=True)
    return ((s - mu) * lax.rsqrt(var + LN_EPS) * g + b,)


def _f_swiglu(hg, hu):
    return (jax.nn.silu(hg) * hu,)


def _full(a):
    return (a, 0, a.shape[1])


def _blockdiag(t):
    g, a, b = t.shape
    eye = jnp.eye(g, dtype=t.dtype)
    return (t[:, :, None, :] * eye[:, None, :, None]).reshape(g * a, g * b)


def _blockdiag_take(d, g, a, b):
    eye = jnp.eye(g, dtype=d.dtype)
    return (d.reshape(g, a, g, b) * eye[:, None, :, None]).sum(axis=2)


def _delay(a, j):
    return a if j == 0 else jnp.pad(a, ((j, 0), (0, 0)))[:a.shape[0]]


def _advance(a, j):
    return a if j == 0 else jnp.pad(a, ((0, j), (0, 0)))[j:]


def _heads(a, c0=0):
    length = a.shape[0]
    return a[:, c0:c0 + BRANCH].reshape(length, FOX_HEADS, FOX_HEAD_DIM).transpose(1, 0, 2)


def _unheads(a):
    return a.transpose(1, 0, 2).reshape(a.shape[1], BRANCH)


def _row(v):
    return v.reshape(1, -1).astype(F32)


def _col(v):
    return v.reshape(-1, 1).astype(F32)


def _prep_layer(w):
    p = {}
    w_in = w['w_in']
    p['wc'] = jnp.concatenate(
        [w_in[:, :Z_MAIN], w_in[:, Z_MAIN + FOX_HEADS:], w_in[:, Z_MAIN:Z_MAIN + FOX_HEADS],
         jnp.zeros((D_MODEL, FG_PAD - FOX_HEADS), w_in.dtype)], axis=1)
    p['b_f'] = jnp.pad(_row(w['b_f']), ((0, 0), (0, FG_PAD - FOX_HEADS)))
    p['b_gate'] = [_row(w['b_gate'][k * D_MODEL:(k + 1) * D_MODEL]) for k in range(3)]
    p['disc_in'] = [_col(w['s5_a_re']), _col(w['s5_a_im']), _col(jnp.repeat(w['s5_log_dt'], S5_STATE)),
                    w['s5_b_re'].reshape(S5_N, S5_GROUP), w['s5_b_im'].reshape(S5_N, S5_GROUP)]
    lam_re, lam_im, bb_re, bb_im = _ew("s5_disc", _f_s5_disc, [_full(a) for a in p['disc_in']], [],
                                       [1, 1, S5_GROUP, S5_GROUP], tl=S5_N)
    p['lam_re'], p['lam_im'] = lam_re.reshape(1, S5_N), lam_im.reshape(1, S5_N)
    to_blk = lambda t: _blockdiag(t.reshape(S5_GROUPS, S5_STATE, S5_GROUP).transpose(0, 2, 1)).astype(BF16)
    p['s5_bre'], p['s5_bim'] = to_blk(bb_re), to_blk(bb_im)
    p['s5_cre'] = _blockdiag(w['s5_c_re'].transpose(0, 2, 1)).astype(BF16)
    p['s5_cimn'] = _blockdiag(-w['s5_c_im'].transpose(0, 2, 1)).astype(BF16)
    p['s5_d'], p['wglu'], p['bglu'] = _row(w['s5_d']), w['s5_w_glu'], _row(w['s5_b_glu'])
    p['conv_w'] = [_row(w['lru_conv_w'][CONV_WIDTH - 1 - j]) for j in range(CONV_WIDTH)]
    p['conv_b'] = _row(w['lru_conv_b'])
    p['wax'] = jnp.concatenate([_blockdiag(w['lru_w_a']), _blockdiag(w['lru_w_x'])], axis=1).astype(BF16)
    p['b_a'], p['b_x'], p['lam'] = _row(w['lru_b_a']), _row(w['lru_b_x']), _row(w['lru_lambda'])
    p['wb'] = [w['w_branch'][k] for k in range(3)]
    p['wout'] = w['w_out']
    p['ln1'] = [_row(w['ln1_g']), _row(w['ln1_b'])]
    p['wgu'] = jnp.concatenate([w['w_ffn_gate'], w['w_ffn_up']], axis=1)
    p['wd'] = w['w_ffn_down']
    p['ln2'] = [_row(w['ln2_g']), _row(w['ln2_b'])]
    return p


def _with_copy(f):
    def g(*args):
        (y,) = f(*args)
        return y, y
    return g


def _layer_fwd(x, x_bf, p, riders=None):
    length = x.shape[0]
    riders, carried = riders or {}, {}
    r = {'x': x, 'x_bf': x_bf}
    z = _mm("z_in", x_bf, p['wc'], rider=riders.get('z_in'))
    if 'z_in' in riders:
        z, carried['z_in'] = z
    r['z'] = z
    bu_re = _mm("s5_bu_re", z, p['s5_bre'], a_w=BRANCH)
    bu_im = _mm("s5_bu_im", z, p['s5_bim'], a_w=BRANCH)
    scanned = _scan_cplx("s5_scan", p['lam_re'], p['lam_im'], bu_re, bu_im, bn=256, rider=riders.get('s5_scan'))
    r['h_re'], r['h_im'] = scanned[0], scanned[1]
    if 's5_scan' in riders:
        carried['s5_scan'] = scanned[2]
    r['hc_re'] = _mm("s5_hc_re", r['h_re'], p['s5_cre'])
    r['hc_im'] = _mm("s5_hc_im", r['h_im'], p['s5_cimn'])
    r['y1'], r['y1_bf'] = _ew("s5_y1", _with_copy(_f_s5_y1), [_full(r['hc_re']), _full(r['hc_im']), (z, 0, BRANCH)],
                              [p['s5_d']], [BRANCH, BRANCH], out_dtypes=[F32, BF16])
    r['pre'] = _mm("s5_glu_pre", r['y1_bf'], p['wglu'])
    (r['ys5'],) = _ew("s5_glu", _f_s5_glu, [_full(r['y1']), _full(r['pre'])], [p['bglu']], [BRANCH], out_dtypes=[BF16])
    xl = z[:, BRANCH:2 * BRANCH]
    r['xd'] = [_delay(xl, j) for j in range(1, CONV_WIDTH)]
    r['xc'], r['xc_bf'] = _ew("lru_conv", _with_copy(_f_conv), [(z, BRANCH, BRANCH)] + [_full(a) for a in r['xd']],
                              p['conv_w'] + [p['conv_b']], [BRANCH, BRANCH], out_dtypes=[F32, BF16])
    r['papx'] = _mm("lru_gate_mm", r['xc_bf'], p['wax'])
    r['a'], b = _ew("lru_gates", _f_lru_gates, [_full(r['xc']), (r['papx'], 0, BRANCH), (r['papx'], BRANCH, BRANCH)],
                    [p['b_a'], p['b_x'], p['lam']], [BRANCH, BRANCH])
    r['h'] = _scan_real("lru_scan", r['a'], b)
    (r['ylru'],) = _ew("lru_out", _f_lru_out, [(z, 2 * BRANCH, BRANCH), _full(r['h'])], [], [BRANCH], out_dtypes=[BF16])
    (logf,) = _ew("fox_logf", _f_logf, [(z, Z_FG0, FG_PAD)], [p['b_f']], [FG_PAD])
    cum = _scan_real("fox_cum", jnp.ones((length, FG_PAD), F32), logf)
    qa, ka, kat, v, vt = _fox_operands(z, cum)
    r['fox'] = (qa, ka, kat, v)
    r['ot'], r['lse'], brought = _fox_fwd(qa, ka, vt, rider=riders.get('fox_fwd'))
    if 'fox_fwd' in riders:
        carried['fox_fwd'] = brought
    r['yfox'] = r['ot'].reshape(BRANCH, length).T.astype(BF16)
    ys = [r['ys5'], r['ylru'], r['yfox']]
    r['proj'] = [_mm("proj_%d" % k, ys[k], p['wb'][k]) for k in range(3)]
    gate_rows = [(z, Z_GATE0 + k * D_MODEL, D_MODEL) for k in range(3)]
    (r['mix'],) = _ew("merge", _f_merge, [_full(a) for a in r['proj']] + gate_rows, p['b_gate'], [D_MODEL], tl=128,
                      out_dtypes=[BF16])
    r['mixed'] = _mm("w_out", r['mix'], p['wout'])
    two = dict(out_ws=[D_MODEL, D_MODEL], out_dtypes=[F32, BF16])
    x1, r['x1_bf'] = _ew("ln1", _with_copy(_f_ln), [_full(x), _full(r['mixed'])], p['ln1'], **two)
    r['x1'] = x1
    r['hgu'] = _mm("ffn_in", r['x1_bf'], p['wgu'], rider=riders.get('ffn_in'))
    if 'ffn_in' in riders:
        r['hgu'], carried['ffn_in'] = r['hgu']
    (r['hid'],) = _ew("swiglu", _f_swiglu, [(r['hgu'], 0, FFN_HIDDEN), (r['hgu'], FFN_HIDDEN, FFN_HIDDEN)], [],
                      [FFN_HIDDEN], tl=128, out_dtypes=[BF16])
    r['f'] = _mm("ffn_out", r['hid'], p['wd'])
    x2, x2_bf = _ew("ln2", _with_copy(_f_ln), [_full(x1), _full(r['f'])], p['ln2'], **two)
    return x2, x2_bf, r, carried


def _layer_bwd(dx2, r, p, riders=None):
    g, riders, carried = {}, riders or {}, {}
    x, z, x1 = r['x'], r['z'], r['x1']
    dx1_n, df, g['ln2_g'], g['ln2_b'] = _ew_bwd("ln2_bwd", _f_ln, [_full(x1), _full(r['f'])], p['ln2'], [dx2],
                                                [True, BF16])
    dhid = _mm("ffn_out_dx", df, p['wd'], tb=True)
    g['w_ffn_down'] = _mm("ffn_out_dw", r['hid'], df, ta=True, out_dtype=BF16)
    hgu_rows = [(r['hgu'], 0, FFN_HIDDEN), (r['hgu'], FFN_HIDDEN, FFN_HIDDEN)]
    dhg, dhu = _ew_bwd("swiglu_bwd", _f_swiglu, hgu_rows, [], [dhid], [BF16, BF16], tl=128)
    g['w_ffn_gate'] = _mm("ffn_gate_dw", r['x1_bf'], dhg, ta=True, out_dtype=BF16)
    g['w_ffn_up'] = _mm("ffn_up_dw", r['x1_bf'], dhu, ta=True, out_dtype=BF16)
    dx1 = _mm("ffn_gate_dx", dhg, p['wgu'], tb=True, b_w=FFN_HIDDEN, add=dx1_n)
    dx1 = _mm("ffn_up_dx", dhu, p['wgu'], tb=True, b_c0=FFN_HIDDEN, b_w=FFN_HIDDEN, add=dx1)
    dx_n, dmixed, g['ln1_g'], g['ln1_b'] = _ew_bwd("ln1_bwd", _f_ln, [_full(x), _full(r['mixed'])], p['ln1'], [dx1],
                                                   [True, BF16])
    dmix = _mm("w_out_dx", dmixed, p['wout'], tb=True)
    g['w_out'] = _mm("w_out_dw", r['mix'], dmixed, ta=True, out_dtype=BF16)
    gate_rows = [(z, Z_GATE0 + k * D_MODEL, D_MODEL) for k in range(3)]
    mg = _ew_bwd("merge_bwd", _f_merge, [_full(a) for a in r['proj']] + gate_rows, p['b_gate'], [dmix], [BF16] * 6,
                 tl=128)
    dproj, dzg = mg[0:3], mg[3:6]
    g['b_gate'] = jnp.concatenate([b.reshape(-1) for b in mg[6:9]])
    ys = [r['ys5'], r['ylru'], r['yfox']]
    dys = [_mm("proj_%d_dx" % k, dproj[k], p['wb'][k], tb=True) for k in range(3)]
    g['w_branch'] = jnp.stack([_mm("proj_%d_dw" % k, ys[k], dproj[k], ta=True, out_dtype=BF16) for k in range(3)])
    qa, ka, kat, v = r['fox']
    do = _heads(dys[2]).astype(BF16)
    dot = do.transpose(0, 2, 1)
    own = riders['own'](g) if 'own' in riders else {}
    dqat, carried['fox_bwd_q'] = _fox_bwd_q(qa, ka, kat, v, dot, r['ot'], r['lse'], rider=riders.get('fox_bwd_q'))
    dkh, dvh, dck, carried['fox_bwd_kv'] = _fox_bwd_kv(qa, ka, v, do, dot, r['ot'], r['lse'],
                                                       rider=own.get('fox_bwd_kv'))
    pad_heads = lambda a: jnp.pad(a.T, ((0, 0), (0, FG_PAD - FOX_HEADS)))
    dlogf = _scan_real("fox_cum_bwd", jnp.ones((x.shape[0], FG_PAD), F32), pad_heads(dqat[:, FOX_CQ0, :]),
                       pad_heads(-dck[:, :, 0]), reverse=True)
    dq = (dqat[:, :FOX_HEAD_DIM, :].reshape(BRANCH, x.shape[0]).T * FOX_SCALE).astype(BF16)
    dqkv = [dq, _unheads(dkh).astype(BF16), _unheads(dvh).astype(BF16)]
    dzf, dbf = _ew_bwd("fox_logf_bwd", _f_logf, [(z, Z_FG0, FG_PAD)], [p['b_f']], [dlogf], [BF16])
    g['b_f'] = dbf[0, :FOX_HEADS]
    dgate, dh = _ew_bwd("lru_out_bwd", _f_lru_out, [(z, 2 * BRANCH, BRANCH), _full(r['h'])], [], [dys[1]], [BF16, True])
    db = _scan_real("lru_scan_bwd", _advance(r['a'], 1), dh, reverse=True)
    gates_rows = [_full(r['xc']), (r['papx'], 0, BRANCH), (r['papx'], BRANCH, BRANCH), _full(_delay(r['h'], 1))]
    dxc, dpa, dpx, db_a, db_x, dlam = _ew_bwd("lru_gates_bwd", _f_lru_step, gates_rows, [p['b_a'], p['b_x'], p['lam']],
                                              [db], [True, BF16, BF16, False])
    dxc = _mm("lru_a_dx", dpa, p['wax'], tb=True, b_w=BRANCH, add=dxc)
    dxc = _mm("lru_x_dx", dpx, p['wax'], tb=True, b_c0=BRANCH, b_w=BRANCH, add=dxc)
    take_heads = lambda d: _blockdiag_take(d, LRU_HEADS, LRU_HEAD_DIM, LRU_HEAD_DIM)
    g['lru_w_a'] = take_heads(_mm("lru_a_dw", r['xc_bf'], dpa, ta=True))
    g['lru_w_x'] = take_heads(_mm("lru_x_dw", r['xc_bf'], dpx, ta=True))
    g['lru_b_a'] = db_a.reshape(LRU_HEADS, LRU_HEAD_DIM)
    g['lru_b_x'] = db_x.reshape(LRU_HEADS, LRU_HEAD_DIM)
    g['lru_lambda'] = dlam.reshape(-1)
    conv_rows = [(z, BRANCH, BRANCH)] + [_full(a) for a in r['xd']]
    cw = _ew_bwd("lru_conv_dw", _f_conv, conv_rows, p['conv_w'] + [p['conv_b']], [dxc], [False] * CONV_WIDTH)
    g['lru_conv_w'] = jnp.concatenate([cw[CONV_WIDTH - 1 - k] for k in range(CONV_WIDTH)], axis=0)
    g['lru_conv_b'] = cw[CONV_WIDTH].reshape(-1)
    (dxl,) = _ew("lru_conv_dx", _f_conv_t, [_full(_advance(dxc, j)) for j in range(CONV_WIDTH)], p['conv_w'], [BRANCH],
                 out_dtypes=[BF16])
    dy1, dpre, dbglu = _ew_bwd("s5_glu_bwd", _f_s5_glu, [_full(r['y1']), _full(r['pre'])], [p['bglu']], [dys[0]],
                               [True, BF16])
    g['s5_b_glu'] = dbglu.reshape(-1)
    g['s5_w_glu'] = _mm("s5_glu_dw", r['y1_bf'], dpre, ta=True, out_dtype=BF16)
    dy1 = _mm("s5_glu_dx", dpre, p['wglu'], tb=True, add=dy1)
    dy0, du, dd = _ew_bwd("s5_y1_bwd", _f_s5_y1, [_full(r['hc_re']), _full(r['hc_im']), (z, 0, BRANCH)], [p['s5_d']],
                          [dy1], [BF16, False, True])
    g['s5_d'] = dd.reshape(-1)
    dh_re = _mm("s5_hc_re_dx", dy0, p['s5_cre'], tb=True)
    dh_im = _mm("s5_hc_im_dx", dy0, p['s5_cimn'], tb=True)
    take_c = lambda d: _blockdiag_take(d, S5_GROUPS, S5_STATE, S5_GROUP).transpose(0, 2, 1)
    g['s5_c_re'] = take_c(_mm("s5_hc_re_dw", r['h_re'], dy0, ta=True))
    g['s5_c_im'] = -take_c(_mm("s5_hc_im_dw", r['h_im'], dy0, ta=True))
    scanned = _scan_cplx("s5_scan_bwd", p['lam_re'], -p['lam_im'], dh_re, dh_im, reverse=True, h_re=r['h_re'],
                         h_im=r['h_im'], bn=256, rider=own.get('s5_scan_bwd'))
    gb_re, gb_im, dl_re, dl_im = scanned[:4]
    carried['s5_scan_bwd'] = scanned[4] if 's5_scan_bwd' in own else []
    du = _mm("s5_bu_re_dx", gb_re, p['s5_bre'], tb=True, add=du)
    du = _mm("s5_bu_im_dx", gb_im, p['s5_bim'], tb=True, add=du, out_dtype=BF16)
    take_b = lambda d: _blockdiag_take(d, S5_GROUPS, S5_GROUP, S5_STATE).transpose(0, 2, 1).reshape(S5_N, S5_GROUP)
    dbb_re = take_b(_mm("s5_bu_re_dw", z, gb_re, ta=True, a_w=BRANCH))
    dbb_im = take_b(_mm("s5_bu_im_dw", z, gb_im, ta=True, a_w=BRANCH))
    disc = _ew_bwd("s5_disc_bwd", _f_s5_disc, [_full(a) for a in p['disc_in']], [],
                   [dl_re.reshape(S5_N, 1), dl_im.reshape(S5_N, 1), dbb_re, dbb_im], [True] * 5, tl=S5_N)
    grp = (S5_GROUPS, S5_STATE)
    g['s5_a_re'], g['s5_a_im'] = disc[0].reshape(grp), disc[1].reshape(grp)
    g['s5_log_dt'] = disc[2].reshape(grp).sum(axis=1)
    g['s5_b_re'], g['s5_b_im'] = disc[3].reshape(grp + (S5_GROUP,)), disc[4].reshape(grp + (S5_GROUP,))
    dz = jnp.concatenate([du, dxl, dgate] + dqkv + list(dzg) + [dzf], axis=1)
    dwc = _mm("z_in_dw", r['x_bf'], dz, ta=True, out_dtype=BF16)
    g['w_in'] = jnp.concatenate([dwc[:, :Z_MAIN], dwc[:, Z_FG0:Z_FG0 + FOX_HEADS], dwc[:, Z_GATE0:Z_FG0]], axis=1)
    dx = _mm("z_in_dx", dz, p['wc'], tb=True, add=dx_n)
    return dx, g, carried


def _loss_head(y, target, tl=256):
    length, width = y.shape
    tl = min(tl, length)
    nt = length // tl

    def body(y_ref, t_ref, dy_ref, loss_ref, acc_sc):
        i = pl.program_id(0)

        @pl.when(i == 0)
        def _():
            acc_sc[...] = jnp.zeros_like(acc_sc)

        err = y_ref[...] - t_ref[...]
        dy_ref[...] = err / width
        acc_sc[...] += jnp.sum(jnp.square(err), axis=0, keepdims=True)

        @pl.when(i == nt - 1)
        def _():
            total = jnp.sum(acc_sc[...], axis=1, keepdims=True) * (0.5 / width)
            loss_ref[...] = jnp.broadcast_to(total, loss_ref.shape)

    spec = pl.BlockSpec((tl, width), lambda i: (i, 0))
    dy, loss = pl.pallas_call(
        body, grid=(nt,), in_specs=[spec, spec], out_specs=[spec, pl.BlockSpec((1, LANES), lambda i: (0, 0))],
        out_shape=[jax.ShapeDtypeStruct((length, width), F32), jax.ShapeDtypeStruct((1, LANES), F32)],
        scratch_shapes=[pltpu.VMEM((1, width), F32)], compiler_params=_params("arbitrary"), name="loss_head")(y, target)
    return loss[0, 0], dy


def _sum_parts(name, parts):
    count, rows, cols = parts.shape
    tr = 256

    def body(p_ref, o_ref):
        total = p_ref[0]
        for dev in range(1, count):
            total = total + p_ref[dev]
        o_ref[...] = total

    return pl.pallas_call(body, grid=(rows // tr,), in_specs=[pl.BlockSpec((count, tr, cols), lambda i: (0, i, 0))],
                          out_specs=pl.BlockSpec((tr, cols), lambda i: (i, 0)),
                          out_shape=jax.ShapeDtypeStruct((rows, cols), F32), compiler_params=_params("parallel"),
                          name=name)(parts)


def _adamw(name, w, parts, m, v):
    rows, cols = w.shape
    count = parts[0].shape[0]
    span = rows // len(parts)
    tr = span
    for cand in (256, 128, 64, 32, 16):
        if span % cand == 0:
            tr = cand
            break
    per_span = span // tr

    def body(*refs):
        w_ref, p_refs = refs[0], refs[1:1 + len(parts)]
        m_ref, v_ref, g_ref, d_ref, m2_ref, v2_ref = refs[1 + len(parts):]
        step = pl.program_id(0)
        grad = None
        for j, p_ref in enumerate(p_refs):
            total = p_ref[0].astype(F32)
            for dev in range(1, count):
                total = total + p_ref[dev].astype(F32)
            grad = total if grad is None else jnp.where(step >= j * per_span, total, grad)
        m2 = ADAM_B1 * m_ref[...] + (1.0 - ADAM_B1) * grad
        v2 = ADAM_B2 * v_ref[...] + (1.0 - ADAM_B2) * jnp.square(grad)
        m_hat = m2 / (1.0 - ADAM_B1 ** ADAM_STEP)
        v_hat = v2 / (1.0 - ADAM_B2 ** ADAM_STEP)
        g_ref[...] = grad
        d_ref[...] = -ADAM_LR * (m_hat / (jnp.sqrt(v_hat) + ADAM_EPS) + ADAM_WD * w_ref[...])
        m2_ref[...] = m2
        v2_ref[...] = v2

    spec = pl.BlockSpec((tr, cols), lambda i: (i, 0))
    pspecs = [pl.BlockSpec((count, tr, cols),
                           lambda i, j=j: (0, jnp.minimum(jnp.maximum(i - j * per_span, 0), per_span - 1), 0))
              for j in range(len(parts))]
    shape = jax.ShapeDtypeStruct((rows, cols), F32)
    return pl.pallas_call(body, grid=(rows // tr,), in_specs=[spec] + pspecs + [spec, spec], out_specs=[spec] * 4,
                          out_shape=[shape] * 4, compiler_params=_params("arbitrary"), name=name)(w, *parts, m, v)


class _NoExchange:
    def __init__(self, layers):
        self.layers = layers

    def weights(self, l, carried):
        return self.layers[l]

    def forward_riders(self, l):
        return {}

    def backward_riders(self, l):
        return {}

    def collect(self, l, grads, carried):
        pass


def _forward_backward(x, target, hooks):
    prepared, saved, carried = [], [], None
    x_bf = x.astype(BF16)
    for l in range(DEPTH):
        p = _prep_layer(hooks.weights(l, carried))
        x, x_bf, r, carried = _layer_fwd(x, x_bf, p, hooks.forward_riders(l))
        prepared.append(p)
        saved.append(r)
    loss, dx = _loss_head(x, target)
    grads = [None] * DEPTH
    for l in reversed(range(DEPTH)):
        dx, grads[l], carried = _layer_bwd(dx, saved[l], prepared[l], hooks.backward_riders(l))
        hooks.collect(l, grads[l], carried)
    return loss, dx, grads


def _exchange_copies(ins, outs, sems, scatter, with_arrivals):
    send_sems, recv_sems, local_sems = sems
    x, y, c = lax.axis_index("x"), lax.axis_index("y"), lax.axis_index("c")
    me = 4 * x + 2 * y + c
    local, sends, arrivals = [], [], []
    for a in range(len(ins)):
        local.append(pltpu.make_async_copy(ins[a].at[me] if scatter else ins[a], outs[a].at[me], local_sems.at[a]))
    for k in range(1, N_DEV):
        px = 1 - x if k & 4 else x
        py = 1 - y if k & 2 else y
        pc = 1 - c if k & 1 else c
        idx = 4 * px + 2 * py + pc
        for a in range(len(ins)):
            s = a * (N_DEV - 1) + k - 1
            src = ins[a].at[idx] if scatter else ins[a]
            common = dict(src_ref=src, send_sem=send_sems.at[s], recv_sem=recv_sems.at[s], device_id=(px, py, pc),
                          device_id_type=pl.DeviceIdType.MESH)
            sends.append(pltpu.make_async_remote_copy(dst_ref=outs[a].at[me], **common))
            if with_arrivals:
                arrivals.append(pltpu.make_async_remote_copy(dst_ref=outs[a].at[idx], **common))
    return local, sends, arrivals


def _exchange_start(ins, outs, sems, scatter):
    local, sends, _ = _exchange_copies(ins, outs, sems, scatter, False)
    for cp in local + sends:
        cp.start()


def _exchange_wait(ins, outs, sems, scatter):
    local, sends, arrivals = _exchange_copies(ins, outs, sems, scatter, True)
    for cp in local:
        cp.wait()
    for cp in sends:
        cp.wait_send()
    for cp in arrivals:
        cp.wait_recv()


def _exchange_parts(arrays, scatter):
    n = len(arrays)
    hbm = [pl.BlockSpec(memory_space=pltpu.HBM)] * n
    out_shape = [jax.ShapeDtypeStruct(a.shape if scatter else (N_DEV,) + a.shape, a.dtype) for a in arrays]
    nsem = n * (N_DEV - 1)
    sems = [pltpu.SemaphoreType.DMA((nsem,)), pltpu.SemaphoreType.DMA((nsem,)), pltpu.SemaphoreType.DMA((n,))]
    return hbm, out_shape, sems


def _exchange(name, arrays, scatter):
    n = len(arrays)
    hbm, out_shape, sems = _exchange_parts(arrays, scatter)

    def body(*refs):
        ins, outs, sem_refs = refs[:n], refs[n:2 * n], refs[2 * n:]
        _exchange_start(ins, outs, sem_refs, scatter)
        _exchange_wait(ins, outs, sem_refs, scatter)

    return pl.pallas_call(body, in_specs=hbm, out_specs=hbm, out_shape=out_shape, scratch_shapes=sems,
                          name=name)(*arrays)


def _call_carrying(name, body, grid, ins, in_specs, out_shape, out_specs, scratch, semantics, rider,
                   vmem=VMEM_LIMIT_BYTES):
    if rider is None:
        r_arrays, r_hbm, r_shape, r_sems = [], [], [], []
    else:
        r_arrays, scatter = rider
        r_hbm, r_shape, r_sems = _exchange_parts(r_arrays, scatter)
        semantics = ("arbitrary",) * len(grid)
    n_in, n_out, n_scr, n_r = len(ins), len(out_shape), len(scratch), len(r_arrays)

    def full_body(*refs):
        at = [0]

        def take(count):
            at[0] += count
            return refs[at[0] - count:at[0]]

        in_refs, r_in, out_refs, r_out, scr, r_scr = take(n_in), take(n_r), take(n_out), take(n_r), take(n_scr), take(3)
        ids = [pl.program_id(d) for d in range(len(grid))]
        if rider is not None:
            first = functools.reduce(jnp.logical_and, [i == 0 for i in ids])
            pl.when(first)(functools.partial(_exchange_start, r_in, r_out, r_scr, scatter))
        body(in_refs, out_refs, scr)
        if rider is not None:
            last = functools.reduce(jnp.logical_and, [i == g - 1 for i, g in zip(ids, grid)])
            pl.when(last)(functools.partial(_exchange_wait, r_in, r_out, r_scr, scatter))

    res = pl.pallas_call(
        full_body, grid=grid, in_specs=list(in_specs) + r_hbm, out_specs=list(out_specs) + r_hbm,
        out_shape=list(out_shape) + r_shape, scratch_shapes=list(scratch) + r_sems,
        compiler_params=_params(*semantics, vmem=vmem), name=name if rider is None else name + "_carrying")(
            *ins, *r_arrays)
    return res[:n_out], res[n_out:]


def _shard_2d(a):
    return a.reshape(-1, a.shape[-1])


def _full_layer_weight(name, t):
    if name in ('s5_w_glu', 'w_out', 'w_ffn_down'):
        return t.reshape(-1, t.shape[-1])
    if name == 'w_branch':
        return t.transpose(1, 2, 0, 3).reshape(3, BRANCH, D_MODEL)
    return t.transpose(1, 0, 2).reshape(t.shape[1], -1)


def _split_layer_grad(name, g):
    if name in ('s5_w_glu', 'w_out', 'w_ffn_down'):
        return g.reshape(N_DEV, g.shape[0] // N_DEV, g.shape[1])
    if name == 'w_branch':
        return g.reshape(3, BRANCH, N_DEV, D_MODEL // N_DEV).transpose(2, 0, 1, 3)
    return g.reshape(g.shape[0], N_DEV, g.shape[1] // N_DEV).transpose(1, 0, 2)


RIDING = [n for n in SHARDED if n != 'lru_conv_w']
FORWARD_CARRIERS = {'z_in': ['w_ffn_gate'], 's5_scan': ['w_ffn_down'],
                    'fox_fwd': ['w_in', 's5_w_glu', 'w_branch', 'w_out'], 'ffn_in': ['w_ffn_up']}
OWN_GRAD_CARRIERS = {'fox_bwd_kv': ['w_ffn_gate', 'w_ffn_up'], 's5_scan_bwd': ['w_ffn_down']}
NEXT_GRAD_CARRIER, NEXT_GRADS = 'fox_bwd_q', ['w_in', 's5_w_glu', 'w_branch', 'w_out']


class _Fsdp:
    def __init__(self, weights):
        self.weights_in = weights
        self.shard = {n: _shard_2d(weights[n]).astype(BF16) for n in RIDING}
        self.rows = {n: self.shard[n].shape[0] // DEPTH for n in RIDING}
        first = _exchange("gather_first_layer", [self.layer_shard(n, 0) for n in RIDING]
                          + [_shard_2d(weights['lru_conv_w'])], scatter=False)
        self.first = first[:-1]
        self.conv = first[-1].reshape((N_DEV,) + weights['lru_conv_w'].shape)
        self.outgoing = None
        self.incoming = {n: [None] * DEPTH for n in RIDING}

    def layer_shard(self, n, l):
        return self.shard[n][l * self.rows[n]:(l + 1) * self.rows[n]]

    def weights(self, l, carried):
        if l == 0:
            got = dict(zip(RIDING, self.first))
        else:
            got = {n: t for c, names in FORWARD_CARRIERS.items() for n, t in zip(names, carried[c])}
        w = {n: self.weights_in[n][l] for n in REPLICATED}
        for n in RIDING:
            w[n] = _full_layer_weight(n, got[n].reshape((N_DEV,) + self.weights_in[n].shape[1:]))
        w['lru_conv_w'] = _full_layer_weight('lru_conv_w', self.conv[:, l])
        return w

    def forward_riders(self, l):
        if l + 1 == DEPTH:
            return {}
        return {c: ([self.layer_shard(n, l + 1) for n in names], False) for c, names in FORWARD_CARRIERS.items()}

    def blocks(self, grads, names):
        return [_split_layer_grad(n, grads[n]).reshape(N_DEV, self.rows[n], -1).astype(BF16) for n in names]

    def backward_riders(self, l):
        riders = {'own': lambda grads: {c: (self.blocks(grads, names), True)
                                        for c, names in OWN_GRAD_CARRIERS.items()}}
        if self.outgoing is not None:
            riders[NEXT_GRAD_CARRIER] = (self.outgoing, True)
        return riders

    def collect(self, l, grads, carried):
        for n, t in zip(NEXT_GRADS, carried[NEXT_GRAD_CARRIER]):
            self.incoming[n][l + 1] = t
        for c, names in OWN_GRAD_CARRIERS.items():
            for n, t in zip(names, carried[c]):
                self.incoming[n][l] = t
        self.outgoing = self.blocks(grads, NEXT_GRADS)

    def finish(self, grads):
        conv = jnp.stack([_split_layer_grad('lru_conv_w', grads[l]['lru_conv_w']) for l in range(DEPTH)], axis=1)
        conv = conv.reshape(N_DEV, -1, conv.shape[-1]).astype(F32)
        last = _exchange("scatter_last_layer", self.outgoing + [conv], scatter=True)
        for n, t in zip(NEXT_GRADS, last[:-1]):
            self.incoming[n][0] = t
        return {**self.incoming, 'lru_conv_w': [last[-1]]}


def kernel(x, w_in, b_f, b_gate, s5_a_re, s5_a_im, s5_log_dt, s5_b_re, s5_b_im, s5_c_re, s5_c_im, s5_d, s5_w_glu, s5_b_glu, lru_conv_w, lru_conv_b, lru_w_a, lru_b_a, lru_w_x, lru_b_x, lru_lambda, w_branch, w_out, ln1_g, ln1_b, w_ffn_gate, w_ffn_up, w_ffn_down, ln2_g, ln2_b, loss_target, m_w_in, m_b_f, m_b_gate, m_s5_a_re, m_s5_a_im, m_s5_log_dt, m_s5_b_re, m_s5_b_im, m_s5_c_re, m_s5_c_im, m_s5_d, m_s5_w_glu, m_s5_b_glu, m_lru_conv_w, m_lru_conv_b, m_lru_w_a, m_lru_b_a, m_lru_w_x, m_lru_b_x, m_lru_lambda, m_w_branch, m_w_out, m_ln1_g, m_ln1_b, m_w_ffn_gate, m_w_ffn_up, m_w_ffn_down, m_ln2_g, m_ln2_b, v_w_in, v_b_f, v_b_gate, v_s5_a_re, v_s5_a_im, v_s5_log_dt, v_s5_b_re, v_s5_b_im, v_s5_c_re, v_s5_c_im, v_s5_d, v_s5_w_glu, v_s5_b_glu, v_lru_conv_w, v_lru_conv_b, v_lru_w_a, v_lru_b_a, v_lru_w_x, v_lru_b_x, v_lru_lambda, v_w_branch, v_w_out, v_ln1_g, v_ln1_b, v_w_ffn_gate, v_w_ffn_up, v_w_ffn_down, v_ln2_g, v_ln2_b):
    given = dict(locals())
    weights = {n: given[n] for n in WEIGHTS}
    moments_m = {n: given['m_' + n] for n in WEIGHTS}
    moments_v = {n: given['v_' + n] for n in WEIGHTS}

    hooks = _Fsdp(weights)
    loss_local, dx, grads = _forward_backward(x[0], loss_target[0], hooks)
    loss = lax.psum(loss_local, MESH_AXES)
    incoming = hooks.finish(grads)

    new = {}
    for n in SHARDED:
        res = _adamw("adamw_" + n, _shard_2d(weights[n]), incoming[n], _shard_2d(moments_m[n]), _shard_2d(moments_v[n]))
        new[n] = [t.reshape(weights[n].shape) for t in res]

    flat = jnp.concatenate([jnp.stack([grads[l][n] for l in range(DEPTH)]).astype(F32).reshape(-1) for n in REPLICATED])
    rows = -(-flat.shape[0] // (LANES * 256)) * 256
    packed = jnp.pad(flat, (0, rows * LANES - flat.shape[0])).reshape(rows, LANES)
    (arrived,) = _exchange("gather_small_grads", [packed], scatter=False)
    total, at = _sum_parts("sum_small_grads", arrived).reshape(-1), 0
    for n in REPLICATED:
        w2 = _shard_2d(weights[n])
        grad = total[at:at + w2.size].reshape((1,) + w2.shape)
        at += w2.size
        res = _adamw("adamw_" + n, w2, [grad], _shard_2d(moments_m[n]), _shard_2d(moments_v[n]))
        new[n] = [t.reshape(weights[n].shape) for t in res]

    return (loss, dx[None], *[new[n][0] for n in WEIGHTS], *[new[n][1] for n in WEIGHTS],
            *[new[n][2] for n in WEIGHTS], *[new[n][3] for n in WEIGHTS])
```

```python
import functools
import math

import jax
import jax.numpy as jnp
from jax import lax
from jax.experimental import pallas as pl
from jax.experimental.pallas import tpu as pltpu

F32 = jnp.float32
BF16 = jnp.bfloat16

D_MODEL = 1024
DEPTH = 4
BRANCH = 512
S5_GROUPS, S5_GROUP, S5_STATE = 32, 16, 64
S5_N = S5_GROUPS * S5_STATE
LRU_HEADS, LRU_HEAD_DIM = 8, 64
LRU_C = 8.0
CONV_WIDTH = 4
FOX_HEADS, FOX_HEAD_DIM = 8, 64
FFN_HIDDEN = 2816
ALPHA = (2.0 * DEPTH) ** 0.25
LN_EPS = 1e-5
IN_TOTAL = 6 * BRANCH + FOX_HEADS + 3 * D_MODEL
FG_PAD = 128
Z_MAIN = 6 * BRANCH
Z_GATE0 = Z_MAIN
Z_FG0 = Z_MAIN + 3 * D_MODEL
Z_TOTAL = Z_FG0 + FG_PAD
N_DEV = 8
MESH_AXES = ("x", "y", "c")

ADAM_LR, ADAM_B1, ADAM_B2, ADAM_EPS, ADAM_WD, ADAM_STEP = 0.001, 0.9, 0.999, 1e-08, 0.01, 10

VMEM_LIMIT_BYTES = 48 * 1024 * 1024
SCAN_SLACK_BYTES = 6 * 1024 * 1024
SUBLANES = 8
LANES = 128
NEG_BIG = -1e30

WEIGHTS = ['w_in', 'b_f', 'b_gate', 's5_a_re', 's5_a_im', 's5_log_dt', 's5_b_re', 's5_b_im', 's5_c_re', 's5_c_im',
           's5_d', 's5_w_glu', 's5_b_glu', 'lru_conv_w', 'lru_conv_b', 'lru_w_a', 'lru_b_a', 'lru_w_x', 'lru_b_x',
           'lru_lambda', 'w_branch', 'w_out', 'ln1_g', 'ln1_b', 'w_ffn_gate', 'w_ffn_up', 'w_ffn_down', 'ln2_g',
           'ln2_b']
SHARDED = ['w_in', 's5_w_glu', 'lru_conv_w', 'w_branch', 'w_out', 'w_ffn_gate', 'w_ffn_up', 'w_ffn_down']
REPLICATED = [n for n in WEIGHTS if n not in SHARDED]


def _params(*sem, vmem=VMEM_LIMIT_BYTES):
    return pltpu.CompilerParams(dimension_semantics=sem, vmem_limit_bytes=vmem)


def _tile(dim, want):
    if dim % LANES:
        return dim
    t = min(want, dim) // LANES * LANES
    while dim % t:
        t -= LANES
    return t


MM_VMEM_BUDGET_BYTES = 30 * 1024 * 1024
MM_MAX_TILE = 1024


def _divisor_tiles(dim, cap, must_divide=0):
    if dim % LANES:
        return [dim]
    out = [t for t in range(min(cap, dim) // LANES * LANES, 0, -LANES) if dim % t == 0 and must_divide % t == 0]
    return out or [dim]


def _mm_tiles(m, n, k, a_bytes, b_bytes, o_bytes, has_add, m_c0, n_c0, k_c0):
    for tk in _divisor_tiles(k, k, k_c0):
        best = None
        for tm in _divisor_tiles(m, MM_MAX_TILE, m_c0):
            for tn in _divisor_tiles(n, MM_MAX_TILE, n_c0):
                used = 2 * (tm * tk * a_bytes + tk * tn * b_bytes + tm * tn * o_bytes) + tm * tn * 4
                used += tm * tn * 4 if tk < k else 0
                used += 2 * tm * tn * 4 if has_add else 0
                if used <= MM_VMEM_BUDGET_BYTES and (best is None or tm * tn / (tm + tn) > best[0]):
                    best = (tm * tn / (tm + tn), tm, tn)
        if best is not None and (min(best[1], best[2]) >= 256 or tk <= 512):
            return best[1], best[2], tk
    raise ValueError("no matmul tiling fits VMEM")


def _mm(name, a, b, *, ta=False, tb=False, a_c0=0, a_w=None, b_c0=0, b_w=None, add=None, out_dtype=F32, rider=None):
    a_w = a.shape[1] if a_w is None else a_w
    b_w = b.shape[1] if b_w is None else b_w
    m, k = (a_w, a.shape[0]) if ta else (a.shape[0], a_w)
    n = b.shape[0] if tb else b_w
    assert k == (b_w if tb else b.shape[0]), (name, a.shape, b.shape)
    tm, tn, tk = _mm_tiles(m, n, k, a.dtype.itemsize, b.dtype.itemsize, jnp.dtype(out_dtype).itemsize,
                           add is not None, a_c0 if ta else 0, 0 if tb else b_c0,
                           math.gcd(0 if ta else a_c0, b_c0 if tb else 0))
    nk = k // tk
    a_off = a_c0 // (tm if ta else tk)
    b_off = b_c0 // (tk if tb else tn)
    assert a_c0 % (tm if ta else tk) == 0 and b_c0 % (tk if tb else tn) == 0, name
    dims = (((0 if ta else 1,), (1 if tb else 0,)), ((), ()))
    a_total, b_total = m * k * a.dtype.itemsize, n * k * b.dtype.itemsize
    a_stays = a_total + b_total * (m // tm) <= b_total + a_total * (n // tn)
    if nk > 1:
        a_stays = True

    def mn(o, i):
        return (o, i) if a_stays else (i, o)

    def body(in_refs, out_refs, scratch_refs):
        a_ref, b_ref = in_refs[0], in_refs[1]
        add_ref = in_refs[2] if add is not None else None
        (o_ref,) = out_refs
        part = lax.dot_general(a_ref[...].astype(BF16), b_ref[...].astype(BF16), dims, preferred_element_type=F32)

        def finish(r):
            if add is not None:
                r = r + add_ref[...]
            o_ref[...] = r.astype(o_ref.dtype)

        if nk == 1:
            finish(part)
            return
        (acc_ref,) = scratch_refs
        kk = pl.program_id(2)

        @pl.when(kk == 0)
        def _():
            acc_ref[...] = part

        @pl.when(kk > 0)
        def _():
            acc_ref[...] += part

        @pl.when(kk == nk - 1)
        def _():
            finish(acc_ref[...])

    def a_map(o, i, kk):
        im = mn(o, i)[0]
        return (kk, im + a_off) if ta else (im, kk + a_off)

    def b_map(o, i, kk):
        jn = mn(o, i)[1]
        return (jn, kk + b_off) if tb else (kk, jn + b_off)

    a_spec = pl.BlockSpec((tk, tm) if ta else (tm, tk), a_map)
    b_spec = pl.BlockSpec((tn, tk) if tb else (tk, tn), b_map)
    o_spec = pl.BlockSpec((tm, tn), lambda o, i, kk: mn(o, i))
    ins, in_specs = [a, b], [a_spec, b_spec]
    if add is not None:
        ins.append(add)
        in_specs.append(o_spec)
    grid = (m // tm, n // tn, nk) if a_stays else (n // tn, m // tm, nk)
    (out,), carried = _call_carrying(
        name, body, grid, ins, in_specs, [jax.ShapeDtypeStruct((m, n), out_dtype)], [o_spec],
        [pltpu.VMEM((tm, tn), F32)] if nk > 1 else [], ("parallel", "parallel", "arbitrary"), rider)
    return out if rider is None else (out, carried)


def _row_spec(tl, c0, w):
    assert c0 % w == 0
    return pl.BlockSpec((tl, w), lambda i: (i, c0 // w))


def _whole_spec(p):
    return pl.BlockSpec(p.shape, lambda i: (0,) * p.ndim)


def _ew(name, f, rows, prm, out_ws, tl=256, out_dtypes=None):
    out_dtypes = out_dtypes or [F32] * len(out_ws)
    nrows, nprm = len(rows), len(prm)
    length = rows[0][0].shape[0]
    tl = min(tl, length)

    def body(*refs):
        vals = [r[...] for r in refs[:nrows + nprm]]
        outs = f(*vals)
        for o_ref, o in zip(refs[nrows + nprm:], outs):
            o_ref[...] = o.astype(o_ref.dtype)

    return pl.pallas_call(
        body, grid=(length // tl,),
        in_specs=[_row_spec(tl, c0, w) for (_, c0, w) in rows] + [_whole_spec(p) for p in prm],
        out_specs=[_row_spec(tl, 0, w) for w in out_ws],
        out_shape=[jax.ShapeDtypeStruct((length, w), dt) for w, dt in zip(out_ws, out_dtypes)],
        compiler_params=_params("parallel"), name=name)(*[r[0] for r in rows], *prm)


def _ew_bwd(name, f, rows, prm, douts, row_grad, tl=256):
    nrows, nprm, nd = len(rows), len(prm), len(douts)
    length = rows[0][0].shape[0]
    tl = min(tl, length)
    want = [i for i in range(nrows) if row_grad[i]]

    def body(*refs):
        vals = [r[...] for r in refs[:nrows + nprm]]
        cts = tuple(r[...] for r in refs[nrows + nprm:nrows + nprm + nd])
        out_refs = refs[nrows + nprm + nd:]
        _, vjp = jax.vjp(lambda *v: tuple(f(*v)), *vals)
        grads = vjp(cts)
        for o_ref, i in zip(out_refs[:len(want)], want):
            o_ref[...] = grads[i].astype(o_ref.dtype)

        @pl.when(pl.program_id(0) == 0)
        def _():
            for o_ref in out_refs[len(want):]:
                o_ref[...] = jnp.zeros_like(o_ref)

        for o_ref, g in zip(out_refs[len(want):], grads[nrows:]):
            o_ref[...] += g

    return pl.pallas_call(
        body, grid=(length // tl,),
        in_specs=([_row_spec(tl, c0, w) for (_, c0, w) in rows] + [_whole_spec(p) for p in prm]
                  + [_row_spec(tl, 0, d.shape[1]) for d in douts]),
        out_specs=[_row_spec(tl, 0, rows[i][2]) for i in want] + [_whole_spec(p) for p in prm],
        out_shape=([jax.ShapeDtypeStruct((length, rows[i][2]), F32 if row_grad[i] is True else row_grad[i])
                    for i in want]
                   + [jax.ShapeDtypeStruct(p.shape, F32) for p in prm]),
        compiler_params=_params("arbitrary"), name=name)(*[r[0] for r in rows], *prm, *douts)


def _row_ids(width):
    return lax.broadcasted_iota(jnp.int32, (SUBLANES, width), 0)


def _shift_rows(v, d, reverse):
    return pltpu.roll(v, (SUBLANES - d) if reverse else d, 0)


def _scan_real(name, a, b, b2=None, *, reverse=False, bn=256):
    length, n = a.shape
    bn = _tile(n, bn)
    nb = length // SUBLANES
    operands = [a, b] if b2 is None else [a, b, b2]

    def body(*refs):
        a_ref, b_ref, h_ref = refs[0], refs[1], refs[-1]
        rows = _row_ids(bn)

        def step(it, carry):
            i = (nb - 1 - it) if reverse else it
            sl = pl.ds(pl.multiple_of(i * SUBLANES, SUBLANES), SUBLANES)
            av, bv = a_ref[sl, :], b_ref[sl, :]
            if b2 is not None:
                bv = bv + refs[2][sl, :]
            for d in (1, 2, 4):
                live = (rows < SUBLANES - d) if reverse else (rows >= d)
                a_in = jnp.where(live, _shift_rows(av, d, reverse), 1.0)
                b_in = jnp.where(live, _shift_rows(bv, d, reverse), 0.0)
                bv = bv + av * b_in
                av = av * a_in
            hv = bv + av * carry
            h_ref[sl, :] = hv
            edge = hv[0:1, :] if reverse else hv[SUBLANES - 1:SUBLANES, :]
            return jnp.broadcast_to(edge, (SUBLANES, bn))

        lax.fori_loop(0, nb, step, jnp.zeros((SUBLANES, bn), F32))

    spec = pl.BlockSpec((length, bn), lambda j: (0, j))
    return pl.pallas_call(body, grid=(n // bn,), in_specs=[spec] * len(operands), out_specs=spec,
                          out_shape=jax.ShapeDtypeStruct((length, n), F32),
                          compiler_params=_params("parallel"), name=name)(*operands)


def _cmul(ar, ai, br, bi):
    return ar * br - ai * bi, ar * bi + ai * br


def _scan_cplx(name, lam_re, lam_im, x_re, x_im, *, reverse=False, h_re=None, h_im=None, bn=128, rider=None):
    length, n = x_re.shape
    bn = _tile(n, bn)
    nb = length // SUBLANES
    with_dot = h_re is not None

    def body(in_refs, out_refs, scratch_refs):
        refs = tuple(in_refs) + tuple(out_refs)
        if with_dot:
            lr_ref, li_ref, xr_ref, xi_ref, hr_ref, hi_ref, gr_ref, gi_ref, dr_ref, di_ref = refs
        else:
            lr_ref, li_ref, xr_ref, xi_ref, gr_ref, gi_ref = refs
        rows = _row_ids(bn)
        lr = jnp.broadcast_to(lr_ref[...], (SUBLANES, bn))
        li = jnp.broadcast_to(li_ref[...], (SUBLANES, bn))
        powers = [(lr, li)]
        for _ in range(SUBLANES - 1):
            powers.append(_cmul(powers[-1][0], powers[-1][1], lr, li))
        zero = jnp.zeros((SUBLANES, bn), F32)
        steps = []
        for d in (1, 2, 4):
            live = (rows < SUBLANES - d) if reverse else (rows >= d)
            steps.append((d, jnp.where(live, powers[d - 1][0], 0.0), jnp.where(live, powers[d - 1][1], 0.0)))
        cr, ci = zero, zero
        for r in range(SUBLANES):
            e = (SUBLANES - r) if reverse else (r + 1)
            cr = jnp.where(rows == r, powers[e - 1][0], cr)
            ci = jnp.where(rows == r, powers[e - 1][1], ci)

        def step(it, carry):
            i = (nb - 1 - it) if reverse else it
            sl = pl.ds(pl.multiple_of(i * SUBLANES, SUBLANES), SUBLANES)
            vr, vi = xr_ref[sl, :], xi_ref[sl, :]
            for d, pr, pi in steps:
                sr, si = _cmul(pr, pi, _shift_rows(vr, d, reverse), _shift_rows(vi, d, reverse))
                vr, vi = vr + sr, vi + si
            kr, ki = _cmul(cr, ci, carry[0], carry[1])
            vr, vi = vr + kr, vi + ki
            gr_ref[sl, :] = vr
            gi_ref[sl, :] = vi
            er = vr[0:1, :] if reverse else vr[SUBLANES - 1:SUBLANES, :]
            ei = vi[0:1, :] if reverse else vi[SUBLANES - 1:SUBLANES, :]
            new = (jnp.broadcast_to(er, (SUBLANES, bn)), jnp.broadcast_to(ei, (SUBLANES, bn)))
            if not with_dot:
                return new
            prev = pl.ds(pl.multiple_of(jnp.maximum(i - 1, 0) * SUBLANES, SUBLANES), SUBLANES)
            keep = jnp.where(i > 0, 1.0, 0.0)
            pr_ = jnp.broadcast_to(hr_ref[prev, :][SUBLANES - 1:SUBLANES, :], (SUBLANES, bn)) * keep
            pi_ = jnp.broadcast_to(hi_ref[prev, :][SUBLANES - 1:SUBLANES, :], (SUBLANES, bn)) * keep
            hr = jnp.where(rows == 0, pr_, pltpu.roll(hr_ref[sl, :], 1, 0))
            hi = jnp.where(rows == 0, pi_, pltpu.roll(hi_ref[sl, :], 1, 0))
            return new + (carry[2] + vr * hr + vi * hi, carry[3] + vi * hr - vr * hi)

        init = (zero, zero, zero, zero) if with_dot else (zero, zero)
        out = lax.fori_loop(0, nb, step, init)
        if with_dot:
            dr_ref[...] = jnp.sum(out[2], axis=0, keepdims=True)
            di_ref[...] = jnp.sum(out[3], axis=0, keepdims=True)

    col = pl.BlockSpec((length, bn), lambda j: (0, j))
    vec = pl.BlockSpec((1, bn), lambda j: (0, j))
    ins = [lam_re, lam_im, x_re, x_im] + ([h_re, h_im] if with_dot else [])
    in_specs = [vec, vec, col, col] + ([col, col] if with_dot else [])
    out_specs = [col, col] + ([vec, vec] if with_dot else [])
    full = jax.ShapeDtypeStruct((length, n), F32)
    row = jax.ShapeDtypeStruct((1, n), F32)
    out_shape = [full, full] + ([row, row] if with_dot else [])
    columns = 6 if with_dot else 4
    outs, carried = _call_carrying(name, body, (n // bn,), ins, in_specs, out_shape, out_specs, [], ("parallel",), rider,
                                   vmem=2 * columns * length * bn * 4 + SCAN_SLACK_BYTES)
    return list(outs) if rider is None else list(outs) + [carried]


FOX_SCALE = FOX_HEAD_DIM ** -0.5
FOX_AUG = 128
FOX_CQ0 = FOX_HEAD_DIM
FOX_CK0 = FOX_HEAD_DIM + 3
NT = (((1,), (1,)), ((), ()))


def _fox_logits_t(ka, qa, on_diagonal):
    st = lax.dot_general(ka, qa, NT, preferred_element_type=F32)
    if on_diagonal:
        key = lax.broadcasted_iota(jnp.int32, st.shape, 0)
        query = lax.broadcasted_iota(jnp.int32, st.shape, 1)
        st = jnp.where(key <= query, st, NEG_BIG)
    return st


def _fox_pair(s, nt, q_first):
    if q_first:
        qi = sum((s >= (m * (m + 1)) // 2).astype(jnp.int32) for m in range(1, nt))
        return qi, s - ((qi * (qi + 1)) >> 1)
    ki = sum((s >= m * nt - (m * (m - 1)) // 2).astype(jnp.int32) for m in range(1, nt))
    return ki + s - (ki * nt - ((ki * (ki - 1)) >> 1)), ki


def _fox_specs(t, nt, q_first):
    q_idx = lambda s: _fox_pair(s, nt, q_first)[0]
    k_idx = lambda s: _fox_pair(s, nt, q_first)[1]
    rows = lambda idx, w: pl.BlockSpec((None, t, w), lambda h, s: (h, idx(s), 0))
    cols = lambda idx, w: pl.BlockSpec((None, w, t), lambda h, s: (h, 0, idx(s)))
    return rows, cols, q_idx, k_idx, (nt * (nt + 1)) // 2


def _fox_fwd(qa, ka, vt, t=512, rider=None):
    heads, length, _ = qa.shape
    dh = vt.shape[1]
    t = min(t, length)
    nt = length // t

    def body(in_refs, out_refs, scratch_refs):
        (qa_ref, ka_ref, vt_ref), (o_ref, lse_ref), (m_sc, l_sc, acc_sc) = in_refs, out_refs, scratch_refs
        qi, ki = _fox_pair(pl.program_id(1), nt, True)

        @pl.when(ki == 0)
        def _():
            m_sc[...] = jnp.full_like(m_sc, NEG_BIG)
            l_sc[...] = jnp.zeros_like(l_sc)
            acc_sc[...] = jnp.zeros_like(acc_sc)

        def step(on_diagonal):
            st = _fox_logits_t(ka_ref[...], qa_ref[...], on_diagonal)
            m_old = m_sc[...]
            m_new = jnp.maximum(m_old, jnp.max(st, axis=0, keepdims=True))
            pt = jnp.exp(st - m_new)
            scale = jnp.exp(m_old - m_new)
            l_sc[...] = scale * l_sc[...] + jnp.sum(pt, axis=0, keepdims=True)
            acc_sc[...] = scale * acc_sc[...] + jnp.dot(vt_ref[...], pt.astype(BF16), preferred_element_type=F32)
            m_sc[...] = m_new

        pl.when(ki < qi)(functools.partial(step, False))
        pl.when(ki == qi)(functools.partial(step, True))

        @pl.when(ki == qi)
        def _():
            o_ref[...] = acc_sc[...] / l_sc[...]
            lse_ref[...] = m_sc[...] + jnp.log(l_sc[...])

    rows, cols, q_idx, k_idx, pairs = _fox_specs(t, nt, True)
    (ot, lse), carried = _call_carrying(
        "fox_fwd", body, (heads, pairs), [qa, ka, vt],
        [rows(q_idx, FOX_AUG), rows(k_idx, FOX_AUG), cols(k_idx, dh)],
        [jax.ShapeDtypeStruct((heads, dh, length), F32), jax.ShapeDtypeStruct((heads, 1, length), F32)],
        [cols(q_idx, dh), cols(q_idx, 1)],
        [pltpu.VMEM((1, t), F32), pltpu.VMEM((1, t), F32), pltpu.VMEM((dh, t), F32)],
        ("parallel", "arbitrary"), rider)
    return ot, lse, carried


def _fox_ds_t(qa_ref, ka_ref, v_ref, dot_ref, ot_ref, lse_ref, on_diagonal):
    pt = jnp.exp(_fox_logits_t(ka_ref[...], qa_ref[...], on_diagonal) - lse_ref[...])
    dpt = jnp.dot(v_ref[...], dot_ref[...], preferred_element_type=F32)
    delta = jnp.sum(dot_ref[...].astype(F32) * ot_ref[...], axis=0, keepdims=True)
    return pt, pt * (dpt - delta)


def _fox_bwd_q(qa, ka, kat, v, dot, ot, lse, t=512, rider=None):
    heads, length, _ = qa.shape
    dh = v.shape[2]
    t = min(t, length)
    nt = length // t

    def body(in_refs, out_refs, scratch_refs):
        qa_ref, ka_ref, kat_ref, v_ref, dot_ref, ot_ref, lse_ref = in_refs
        (dq_ref, dc_ref), (acc_sc,) = out_refs, scratch_refs
        qi, ki = _fox_pair(pl.program_id(1), nt, True)

        @pl.when(ki == 0)
        def _():
            acc_sc[...] = jnp.zeros_like(acc_sc)

        def step(on_diagonal):
            _, dst = _fox_ds_t(qa_ref, ka_ref, v_ref, dot_ref, ot_ref, lse_ref, on_diagonal)
            acc_sc[...] += jnp.dot(kat_ref[...], dst.astype(BF16), preferred_element_type=F32)

        pl.when(ki < qi)(functools.partial(step, False))
        pl.when(ki == qi)(functools.partial(step, True))

        @pl.when(ki == qi)
        def _():
            dq_ref[...] = (acc_sc[0:dh, :] * FOX_SCALE).astype(dq_ref.dtype)
            dc_ref[...] = acc_sc[FOX_CQ0:FOX_CQ0 + 1, :]

    rows, cols, q_idx, k_idx, pairs = _fox_specs(t, nt, True)
    (dqt, dcq), carried = _call_carrying(
        "fox_bwd_q", body, (heads, pairs), [qa, ka, kat, v, dot, ot, lse],
        [rows(q_idx, FOX_AUG), rows(k_idx, FOX_AUG), cols(k_idx, FOX_AUG), rows(k_idx, dh), cols(q_idx, dh),
         cols(q_idx, dh), cols(q_idx, 1)],
        [jax.ShapeDtypeStruct((heads, dh, length), BF16), jax.ShapeDtypeStruct((heads, 1, length), F32)],
        [cols(q_idx, dh), cols(q_idx, 1)],
        [pltpu.VMEM((FOX_AUG, t), F32)], ("parallel", "arbitrary"), rider)
    return dqt, dcq, carried


def _fox_bwd_kv(qa, ka, v, do, dot, ot, lse, t=512, rider=None):
    heads, length, _ = qa.shape
    dh = v.shape[2]
    t = min(t, length)
    nt = length // t

    def body(in_refs, out_refs, scratch_refs):
        qa_ref, ka_ref, v_ref, do_ref, dot_ref, ot_ref, lse_ref = in_refs
        (dk_ref, dv_ref, dc_ref), (dka_sc, dv_sc) = out_refs, scratch_refs
        qi, ki = _fox_pair(pl.program_id(1), nt, False)

        @pl.when(qi == ki)
        def _():
            dka_sc[...] = jnp.zeros_like(dka_sc)
            dv_sc[...] = jnp.zeros_like(dv_sc)

        def step(on_diagonal):
            pt, dst = _fox_ds_t(qa_ref, ka_ref, v_ref, dot_ref, ot_ref, lse_ref, on_diagonal)
            dv_sc[...] += jnp.dot(pt.astype(BF16), do_ref[...], preferred_element_type=F32)
            dka_sc[...] += jnp.dot(dst.astype(BF16), qa_ref[...], preferred_element_type=F32)

        pl.when(qi > ki)(functools.partial(step, False))
        pl.when(qi == ki)(functools.partial(step, True))

        @pl.when(qi == nt - 1)
        def _():
            dka = dka_sc[...]
            lane = lax.broadcasted_iota(jnp.int32, dka.shape, 1)
            dk_ref[...] = dka_sc[:, :dh].astype(dk_ref.dtype)
            dc_ref[...] = jnp.sum(jnp.where(lane == FOX_CK0, dka, 0.0), axis=1, keepdims=True)
            dv_ref[...] = dv_sc[...].astype(dv_ref.dtype)

    rows, cols, q_idx, k_idx, pairs = _fox_specs(t, nt, False)
    big = jax.ShapeDtypeStruct((heads, length, dh), BF16)
    (dk, dv, dc), carried = _call_carrying(
        "fox_bwd_kv", body, (heads, pairs), [qa, ka, v, do, dot, ot, lse],
        [rows(q_idx, FOX_AUG), rows(k_idx, FOX_AUG), rows(k_idx, dh), rows(q_idx, dh), cols(q_idx, dh),
         cols(q_idx, dh), cols(q_idx, 1)],
        [big, big, jax.ShapeDtypeStruct((heads, length, 1), F32)], [rows(k_idx, dh), rows(k_idx, dh), rows(k_idx, 1)],
        [pltpu.VMEM((t, FOX_AUG), F32), pltpu.VMEM((t, dh), F32)], ("parallel", "arbitrary"), rider)
    return dk, dv, dc, carried


def _split3(x):
    hi = lax.reduce_precision(x, 8, 7)
    mid = lax.reduce_precision(x - hi, 8, 7)
    return [hi, mid, lax.reduce_precision(x - hi - mid, 8, 7)]


def _fox_operands(z, cum):
    length = z.shape[0]
    parts = jnp.stack(_split3(cum[:, :FOX_HEADS].T), axis=-1)
    ones = jnp.ones_like(parts)
    pad = jnp.zeros((FOX_HEADS, length, FOX_AUG - FOX_HEAD_DIM - 6), F32)
    qa = jnp.concatenate([_heads(z, 3 * BRANCH) * FOX_SCALE, parts, ones, pad], axis=-1).astype(BF16)
    ka = jnp.concatenate([_heads(z, 4 * BRANCH), ones, -parts, pad], axis=-1).astype(BF16)
    v = _heads(z, 5 * BRANCH).astype(BF16)
    return qa, ka, ka.transpose(0, 2, 1), v, v.transpose(0, 2, 1)


def _softplus(x):
    return jnp.maximum(x, 0.0) + jnp.log1p(jnp.exp(-jnp.abs(x)))


def _f_s5_disc(a_re, a_im, log_dt, b_re, b_im):
    dt = jnp.exp(log_dt)
    mag = jnp.exp(a_re * dt)
    lr, li = mag * jnp.cos(a_im * dt), mag * jnp.sin(a_im * dt)
    den = a_re * a_re + a_im * a_im
    qr = ((lr - 1.0) * a_re + li * a_im) / den
    qi = (li * a_re - (lr - 1.0) * a_im) / den
    return lr, li, qr * b_re - qi * b_im, qr * b_im + qi * b_re


def _f_s5_y1(hc_re, hc_im, u, d):
    return (jax.nn.gelu(hc_re + hc_im + d * u),)


def _f_s5_glu(y1, pre, b):
    return (y1 * jax.nn.sigmoid(pre + b),)


def _f_conv(x0, x1, x2, x3, w0, w1, w2, w3, b):
    return (b + w0 * x0 + w1 * x1 + w2 * x2 + w3 * x3,)


def _f_conv_t(d0, d1, d2, d3, w0, w1, w2, w3):
    return (w0 * d0 + w1 * d1 + w2 * d2 + w3 * d3,)


def _lru_coeffs(xc, pa, px, b_a, b_x, lam):
    r = jax.nn.sigmoid(pa + b_a)
    i = jax.nn.sigmoid(px + b_x)
    log_a = -LRU_C * _softplus(-lam) * r
    a = jnp.exp(log_a)
    mult = jnp.sqrt(-jnp.tanh(log_a) * (a * a + 1.0))
    return a, mult * (i * xc)


def _f_lru_gates(xc, pa, px, b_a, b_x, lam):
    return _lru_coeffs(xc, pa, px, b_a, b_x, lam)


def _f_lru_step(xc, pa, px, h_prev, b_a, b_x, lam):
    a, b = _lru_coeffs(xc, pa, px, b_a, b_x, lam)
    return (a * h_prev + b,)


def _f_lru_out(gate, h):
    return (jax.nn.gelu(gate) * h,)


def _f_logf(zf, bf):
    return (-_softplus(-(zf + bf)),)


def _f_merge(p0, p1, p2, z0, z1, z2, b0, b1, b2):
    return (jax.nn.sigmoid(z0 + b0) * p0 + jax.nn.sigmoid(z1 + b1) * p1 + jax.nn.sigmoid(z2 + b2) * p2,)


def _f_ln(x, r, g, b):
    s = ALPHA * x + r
    mu = jnp.mean(s, axis=-1, keepdims=True)
    var = jnp.mean(jnp.square(s - mu), axis=-1, keepdims=True)
    return ((s - mu) * lax.rsqrt(var + LN_EPS) * g + b,)


def _f_swiglu(hg, hu):
    return (jax.nn.silu(hg) * hu,)


def _full(a):
    return (a, 0, a.shape[1])


def _blockdiag(t):
    g, a, b = t.shape
    eye = jnp.eye(g, dtype=t.dtype)
    return (t[:, :, None, :] * eye[:, None, :, None]).reshape(g * a, g * b)


def _blockdiag_take(d, g, a, b):
    eye = jnp.eye(g, dtype=d.dtype)
    return (d.reshape(g, a, g, b) * eye[:, None, :, None]).sum(axis=2)


def _delay(a, j):
    return a if j == 0 else jnp.pad(a, ((j, 0), (0, 0)))[:a.shape[0]]


def _advance(a, j):
    return a if j == 0 else jnp.pad(a, ((0, j), (0, 0)))[j:]


def _heads(a, c0=0):
    length = a.shape[0]
    return a[:, c0:c0 + BRANCH].reshape(length, FOX_HEADS, FOX_HEAD_DIM).transpose(1, 0, 2)


def _unheads(a):
    return a.transpose(1, 0, 2).reshape(a.shape[1], BRANCH)


def _row(v):
    return v.reshape(1, -1).astype(F32)


def _col(v):
    return v.reshape(-1, 1).astype(F32)


def _prep_layer(w):
    p = {}
    w_in = w['w_in']
    p['wc'] = jnp.concatenate(
        [w_in[:, :Z_MAIN], w_in[:, Z_MAIN + FOX_HEADS:], w_in[:, Z_MAIN:Z_MAIN + FOX_HEADS],
         jnp.zeros((D_MODEL, FG_PAD - FOX_HEADS), w_in.dtype)], axis=1)
    p['b_f'] = jnp.pad(_row(w['b_f']), ((0, 0), (0, FG_PAD - FOX_HEADS)))
    p['b_gate'] = [_row(w['b_gate'][k * D_MODEL:(k + 1) * D_MODEL]) for k in range(3)]
    p['disc_in'] = [_col(w['s5_a_re']), _col(w['s5_a_im']), _col(jnp.repeat(w['s5_log_dt'], S5_STATE)),
                    w['s5_b_re'].reshape(S5_N, S5_GROUP), w['s5_b_im'].reshape(S5_N, S5_GROUP)]
    lam_re, lam_im, bb_re, bb_im = _ew("s5_disc", _f_s5_disc, [_full(a) for a in p['disc_in']], [],
                                       [1, 1, S5_GROUP, S5_GROUP], tl=S5_N)
    p['lam_re'], p['lam_im'] = lam_re.reshape(1, S5_N), lam_im.reshape(1, S5_N)
    to_blk = lambda t: _blockdiag(t.reshape(S5_GROUPS, S5_STATE, S5_GROUP).transpose(0, 2, 1)).astype(BF16)
    p['s5_bre'], p['s5_bim'] = to_blk(bb_re), to_blk(bb_im)
    p['s5_cre'] = _blockdiag(w['s5_c_re'].transpose(0, 2, 1)).astype(BF16)
    p['s5_cimn'] = _blockdiag(-w['s5_c_im'].transpose(0, 2, 1)).astype(BF16)
    p['s5_d'], p['wglu'], p['bglu'] = _row(w['s5_d']), w['s5_w_glu'], _row(w['s5_b_glu'])
    p['conv_w'] = [_row(w['lru_conv_w'][CONV_WIDTH - 1 - j]) for j in range(CONV_WIDTH)]
    p['conv_b'] = _row(w['lru_conv_b'])
    p['wax'] = jnp.concatenate([_blockdiag(w['lru_w_a']), _blockdiag(w['lru_w_x'])], axis=1).astype(BF16)
    p['b_a'], p['b_x'], p['lam'] = _row(w['lru_b_a']), _row(w['lru_b_x']), _row(w['lru_lambda'])
    p['wb'] = [w['w_branch'][k] for k in range(3)]
    p['wout'] = w['w_out']
    p['ln1'] = [_row(w['ln1_g']), _row(w['ln1_b'])]
    p['wgu'] = jnp.concatenate([w['w_ffn_gate'], w['w_ffn_up']], axis=1)
    p['wd'] = w['w_ffn_down']
    p['ln2'] = [_row(w['ln2_g']), _row(w['ln2_b'])]
    return p


def _with_copy(f):
    def g(*args):
        (y,) = f(*args)
        return y, y
    return g


def _layer_fwd(x, x_bf, p, riders=None):
    length = x.shape[0]
    riders, carried = riders or {}, {}
    r = {'x': x, 'x_bf': x_bf}
    z = _mm("z_in", x_bf, p['wc'], rider=riders.get('z_in'))
    if 'z_in' in riders:
        z, carried['z_in'] = z
    r['z'] = z
    bu_re = _mm("s5_bu_re", z, p['s5_bre'], a_w=BRANCH)
    bu_im = _mm("s5_bu_im", z, p['s5_bim'], a_w=BRANCH)
    scanned = _scan_cplx("s5_scan", p['lam_re'], p['lam_im'], bu_re, bu_im, bn=256, rider=riders.get('s5_scan'))
    r['h_re'], r['h_im'] = scanned[0], scanned[1]
    if 's5_scan' in riders:
        carried['s5_scan'] = scanned[2]
    r['hc_re'] = _mm("s5_hc_re", r['h_re'], p['s5_cre'])
    r['hc_im'] = _mm("s5_hc_im", r['h_im'], p['s5_cimn'])
    r['y1'], r['y1_bf'] = _ew("s5_y1", _with_copy(_f_s5_y1), [_full(r['hc_re']), _full(r['hc_im']), (z, 0, BRANCH)],
                              [p['s5_d']], [BRANCH, BRANCH], out_dtypes=[F32, BF16])
    r['pre'] = _mm("s5_glu_pre", r['y1_bf'], p['wglu'])
    (r['ys5'],) = _ew("s5_glu", _f_s5_glu, [_full(r['y1']), _full(r['pre'])], [p['bglu']], [BRANCH], out_dtypes=[BF16])
    xl = z[:, BRANCH:2 * BRANCH]
    r['xd'] = [_delay(xl, j) for j in range(1, CONV_WIDTH)]
    r['xc'], r['xc_bf'] = _ew("lru_conv", _with_copy(_f_conv), [(z, BRANCH, BRANCH)] + [_full(a) for a in r['xd']],
                              p['conv_w'] + [p['conv_b']], [BRANCH, BRANCH], out_dtypes=[F32, BF16])
    r['papx'] = _mm("lru_gate_mm", r['xc_bf'], p['wax'])
    r['a'], b = _ew("lru_gates", _f_lru_gates, [_full(r['xc']), (r['papx'], 0, BRANCH), (r['papx'], BRANCH, BRANCH)],
                    [p['b_a'], p['b_x'], p['lam']], [BRANCH, BRANCH])
    r['h'] = _scan_real("lru_scan", r['a'], b)
    (r['ylru'],) = _ew("lru_out", _f_lru_out, [(z, 2 * BRANCH, BRANCH), _full(r['h'])], [], [BRANCH], out_dtypes=[BF16])
    (logf,) = _ew("fox_logf", _f_logf, [(z, Z_FG0, FG_PAD)], [p['b_f']], [FG_PAD])
    cum = _scan_real("fox_cum", jnp.ones((length, FG_PAD), F32), logf)
    qa, ka, kat, v, vt = _fox_operands(z, cum)
    r['fox'] = (qa, ka, kat, v)
    r['ot'], r['lse'], brought = _fox_fwd(qa, ka, vt, rider=riders.get('fox_fwd'))
    if 'fox_fwd' in riders:
        carried['fox_fwd'] = brought
    r['yfox'] = r['ot'].reshape(BRANCH, length).T.astype(BF16)
    ys = [r['ys5'], r['ylru'], r['yfox']]
    r['proj'] = [_mm("proj_%d" % k, ys[k], p['wb'][k]) for k in range(3)]
    gate_rows = [(z, Z_GATE0 + k * D_MODEL, D_MODEL) for k in range(3)]
    (r['mix'],) = _ew("merge", _f_merge, [_full(a) for a in r['proj']] + gate_rows, p['b_gate'], [D_MODEL], tl=128,
                      out_dtypes=[BF16])
    r['mixed'] = _mm("w_out", r['mix'], p['wout'])
    two = dict(out_ws=[D_MODEL, D_MODEL], out_dtypes=[F32, BF16])
    x1, r['x1_bf'] = _ew("ln1", _with_copy(_f_ln), [_full(x), _full(r['mixed'])], p['ln1'], **two)
    r['x1'] = x1
    r['hgu'] = _mm("ffn_in", r['x1_bf'], p['wgu'], rider=riders.get('ffn_in'))
    if 'ffn_in' in riders:
        r['hgu'], carried['ffn_in'] = r['hgu']
    (r['hid'],) = _ew("swiglu", _f_swiglu, [(r['hgu'], 0, FFN_HIDDEN), (r['hgu'], FFN_HIDDEN, FFN_HIDDEN)], [],
                      [FFN_HIDDEN], tl=128, out_dtypes=[BF16])
    r['f'] = _mm("ffn_out", r['hid'], p['wd'])
    x2, x2_bf = _ew("ln2", _with_copy(_f_ln), [_full(x1), _full(r['f'])], p['ln2'], **two)
    return x2, x2_bf, r, carried


def _layer_bwd(dx2, r, p, riders=None):
    g, riders, carried = {}, riders or {}, {}
    x, z, x1 = r['x'], r['z'], r['x1']
    dx1_n, df, g['ln2_g'], g['ln2_b'] = _ew_bwd("ln2_bwd", _f_ln, [_full(x1), _full(r['f'])], p['ln2'], [dx2],
                                                [True, BF16])
    dhid = _mm("ffn_out_dx", df, p['wd'], tb=True)
    g['w_ffn_down'] = _mm("ffn_out_dw", r['hid'], df, ta=True, out_dtype=BF16)
    hgu_rows = [(r['hgu'], 0, FFN_HIDDEN), (r['hgu'], FFN_HIDDEN, FFN_HIDDEN)]
    dhg, dhu = _ew_bwd("swiglu_bwd", _f_swiglu, hgu_rows, [], [dhid], [BF16, BF16], tl=128)
    g['w_ffn_gate'] = _mm("ffn_gate_dw", r['x1_bf'], dhg, ta=True, out_dtype=BF16)
    g['w_ffn_up'] = _mm("ffn_up_dw", r['x1_bf'], dhu, ta=True, out_dtype=BF16)
    dx1 = _mm("ffn_gate_dx", dhg, p['wgu'], tb=True, b_w=FFN_HIDDEN, add=dx1_n)
    dx1 = _mm("ffn_up_dx", dhu, p['wgu'], tb=True, b_c0=FFN_HIDDEN, b_w=FFN_HIDDEN, add=dx1)
    dx_n, dmixed, g['ln1_g'], g['ln1_b'] = _ew_bwd("ln1_bwd", _f_ln, [_full(x), _full(r['mixed'])], p['ln1'], [dx1],
                                                   [True, BF16])
    dmix = _mm("w_out_dx", dmixed, p['wout'], tb=True)
    g['w_out'] = _mm("w_out_dw", r['mix'], dmixed, ta=True, out_dtype=BF16)
    gate_rows = [(z, Z_GATE0 + k * D_MODEL, D_MODEL) for k in range(3)]
    mg = _ew_bwd("merge_bwd", _f_merge, [_full(a) for a in r['proj']] + gate_rows, p['b_gate'], [dmix], [BF16] * 6,
                 tl=128)
    dproj, dzg = mg[0:3], mg[3:6]
    g['b_gate'] = jnp.concatenate([b.reshape(-1) for b in mg[6:9]])
    ys = [r['ys5'], r['ylru'], r['yfox']]
    dys = [_mm("proj_%d_dx" % k, dproj[k], p['wb'][k], tb=True, out_dtype=BF16 if k == 2 else F32) for k in range(3)]
    g['w_branch'] = jnp.stack([_mm("proj_%d_dw" % k, ys[k], dproj[k], ta=True, out_dtype=BF16) for k in range(3)])
    qa, ka, kat, v = r['fox']
    do = _heads(dys[2])
    dot = do.transpose(0, 2, 1)
    own = riders['own'](g) if 'own' in riders else {}
    dqt, dcq, carried['fox_bwd_q'] = _fox_bwd_q(qa, ka, kat, v, dot, r['ot'], r['lse'], rider=riders.get('fox_bwd_q'))
    dkh, dvh, dck, carried['fox_bwd_kv'] = _fox_bwd_kv(qa, ka, v, do, dot, r['ot'], r['lse'],
                                                       rider=own.get('fox_bwd_kv'))
    pad_heads = lambda a: jnp.pad(a.T, ((0, 0), (0, FG_PAD - FOX_HEADS)))
    dlogf = _scan_real("fox_cum_bwd", jnp.ones((x.shape[0], FG_PAD), F32), pad_heads(dcq[:, 0, :]),
                       pad_heads(-dck[:, :, 0]), reverse=True)
    dqkv = [dqt.reshape(BRANCH, x.shape[0]).T, _unheads(dkh), _unheads(dvh)]
    dzf, dbf = _ew_bwd("fox_logf_bwd", _f_logf, [(z, Z_FG0, FG_PAD)], [p['b_f']], [dlogf], [BF16])
    g['b_f'] = dbf[0, :FOX_HEADS]
    dgate, dh = _ew_bwd("lru_out_bwd", _f_lru_out, [(z, 2 * BRANCH, BRANCH), _full(r['h'])], [], [dys[1]], [BF16, True])
    db = _scan_real("lru_scan_bwd", _advance(r['a'], 1), dh, reverse=True)
    gates_rows = [_full(r['xc']), (r['papx'], 0, BRANCH), (r['papx'], BRANCH, BRANCH), _full(_delay(r['h'], 1))]
    dxc, dpa, dpx, db_a, db_x, dlam = _ew_bwd("lru_gates_bwd", _f_lru_step, gates_rows, [p['b_a'], p['b_x'], p['lam']],
                                              [db], [True, BF16, BF16, False])
    dxc = _mm("lru_a_dx", dpa, p['wax'], tb=True, b_w=BRANCH, add=dxc)
    dxc = _mm("lru_x_dx", dpx, p['wax'], tb=True, b_c0=BRANCH, b_w=BRANCH, add=dxc)
    take_heads = lambda d: _blockdiag_take(d, LRU_HEADS, LRU_HEAD_DIM, LRU_HEAD_DIM)
    g['lru_w_a'] = take_heads(_mm("lru_a_dw", r['xc_bf'], dpa, ta=True))
    g['lru_w_x'] = take_heads(_mm("lru_x_dw", r['xc_bf'], dpx, ta=True))
    g['lru_b_a'] = db_a.reshape(LRU_HEADS, LRU_HEAD_DIM)
    g['lru_b_x'] = db_x.reshape(LRU_HEADS, LRU_HEAD_DIM)
    g['lru_lambda'] = dlam.reshape(-1)
    conv_rows = [(z, BRANCH, BRANCH)] + [_full(a) for a in r['xd']]
    cw = _ew_bwd("lru_conv_dw", _f_conv, conv_rows, p['conv_w'] + [p['conv_b']], [dxc], [False] * CONV_WIDTH)
    g['lru_conv_w'] = jnp.concatenate([cw[CONV_WIDTH - 1 - k] for k in range(CONV_WIDTH)], axis=0)
    g['lru_conv_b'] = cw[CONV_WIDTH].reshape(-1)
    (dxl,) = _ew("lru_conv_dx", _f_conv_t, [_full(_advance(dxc, j)) for j in range(CONV_WIDTH)], p['conv_w'], [BRANCH],
                 out_dtypes=[BF16])
    dy1, dpre, dbglu = _ew_bwd("s5_glu_bwd", _f_s5_glu, [_full(r['y1']), _full(r['pre'])], [p['bglu']], [dys[0]],
                               [True, BF16])
    g['s5_b_glu'] = dbglu.reshape(-1)
    g['s5_w_glu'] = _mm("s5_glu_dw", r['y1_bf'], dpre, ta=True, out_dtype=BF16)
    dy1 = _mm("s5_glu_dx", dpre, p['wglu'], tb=True, add=dy1)
    dy0, du, dd = _ew_bwd("s5_y1_bwd", _f_s5_y1, [_full(r['hc_re']), _full(r['hc_im']), (z, 0, BRANCH)], [p['s5_d']],
                          [dy1], [BF16, False, True])
    g['s5_d'] = dd.reshape(-1)
    dh_re = _mm("s5_hc_re_dx", dy0, p['s5_cre'], tb=True)
    dh_im = _mm("s5_hc_im_dx", dy0, p['s5_cimn'], tb=True)
    take_c = lambda d: _blockdiag_take(d, S5_GROUPS, S5_STATE, S5_GROUP).transpose(0, 2, 1)
    g['s5_c_re'] = take_c(_mm("s5_hc_re_dw", r['h_re'], dy0, ta=True))
    g['s5_c_im'] = -take_c(_mm("s5_hc_im_dw", r['h_im'], dy0, ta=True))
    scanned = _scan_cplx("s5_scan_bwd", p['lam_re'], -p['lam_im'], dh_re, dh_im, reverse=True, h_re=r['h_re'],
                         h_im=r['h_im'], bn=256, rider=own.get('s5_scan_bwd'))
    gb_re, gb_im, dl_re, dl_im = scanned[:4]
    carried['s5_scan_bwd'] = scanned[4] if 's5_scan_bwd' in own else []
    du = _mm("s5_bu_re_dx", gb_re, p['s5_bre'], tb=True, add=du)
    du = _mm("s5_bu_im_dx", gb_im, p['s5_bim'], tb=True, add=du, out_dtype=BF16)
    take_b = lambda d: _blockdiag_take(d, S5_GROUPS, S5_GROUP, S5_STATE).transpose(0, 2, 1).reshape(S5_N, S5_GROUP)
    dbb_re = take_b(_mm("s5_bu_re_dw", z, gb_re, ta=True, a_w=BRANCH))
    dbb_im = take_b(_mm("s5_bu_im_dw", z, gb_im, ta=True, a_w=BRANCH))
    disc = _ew_bwd("s5_disc_bwd", _f_s5_disc, [_full(a) for a in p['disc_in']], [],
                   [dl_re.reshape(S5_N, 1), dl_im.reshape(S5_N, 1), dbb_re, dbb_im], [True] * 5, tl=S5_N)
    grp = (S5_GROUPS, S5_STATE)
    g['s5_a_re'], g['s5_a_im'] = disc[0].reshape(grp), disc[1].reshape(grp)
    g['s5_log_dt'] = disc[2].reshape(grp).sum(axis=1)
    g['s5_b_re'], g['s5_b_im'] = disc[3].reshape(grp + (S5_GROUP,)), disc[4].reshape(grp + (S5_GROUP,))
    dz = jnp.concatenate([du, dxl, dgate] + dqkv + list(dzg) + [dzf], axis=1)
    dwc = _mm("z_in_dw", r['x_bf'], dz, ta=True, out_dtype=BF16)
    g['w_in'] = jnp.concatenate([dwc[:, :Z_MAIN], dwc[:, Z_FG0:Z_FG0 + FOX_HEADS], dwc[:, Z_GATE0:Z_FG0]], axis=1)
    dx = _mm("z_in_dx", dz, p['wc'], tb=True, add=dx_n)
    return dx, g, carried


def _loss_head(y, target, tl=256):
    length, width = y.shape
    tl = min(tl, length)
    nt = length // tl

    def body(y_ref, t_ref, dy_ref, loss_ref, acc_sc):
        i = pl.program_id(0)

        @pl.when(i == 0)
        def _():
            acc_sc[...] = jnp.zeros_like(acc_sc)

        err = y_ref[...] - t_ref[...]
        dy_ref[...] = err / width
        acc_sc[...] += jnp.sum(jnp.square(err), axis=0, keepdims=True)

        @pl.when(i == nt - 1)
        def _():
            total = jnp.sum(acc_sc[...], axis=1, keepdims=True) * (0.5 / width)
            loss_ref[...] = jnp.broadcast_to(total, loss_ref.shape)

    spec = pl.BlockSpec((tl, width), lambda i: (i, 0))
    dy, loss = pl.pallas_call(
        body, grid=(nt,), in_specs=[spec, spec], out_specs=[spec, pl.BlockSpec((1, LANES), lambda i: (0, 0))],
        out_shape=[jax.ShapeDtypeStruct((length, width), F32), jax.ShapeDtypeStruct((1, LANES), F32)],
        scratch_shapes=[pltpu.VMEM((1, width), F32)], compiler_params=_params("arbitrary"), name="loss_head")(y, target)
    return loss[0, 0], dy


def _sum_parts(name, parts):
    count, rows, cols = parts.shape
    tr = 256

    def body(p_ref, o_ref):
        total = p_ref[0]
        for dev in range(1, count):
            total = total + p_ref[dev]
        o_ref[...] = total

    return pl.pallas_call(body, grid=(rows // tr,), in_specs=[pl.BlockSpec((count, tr, cols), lambda i: (0, i, 0))],
                          out_specs=pl.BlockSpec((tr, cols), lambda i: (i, 0)),
                          out_shape=jax.ShapeDtypeStruct((rows, cols), F32), compiler_params=_params("parallel"),
                          name=name)(parts)


def _adamw(name, w, parts, m, v):
    rows, cols = w.shape
    count = parts[0].shape[0]
    span = rows // len(parts)
    tr = span
    for cand in (256, 128, 64, 32, 16):
        if span % cand == 0:
            tr = cand
            break
    per_span = span // tr

    def body(*refs):
        w_ref, p_refs = refs[0], refs[1:1 + len(parts)]
        m_ref, v_ref, g_ref, d_ref, m2_ref, v2_ref = refs[1 + len(parts):]
        step = pl.program_id(0)
        grad = None
        for j, p_ref in enumerate(p_refs):
            total = p_ref[0].astype(F32)
            for dev in range(1, count):
                total = total + p_ref[dev].astype(F32)
            grad = total if grad is None else jnp.where(step >= j * per_span, total, grad)
        m2 = ADAM_B1 * m_ref[...] + (1.0 - ADAM_B1) * grad
        v2 = ADAM_B2 * v_ref[...] + (1.0 - ADAM_B2) * jnp.square(grad)
        m_hat = m2 / (1.0 - ADAM_B1 ** ADAM_STEP)
        v_hat = v2 / (1.0 - ADAM_B2 ** ADAM_STEP)
        g_ref[...] = grad
        d_ref[...] = -ADAM_LR * (m_hat / (jnp.sqrt(v_hat) + ADAM_EPS) + ADAM_WD * w_ref[...])
        m2_ref[...] = m2
        v2_ref[...] = v2

    spec = pl.BlockSpec((tr, cols), lambda i: (i, 0))
    pspecs = [pl.BlockSpec((count, tr, cols),
                           lambda i, j=j: (0, jnp.minimum(jnp.maximum(i - j * per_span, 0), per_span - 1), 0))
              for j in range(len(parts))]
    shape = jax.ShapeDtypeStruct((rows, cols), F32)
    return pl.pallas_call(body, grid=(rows // tr,), in_specs=[spec] + pspecs + [spec, spec], out_specs=[spec] * 4,
                          out_shape=[shape] * 4, compiler_params=_params("arbitrary"), name=name)(w, *parts, m, v)


class _NoExchange:
    def __init__(self, layers):
        self.layers = layers

    def weights(self, l, carried):
        return self.layers[l]

    def forward_riders(self, l):
        return {}

    def backward_riders(self, l):
        return {}

    def collect(self, l, grads, carried):
        pass


def _forward_backward(x, target, hooks):
    prepared, saved, carried = [], [], None
    x_bf = x.astype(BF16)
    for l in range(DEPTH):
        p = _prep_layer(hooks.weights(l, carried))
        x, x_bf, r, carried = _layer_fwd(x, x_bf, p, hooks.forward_riders(l))
        prepared.append(p)
        saved.append(r)
    loss, dx = _loss_head(x, target)
    grads = [None] * DEPTH
    for l in reversed(range(DEPTH)):
        dx, grads[l], carried = _layer_bwd(dx, saved[l], prepared[l], hooks.backward_riders(l))
        hooks.collect(l, grads[l], carried)
    return loss, dx, grads


def _exchange_copies(ins, outs, sems, scatter, with_arrivals):
    send_sems, recv_sems, local_sems = sems
    x, y, c = lax.axis_index("x"), lax.axis_index("y"), lax.axis_index("c")
    me = 4 * x + 2 * y + c
    local, sends, arrivals = [], [], []
    for a in range(len(ins)):
        local.append(pltpu.make_async_copy(ins[a].at[me] if scatter else ins[a], outs[a].at[me], local_sems.at[a]))
    for k in range(1, N_DEV):
        px = 1 - x if k & 4 else x
        py = 1 - y if k & 2 else y
        pc = 1 - c if k & 1 else c
        idx = 4 * px + 2 * py + pc
        for a in range(len(ins)):
            s = a * (N_DEV - 1) + k - 1
            src = ins[a].at[idx] if scatter else ins[a]
            common = dict(src_ref=src, send_sem=send_sems.at[s], recv_sem=recv_sems.at[s], device_id=(px, py, pc),
                          device_id_type=pl.DeviceIdType.MESH)
            sends.append(pltpu.make_async_remote_copy(dst_ref=outs[a].at[me], **common))
            if with_arrivals:
                arrivals.append(pltpu.make_async_remote_copy(dst_ref=outs[a].at[idx], **common))
    return local, sends, arrivals


def _exchange_start(ins, outs, sems, scatter):
    local, sends, _ = _exchange_copies(ins, outs, sems, scatter, False)
    for cp in local + sends:
        cp.start()


def _exchange_wait(ins, outs, sems, scatter):
    local, sends, arrivals = _exchange_copies(ins, outs, sems, scatter, True)
    for cp in local:
        cp.wait()
    for cp in sends:
        cp.wait_send()
    for cp in arrivals:
        cp.wait_recv()


def _exchange_parts(arrays, scatter):
    n = len(arrays)
    hbm = [pl.BlockSpec(memory_space=pltpu.HBM)] * n
    out_shape = [jax.ShapeDtypeStruct(a.shape if scatter else (N_DEV,) + a.shape, a.dtype) for a in arrays]
    nsem = n * (N_DEV - 1)
    sems = [pltpu.SemaphoreType.DMA((nsem,)), pltpu.SemaphoreType.DMA((nsem,)), pltpu.SemaphoreType.DMA((n,))]
    return hbm, out_shape, sems


def _exchange(name, arrays, scatter):
    n = len(arrays)
    hbm, out_shape, sems = _exchange_parts(arrays, scatter)

    def body(*refs):
        ins, outs, sem_refs = refs[:n], refs[n:2 * n], refs[2 * n:]
        _exchange_start(ins, outs, sem_refs, scatter)
        _exchange_wait(ins, outs, sem_refs, scatter)

    return pl.pallas_call(body, in_specs=hbm, out_specs=hbm, out_shape=out_shape, scratch_shapes=sems,
                          name=name)(*arrays)


def _call_carrying(name, body, grid, ins, in_specs, out_shape, out_specs, scratch, semantics, rider,
                   vmem=VMEM_LIMIT_BYTES):
    if rider is None:
        r_arrays, r_hbm, r_shape, r_sems = [], [], [], []
    else:
        r_arrays, scatter = rider
        r_hbm, r_shape, r_sems = _exchange_parts(r_arrays, scatter)
        semantics = ("arbitrary",) * len(grid)
    n_in, n_out, n_scr, n_r = len(ins), len(out_shape), len(scratch), len(r_arrays)

    def full_body(*refs):
        at = [0]

        def take(count):
            at[0] += count
            return refs[at[0] - count:at[0]]

        in_refs, r_in, out_refs, r_out, scr, r_scr = take(n_in), take(n_r), take(n_out), take(n_r), take(n_scr), take(3)
        ids = [pl.program_id(d) for d in range(len(grid))]
        if rider is not None:
            first = functools.reduce(jnp.logical_and, [i == 0 for i in ids])
            pl.when(first)(functools.partial(_exchange_start, r_in, r_out, r_scr, scatter))
        body(in_refs, out_refs, scr)
        if rider is not None:
            last = functools.reduce(jnp.logical_and, [i == g - 1 for i, g in zip(ids, grid)])
            pl.when(last)(functools.partial(_exchange_wait, r_in, r_out, r_scr, scatter))

    res = pl.pallas_call(
        full_body, grid=grid, in_specs=list(in_specs) + r_hbm, out_specs=list(out_specs) + r_hbm,
        out_shape=list(out_shape) + r_shape, scratch_shapes=list(scratch) + r_sems,
        compiler_params=_params(*semantics, vmem=vmem), name=name if rider is None else name + "_carrying")(
            *ins, *r_arrays)
    return res[:n_out], res[n_out:]


def _shard_2d(a):
    return a.reshape(-1, a.shape[-1])


def _full_layer_weight(name, t):
    if name in ('s5_w_glu', 'w_out', 'w_ffn_down'):
        return t.reshape(-1, t.shape[-1])
    if name == 'w_branch':
        return t.transpose(1, 2, 0, 3).reshape(3, BRANCH, D_MODEL)
    return t.transpose(1, 0, 2).reshape(t.shape[1], -1)


def _split_layer_grad(name, g):
    if name in ('s5_w_glu', 'w_out', 'w_ffn_down'):
        return g.reshape(N_DEV, g.shape[0] // N_DEV, g.shape[1])
    if name == 'w_branch':
        return g.reshape(3, BRANCH, N_DEV, D_MODEL // N_DEV).transpose(2, 0, 1, 3)
    return g.reshape(g.shape[0], N_DEV, g.shape[1] // N_DEV).transpose(1, 0, 2)


RIDING = [n for n in SHARDED if n != 'lru_conv_w']
FORWARD_CARRIERS = {'z_in': ['w_ffn_gate'], 's5_scan': ['w_ffn_down'],
                    'fox_fwd': ['w_in', 's5_w_glu', 'w_branch', 'w_out'], 'ffn_in': ['w_ffn_up']}
OWN_GRAD_CARRIERS = {'fox_bwd_kv': ['w_ffn_gate', 'w_ffn_up'], 's5_scan_bwd': ['w_ffn_down']}
NEXT_GRAD_CARRIER, NEXT_GRADS = 'fox_bwd_q', ['w_in', 's5_w_glu', 'w_branch', 'w_out']


class _Fsdp:
    def __init__(self, weights):
        self.weights_in = weights
        self.shard = {n: _shard_2d(weights[n]).astype(BF16) for n in RIDING}
        self.rows = {n: self.shard[n].shape[0] // DEPTH for n in RIDING}
        first = _exchange("gather_first_layer", [self.layer_shard(n, 0) for n in RIDING]
                          + [_shard_2d(weights['lru_conv_w'])], scatter=False)
        self.first = first[:-1]
        self.conv = first[-1].reshape((N_DEV,) + weights['lru_conv_w'].shape)
        self.outgoing = None
        self.incoming = {n: [None] * DEPTH for n in RIDING}

    def layer_shard(self, n, l):
        return self.shard[n][l * self.rows[n]:(l + 1) * self.rows[n]]

    def weights(self, l, carried):
        if l == 0:
            got = dict(zip(RIDING, self.first))
        else:
            got = {n: t for c, names in FORWARD_CARRIERS.items() for n, t in zip(names, carried[c])}
        w = {n: self.weights_in[n][l] for n in REPLICATED}
        for n in RIDING:
            w[n] = _full_layer_weight(n, got[n].reshape((N_DEV,) + self.weights_in[n].shape[1:]))
        w['lru_conv_w'] = _full_layer_weight('lru_conv_w', self.conv[:, l])
        return w

    def forward_riders(self, l):
        if l + 1 == DEPTH:
            return {}
        return {c: ([self.layer_shard(n, l + 1) for n in names], False) for c, names in FORWARD_CARRIERS.items()}

    def blocks(self, grads, names):
        return [_split_layer_grad(n, grads[n]).reshape(N_DEV, self.rows[n], -1).astype(BF16) for n in names]

    def backward_riders(self, l):
        riders = {'own': lambda grads: {c: (self.blocks(grads, names), True)
                                        for c, names in OWN_GRAD_CARRIERS.items()}}
        if self.outgoing is not None:
            riders[NEXT_GRAD_CARRIER] = (self.outgoing, True)
        return riders

    def collect(self, l, grads, carried):
        for n, t in zip(NEXT_GRADS, carried[NEXT_GRAD_CARRIER]):
            self.incoming[n][l + 1] = t
        for c, names in OWN_GRAD_CARRIERS.items():
            for n, t in zip(names, carried[c]):
                self.incoming[n][l] = t
        self.outgoing = self.blocks(grads, NEXT_GRADS)

    def finish(self, grads):
        conv = jnp.stack([_split_layer_grad('lru_conv_w', grads[l]['lru_conv_w']) for l in range(DEPTH)], axis=1)
        conv = conv.reshape(N_DEV, -1, conv.shape[-1]).astype(F32)
        last = _exchange("scatter_last_layer", self.outgoing + [conv], scatter=True)
        for n, t in zip(NEXT_GRADS, last[:-1]):
            self.incoming[n][0] = t
        return {**self.incoming, 'lru_conv_w': [last[-1]]}


def kernel(x, w_in, b_f, b_gate, s5_a_re, s5_a_im, s5_log_dt, s5_b_re, s5_b_im, s5_c_re, s5_c_im, s5_d, s5_w_glu, s5_b_glu, lru_conv_w, lru_conv_b, lru_w_a, lru_b_a, lru_w_x, lru_b_x, lru_lambda, w_branch, w_out, ln1_g, ln1_b, w_ffn_gate, w_ffn_up, w_ffn_down, ln2_g, ln2_b, loss_target, m_w_in, m_b_f, m_b_gate, m_s5_a_re, m_s5_a_im, m_s5_log_dt, m_s5_b_re, m_s5_b_im, m_s5_c_re, m_s5_c_im, m_s5_d, m_s5_w_glu, m_s5_b_glu, m_lru_conv_w, m_lru_conv_b, m_lru_w_a, m_lru_b_a, m_lru_w_x, m_lru_b_x, m_lru_lambda, m_w_branch, m_w_out, m_ln1_g, m_ln1_b, m_w_ffn_gate, m_w_ffn_up, m_w_ffn_down, m_ln2_g, m_ln2_b, v_w_in, v_b_f, v_b_gate, v_s5_a_re, v_s5_a_im, v_s5_log_dt, v_s5_b_re, v_s5_b_im, v_s5_c_re, v_s5_c_im, v_s5_d, v_s5_w_glu, v_s5_b_glu, v_lru_conv_w, v_lru_conv_b, v_lru_w_a, v_lru_b_a, v_lru_w_x, v_lru_b_x, v_lru_lambda, v_w_branch, v_w_out, v_ln1_g, v_ln1_b, v_w_ffn_gate, v_w_ffn_up, v_w_ffn_down, v_ln2_g, v_ln2_b):
    given = dict(locals())
    weights = {n: given[n] for n in WEIGHTS}
    moments_m = {n: given['m_' + n] for n in WEIGHTS}
    moments_v = {n: given['v_' + n] for n in WEIGHTS}

    hooks = _Fsdp(weights)
    loss_local, dx, grads = _forward_backward(x[0], loss_target[0], hooks)
    loss = lax.psum(loss_local, MESH_AXES)
    incoming = hooks.finish(grads)

    new = {}
    for n in SHARDED:
        res = _adamw("adamw_" + n, _shard_2d(weights[n]), incoming[n], _shard_2d(moments_m[n]), _shard_2d(moments_v[n]))
        new[n] = [t.reshape(weights[n].shape) for t in res]

    flat = jnp.concatenate([jnp.stack([grads[l][n] for l in range(DEPTH)]).astype(F32).reshape(-1) for n in REPLICATED])
    rows = -(-flat.shape[0] // (LANES * 256)) * 256
    packed = jnp.pad(flat, (0, rows * LANES - flat.shape[0])).reshape(rows, LANES)
    (arrived,) = _exchange("gather_small_grads", [packed], scatter=False)
    total, at = _sum_parts("sum_small_grads", arrived).reshape(-1), 0
    for n in REPLICATED:
        w2 = _shard_2d(weights[n])
        grad = total[at:at + w2.size].reshape((1,) + w2.shape)
        at += w2.size
        res = _adamw("adamw_" + n, w2, [grad], _shard_2d(moments_m[n]), _shard_2d(moments_v[n]))
        new[n] = [t.reshape(weights[n].shape) for t in res]

    return (loss, dx[None], *[new[n][0] for n in WEIGHTS], *[new[n][1] for n in WEIGHTS],
            *[new[n][2] for n in WEIGHTS], *[new[n][3] for n in WEIGHTS])
```

```python
import functools
import math

import jax
import jax.numpy as jnp
from jax import lax
from jax.experimental import pallas as pl
from jax.experimental.pallas import tpu as pltpu

F32 = jnp.float32
BF16 = jnp.bfloat16

D_MODEL = 1024
DEPTH = 4
BRANCH = 512
S5_GROUPS, S5_GROUP, S5_STATE = 32, 16, 64
S5_N = S5_GROUPS * S5_STATE
LRU_HEADS, LRU_HEAD_DIM = 8, 64
LRU_C = 8.0
CONV_WIDTH = 4
FOX_HEADS, FOX_HEAD_DIM = 8, 64
FFN_HIDDEN = 2816
ALPHA = (2.0 * DEPTH) ** 0.25
LN_EPS = 1e-5
IN_TOTAL = 6 * BRANCH + FOX_HEADS + 3 * D_MODEL
FG_PAD = 128
Z_MAIN = 6 * BRANCH
Z_GATE0 = Z_MAIN
Z_FG0 = Z_MAIN + 3 * D_MODEL
Z_TOTAL = Z_FG0 + FG_PAD
N_DEV = 8
MESH_AXES = ("x", "y", "c")

ADAM_LR, ADAM_B1, ADAM_B2, ADAM_EPS, ADAM_WD, ADAM_STEP = 0.001, 0.9, 0.999, 1e-08, 0.01, 10

VMEM_LIMIT_BYTES = 48 * 1024 * 1024
SCAN_SLACK_BYTES = 6 * 1024 * 1024
SUBLANES = 8
LANES = 128
NEG_BIG = -1e30

WEIGHTS = ['w_in', 'b_f', 'b_gate', 's5_a_re', 's5_a_im', 's5_log_dt', 's5_b_re', 's5_b_im', 's5_c_re', 's5_c_im',
           's5_d', 's5_w_glu', 's5_b_glu', 'lru_conv_w', 'lru_conv_b', 'lru_w_a', 'lru_b_a', 'lru_w_x', 'lru_b_x',
           'lru_lambda', 'w_branch', 'w_out', 'ln1_g', 'ln1_b', 'w_ffn_gate', 'w_ffn_up', 'w_ffn_down', 'ln2_g',
           'ln2_b']
SHARDED = ['w_in', 's5_w_glu', 'lru_conv_w', 'w_branch', 'w_out', 'w_ffn_gate', 'w_ffn_up', 'w_ffn_down']
REPLICATED = [n for n in WEIGHTS if n not in SHARDED]


def _params(*sem, vmem=VMEM_LIMIT_BYTES):
    return pltpu.CompilerParams(dimension_semantics=sem, vmem_limit_bytes=vmem)


def _tile(dim, want):
    if dim % LANES:
        return dim
    t = min(want, dim) // LANES * LANES
    while dim % t:
        t -= LANES
    return t


MM_VMEM_BUDGET_BYTES = 30 * 1024 * 1024
MM_MAX_TILE = 1024


def _divisor_tiles(dim, cap, must_divide=0):
    if dim % LANES:
        return [dim]
    out = [t for t in range(min(cap, dim) // LANES * LANES, 0, -LANES) if dim % t == 0 and must_divide % t == 0]
    return out or [dim]


def _mm_tiles(m, n, k, a_bytes, b_bytes, o_bytes, has_add, m_c0, n_c0, k_c0):
    for tk in _divisor_tiles(k, k, k_c0):
        best = None
        for tm in _divisor_tiles(m, MM_MAX_TILE, m_c0):
            for tn in _divisor_tiles(n, MM_MAX_TILE, n_c0):
                used = 2 * (tm * tk * a_bytes + tk * tn * b_bytes + tm * tn * o_bytes) + tm * tn * 4
                used += tm * tn * 4 if tk < k else 0
                used += 2 * tm * tn * 4 if has_add else 0
                if used <= MM_VMEM_BUDGET_BYTES and (best is None or tm * tn / (tm + tn) > best[0]):
                    best = (tm * tn / (tm + tn), tm, tn)
        if best is not None and (min(best[1], best[2]) >= 256 or tk <= 512):
            return best[1], best[2], tk
    raise ValueError("no matmul tiling fits VMEM")


def _mm(name, a, b, *, ta=False, tb=False, a_c0=0, a_w=None, b_c0=0, b_w=None, add=None, out_dtype=F32, rider=None):
    a_w = a.shape[1] if a_w is None else a_w
    b_w = b.shape[1] if b_w is None else b_w
    m, k = (a_w, a.shape[0]) if ta else (a.shape[0], a_w)
    n = b.shape[0] if tb else b_w
    assert k == (b_w if tb else b.shape[0]), (name, a.shape, b.shape)
    tm, tn, tk = _mm_tiles(m, n, k, a.dtype.itemsize, b.dtype.itemsize, jnp.dtype(out_dtype).itemsize,
                           add is not None, a_c0 if ta else 0, 0 if tb else b_c0,
                           math.gcd(0 if ta else a_c0, b_c0 if tb else 0))
    nk = k // tk
    a_off = a_c0 // (tm if ta else tk)
    b_off = b_c0 // (tk if tb else tn)
    assert a_c0 % (tm if ta else tk) == 0 and b_c0 % (tk if tb else tn) == 0, name
    dims = (((0 if ta else 1,), (1 if tb else 0,)), ((), ()))
    a_total, b_total = m * k * a.dtype.itemsize, n * k * b.dtype.itemsize
    a_stays = a_total + b_total * (m // tm) <= b_total + a_total * (n // tn)
    if nk > 1:
        a_stays = True

    def mn(o, i):
        return (o, i) if a_stays else (i, o)

    def body(in_refs, out_refs, scratch_refs):
        a_ref, b_ref = in_refs[0], in_refs[1]
        add_ref = in_refs[2] if add is not None else None
        (o_ref,) = out_refs
        part = lax.dot_general(a_ref[...].astype(BF16), b_ref[...].astype(BF16), dims, preferred_element_type=F32)

        def finish(r):
            if add is not None:
                r = r + add_ref[...]
            o_ref[...] = r.astype(o_ref.dtype)

        if nk == 1:
            finish(part)
            return
        (acc_ref,) = scratch_refs
        kk = pl.program_id(2)

        @pl.when(kk == 0)
        def _():
            acc_ref[...] = part

        @pl.when(kk > 0)
        def _():
            acc_ref[...] += part

        @pl.when(kk == nk - 1)
        def _():
            finish(acc_ref[...])

    def a_map(o, i, kk):
        im = mn(o, i)[0]
        return (kk, im + a_off) if ta else (im, kk + a_off)

    def b_map(o, i, kk):
        jn = mn(o, i)[1]
        return (jn, kk + b_off) if tb else (kk, jn + b_off)

    a_spec = pl.BlockSpec((tk, tm) if ta else (tm, tk), a_map)
    b_spec = pl.BlockSpec((tn, tk) if tb else (tk, tn), b_map)
    o_spec = pl.BlockSpec((tm, tn), lambda o, i, kk: mn(o, i))
    ins, in_specs = [a, b], [a_spec, b_spec]
    if add is not None:
        ins.append(add)
        in_specs.append(o_spec)
    grid = (m // tm, n // tn, nk) if a_stays else (n // tn, m // tm, nk)
    (out,), carried = _call_carrying(
        name, body, grid, ins, in_specs, [jax.ShapeDtypeStruct((m, n), out_dtype)], [o_spec],
        [pltpu.VMEM((tm, tn), F32)] if nk > 1 else [], ("parallel", "parallel", "arbitrary"), rider)
    return out if rider is None else (out, carried)


def _row_spec(tl, c0, w):
    assert c0 % w == 0
    return pl.BlockSpec((tl, w), lambda i: (i, c0 // w))


def _whole_spec(p):
    return pl.BlockSpec(p.shape, lambda i: (0,) * p.ndim)


def _ew(name, f, rows, prm, out_ws, tl=256, out_dtypes=None):
    out_dtypes = out_dtypes or [F32] * len(out_ws)
    nrows, nprm = len(rows), len(prm)
    length = rows[0][0].shape[0]
    tl = min(tl, length)

    def body(*refs):
        vals = [r[...] for r in refs[:nrows + nprm]]
        outs = f(*vals)
        for o_ref, o in zip(refs[nrows + nprm:], outs):
            o_ref[...] = o.astype(o_ref.dtype)

    return pl.pallas_call(
        body, grid=(length // tl,),
        in_specs=[_row_spec(tl, c0, w) for (_, c0, w) in rows] + [_whole_spec(p) for p in prm],
        out_specs=[_row_spec(tl, 0, w) for w in out_ws],
        out_shape=[jax.ShapeDtypeStruct((length, w), dt) for w, dt in zip(out_ws, out_dtypes)],
        compiler_params=_params("parallel"), name=name)(*[r[0] for r in rows], *prm)


def _ew_bwd(name, f, rows, prm, douts, row_grad, tl=256):
    nrows, nprm, nd = len(rows), len(prm), len(douts)
    length = rows[0][0].shape[0]
    tl = min(tl, length)
    want = [i for i in range(nrows) if row_grad[i]]

    def body(*refs):
        vals = [r[...] for r in refs[:nrows + nprm]]
        cts = tuple(r[...] for r in refs[nrows + nprm:nrows + nprm + nd])
        out_refs = refs[nrows + nprm + nd:]
        _, vjp = jax.vjp(lambda *v: tuple(f(*v)), *vals)
        grads = vjp(cts)
        for o_ref, i in zip(out_refs[:len(want)], want):
            o_ref[...] = grads[i].astype(o_ref.dtype)

        @pl.when(pl.program_id(0) == 0)
        def _():
            for o_ref in out_refs[len(want):]:
                o_ref[...] = jnp.zeros_like(o_ref)

        for o_ref, g in zip(out_refs[len(want):], grads[nrows:]):
            o_ref[...] += g

    return pl.pallas_call(
        body, grid=(length // tl,),
        in_specs=([_row_spec(tl, c0, w) for (_, c0, w) in rows] + [_whole_spec(p) for p in prm]
                  + [_row_spec(tl, 0, d.shape[1]) for d in douts]),
        out_specs=[_row_spec(tl, 0, rows[i][2]) for i in want] + [_whole_spec(p) for p in prm],
        out_shape=([jax.ShapeDtypeStruct((length, rows[i][2]), F32 if row_grad[i] is True else row_grad[i])
                    for i in want]
                   + [jax.ShapeDtypeStruct(p.shape, F32) for p in prm]),
        compiler_params=_params("arbitrary"), name=name)(*[r[0] for r in rows], *prm, *douts)


def _row_ids(width):
    return lax.broadcasted_iota(jnp.int32, (SUBLANES, width), 0)


def _shift_rows(v, d, reverse):
    return pltpu.roll(v, (SUBLANES - d) if reverse else d, 0)


def _scan_real(name, a, b, b2=None, *, reverse=False, bn=256):
    length, n = a.shape
    bn = _tile(n, bn)
    nb = length // SUBLANES
    operands = [a, b] if b2 is None else [a, b, b2]

    def body(*refs):
        a_ref, b_ref, h_ref = refs[0], refs[1], refs[-1]
        rows = _row_ids(bn)

        def step(it, carry):
            i = (nb - 1 - it) if reverse else it
            sl = pl.ds(pl.multiple_of(i * SUBLANES, SUBLANES), SUBLANES)
            av, bv = a_ref[sl, :], b_ref[sl, :]
            if b2 is not None:
                bv = bv + refs[2][sl, :]
            for d in (1, 2, 4):
                live = (rows < SUBLANES - d) if reverse else (rows >= d)
                a_in = jnp.where(live, _shift_rows(av, d, reverse), 1.0)
                b_in = jnp.where(live, _shift_rows(bv, d, reverse), 0.0)
                bv = bv + av * b_in
                av = av * a_in
            hv = bv + av * carry
            h_ref[sl, :] = hv
            edge = hv[0:1, :] if reverse else hv[SUBLANES - 1:SUBLANES, :]
            return jnp.broadcast_to(edge, (SUBLANES, bn))

        lax.fori_loop(0, nb, step, jnp.zeros((SUBLANES, bn), F32))

    spec = pl.BlockSpec((length, bn), lambda j: (0, j))
    return pl.pallas_call(body, grid=(n // bn,), in_specs=[spec] * len(operands), out_specs=spec,
                          out_shape=jax.ShapeDtypeStruct((length, n), F32),
                          compiler_params=_params("parallel"), name=name)(*operands)


def _cmul(ar, ai, br, bi):
    return ar * br - ai * bi, ar * bi + ai * br


def _scan_cplx(name, lam_re, lam_im, x_re, x_im, *, reverse=False, h_re=None, h_im=None, bn=128, rider=None):
    length, n = x_re.shape
    bn = _tile(n, bn)
    nb = length // SUBLANES
    with_dot = h_re is not None

    def body(in_refs, out_refs, scratch_refs):
        refs = tuple(in_refs) + tuple(out_refs)
        if with_dot:
            lr_ref, li_ref, xr_ref, xi_ref, hr_ref, hi_ref, gr_ref, gi_ref, dr_ref, di_ref = refs
        else:
            lr_ref, li_ref, xr_ref, xi_ref, gr_ref, gi_ref = refs
        rows = _row_ids(bn)
        lr = jnp.broadcast_to(lr_ref[...], (SUBLANES, bn))
        li = jnp.broadcast_to(li_ref[...], (SUBLANES, bn))
        powers = [(lr, li)]
        for _ in range(SUBLANES - 1):
            powers.append(_cmul(powers[-1][0], powers[-1][1], lr, li))
        zero = jnp.zeros((SUBLANES, bn), F32)
        steps = []
        for d in (1, 2, 4):
            live = (rows < SUBLANES - d) if reverse else (rows >= d)
            steps.append((d, jnp.where(live, powers[d - 1][0], 0.0), jnp.where(live, powers[d - 1][1], 0.0)))
        cr, ci = zero, zero
        for r in range(SUBLANES):
            e = (SUBLANES - r) if reverse else (r + 1)
            cr = jnp.where(rows == r, powers[e - 1][0], cr)
            ci = jnp.where(rows == r, powers[e - 1][1], ci)

        def step(it, carry):
            i = (nb - 1 - it) if reverse else it
            sl = pl.ds(pl.multiple_of(i * SUBLANES, SUBLANES), SUBLANES)
            vr, vi = xr_ref[sl, :], xi_ref[sl, :]
            for d, pr, pi in steps:
                sr, si = _cmul(pr, pi, _shift_rows(vr, d, reverse), _shift_rows(vi, d, reverse))
                vr, vi = vr + sr, vi + si
            kr, ki = _cmul(cr, ci, carry[0], carry[1])
            vr, vi = vr + kr, vi + ki
            gr_ref[sl, :] = vr
            gi_ref[sl, :] = vi
            er = vr[0:1, :] if reverse else vr[SUBLANES - 1:SUBLANES, :]
            ei = vi[0:1, :] if reverse else vi[SUBLANES - 1:SUBLANES, :]
            new = (jnp.broadcast_to(er, (SUBLANES, bn)), jnp.broadcast_to(ei, (SUBLANES, bn)))
            if not with_dot:
                return new
            prev = pl.ds(pl.multiple_of(jnp.maximum(i - 1, 0) * SUBLANES, SUBLANES), SUBLANES)
            keep = jnp.where(i > 0, 1.0, 0.0)
            pr_ = jnp.broadcast_to(hr_ref[prev, :][SUBLANES - 1:SUBLANES, :], (SUBLANES, bn)) * keep
            pi_ = jnp.broadcast_to(hi_ref[prev, :][SUBLANES - 1:SUBLANES, :], (SUBLANES, bn)) * keep
            hr = jnp.where(rows == 0, pr_, pltpu.roll(hr_ref[sl, :], 1, 0))
            hi = jnp.where(rows == 0, pi_, pltpu.roll(hi_ref[sl, :], 1, 0))
            return new + (carry[2] + vr * hr + vi * hi, carry[3] + vi * hr - vr * hi)

        init = (zero, zero, zero, zero) if with_dot else (zero, zero)
        out = lax.fori_loop(0, nb, step, init)
        if with_dot:
            dr_ref[...] = jnp.sum(out[2], axis=0, keepdims=True)
            di_ref[...] = jnp.sum(out[3], axis=0, keepdims=True)

    col = pl.BlockSpec((length, bn), lambda j: (0, j))
    vec = pl.BlockSpec((1, bn), lambda j: (0, j))
    ins = [lam_re, lam_im, x_re, x_im] + ([h_re, h_im] if with_dot else [])
    in_specs = [vec, vec, col, col] + ([col, col] if with_dot else [])
    out_specs = [col, col] + ([vec, vec] if with_dot else [])
    full = jax.ShapeDtypeStruct((length, n), F32)
    row = jax.ShapeDtypeStruct((1, n), F32)
    out_shape = [full, full] + ([row, row] if with_dot else [])
    columns = 6 if with_dot else 4
    outs, carried = _call_carrying(name, body, (n // bn,), ins, in_specs, out_shape, out_specs, [], ("parallel",), rider,
                                   vmem=2 * columns * length * bn * 4 + SCAN_SLACK_BYTES)
    return list(outs) if rider is None else list(outs) + [carried]


FOX_SCALE = FOX_HEAD_DIM ** -0.5
FOX_AUG = 128
FOX_CQ0 = FOX_HEAD_DIM
FOX_CK0 = FOX_HEAD_DIM + 3
NT = (((1,), (1,)), ((), ()))


def _fox_logits_t(ka, qa, on_diagonal):
    st = lax.dot_general(ka, qa, NT, preferred_element_type=F32)
    if on_diagonal:
        key = lax.broadcasted_iota(jnp.int32, st.shape, 0)
        query = lax.broadcasted_iota(jnp.int32, st.shape, 1)
        st = jnp.where(key <= query, st, NEG_BIG)
    return st


def _fox_pair(s, nt, q_first):
    if q_first:
        qi = sum((s >= (m * (m + 1)) // 2).astype(jnp.int32) for m in range(1, nt))
        return qi, s - ((qi * (qi + 1)) >> 1)
    ki = sum((s >= m * nt - (m * (m - 1)) // 2).astype(jnp.int32) for m in range(1, nt))
    return ki + s - (ki * nt - ((ki * (ki - 1)) >> 1)), ki


FOX_HEADS_PER_STEP = 2


class _HeadView:
    def __init__(self, ref, head):
        self.ref, self.head = ref, head
        self.shape, self.dtype = ref.shape[1:], ref.dtype

    def _index(self, idx):
        idx = idx if isinstance(idx, tuple) else (idx,)
        return (self.head,) + (() if idx == (Ellipsis,) else idx)

    def __getitem__(self, idx):
        return self.ref[self._index(idx)]

    def __setitem__(self, idx, value):
        self.ref[self._index(idx)] = value


def _head_views(in_refs, out_refs, scratch_refs):
    return [tuple([_HeadView(r, h) for r in refs] for refs in (in_refs, out_refs, scratch_refs))
            for h in range(FOX_HEADS_PER_STEP)]


def _fox_specs(t, nt, q_first):
    hb = FOX_HEADS_PER_STEP
    q_idx = lambda s: _fox_pair(s, nt, q_first)[0]
    k_idx = lambda s: _fox_pair(s, nt, q_first)[1]
    rows = lambda idx, w: pl.BlockSpec((hb, t, w), lambda h, s: (h, idx(s), 0))
    cols = lambda idx, w: pl.BlockSpec((hb, w, t), lambda h, s: (h, 0, idx(s)))
    return rows, cols, q_idx, k_idx, (nt * (nt + 1)) // 2


def _fox_fwd(qa, ka, vt, t=512, rider=None):
    heads, length, _ = qa.shape
    dh = vt.shape[1]
    t = min(t, length)
    nt = length // t

    def body(in_refs, out_refs, scratch_refs):
        views = _head_views(in_refs, out_refs, scratch_refs)
        qi, ki = _fox_pair(pl.program_id(1), nt, True)

        @pl.when(ki == 0)
        def _():
            for _, _, (m_sc, l_sc, acc_sc) in views:
                m_sc[...] = jnp.full(m_sc.shape, NEG_BIG, F32)
                l_sc[...] = jnp.zeros(l_sc.shape, F32)
                acc_sc[...] = jnp.zeros(acc_sc.shape, F32)

        def step(on_diagonal):
            for (qa_ref, ka_ref, vt_ref), _, (m_sc, l_sc, acc_sc) in views:
                st = _fox_logits_t(ka_ref[...], qa_ref[...], on_diagonal)
                m_old = m_sc[...]
                m_new = jnp.maximum(m_old, jnp.max(st, axis=0, keepdims=True))
                pt = jnp.exp(st - m_new)
                scale = jnp.exp(m_old - m_new)
                l_sc[...] = scale * l_sc[...] + jnp.sum(pt, axis=0, keepdims=True)
                acc_sc[...] = scale * acc_sc[...] + jnp.dot(vt_ref[...], pt.astype(BF16), preferred_element_type=F32)
                m_sc[...] = m_new

        pl.when(ki < qi)(functools.partial(step, False))
        pl.when(ki == qi)(functools.partial(step, True))

        @pl.when(ki == qi)
        def _():
            for _, (o_ref, lse_ref), (m_sc, l_sc, acc_sc) in views:
                o_ref[...] = acc_sc[...] / l_sc[...]
                lse_ref[...] = m_sc[...] + jnp.log(l_sc[...])

    hb = FOX_HEADS_PER_STEP
    rows, cols, q_idx, k_idx, pairs = _fox_specs(t, nt, True)
    (ot, lse), carried = _call_carrying(
        "fox_fwd", body, (heads // hb, pairs), [qa, ka, vt],
        [rows(q_idx, FOX_AUG), rows(k_idx, FOX_AUG), cols(k_idx, dh)],
        [jax.ShapeDtypeStruct((heads, dh, length), F32), jax.ShapeDtypeStruct((heads, 1, length), F32)],
        [cols(q_idx, dh), cols(q_idx, 1)],
        [pltpu.VMEM((hb, 1, t), F32), pltpu.VMEM((hb, 1, t), F32), pltpu.VMEM((hb, dh, t), F32)],
        ("parallel", "arbitrary"), rider)
    return ot, lse, carried


def _fox_ds_t(qa_ref, ka_ref, v_ref, dot_ref, ot_ref, lse_ref, on_diagonal):
    pt = jnp.exp(_fox_logits_t(ka_ref[...], qa_ref[...], on_diagonal) - lse_ref[...])
    dpt = jnp.dot(v_ref[...], dot_ref[...], preferred_element_type=F32)
    delta = jnp.sum(dot_ref[...].astype(F32) * ot_ref[...], axis=0, keepdims=True)
    return pt, pt * (dpt - delta)


def _fox_bwd_q(qa, ka, kat, v, dot, ot, lse, t=512, rider=None):
    heads, length, _ = qa.shape
    dh = v.shape[2]
    t = min(t, length)
    nt = length // t

    def body(in_refs, out_refs, scratch_refs):
        views = _head_views(in_refs, out_refs, scratch_refs)
        qi, ki = _fox_pair(pl.program_id(1), nt, True)

        @pl.when(ki == 0)
        def _():
            for _, _, (acc_sc,) in views:
                acc_sc[...] = jnp.zeros(acc_sc.shape, F32)

        def step(on_diagonal):
            for (qa_ref, ka_ref, kat_ref, v_ref, dot_ref, ot_ref, lse_ref), _, (acc_sc,) in views:
                _, dst = _fox_ds_t(qa_ref, ka_ref, v_ref, dot_ref, ot_ref, lse_ref, on_diagonal)
                acc_sc[...] += jnp.dot(kat_ref[...], dst.astype(BF16), preferred_element_type=F32)

        pl.when(ki < qi)(functools.partial(step, False))
        pl.when(ki == qi)(functools.partial(step, True))

        @pl.when(ki == qi)
        def _():
            for _, (dq_ref, dc_ref), (acc_sc,) in views:
                dq_ref[...] = (acc_sc[0:dh, :] * FOX_SCALE).astype(dq_ref.dtype)
                dc_ref[...] = acc_sc[FOX_CQ0:FOX_CQ0 + 1, :]

    hb = FOX_HEADS_PER_STEP
    rows, cols, q_idx, k_idx, pairs = _fox_specs(t, nt, True)
    (dqt, dcq), carried = _call_carrying(
        "fox_bwd_q", body, (heads // hb, pairs), [qa, ka, kat, v, dot, ot, lse],
        [rows(q_idx, FOX_AUG), rows(k_idx, FOX_AUG), cols(k_idx, FOX_AUG), rows(k_idx, dh), cols(q_idx, dh),
         cols(q_idx, dh), cols(q_idx, 1)],
        [jax.ShapeDtypeStruct((heads, dh, length), BF16), jax.ShapeDtypeStruct((heads, 1, length), F32)],
        [cols(q_idx, dh), cols(q_idx, 1)],
        [pltpu.VMEM((hb, FOX_AUG, t), F32)], ("parallel", "arbitrary"), rider)
    return dqt, dcq, carried


def _fox_bwd_kv(qa, ka, v, do, dot, ot, lse, t=512, rider=None):
    heads, length, _ = qa.shape
    dh = v.shape[2]
    t = min(t, length)
    nt = length // t

    def body(in_refs, out_refs, scratch_refs):
        views = _head_views(in_refs, out_refs, scratch_refs)
        qi, ki = _fox_pair(pl.program_id(1), nt, False)

        @pl.when(qi == ki)
        def _():
            for _, _, (dka_sc, dv_sc) in views:
                dka_sc[...] = jnp.zeros(dka_sc.shape, F32)
                dv_sc[...] = jnp.zeros(dv_sc.shape, F32)

        def step(on_diagonal):
            for (qa_ref, ka_ref, v_ref, do_ref, dot_ref, ot_ref, lse_ref), _, (dka_sc, dv_sc) in views:
                pt, dst = _fox_ds_t(qa_ref, ka_ref, v_ref, dot_ref, ot_ref, lse_ref, on_diagonal)
                dv_sc[...] += jnp.dot(pt.astype(BF16), do_ref[...], preferred_element_type=F32)
                dka_sc[...] += jnp.dot(dst.astype(BF16), qa_ref[...], preferred_element_type=F32)

        pl.when(qi > ki)(functools.partial(step, False))
        pl.when(qi == ki)(functools.partial(step, True))

        @pl.when(qi == nt - 1)
        def _():
            for _, (dk_ref, dv_ref, dc_ref), (dka_sc, dv_sc) in views:
                dka = dka_sc[...]
                lane = lax.broadcasted_iota(jnp.int32, dka.shape, 1)
                dk_ref[...] = dka_sc[:, :dh].astype(dk_ref.dtype)
                dc_ref[...] = jnp.sum(jnp.where(lane == FOX_CK0, dka, 0.0), axis=1, keepdims=True)
                dv_ref[...] = dv_sc[...].astype(dv_ref.dtype)

    hb = FOX_HEADS_PER_STEP
    rows, cols, q_idx, k_idx, pairs = _fox_specs(t, nt, False)
    big = jax.ShapeDtypeStruct((heads, length, dh), BF16)
    (dk, dv, dc), carried = _call_carrying(
        "fox_bwd_kv", body, (heads // hb, pairs), [qa, ka, v, do, dot, ot, lse],
        [rows(q_idx, FOX_AUG), rows(k_idx, FOX_AUG), rows(k_idx, dh), rows(q_idx, dh), cols(q_idx, dh),
         cols(q_idx, dh), cols(q_idx, 1)],
        [big, big, jax.ShapeDtypeStruct((heads, length, 1), F32)], [rows(k_idx, dh), rows(k_idx, dh), rows(k_idx, 1)],
        [pltpu.VMEM((hb, t, FOX_AUG), F32), pltpu.VMEM((hb, t, dh), F32)], ("parallel", "arbitrary"), rider)
    return dk, dv, dc, carried


def _split3(x):
    hi = lax.reduce_precision(x, 8, 7)
    mid = lax.reduce_precision(x - hi, 8, 7)
    return [hi, mid, lax.reduce_precision(x - hi - mid, 8, 7)]


def _fox_operands(z, cum):
    length = z.shape[0]
    parts = jnp.stack(_split3(cum[:, :FOX_HEADS].T), axis=-1)
    ones = jnp.ones_like(parts)
    pad = jnp.zeros((FOX_HEADS, length, FOX_AUG - FOX_HEAD_DIM - 6), F32)
    qa = jnp.concatenate([_heads(z, 3 * BRANCH) * FOX_SCALE, parts, ones, pad], axis=-1).astype(BF16)
    ka = jnp.concatenate([_heads(z, 4 * BRANCH), ones, -parts, pad], axis=-1).astype(BF16)
    v = _heads(z, 5 * BRANCH).astype(BF16)
    return qa, ka, ka.transpose(0, 2, 1), v, v.transpose(0, 2, 1)


def _softplus(x):
    return jnp.maximum(x, 0.0) + jnp.log1p(jnp.exp(-jnp.abs(x)))


def _f_s5_disc(a_re, a_im, log_dt, b_re, b_im):
    dt = jnp.exp(log_dt)
    mag = jnp.exp(a_re * dt)
    lr, li = mag * jnp.cos(a_im * dt), mag * jnp.sin(a_im * dt)
    den = a_re * a_re + a_im * a_im
    qr = ((lr - 1.0) * a_re + li * a_im) / den
    qi = (li * a_re - (lr - 1.0) * a_im) / den
    return lr, li, qr * b_re - qi * b_im, qr * b_im + qi * b_re


def _f_s5_y1(hc_re, hc_im, u, d):
    return (jax.nn.gelu(hc_re + hc_im + d * u),)


def _f_s5_glu(y1, pre, b):
    return (y1 * jax.nn.sigmoid(pre + b),)


def _f_conv(x0, x1, x2, x3, w0, w1, w2, w3, b):
    return (b + w0 * x0 + w1 * x1 + w2 * x2 + w3 * x3,)


def _f_conv_t(d0, d1, d2, d3, w0, w1, w2, w3):
    return (w0 * d0 + w1 * d1 + w2 * d2 + w3 * d3,)


def _lru_coeffs(xc, pa, px, b_a, b_x, lam):
    r = jax.nn.sigmoid(pa + b_a)
    i = jax.nn.sigmoid(px + b_x)
    log_a = -LRU_C * _softplus(-lam) * r
    a = jnp.exp(log_a)
    mult = jnp.sqrt(-jnp.tanh(log_a) * (a * a + 1.0))
    return a, mult * (i * xc)


def _f_lru_gates(xc, pa, px, b_a, b_x, lam):
    return _lru_coeffs(xc, pa, px, b_a, b_x, lam)


def _f_lru_step(xc, pa, px, h_prev, b_a, b_x, lam):
    a, b = _lru_coeffs(xc, pa, px, b_a, b_x, lam)
    return (a * h_prev + b,)


def _f_lru_out(gate, h):
    return (jax.nn.gelu(gate) * h,)


def _f_logf(zf, bf):
    return (-_softplus(-(zf + bf)),)


def _f_merge(p0, p1, p2, z0, z1, z2, b0, b1, b2):
    return (jax.nn.sigmoid(z0 + b0) * p0 + jax.nn.sigmoid(z1 + b1) * p1 + jax.nn.sigmoid(z2 + b2) * p2,)


def _f_ln(x, r, g, b):
    s = ALPHA * x + r
    mu = jnp.mean(s, axis=-1, keepdims=True)
    var = jnp.mean(jnp.square(s - mu), axis=-1, keepdims=True)
    return ((s - mu) * lax.rsqrt(var + LN_EPS) * g + b,)


def _f_swiglu(hg, hu):
    return (jax.nn.silu(hg) * hu,)


def _full(a):
    return (a, 0, a.shape[1])


def _blockdiag(t):
    g, a, b = t.shape
    eye = jnp.eye(g, dtype=t.dtype)
    return (t[:, :, None, :] * eye[:, None, :, None]).reshape(g * a, g * b)


def _blockdiag_take(d, g, a, b):
    eye = jnp.eye(g, dtype=d.dtype)
    return (d.reshape(g, a, g, b) * eye[:, None, :, None]).sum(axis=2)


def _delay(a, j):
    return a if j == 0 else jnp.pad(a, ((j, 0), (0, 0)))[:a.shape[0]]


def _advance(a, j):
    return a if j == 0 else jnp.pad(a, ((0, j), (0, 0)))[j:]


def _heads(a, c0=0):
    length = a.shape[0]
    return a[:, c0:c0 + BRANCH].reshape(length, FOX_HEADS, FOX_HEAD_DIM).transpose(1, 0, 2)


def _unheads(a):
    return a.transpose(1, 0, 2).reshape(a.shape[1], BRANCH)


def _row(v):
    return v.reshape(1, -1).astype(F32)


def _col(v):
    return v.reshape(-1, 1).astype(F32)


def _prep_layer(w):
    p = {}
    w_in = w['w_in']
    p['wc'] = jnp.concatenate(
        [w_in[:, :Z_MAIN], w_in[:, Z_MAIN + FOX_HEADS:], w_in[:, Z_MAIN:Z_MAIN + FOX_HEADS],
         jnp.zeros((D_MODEL, FG_PAD - FOX_HEADS), w_in.dtype)], axis=1)
    p['b_f'] = jnp.pad(_row(w['b_f']), ((0, 0), (0, FG_PAD - FOX_HEADS)))
    p['b_gate'] = [_row(w['b_gate'][k * D_MODEL:(k + 1) * D_MODEL]) for k in range(3)]
    p['disc_in'] = [_col(w['s5_a_re']), _col(w['s5_a_im']), _col(jnp.repeat(w['s5_log_dt'], S5_STATE)),
                    w['s5_b_re'].reshape(S5_N, S5_GROUP), w['s5_b_im'].reshape(S5_N, S5_GROUP)]
    lam_re, lam_im, bb_re, bb_im = _ew("s5_disc", _f_s5_disc, [_full(a) for a in p['disc_in']], [],
                                       [1, 1, S5_GROUP, S5_GROUP], tl=S5_N)
    p['lam_re'], p['lam_im'] = lam_re.reshape(1, S5_N), lam_im.reshape(1, S5_N)
    to_blk = lambda t: _blockdiag(t.reshape(S5_GROUPS, S5_STATE, S5_GROUP).transpose(0, 2, 1)).astype(BF16)
    p['s5_bre'], p['s5_bim'] = to_blk(bb_re), to_blk(bb_im)
    p['s5_cre'] = _blockdiag(w['s5_c_re'].transpose(0, 2, 1)).astype(BF16)
    p['s5_cimn'] = _blockdiag(-w['s5_c_im'].transpose(0, 2, 1)).astype(BF16)
    p['s5_d'], p['wglu'], p['bglu'] = _row(w['s5_d']), w['s5_w_glu'], _row(w['s5_b_glu'])
    p['conv_w'] = [_row(w['lru_conv_w'][CONV_WIDTH - 1 - j]) for j in range(CONV_WIDTH)]
    p['conv_b'] = _row(w['lru_conv_b'])
    p['wax'] = jnp.concatenate([_blockdiag(w['lru_w_a']), _blockdiag(w['lru_w_x'])], axis=1).astype(BF16)
    p['b_a'], p['b_x'], p['lam'] = _row(w['lru_b_a']), _row(w['lru_b_x']), _row(w['lru_lambda'])
    p['wb'] = [w['w_branch'][k] for k in range(3)]
    p['wout'] = w['w_out']
    p['ln1'] = [_row(w['ln1_g']), _row(w['ln1_b'])]
    p['wgu'] = jnp.concatenate([w['w_ffn_gate'], w['w_ffn_up']], axis=1)
    p['wd'] = w['w_ffn_down']
    p['ln2'] = [_row(w['ln2_g']), _row(w['ln2_b'])]
    return p


def _with_copy(f):
    def g(*args):
        (y,) = f(*args)
        return y, y
    return g


def _layer_fwd(x, x_bf, p, riders=None):
    length = x.shape[0]
    riders, carried = riders or {}, {}
    r = {'x': x, 'x_bf': x_bf}
    z = _mm("z_in", x_bf, p['wc'], rider=riders.get('z_in'))
    if 'z_in' in riders:
        z, carried['z_in'] = z
    r['z'] = z
    bu_re = _mm("s5_bu_re", z, p['s5_bre'], a_w=BRANCH)
    bu_im = _mm("s5_bu_im", z, p['s5_bim'], a_w=BRANCH)
    scanned = _scan_cplx("s5_scan", p['lam_re'], p['lam_im'], bu_re, bu_im, bn=256, rider=riders.get('s5_scan'))
    r['h_re'], r['h_im'] = scanned[0], scanned[1]
    if 's5_scan' in riders:
        carried['s5_scan'] = scanned[2]
    r['hc_re'] = _mm("s5_hc_re", r['h_re'], p['s5_cre'])
    r['hc_im'] = _mm("s5_hc_im", r['h_im'], p['s5_cimn'])
    r['y1'], r['y1_bf'] = _ew("s5_y1", _with_copy(_f_s5_y1), [_full(r['hc_re']), _full(r['hc_im']), (z, 0, BRANCH)],
                              [p['s5_d']], [BRANCH, BRANCH], out_dtypes=[F32, BF16])
    r['pre'] = _mm("s5_glu_pre", r['y1_bf'], p['wglu'])
    (r['ys5'],) = _ew("s5_glu", _f_s5_glu, [_full(r['y1']), _full(r['pre'])], [p['bglu']], [BRANCH], out_dtypes=[BF16])
    xl = z[:, BRANCH:2 * BRANCH]
    r['xd'] = [_delay(xl, j) for j in range(1, CONV_WIDTH)]
    r['xc'], r['xc_bf'] = _ew("lru_conv", _with_copy(_f_conv), [(z, BRANCH, BRANCH)] + [_full(a) for a in r['xd']],
                              p['conv_w'] + [p['conv_b']], [BRANCH, BRANCH], out_dtypes=[F32, BF16])
    r['papx'] = _mm("lru_gate_mm", r['xc_bf'], p['wax'])
    r['a'], b = _ew("lru_gates", _f_lru_gates, [_full(r['xc']), (r['papx'], 0, BRANCH), (r['papx'], BRANCH, BRANCH)],
                    [p['b_a'], p['b_x'], p['lam']], [BRANCH, BRANCH])
    r['h'] = _scan_real("lru_scan", r['a'], b)
    (r['ylru'],) = _ew("lru_out", _f_lru_out, [(z, 2 * BRANCH, BRANCH), _full(r['h'])], [], [BRANCH], out_dtypes=[BF16])
    (logf,) = _ew("fox_logf", _f_logf, [(z, Z_FG0, FG_PAD)], [p['b_f']], [FG_PAD])
    cum = _scan_real("fox_cum", jnp.ones((length, FG_PAD), F32), logf)
    qa, ka, kat, v, vt = _fox_operands(z, cum)
    r['fox'] = (qa, ka, kat, v)
    r['ot'], r['lse'], brought = _fox_fwd(qa, ka, vt, rider=riders.get('fox_fwd'))
    if 'fox_fwd' in riders:
        carried['fox_fwd'] = brought
    r['yfox'] = r['ot'].reshape(BRANCH, length).T.astype(BF16)
    ys = [r['ys5'], r['ylru'], r['yfox']]
    r['proj'] = [_mm("proj_%d" % k, ys[k], p['wb'][k]) for k in range(3)]
    gate_rows = [(z, Z_GATE0 + k * D_MODEL, D_MODEL) for k in range(3)]
    (r['mix'],) = _ew("merge", _f_merge, [_full(a) for a in r['proj']] + gate_rows, p['b_gate'], [D_MODEL], tl=128,
                      out_dtypes=[BF16])
    r['mixed'] = _mm("w_out", r['mix'], p['wout'])
    two = dict(out_ws=[D_MODEL, D_MODEL], out_dtypes=[F32, BF16])
    x1, r['x1_bf'] = _ew("ln1", _with_copy(_f_ln), [_full(x), _full(r['mixed'])], p['ln1'], **two)
    r['x1'] = x1
    r['hgu'] = _mm("ffn_in", r['x1_bf'], p['wgu'], rider=riders.get('ffn_in'))
    if 'ffn_in' in riders:
        r['hgu'], carried['ffn_in'] = r['hgu']
    (r['hid'],) = _ew("swiglu", _f_swiglu, [(r['hgu'], 0, FFN_HIDDEN), (r['hgu'], FFN_HIDDEN, FFN_HIDDEN)], [],
                      [FFN_HIDDEN], tl=128, out_dtypes=[BF16])
    r['f'] = _mm("ffn_out", r['hid'], p['wd'])
    x2, x2_bf = _ew("ln2", _with_copy(_f_ln), [_full(x1), _full(r['f'])], p['ln2'], **two)
    return x2, x2_bf, r, carried


def _layer_bwd(dx2, r, p, riders=None):
    g, riders, carried = {}, riders or {}, {}
    x, z, x1 = r['x'], r['z'], r['x1']
    dx1_n, df, g['ln2_g'], g['ln2_b'] = _ew_bwd("ln2_bwd", _f_ln, [_full(x1), _full(r['f'])], p['ln2'], [dx2],
                                                [True, BF16])
    dhid = _mm("ffn_out_dx", df, p['wd'], tb=True)
    g['w_ffn_down'] = _mm("ffn_out_dw", r['hid'], df, ta=True, out_dtype=BF16)
    hgu_rows = [(r['hgu'], 0, FFN_HIDDEN), (r['hgu'], FFN_HIDDEN, FFN_HIDDEN)]
    dhg, dhu = _ew_bwd("swiglu_bwd", _f_swiglu, hgu_rows, [], [dhid], [BF16, BF16], tl=128)
    g['w_ffn_gate'] = _mm("ffn_gate_dw", r['x1_bf'], dhg, ta=True, out_dtype=BF16)
    g['w_ffn_up'] = _mm("ffn_up_dw", r['x1_bf'], dhu, ta=True, out_dtype=BF16)
    dx1 = _mm("ffn_gate_dx", dhg, p['wgu'], tb=True, b_w=FFN_HIDDEN, add=dx1_n)
    dx1 = _mm("ffn_up_dx", dhu, p['wgu'], tb=True, b_c0=FFN_HIDDEN, b_w=FFN_HIDDEN, add=dx1)
    dx_n, dmixed, g['ln1_g'], g['ln1_b'] = _ew_bwd("ln1_bwd", _f_ln, [_full(x), _full(r['mixed'])], p['ln1'], [dx1],
                                                   [True, BF16])
    dmix = _mm("w_out_dx", dmixed, p['wout'], tb=True)
    g['w_out'] = _mm("w_out_dw", r['mix'], dmixed, ta=True, out_dtype=BF16)
    gate_rows = [(z, Z_GATE0 + k * D_MODEL, D_MODEL) for k in range(3)]
    mg = _ew_bwd("merge_bwd", _f_merge, [_full(a) for a in r['proj']] + gate_rows, p['b_gate'], [dmix], [BF16] * 6,
                 tl=128)
    dproj, dzg = mg[0:3], mg[3:6]
    g['b_gate'] = jnp.concatenate([b.reshape(-1) for b in mg[6:9]])
    ys = [r['ys5'], r['ylru'], r['yfox']]
    dys = [_mm("proj_%d_dx" % k, dproj[k], p['wb'][k], tb=True, out_dtype=BF16 if k == 2 else F32) for k in range(3)]
    g['w_branch'] = jnp.stack([_mm("proj_%d_dw" % k, ys[k], dproj[k], ta=True, out_dtype=BF16) for k in range(3)])
    qa, ka, kat, v = r['fox']
    do = _heads(dys[2])
    dot = do.transpose(0, 2, 1)
    own = riders['own'](g) if 'own' in riders else {}
    dqt, dcq, carried['fox_bwd_q'] = _fox_bwd_q(qa, ka, kat, v, dot, r['ot'], r['lse'], rider=riders.get('fox_bwd_q'))
    dkh, dvh, dck, carried['fox_bwd_kv'] = _fox_bwd_kv(qa, ka, v, do, dot, r['ot'], r['lse'],
                                                       rider=own.get('fox_bwd_kv'))
    pad_heads = lambda a: jnp.pad(a.T, ((0, 0), (0, FG_PAD - FOX_HEADS)))
    dlogf = _scan_real("fox_cum_bwd", jnp.ones((x.shape[0], FG_PAD), F32), pad_heads(dcq[:, 0, :]),
                       pad_heads(-dck[:, :, 0]), reverse=True)
    dqkv = [dqt.reshape(BRANCH, x.shape[0]).T, _unheads(dkh), _unheads(dvh)]
    dzf, dbf = _ew_bwd("fox_logf_bwd", _f_logf, [(z, Z_FG0, FG_PAD)], [p['b_f']], [dlogf], [BF16])
    g['b_f'] = dbf[0, :FOX_HEADS]
    dgate, dh = _ew_bwd("lru_out_bwd", _f_lru_out, [(z, 2 * BRANCH, BRANCH), _full(r['h'])], [], [dys[1]], [BF16, True])
    db = _scan_real("lru_scan_bwd", _advance(r['a'], 1), dh, reverse=True)
    gates_rows = [_full(r['xc']), (r['papx'], 0, BRANCH), (r['papx'], BRANCH, BRANCH), _full(_delay(r['h'], 1))]
    dxc, dpa, dpx, db_a, db_x, dlam = _ew_bwd("lru_gates_bwd", _f_lru_step, gates_rows, [p['b_a'], p['b_x'], p['lam']],
                                              [db], [True, BF16, BF16, False])
    dxc = _mm("lru_a_dx", dpa, p['wax'], tb=True, b_w=BRANCH, add=dxc)
    dxc = _mm("lru_x_dx", dpx, p['wax'], tb=True, b_c0=BRANCH, b_w=BRANCH, add=dxc)
    take_heads = lambda d: _blockdiag_take(d, LRU_HEADS, LRU_HEAD_DIM, LRU_HEAD_DIM)
    g['lru_w_a'] = take_heads(_mm("lru_a_dw", r['xc_bf'], dpa, ta=True))
    g['lru_w_x'] = take_heads(_mm("lru_x_dw", r['xc_bf'], dpx, ta=True))
    g['lru_b_a'] = db_a.reshape(LRU_HEADS, LRU_HEAD_DIM)
    g['lru_b_x'] = db_x.reshape(LRU_HEADS, LRU_HEAD_DIM)
    g['lru_lambda'] = dlam.reshape(-1)
    conv_rows = [(z, BRANCH, BRANCH)] + [_full(a) for a in r['xd']]
    cw = _ew_bwd("lru_conv_dw", _f_conv, conv_rows, p['conv_w'] + [p['conv_b']], [dxc], [False] * CONV_WIDTH)
    g['lru_conv_w'] = jnp.concatenate([cw[CONV_WIDTH - 1 - k] for k in range(CONV_WIDTH)], axis=0)
    g['lru_conv_b'] = cw[CONV_WIDTH].reshape(-1)
    (dxl,) = _ew("lru_conv_dx", _f_conv_t, [_full(_advance(dxc, j)) for j in range(CONV_WIDTH)], p['conv_w'], [BRANCH],
                 out_dtypes=[BF16])
    dy1, dpre, dbglu = _ew_bwd("s5_glu_bwd", _f_s5_glu, [_full(r['y1']), _full(r['pre'])], [p['bglu']], [dys[0]],
                               [True, BF16])
    g['s5_b_glu'] = dbglu.reshape(-1)
    g['s5_w_glu'] = _mm("s5_glu_dw", r['y1_bf'], dpre, ta=True, out_dtype=BF16)
    dy1 = _mm("s5_glu_dx", dpre, p['wglu'], tb=True, add=dy1)
    dy0, du, dd = _ew_bwd("s5_y1_bwd", _f_s5_y1, [_full(r['hc_re']), _full(r['hc_im']), (z, 0, BRANCH)], [p['s5_d']],
                          [dy1], [BF16, False, True])
    g['s5_d'] = dd.reshape(-1)
    dh_re = _mm("s5_hc_re_dx", dy0, p['s5_cre'], tb=True)
    dh_im = _mm("s5_hc_im_dx", dy0, p['s5_cimn'], tb=True)
    take_c = lambda d: _blockdiag_take(d, S5_GROUPS, S5_STATE, S5_GROUP).transpose(0, 2, 1)
    g['s5_c_re'] = take_c(_mm("s5_hc_re_dw", r['h_re'], dy0, ta=True))
    g['s5_c_im'] = -take_c(_mm("s5_hc_im_dw", r['h_im'], dy0, ta=True))
    scanned = _scan_cplx("s5_scan_bwd", p['lam_re'], -p['lam_im'], dh_re, dh_im, reverse=True, h_re=r['h_re'],
                         h_im=r['h_im'], bn=256, rider=own.get('s5_scan_bwd'))
    gb_re, gb_im, dl_re, dl_im = scanned[:4]
    carried['s5_scan_bwd'] = scanned[4] if 's5_scan_bwd' in own else []
    du = _mm("s5_bu_re_dx", gb_re, p['s5_bre'], tb=True, add=du)
    du = _mm("s5_bu_im_dx", gb_im, p['s5_bim'], tb=True, add=du, out_dtype=BF16)
    take_b = lambda d: _blockdiag_take(d, S5_GROUPS, S5_GROUP, S5_STATE).transpose(0, 2, 1).reshape(S5_N, S5_GROUP)
    dbb_re = take_b(_mm("s5_bu_re_dw", z, gb_re, ta=True, a_w=BRANCH))
    dbb_im = take_b(_mm("s5_bu_im_dw", z, gb_im, ta=True, a_w=BRANCH))
    disc = _ew_bwd("s5_disc_bwd", _f_s5_disc, [_full(a) for a in p['disc_in']], [],
                   [dl_re.reshape(S5_N, 1), dl_im.reshape(S5_N, 1), dbb_re, dbb_im], [True] * 5, tl=S5_N)
    grp = (S5_GROUPS, S5_STATE)
    g['s5_a_re'], g['s5_a_im'] = disc[0].reshape(grp), disc[1].reshape(grp)
    g['s5_log_dt'] = disc[2].reshape(grp).sum(axis=1)
    g['s5_b_re'], g['s5_b_im'] = disc[3].reshape(grp + (S5_GROUP,)), disc[4].reshape(grp + (S5_GROUP,))
    dz = jnp.concatenate([du, dxl, dgate] + dqkv + list(dzg) + [dzf], axis=1)
    dwc = _mm("z_in_dw", r['x_bf'], dz, ta=True, out_dtype=BF16)
    g['w_in'] = jnp.concatenate([dwc[:, :Z_MAIN], dwc[:, Z_FG0:Z_FG0 + FOX_HEADS], dwc[:, Z_GATE0:Z_FG0]], axis=1)
    dx = _mm("z_in_dx", dz, p['wc'], tb=True, add=dx_n)
    return dx, g, carried


def _loss_head(y, target, tl=256):
    length, width = y.shape
    tl = min(tl, length)
    nt = length // tl

    def body(y_ref, t_ref, dy_ref, loss_ref, acc_sc):
        i = pl.program_id(0)

        @pl.when(i == 0)
        def _():
            acc_sc[...] = jnp.zeros_like(acc_sc)

        err = y_ref[...] - t_ref[...]
        dy_ref[...] = err / width
        acc_sc[...] += jnp.sum(jnp.square(err), axis=0, keepdims=True)

        @pl.when(i == nt - 1)
        def _():
            total = jnp.sum(acc_sc[...], axis=1, keepdims=True) * (0.5 / width)
            loss_ref[...] = jnp.broadcast_to(total, loss_ref.shape)

    spec = pl.BlockSpec((tl, width), lambda i: (i, 0))
    dy, loss = pl.pallas_call(
        body, grid=(nt,), in_specs=[spec, spec], out_specs=[spec, pl.BlockSpec((1, LANES), lambda i: (0, 0))],
        out_shape=[jax.ShapeDtypeStruct((length, width), F32), jax.ShapeDtypeStruct((1, LANES), F32)],
        scratch_shapes=[pltpu.VMEM((1, width), F32)], compiler_params=_params("arbitrary"), name="loss_head")(y, target)
    return loss[0, 0], dy


def _sum_parts(name, parts):
    count, rows, cols = parts.shape
    tr = 256

    def body(p_ref, o_ref):
        total = p_ref[0]
        for dev in range(1, count):
            total = total + p_ref[dev]
        o_ref[...] = total

    return pl.pallas_call(body, grid=(rows // tr,), in_specs=[pl.BlockSpec((count, tr, cols), lambda i: (0, i, 0))],
                          out_specs=pl.BlockSpec((tr, cols), lambda i: (i, 0)),
                          out_shape=jax.ShapeDtypeStruct((rows, cols), F32), compiler_params=_params("parallel"),
                          name=name)(parts)


def _adamw(name, w, parts, m, v):
    rows, cols = w.shape
    count = parts[0].shape[0]
    span = rows // len(parts)
    tr = span
    for cand in (256, 128, 64, 32, 16):
        if span % cand == 0:
            tr = cand
            break
    per_span = span // tr

    def body(*refs):
        w_ref, p_refs = refs[0], refs[1:1 + len(parts)]
        m_ref, v_ref, g_ref, d_ref, m2_ref, v2_ref = refs[1 + len(parts):]
        step = pl.program_id(0)
        grad = None
        for j, p_ref in enumerate(p_refs):
            total = p_ref[0].astype(F32)
            for dev in range(1, count):
                total = total + p_ref[dev].astype(F32)
            grad = total if grad is None else jnp.where(step >= j * per_span, total, grad)
        m2 = ADAM_B1 * m_ref[...] + (1.0 - ADAM_B1) * grad
        v2 = ADAM_B2 * v_ref[...] + (1.0 - ADAM_B2) * jnp.square(grad)
        m_hat = m2 / (1.0 - ADAM_B1 ** ADAM_STEP)
        v_hat = v2 / (1.0 - ADAM_B2 ** ADAM_STEP)
        g_ref[...] = grad
        d_ref[...] = -ADAM_LR * (m_hat / (jnp.sqrt(v_hat) + ADAM_EPS) + ADAM_WD * w_ref[...])
        m2_ref[...] = m2
        v2_ref[...] = v2

    spec = pl.BlockSpec((tr, cols), lambda i: (i, 0))
    pspecs = [pl.BlockSpec((count, tr, cols),
                           lambda i, j=j: (0, jnp.minimum(jnp.maximum(i - j * per_span, 0), per_span - 1), 0))
              for j in range(len(parts))]
    shape = jax.ShapeDtypeStruct((rows, cols), F32)
    return pl.pallas_call(body, grid=(rows // tr,), in_specs=[spec] + pspecs + [spec, spec], out_specs=[spec] * 4,
                          out_shape=[shape] * 4, compiler_params=_params("arbitrary"), name=name)(w, *parts, m, v)


class _NoExchange:
    def __init__(self, layers):
        self.layers = layers

    def weights(self, l, carried):
        return self.layers[l]

    def forward_riders(self, l):
        return {}

    def backward_riders(self, l):
        return {}

    def collect(self, l, grads, carried):
        pass


def _forward_backward(x, target, hooks):
    prepared, saved, carried = [], [], None
    x_bf = x.astype(BF16)
    for l in range(DEPTH):
        p = _prep_layer(hooks.weights(l, carried))
        x, x_bf, r, carried = _layer_fwd(x, x_bf, p, hooks.forward_riders(l))
        prepared.append(p)
        saved.append(r)
    loss, dx = _loss_head(x, target)
    grads = [None] * DEPTH
    for l in reversed(range(DEPTH)):
        dx, grads[l], carried = _layer_bwd(dx, saved[l], prepared[l], hooks.backward_riders(l))
        hooks.collect(l, grads[l], carried)
    return loss, dx, grads


def _exchange_copies(ins, outs, sems, scatter, with_arrivals):
    send_sems, recv_sems, local_sems = sems
    x, y, c = lax.axis_index("x"), lax.axis_index("y"), lax.axis_index("c")
    me = 4 * x + 2 * y + c
    local, sends, arrivals = [], [], []
    for a in range(len(ins)):
        local.append(pltpu.make_async_copy(ins[a].at[me] if scatter else ins[a], outs[a].at[me], local_sems.at[a]))
    for k in range(1, N_DEV):
        px = 1 - x if k & 4 else x
        py = 1 - y if k & 2 else y
        pc = 1 - c if k & 1 else c
        idx = 4 * px + 2 * py + pc
        for a in range(len(ins)):
            s = a * (N_DEV - 1) + k - 1
            src = ins[a].at[idx] if scatter else ins[a]
            common = dict(src_ref=src, send_sem=send_sems.at[s], recv_sem=recv_sems.at[s], device_id=(px, py, pc),
                          device_id_type=pl.DeviceIdType.MESH)
            sends.append(pltpu.make_async_remote_copy(dst_ref=outs[a].at[me], **common))
            if with_arrivals:
                arrivals.append(pltpu.make_async_remote_copy(dst_ref=outs[a].at[idx], **common))
    return local, sends, arrivals


def _exchange_start(ins, outs, sems, scatter):
    local, sends, _ = _exchange_copies(ins, outs, sems, scatter, False)
    for cp in local + sends:
        cp.start()


def _exchange_wait(ins, outs, sems, scatter):
    local, sends, arrivals = _exchange_copies(ins, outs, sems, scatter, True)
    for cp in local:
        cp.wait()
    for cp in sends:
        cp.wait_send()
    for cp in arrivals:
        cp.wait_recv()


def _exchange_parts(arrays, scatter):
    n = len(arrays)
    hbm = [pl.BlockSpec(memory_space=pltpu.HBM)] * n
    out_shape = [jax.ShapeDtypeStruct(a.shape if scatter else (N_DEV,) + a.shape, a.dtype) for a in arrays]
    nsem = n * (N_DEV - 1)
    sems = [pltpu.SemaphoreType.DMA((nsem,)), pltpu.SemaphoreType.DMA((nsem,)), pltpu.SemaphoreType.DMA((n,))]
    return hbm, out_shape, sems


def _exchange(name, arrays, scatter):
    n = len(arrays)
    hbm, out_shape, sems = _exchange_parts(arrays, scatter)

    def body(*refs):
        ins, outs, sem_refs = refs[:n], refs[n:2 * n], refs[2 * n:]
        _exchange_start(ins, outs, sem_refs, scatter)
        _exchange_wait(ins, outs, sem_refs, scatter)

    return pl.pallas_call(body, in_specs=hbm, out_specs=hbm, out_shape=out_shape, scratch_shapes=sems,
                          name=name)(*arrays)


def _call_carrying(name, body, grid, ins, in_specs, out_shape, out_specs, scratch, semantics, rider,
                   vmem=VMEM_LIMIT_BYTES):
    if rider is None:
        r_arrays, r_hbm, r_shape, r_sems = [], [], [], []
    else:
        r_arrays, scatter = rider
        r_hbm, r_shape, r_sems = _exchange_parts(r_arrays, scatter)
        semantics = ("arbitrary",) * len(grid)
    n_in, n_out, n_scr, n_r = len(ins), len(out_shape), len(scratch), len(r_arrays)

    def full_body(*refs):
        at = [0]

        def take(count):
            at[0] += count
            return refs[at[0] - count:at[0]]

        in_refs, r_in, out_refs, r_out, scr, r_scr = take(n_in), take(n_r), take(n_out), take(n_r), take(n_scr), take(3)
        ids = [pl.program_id(d) for d in range(len(grid))]
        if rider is not None:
            first = functools.reduce(jnp.logical_and, [i == 0 for i in ids])
            pl.when(first)(functools.partial(_exchange_start, r_in, r_out, r_scr, scatter))
        body(in_refs, out_refs, scr)
        if rider is not None:
            last = functools.reduce(jnp.logical_and, [i == g - 1 for i, g in zip(ids, grid)])
            pl.when(last)(functools.partial(_exchange_wait, r_in, r_out, r_scr, scatter))

    res = pl.pallas_call(
        full_body, grid=grid, in_specs=list(in_specs) + r_hbm, out_specs=list(out_specs) + r_hbm,
        out_shape=list(out_shape) + r_shape, scratch_shapes=list(scratch) + r_sems,
        compiler_params=_params(*semantics, vmem=vmem), name=name if rider is None else name + "_carrying")(
            *ins, *r_arrays)
    return res[:n_out], res[n_out:]


def _shard_2d(a):
    return a.reshape(-1, a.shape[-1])


def _full_layer_weight(name, t):
    if name in ('s5_w_glu', 'w_out', 'w_ffn_down'):
        return t.reshape(-1, t.shape[-1])
    if name == 'w_branch':
        return t.transpose(1, 2, 0, 3).reshape(3, BRANCH, D_MODEL)
    return t.transpose(1, 0, 2).reshape(t.shape[1], -1)


def _split_layer_grad(name, g):
    if name in ('s5_w_glu', 'w_out', 'w_ffn_down'):
        return g.reshape(N_DEV, g.shape[0] // N_DEV, g.shape[1])
    if name == 'w_branch':
        return g.reshape(3, BRANCH, N_DEV, D_MODEL // N_DEV).transpose(2, 0, 1, 3)
    return g.reshape(g.shape[0], N_DEV, g.shape[1] // N_DEV).transpose(1, 0, 2)


RIDING = [n for n in SHARDED if n != 'lru_conv_w']
FORWARD_CARRIERS = {'z_in': ['w_ffn_gate'], 's5_scan': ['w_ffn_down'],
                    'fox_fwd': ['w_in', 's5_w_glu', 'w_branch', 'w_out'], 'ffn_in': ['w_ffn_up']}
OWN_GRAD_CARRIERS = {'fox_bwd_kv': ['w_ffn_gate', 'w_ffn_up'], 's5_scan_bwd': ['w_ffn_down']}
NEXT_GRAD_CARRIER, NEXT_GRADS = 'fox_bwd_q', ['w_in', 's5_w_glu', 'w_branch', 'w_out']


class _Fsdp:
    def __init__(self, weights):
        self.weights_in = weights
        self.shard = {n: _shard_2d(weights[n]).astype(BF16) for n in RIDING}
        self.rows = {n: self.shard[n].shape[0] // DEPTH for n in RIDING}
        first = _exchange("gather_first_layer", [self.layer_shard(n, 0) for n in RIDING]
                          + [_shard_2d(weights['lru_conv_w'])], scatter=False)
        self.first = first[:-1]
        self.conv = first[-1].reshape((N_DEV,) + weights['lru_conv_w'].shape)
        self.outgoing = None
        self.incoming = {n: [None] * DEPTH for n in RIDING}

    def layer_shard(self, n, l):
        return self.shard[n][l * self.rows[n]:(l + 1) * self.rows[n]]

    def weights(self, l, carried):
        if l == 0:
            got = dict(zip(RIDING, self.first))
        else:
            got = {n: t for c, names in FORWARD_CARRIERS.items() for n, t in zip(names, carried[c])}
        w = {n: self.weights_in[n][l] for n in REPLICATED}
        for n in RIDING:
            w[n] = _full_layer_weight(n, got[n].reshape((N_DEV,) + self.weights_in[n].shape[1:]))
        w['lru_conv_w'] = _full_layer_weight('lru_conv_w', self.conv[:, l])
        return w

    def forward_riders(self, l):
        if l + 1 == DEPTH:
            return {}
        return {c: ([self.layer_shard(n, l + 1) for n in names], False) for c, names in FORWARD_CARRIERS.items()}

    def blocks(self, grads, names):
        return [_split_layer_grad(n, grads[n]).reshape(N_DEV, self.rows[n], -1).astype(BF16) for n in names]

    def backward_riders(self, l):
        riders = {'own': lambda grads: {c: (self.blocks(grads, names), True)
                                        for c, names in OWN_GRAD_CARRIERS.items()}}
        if self.outgoing is not None:
            riders[NEXT_GRAD_CARRIER] = (self.outgoing, True)
        return riders

    def collect(self, l, grads, carried):
        for n, t in zip(NEXT_GRADS, carried[NEXT_GRAD_CARRIER]):
            self.incoming[n][l + 1] = t
        for c, names in OWN_GRAD_CARRIERS.items():
            for n, t in zip(names, carried[c]):
                self.incoming[n][l] = t
        self.outgoing = self.blocks(grads, NEXT_GRADS)

    def finish(self, grads):
        conv = jnp.stack([_split_layer_grad('lru_conv_w', grads[l]['lru_conv_w']) for l in range(DEPTH)], axis=1)
        conv = conv.reshape(N_DEV, -1, conv.shape[-1]).astype(F32)
        last = _exchange("scatter_last_layer", self.outgoing + [conv], scatter=True)
        for n, t in zip(NEXT_GRADS, last[:-1]):
            self.incoming[n][0] = t
        return {**self.incoming, 'lru_conv_w': [last[-1]]}


def kernel(x, w_in, b_f, b_gate, s5_a_re, s5_a_im, s5_log_dt, s5_b_re, s5_b_im, s5_c_re, s5_c_im, s5_d, s5_w_glu, s5_b_glu, lru_conv_w, lru_conv_b, lru_w_a, lru_b_a, lru_w_x, lru_b_x, lru_lambda, w_branch, w_out, ln1_g, ln1_b, w_ffn_gate, w_ffn_up, w_ffn_down, ln2_g, ln2_b, loss_target, m_w_in, m_b_f, m_b_gate, m_s5_a_re, m_s5_a_im, m_s5_log_dt, m_s5_b_re, m_s5_b_im, m_s5_c_re, m_s5_c_im, m_s5_d, m_s5_w_glu, m_s5_b_glu, m_lru_conv_w, m_lru_conv_b, m_lru_w_a, m_lru_b_a, m_lru_w_x, m_lru_b_x, m_lru_lambda, m_w_branch, m_w_out, m_ln1_g, m_ln1_b, m_w_ffn_gate, m_w_ffn_up, m_w_ffn_down, m_ln2_g, m_ln2_b, v_w_in, v_b_f, v_b_gate, v_s5_a_re, v_s5_a_im, v_s5_log_dt, v_s5_b_re, v_s5_b_im, v_s5_c_re, v_s5_c_im, v_s5_d, v_s5_w_glu, v_s5_b_glu, v_lru_conv_w, v_lru_conv_b, v_lru_w_a, v_lru_b_a, v_lru_w_x, v_lru_b_x, v_lru_lambda, v_w_branch, v_w_out, v_ln1_g, v_ln1_b, v_w_ffn_gate, v_w_ffn_up, v_w_ffn_down, v_ln2_g, v_ln2_b):
    given = dict(locals())
    weights = {n: given[n] for n in WEIGHTS}
    moments_m = {n: given['m_' + n] for n in WEIGHTS}
    moments_v = {n: given['v_' + n] for n in WEIGHTS}

    hooks = _Fsdp(weights)
    loss_local, dx, grads = _forward_backward(x[0], loss_target[0], hooks)
    loss = lax.psum(loss_local, MESH_AXES)
    incoming = hooks.finish(grads)

    new = {}
    for n in SHARDED:
        res = _adamw("adamw_" + n, _shard_2d(weights[n]), incoming[n], _shard_2d(moments_m[n]), _shard_2d(moments_v[n]))
        new[n] = [t.reshape(weights[n].shape) for t in res]

    flat = jnp.concatenate([jnp.stack([grads[l][n] for l in range(DEPTH)]).astype(F32).reshape(-1) for n in REPLICATED])
    rows = -(-flat.shape[0] // (LANES * 256)) * 256
    packed = jnp.pad(flat, (0, rows * LANES - flat.shape[0])).reshape(rows, LANES)
    (arrived,) = _exchange("gather_small_grads", [packed], scatter=False)
    total, at = _sum_parts("sum_small_grads", arrived).reshape(-1), 0
    for n in REPLICATED:
        w2 = _shard_2d(weights[n])
        grad = total[at:at + w2.size].reshape((1,) + w2.shape)
        at += w2.size
        res = _adamw("adamw_" + n, w2, [grad], _shard_2d(moments_m[n]), _shard_2d(moments_v[n]))
        new[n] = [t.reshape(weights[n].shape) for t in res]

    return (loss, dx[None], *[new[n][0] for n in WEIGHTS], *[new[n][1] for n in WEIGHTS],
            *[new[n][2] for n in WEIGHTS], *[new[n][3] for n in WEIGHTS])
```

```python
import functools
import math

import jax
import jax.numpy as jnp
from jax import lax
from jax.experimental import pallas as pl
from jax.experimental.pallas import tpu as pltpu

F32 = jnp.float32
BF16 = jnp.bfloat16

D_MODEL = 1024
DEPTH = 4
BRANCH = 512
S5_GROUPS, S5_GROUP, S5_STATE = 32, 16, 64
S5_N = S5_GROUPS * S5_STATE
LRU_HEADS, LRU_HEAD_DIM = 8, 64
LRU_C = 8.0
CONV_WIDTH = 4
FOX_HEADS, FOX_HEAD_DIM = 8, 64
FFN_HIDDEN = 2816
ALPHA = (2.0 * DEPTH) ** 0.25
LN_EPS = 1e-5
IN_TOTAL = 6 * BRANCH + FOX_HEADS + 3 * D_MODEL
FG_PAD = 128
Z_MAIN = 6 * BRANCH
Z_GATE0 = Z_MAIN
Z_FG0 = Z_MAIN + 3 * D_MODEL
Z_TOTAL = Z_FG0 + FG_PAD
N_DEV = 8
MESH_AXES = ("x", "y", "c")

ADAM_LR, ADAM_B1, ADAM_B2, ADAM_EPS, ADAM_WD, ADAM_STEP = 0.001, 0.9, 0.999, 1e-08, 0.01, 10

VMEM_LIMIT_BYTES = 48 * 1024 * 1024
SCAN_SLACK_BYTES = 6 * 1024 * 1024
SUBLANES = 8
LANES = 128
NEG_BIG = -1e30

WEIGHTS = ['w_in', 'b_f', 'b_gate', 's5_a_re', 's5_a_im', 's5_log_dt', 's5_b_re', 's5_b_im', 's5_c_re', 's5_c_im',
           's5_d', 's5_w_glu', 's5_b_glu', 'lru_conv_w', 'lru_conv_b', 'lru_w_a', 'lru_b_a', 'lru_w_x', 'lru_b_x',
           'lru_lambda', 'w_branch', 'w_out', 'ln1_g', 'ln1_b', 'w_ffn_gate', 'w_ffn_up', 'w_ffn_down', 'ln2_g',
           'ln2_b']
SHARDED = ['w_in', 's5_w_glu', 'lru_conv_w', 'w_branch', 'w_out', 'w_ffn_gate', 'w_ffn_up', 'w_ffn_down']
REPLICATED = [n for n in WEIGHTS if n not in SHARDED]


def _params(*sem, vmem=VMEM_LIMIT_BYTES):
    return pltpu.CompilerParams(dimension_semantics=sem, vmem_limit_bytes=vmem)


def _tile(dim, want):
    if dim % LANES:
        return dim
    t = min(want, dim) // LANES * LANES
    while dim % t:
        t -= LANES
    return t


MM_VMEM_BUDGET_BYTES = 30 * 1024 * 1024
MM_MAX_TILE = 1024


def _divisor_tiles(dim, cap, must_divide=0):
    if dim % LANES:
        return [dim]
    out = [t for t in range(min(cap, dim) // LANES * LANES, 0, -LANES) if dim % t == 0 and must_divide % t == 0]
    return out or [dim]


def _mm_tiles(m, n, k, a_bytes, b_bytes, o_bytes, has_add, m_c0, n_c0, k_c0):
    for tk in _divisor_tiles(k, k, k_c0):
        best = None
        for tm in _divisor_tiles(m, MM_MAX_TILE, m_c0):
            for tn in _divisor_tiles(n, MM_MAX_TILE, n_c0):
                used = 2 * (tm * tk * a_bytes + tk * tn * b_bytes + tm * tn * o_bytes) + tm * tn * 4
                used += tm * tn * 4 if tk < k else 0
                used += 2 * tm * tn * 4 if has_add else 0
                if used <= MM_VMEM_BUDGET_BYTES and (best is None or tm * tn / (tm + tn) > best[0]):
                    best = (tm * tn / (tm + tn), tm, tn)
        if best is not None and (min(best[1], best[2]) >= 256 or tk <= 512):
            return best[1], best[2], tk
    raise ValueError("no matmul tiling fits VMEM")


def _mm(name, a, b, *, ta=False, tb=False, a_c0=0, a_w=None, b_c0=0, b_w=None, add=None, out_dtype=F32, rider=None):
    a_w = a.shape[1] if a_w is None else a_w
    b_w = b.shape[1] if b_w is None else b_w
    m, k = (a_w, a.shape[0]) if ta else (a.shape[0], a_w)
    n = b.shape[0] if tb else b_w
    assert k == (b_w if tb else b.shape[0]), (name, a.shape, b.shape)
    tm, tn, tk = _mm_tiles(m, n, k, a.dtype.itemsize, b.dtype.itemsize, jnp.dtype(out_dtype).itemsize,
                           add is not None, a_c0 if ta else 0, 0 if tb else b_c0,
                           math.gcd(0 if ta else a_c0, b_c0 if tb else 0))
    nk = k // tk
    a_off = a_c0 // (tm if ta else tk)
    b_off = b_c0 // (tk if tb else tn)
    assert a_c0 % (tm if ta else tk) == 0 and b_c0 % (tk if tb else tn) == 0, name
    dims = (((0 if ta else 1,), (1 if tb else 0,)), ((), ()))
    a_total, b_total = m * k * a.dtype.itemsize, n * k * b.dtype.itemsize
    a_stays = a_total + b_total * (m // tm) <= b_total + a_total * (n // tn)
    if nk > 1:
        a_stays = True

    def mn(o, i):
        return (o, i) if a_stays else (i, o)

    def body(in_refs, out_refs, scratch_refs):
        a_ref, b_ref = in_refs[0], in_refs[1]
        add_ref = in_refs[2] if add is not None else None
        (o_ref,) = out_refs
        part = lax.dot_general(a_ref[...].astype(BF16), b_ref[...].astype(BF16), dims, preferred_element_type=F32)

        def finish(r):
            if add is not None:
                r = r + add_ref[...]
            o_ref[...] = r.astype(o_ref.dtype)

        if nk == 1:
            finish(part)
            return
        (acc_ref,) = scratch_refs
        kk = pl.program_id(2)

        @pl.when(kk == 0)
        def _():
            acc_ref[...] = part

        @pl.when(kk > 0)
        def _():
            acc_ref[...] += part

        @pl.when(kk == nk - 1)
        def _():
            finish(acc_ref[...])

    def a_map(o, i, kk):
        im = mn(o, i)[0]
        return (kk, im + a_off) if ta else (im, kk + a_off)

    def b_map(o, i, kk):
        jn = mn(o, i)[1]
        return (jn, kk + b_off) if tb else (kk, jn + b_off)

    a_spec = pl.BlockSpec((tk, tm) if ta else (tm, tk), a_map)
    b_spec = pl.BlockSpec((tn, tk) if tb else (tk, tn), b_map)
    o_spec = pl.BlockSpec((tm, tn), lambda o, i, kk: mn(o, i))
    ins, in_specs = [a, b], [a_spec, b_spec]
    if add is not None:
        ins.append(add)
        in_specs.append(o_spec)
    grid = (m // tm, n // tn, nk) if a_stays else (n // tn, m // tm, nk)
    (out,), carried = _call_carrying(
        name, body, grid, ins, in_specs, [jax.ShapeDtypeStruct((m, n), out_dtype)], [o_spec],
        [pltpu.VMEM((tm, tn), F32)] if nk > 1 else [], ("parallel", "parallel", "arbitrary"), rider)
    return out if rider is None else (out, carried)


def _row_spec(tl, c0, w):
    assert c0 % w == 0
    return pl.BlockSpec((tl, w), lambda i: (i, c0 // w))


def _whole_spec(p):
    return pl.BlockSpec(p.shape, lambda i: (0,) * p.ndim)


def _ew(name, f, rows, prm, out_ws, tl=256, out_dtypes=None):
    out_dtypes = out_dtypes or [F32] * len(out_ws)
    nrows, nprm = len(rows), len(prm)
    length = rows[0][0].shape[0]
    tl = min(tl, length)

    def body(*refs):
        vals = [r[...] for r in refs[:nrows + nprm]]
        outs = f(*vals)
        for o_ref, o in zip(refs[nrows + nprm:], outs):
            o_ref[...] = o.astype(o_ref.dtype)

    return pl.pallas_call(
        body, grid=(length // tl,),
        in_specs=[_row_spec(tl, c0, w) for (_, c0, w) in rows] + [_whole_spec(p) for p in prm],
        out_specs=[_row_spec(tl, 0, w) for w in out_ws],
        out_shape=[jax.ShapeDtypeStruct((length, w), dt) for w, dt in zip(out_ws, out_dtypes)],
        compiler_params=_params("parallel"), name=name)(*[r[0] for r in rows], *prm)


def _ew_bwd(name, f, rows, prm, douts, row_grad, tl=256):
    nrows, nprm, nd = len(rows), len(prm), len(douts)
    length = rows[0][0].shape[0]
    tl = min(tl, length)
    want = [i for i in range(nrows) if row_grad[i]]

    def body(*refs):
        vals = [r[...] for r in refs[:nrows + nprm]]
        cts = tuple(r[...] for r in refs[nrows + nprm:nrows + nprm + nd])
        out_refs = refs[nrows + nprm + nd:]
        _, vjp = jax.vjp(lambda *v: tuple(f(*v)), *vals)
        grads = vjp(cts)
        for o_ref, i in zip(out_refs[:len(want)], want):
            o_ref[...] = grads[i].astype(o_ref.dtype)

        @pl.when(pl.program_id(0) == 0)
        def _():
            for o_ref in out_refs[len(want):]:
                o_ref[...] = jnp.zeros_like(o_ref)

        for o_ref, g in zip(out_refs[len(want):], grads[nrows:]):
            o_ref[...] += g

    return pl.pallas_call(
        body, grid=(length // tl,),
        in_specs=([_row_spec(tl, c0, w) for (_, c0, w) in rows] + [_whole_spec(p) for p in prm]
                  + [_row_spec(tl, 0, d.shape[1]) for d in douts]),
        out_specs=[_row_spec(tl, 0, rows[i][2]) for i in want] + [_whole_spec(p) for p in prm],
        out_shape=([jax.ShapeDtypeStruct((length, rows[i][2]), F32 if row_grad[i] is True else row_grad[i])
                    for i in want]
                   + [jax.ShapeDtypeStruct(p.shape, F32) for p in prm]),
        compiler_params=_params("arbitrary"), name=name)(*[r[0] for r in rows], *prm, *douts)


def _row_ids(width):
    return lax.broadcasted_iota(jnp.int32, (SUBLANES, width), 0)


def _shift_rows(v, d, reverse):
    return pltpu.roll(v, (SUBLANES - d) if reverse else d, 0)


def _scan_real(name, a, b, b2=None, *, reverse=False, bn=256):
    length, n = a.shape
    bn = _tile(n, bn)
    nb = length // SUBLANES
    operands = [a, b] if b2 is None else [a, b, b2]

    def body(*refs):
        a_ref, b_ref, h_ref = refs[0], refs[1], refs[-1]
        rows = _row_ids(bn)

        def step(it, carry):
            i = (nb - 1 - it) if reverse else it
            sl = pl.ds(pl.multiple_of(i * SUBLANES, SUBLANES), SUBLANES)
            av, bv = a_ref[sl, :], b_ref[sl, :]
            if b2 is not None:
                bv = bv + refs[2][sl, :]
            for d in (1, 2, 4):
                live = (rows < SUBLANES - d) if reverse else (rows >= d)
                a_in = jnp.where(live, _shift_rows(av, d, reverse), 1.0)
                b_in = jnp.where(live, _shift_rows(bv, d, reverse), 0.0)
                bv = bv + av * b_in
                av = av * a_in
            hv = bv + av * carry
            h_ref[sl, :] = hv
            edge = hv[0:1, :] if reverse else hv[SUBLANES - 1:SUBLANES, :]
            return jnp.broadcast_to(edge, (SUBLANES, bn))

        lax.fori_loop(0, nb, step, jnp.zeros((SUBLANES, bn), F32))

    spec = pl.BlockSpec((length, bn), lambda j: (0, j))
    return pl.pallas_call(body, grid=(n // bn,), in_specs=[spec] * len(operands), out_specs=spec,
                          out_shape=jax.ShapeDtypeStruct((length, n), F32),
                          compiler_params=_params("parallel"), name=name)(*operands)


def _cmul(ar, ai, br, bi):
    return ar * br - ai * bi, ar * bi + ai * br


def _scan_cplx(name, lam_re, lam_im, x_re, x_im, *, reverse=False, h_re=None, h_im=None, bn=128, rider=None):
    length, n = x_re.shape
    bn = _tile(n, bn)
    nb = length // SUBLANES
    with_dot = h_re is not None

    def body(in_refs, out_refs, scratch_refs):
        refs = tuple(in_refs) + tuple(out_refs)
        if with_dot:
            lr_ref, li_ref, xr_ref, xi_ref, hr_ref, hi_ref, gr_ref, gi_ref, dr_ref, di_ref = refs
        else:
            lr_ref, li_ref, xr_ref, xi_ref, gr_ref, gi_ref = refs
        rows = _row_ids(bn)
        lr = jnp.broadcast_to(lr_ref[...], (SUBLANES, bn))
        li = jnp.broadcast_to(li_ref[...], (SUBLANES, bn))
        powers = [(lr, li)]
        for _ in range(SUBLANES - 1):
            powers.append(_cmul(powers[-1][0], powers[-1][1], lr, li))
        zero = jnp.zeros((SUBLANES, bn), F32)
        steps = []
        for d in (1, 2, 4):
            live = (rows < SUBLANES - d) if reverse else (rows >= d)
            steps.append((d, jnp.where(live, powers[d - 1][0], 0.0), jnp.where(live, powers[d - 1][1], 0.0)))
        cr, ci = zero, zero
        for r in range(SUBLANES):
            e = (SUBLANES - r) if reverse else (r + 1)
            cr = jnp.where(rows == r, powers[e - 1][0], cr)
            ci = jnp.where(rows == r, powers[e - 1][1], ci)

        def step(it, carry):
            i = (nb - 1 - it) if reverse else it
            sl = pl.ds(pl.multiple_of(i * SUBLANES, SUBLANES), SUBLANES)
            vr, vi = xr_ref[sl, :], xi_ref[sl, :]
            for d, pr, pi in steps:
                sr, si = _cmul(pr, pi, _shift_rows(vr, d, reverse), _shift_rows(vi, d, reverse))
                vr, vi = vr + sr, vi + si
            kr, ki = _cmul(cr, ci, carry[0], carry[1])
            vr, vi = vr + kr, vi + ki
            gr_ref[sl, :] = vr
            gi_ref[sl, :] = vi
            er = vr[0:1, :] if reverse else vr[SUBLANES - 1:SUBLANES, :]
            ei = vi[0:1, :] if reverse else vi[SUBLANES - 1:SUBLANES, :]
            new = (jnp.broadcast_to(er, (SUBLANES, bn)), jnp.broadcast_to(ei, (SUBLANES, bn)))
            if not with_dot:
                return new
            prev = pl.ds(pl.multiple_of(jnp.maximum(i - 1, 0) * SUBLANES, SUBLANES), SUBLANES)
            keep = jnp.where(i > 0, 1.0, 0.0)
            pr_ = jnp.broadcast_to(hr_ref[prev, :][SUBLANES - 1:SUBLANES, :], (SUBLANES, bn)) * keep
            pi_ = jnp.broadcast_to(hi_ref[prev, :][SUBLANES - 1:SUBLANES, :], (SUBLANES, bn)) * keep
            hr = jnp.where(rows == 0, pr_, pltpu.roll(hr_ref[sl, :], 1, 0))
            hi = jnp.where(rows == 0, pi_, pltpu.roll(hi_ref[sl, :], 1, 0))
            return new + (carry[2] + vr * hr + vi * hi, carry[3] + vi * hr - vr * hi)

        init = (zero, zero, zero, zero) if with_dot else (zero, zero)
        out = lax.fori_loop(0, nb, step, init)
        if with_dot:
            dr_ref[...] = jnp.sum(out[2], axis=0, keepdims=True)
            di_ref[...] = jnp.sum(out[3], axis=0, keepdims=True)

    col = pl.BlockSpec((length, bn), lambda j: (0, j))
    vec = pl.BlockSpec((1, bn), lambda j: (0, j))
    ins = [lam_re, lam_im, x_re, x_im] + ([h_re, h_im] if with_dot else [])
    in_specs = [vec, vec, col, col] + ([col, col] if with_dot else [])
    out_specs = [col, col] + ([vec, vec] if with_dot else [])
    full = jax.ShapeDtypeStruct((length, n), F32)
    row = jax.ShapeDtypeStruct((1, n), F32)
    out_shape = [full, full] + ([row, row] if with_dot else [])
    columns = 6 if with_dot else 4
    outs, carried = _call_carrying(name, body, (n // bn,), ins, in_specs, out_shape, out_specs, [], ("parallel",), rider,
                                   vmem=2 * columns * length * bn * 4 + SCAN_SLACK_BYTES)
    return list(outs) if rider is None else list(outs) + [carried]


FOX_SCALE = FOX_HEAD_DIM ** -0.5
FOX_AUG = 128
FOX_CQ0 = FOX_HEAD_DIM
FOX_CK0 = FOX_HEAD_DIM + 3
NT = (((1,), (1,)), ((), ()))


def _fox_logits_t(ka, qa, on_diagonal):
    st = lax.dot_general(ka, qa, NT, preferred_element_type=F32)
    if on_diagonal:
        key = lax.broadcasted_iota(jnp.int32, st.shape, 0)
        query = lax.broadcasted_iota(jnp.int32, st.shape, 1)
        st = jnp.where(key <= query, st, NEG_BIG)
    return st


def _fox_pair(s, nt, q_first):
    if q_first:
        qi = sum((s >= (m * (m + 1)) // 2).astype(jnp.int32) for m in range(1, nt))
        return qi, s - ((qi * (qi + 1)) >> 1)
    ki = sum((s >= m * nt - (m * (m - 1)) // 2).astype(jnp.int32) for m in range(1, nt))
    return ki + s - (ki * nt - ((ki * (ki - 1)) >> 1)), ki


FOX_HEADS_PER_STEP = 4


class _HeadView:
    def __init__(self, ref, head):
        self.ref, self.head = ref, head
        self.shape, self.dtype = ref.shape[1:], ref.dtype

    def _index(self, idx):
        idx = idx if isinstance(idx, tuple) else (idx,)
        return (self.head,) + (() if idx == (Ellipsis,) else idx)

    def __getitem__(self, idx):
        return self.ref[self._index(idx)]

    def __setitem__(self, idx, value):
        self.ref[self._index(idx)] = value


def _head_views(in_refs, out_refs, scratch_refs):
    return [tuple([_HeadView(r, h) for r in refs] for refs in (in_refs, out_refs, scratch_refs))
            for h in range(FOX_HEADS_PER_STEP)]


def _fox_specs(t, nt, q_first):
    hb = FOX_HEADS_PER_STEP
    q_idx = lambda s: _fox_pair(s, nt, q_first)[0]
    k_idx = lambda s: _fox_pair(s, nt, q_first)[1]
    rows = lambda idx, w: pl.BlockSpec((hb, t, w), lambda h, s: (h, idx(s), 0))
    cols = lambda idx, w: pl.BlockSpec((hb, w, t), lambda h, s: (h, 0, idx(s)))
    return rows, cols, q_idx, k_idx, (nt * (nt + 1)) // 2


def _fox_fwd(qa, ka, vt, t=512, rider=None):
    heads, length, _ = qa.shape
    dh = vt.shape[1]
    t = min(t, length)
    nt = length // t

    def body(in_refs, out_refs, scratch_refs):
        views = _head_views(in_refs, out_refs, scratch_refs)
        qi, ki = _fox_pair(pl.program_id(1), nt, True)

        @pl.when(ki == 0)
        def _():
            for _, _, (m_sc, l_sc, acc_sc) in views:
                m_sc[...] = jnp.full(m_sc.shape, NEG_BIG, F32)
                l_sc[...] = jnp.zeros(l_sc.shape, F32)
                acc_sc[...] = jnp.zeros(acc_sc.shape, F32)

        def step(on_diagonal):
            for (qa_ref, ka_ref, vt_ref), _, (m_sc, l_sc, acc_sc) in views:
                st = _fox_logits_t(ka_ref[...], qa_ref[...], on_diagonal)
                m_old = m_sc[...]
                m_new = jnp.maximum(m_old, jnp.max(st, axis=0, keepdims=True))
                pt = jnp.exp(st - m_new)
                scale = jnp.exp(m_old - m_new)
                l_sc[...] = scale * l_sc[...] + jnp.sum(pt, axis=0, keepdims=True)
                acc_sc[...] = scale * acc_sc[...] + jnp.dot(vt_ref[...], pt.astype(BF16), preferred_element_type=F32)
                m_sc[...] = m_new

        pl.when(ki < qi)(functools.partial(step, False))
        pl.when(ki == qi)(functools.partial(step, True))

        @pl.when(ki == qi)
        def _():
            for _, (o_ref, lse_ref), (m_sc, l_sc, acc_sc) in views:
                o_ref[...] = acc_sc[...] / l_sc[...]
                lse_ref[...] = m_sc[...] + jnp.log(l_sc[...])

    hb = FOX_HEADS_PER_STEP
    rows, cols, q_idx, k_idx, pairs = _fox_specs(t, nt, True)
    (ot, lse), carried = _call_carrying(
        "fox_fwd", body, (heads // hb, pairs), [qa, ka, vt],
        [rows(q_idx, FOX_AUG), rows(k_idx, FOX_AUG), cols(k_idx, dh)],
        [jax.ShapeDtypeStruct((heads, dh, length), F32), jax.ShapeDtypeStruct((heads, 1, length), F32)],
        [cols(q_idx, dh), cols(q_idx, 1)],
        [pltpu.VMEM((hb, 1, t), F32), pltpu.VMEM((hb, 1, t), F32), pltpu.VMEM((hb, dh, t), F32)],
        ("parallel", "arbitrary"), rider)
    return ot, lse, carried


def _fox_ds_t(qa_ref, ka_ref, v_ref, dot_ref, ot_ref, lse_ref, on_diagonal):
    pt = jnp.exp(_fox_logits_t(ka_ref[...], qa_ref[...], on_diagonal) - lse_ref[...])
    dpt = jnp.dot(v_ref[...], dot_ref[...], preferred_element_type=F32)
    delta = jnp.sum(dot_ref[...].astype(F32) * ot_ref[...], axis=0, keepdims=True)
    return pt, pt * (dpt - delta)


def _fox_bwd_q(qa, ka, kat, v, dot, ot, lse, t=512, rider=None):
    heads, length, _ = qa.shape
    dh = v.shape[2]
    t = min(t, length)
    nt = length // t

    def body(in_refs, out_refs, scratch_refs):
        views = _head_views(in_refs, out_refs, scratch_refs)
        qi, ki = _fox_pair(pl.program_id(1), nt, True)

        @pl.when(ki == 0)
        def _():
            for _, _, (acc_sc,) in views:
                acc_sc[...] = jnp.zeros(acc_sc.shape, F32)

        def step(on_diagonal):
            for (qa_ref, ka_ref, kat_ref, v_ref, dot_ref, ot_ref, lse_ref), _, (acc_sc,) in views:
                _, dst = _fox_ds_t(qa_ref, ka_ref, v_ref, dot_ref, ot_ref, lse_ref, on_diagonal)
                acc_sc[...] += jnp.dot(kat_ref[...], dst.astype(BF16), preferred_element_type=F32)

        pl.when(ki < qi)(functools.partial(step, False))
        pl.when(ki == qi)(functools.partial(step, True))

        @pl.when(ki == qi)
        def _():
            for _, (dq_ref, dc_ref), (acc_sc,) in views:
                dq_ref[...] = (acc_sc[0:dh, :] * FOX_SCALE).astype(dq_ref.dtype)
                dc_ref[...] = acc_sc[FOX_CQ0:FOX_CQ0 + 1, :]

    hb = FOX_HEADS_PER_STEP
    rows, cols, q_idx, k_idx, pairs = _fox_specs(t, nt, True)
    (dqt, dcq), carried = _call_carrying(
        "fox_bwd_q", body, (heads // hb, pairs), [qa, ka, kat, v, dot, ot, lse],
        [rows(q_idx, FOX_AUG), rows(k_idx, FOX_AUG), cols(k_idx, FOX_AUG), rows(k_idx, dh), cols(q_idx, dh),
         cols(q_idx, dh), cols(q_idx, 1)],
        [jax.ShapeDtypeStruct((heads, dh, length), BF16), jax.ShapeDtypeStruct((heads, 1, length), F32)],
        [cols(q_idx, dh), cols(q_idx, 1)],
        [pltpu.VMEM((hb, FOX_AUG, t), F32)], ("parallel", "arbitrary"), rider)
    return dqt, dcq, carried


def _fox_bwd_kv(qa, ka, v, do, dot, ot, lse, t=512, rider=None):
    heads, length, _ = qa.shape
    dh = v.shape[2]
    t = min(t, length)
    nt = length // t

    def body(in_refs, out_refs, scratch_refs):
        views = _head_views(in_refs, out_refs, scratch_refs)
        qi, ki = _fox_pair(pl.program_id(1), nt, False)

        @pl.when(qi == ki)
        def _():
            for _, _, (dka_sc, dv_sc) in views:
                dka_sc[...] = jnp.zeros(dka_sc.shape, F32)
                dv_sc[...] = jnp.zeros(dv_sc.shape, F32)

        def step(on_diagonal):
            for (qa_ref, ka_ref, v_ref, do_ref, dot_ref, ot_ref, lse_ref), _, (dka_sc, dv_sc) in views:
                pt, dst = _fox_ds_t(qa_ref, ka_ref, v_ref, dot_ref, ot_ref, lse_ref, on_diagonal)
                dv_sc[...] += jnp.dot(pt.astype(BF16), do_ref[...], preferred_element_type=F32)
                dka_sc[...] += jnp.dot(dst.astype(BF16), qa_ref[...], preferred_element_type=F32)

        pl.when(qi > ki)(functools.partial(step, False))
        pl.when(qi == ki)(functools.partial(step, True))

        @pl.when(qi == nt - 1)
        def _():
            for _, (dk_ref, dv_ref, dc_ref), (dka_sc, dv_sc) in views:
                dka = dka_sc[...]
                lane = lax.broadcasted_iota(jnp.int32, dka.shape, 1)
                dk_ref[...] = dka_sc[:, :dh].astype(dk_ref.dtype)
                dc_ref[...] = jnp.sum(jnp.where(lane == FOX_CK0, dka, 0.0), axis=1, keepdims=True)
                dv_ref[...] = dv_sc[...].astype(dv_ref.dtype)

    hb = FOX_HEADS_PER_STEP
    rows, cols, q_idx, k_idx, pairs = _fox_specs(t, nt, False)
    big = jax.ShapeDtypeStruct((heads, length, dh), BF16)
    (dk, dv, dc), carried = _call_carrying(
        "fox_bwd_kv", body, (heads // hb, pairs), [qa, ka, v, do, dot, ot, lse],
        [rows(q_idx, FOX_AUG), rows(k_idx, FOX_AUG), rows(k_idx, dh), rows(q_idx, dh), cols(q_idx, dh),
         cols(q_idx, dh), cols(q_idx, 1)],
        [big, big, jax.ShapeDtypeStruct((heads, length, 1), F32)], [rows(k_idx, dh), rows(k_idx, dh), rows(k_idx, 1)],
        [pltpu.VMEM((hb, t, FOX_AUG), F32), pltpu.VMEM((hb, t, dh), F32)], ("parallel", "arbitrary"), rider)
    return dk, dv, dc, carried


def _split3(x):
    hi = lax.reduce_precision(x, 8, 7)
    mid = lax.reduce_precision(x - hi, 8, 7)
    return [hi, mid, lax.reduce_precision(x - hi - mid, 8, 7)]


def _fox_operands(z, cum):
    length = z.shape[0]
    parts = jnp.stack(_split3(cum[:, :FOX_HEADS].T), axis=-1)
    ones = jnp.ones_like(parts)
    pad = jnp.zeros((FOX_HEADS, length, FOX_AUG - FOX_HEAD_DIM - 6), F32)
    qa = jnp.concatenate([_heads(z, 3 * BRANCH) * FOX_SCALE, parts, ones, pad], axis=-1).astype(BF16)
    ka = jnp.concatenate([_heads(z, 4 * BRANCH), ones, -parts, pad], axis=-1).astype(BF16)
    v = _heads(z, 5 * BRANCH).astype(BF16)
    return qa, ka, ka.transpose(0, 2, 1), v, v.transpose(0, 2, 1)


def _softplus(x):
    return jnp.maximum(x, 0.0) + jnp.log1p(jnp.exp(-jnp.abs(x)))


def _f_s5_disc(a_re, a_im, log_dt, b_re, b_im):
    dt = jnp.exp(log_dt)
    mag = jnp.exp(a_re * dt)
    lr, li = mag * jnp.cos(a_im * dt), mag * jnp.sin(a_im * dt)
    den = a_re * a_re + a_im * a_im
    qr = ((lr - 1.0) * a_re + li * a_im) / den
    qi = (li * a_re - (lr - 1.0) * a_im) / den
    return lr, li, qr * b_re - qi * b_im, qr * b_im + qi * b_re


def _f_s5_y1(hc_re, hc_im, u, d):
    return (jax.nn.gelu(hc_re + hc_im + d * u),)


def _f_s5_glu(y1, pre, b):
    return (y1 * jax.nn.sigmoid(pre + b),)


def _f_conv(x0, x1, x2, x3, w0, w1, w2, w3, b):
    return (b + w0 * x0 + w1 * x1 + w2 * x2 + w3 * x3,)


def _f_conv_t(d0, d1, d2, d3, w0, w1, w2, w3):
    return (w0 * d0 + w1 * d1 + w2 * d2 + w3 * d3,)


def _lru_coeffs(xc, pa, px, b_a, b_x, lam):
    r = jax.nn.sigmoid(pa + b_a)
    i = jax.nn.sigmoid(px + b_x)
    log_a = -LRU_C * _softplus(-lam) * r
    a = jnp.exp(log_a)
    mult = jnp.sqrt(-jnp.tanh(log_a) * (a * a + 1.0))
    return a, mult * (i * xc)


def _f_lru_gates(xc, pa, px, b_a, b_x, lam):
    return _lru_coeffs(xc, pa, px, b_a, b_x, lam)


def _f_lru_step(xc, pa, px, h_prev, b_a, b_x, lam):
    a, b = _lru_coeffs(xc, pa, px, b_a, b_x, lam)
    return (a * h_prev + b,)


def _f_lru_out(gate, h):
    return (jax.nn.gelu(gate) * h,)


def _f_logf(zf, bf):
    return (-_softplus(-(zf + bf)),)


def _f_merge(p0, p1, p2, z0, z1, z2, b0, b1, b2):
    return (jax.nn.sigmoid(z0 + b0) * p0 + jax.nn.sigmoid(z1 + b1) * p1 + jax.nn.sigmoid(z2 + b2) * p2,)


def _f_ln(x, r, g, b):
    s = ALPHA * x + r
    mu = jnp.mean(s, axis=-1, keepdims=True)
    var = jnp.mean(jnp.square(s - mu), axis=-1, keepdims=True)
    return ((s - mu) * lax.rsqrt(var + LN_EPS) * g + b,)


def _f_swiglu(hg, hu):
    return (jax.nn.silu(hg) * hu,)


def _full(a):
    return (a, 0, a.shape[1])


def _blockdiag(t):
    g, a, b = t.shape
    eye = jnp.eye(g, dtype=t.dtype)
    return (t[:, :, None, :] * eye[:, None, :, None]).reshape(g * a, g * b)


def _blockdiag_take(d, g, a, b):
    eye = jnp.eye(g, dtype=d.dtype)
    return (d.reshape(g, a, g, b) * eye[:, None, :, None]).sum(axis=2)


def _delay(a, j):
    return a if j == 0 else jnp.pad(a, ((j, 0), (0, 0)))[:a.shape[0]]


def _advance(a, j):
    return a if j == 0 else jnp.pad(a, ((0, j), (0, 0)))[j:]


def _heads(a, c0=0):
    length = a.shape[0]
    return a[:, c0:c0 + BRANCH].reshape(length, FOX_HEADS, FOX_HEAD_DIM).transpose(1, 0, 2)


def _unheads(a):
    return a.transpose(1, 0, 2).reshape(a.shape[1], BRANCH)


def _row(v):
    return v.reshape(1, -1).astype(F32)


def _col(v):
    return v.reshape(-1, 1).astype(F32)


def _prep_layer(w):
    p = {}
    w_in = w['w_in']
    p['wc'] = jnp.concatenate(
        [w_in[:, :Z_MAIN], w_in[:, Z_MAIN + FOX_HEADS:], w_in[:, Z_MAIN:Z_MAIN + FOX_HEADS],
         jnp.zeros((D_MODEL, FG_PAD - FOX_HEADS), w_in.dtype)], axis=1)
    p['b_f'] = jnp.pad(_row(w['b_f']), ((0, 0), (0, FG_PAD - FOX_HEADS)))
    p['b_gate'] = [_row(w['b_gate'][k * D_MODEL:(k + 1) * D_MODEL]) for k in range(3)]
    p['disc_in'] = [_col(w['s5_a_re']), _col(w['s5_a_im']), _col(jnp.repeat(w['s5_log_dt'], S5_STATE)),
                    w['s5_b_re'].reshape(S5_N, S5_GROUP), w['s5_b_im'].reshape(S5_N, S5_GROUP)]
    lam_re, lam_im, bb_re, bb_im = _ew("s5_disc", _f_s5_disc, [_full(a) for a in p['disc_in']], [],
                                       [1, 1, S5_GROUP, S5_GROUP], tl=S5_N)
    p['lam_re'], p['lam_im'] = lam_re.reshape(1, S5_N), lam_im.reshape(1, S5_N)
    to_blk = lambda t: _blockdiag(t.reshape(S5_GROUPS, S5_STATE, S5_GROUP).transpose(0, 2, 1)).astype(BF16)
    p['s5_bre'], p['s5_bim'] = to_blk(bb_re), to_blk(bb_im)
    p['s5_cre'] = _blockdiag(w['s5_c_re'].transpose(0, 2, 1)).astype(BF16)
    p['s5_cimn'] = _blockdiag(-w['s5_c_im'].transpose(0, 2, 1)).astype(BF16)
    p['s5_d'], p['wglu'], p['bglu'] = _row(w['s5_d']), w['s5_w_glu'], _row(w['s5_b_glu'])
    p['conv_w'] = [_row(w['lru_conv_w'][CONV_WIDTH - 1 - j]) for j in range(CONV_WIDTH)]
    p['conv_b'] = _row(w['lru_conv_b'])
    p['wax'] = jnp.concatenate([_blockdiag(w['lru_w_a']), _blockdiag(w['lru_w_x'])], axis=1).astype(BF16)
    p['b_a'], p['b_x'], p['lam'] = _row(w['lru_b_a']), _row(w['lru_b_x']), _row(w['lru_lambda'])
    p['wb'] = [w['w_branch'][k] for k in range(3)]
    p['wout'] = w['w_out']
    p['ln1'] = [_row(w['ln1_g']), _row(w['ln1_b'])]
    p['wgu'] = jnp.concatenate([w['w_ffn_gate'], w['w_ffn_up']], axis=1)
    p['wd'] = w['w_ffn_down']
    p['ln2'] = [_row(w['ln2_g']), _row(w['ln2_b'])]
    return p


def _with_copy(f):
    def g(*args):
        (y,) = f(*args)
        return y, y
    return g


def _layer_fwd(x, x_bf, p, riders=None):
    length = x.shape[0]
    riders, carried = riders or {}, {}
    r = {'x': x, 'x_bf': x_bf}
    z = _mm("z_in", x_bf, p['wc'], rider=riders.get('z_in'))
    if 'z_in' in riders:
        z, carried['z_in'] = z
    r['z'] = z
    bu_re = _mm("s5_bu_re", z, p['s5_bre'], a_w=BRANCH)
    bu_im = _mm("s5_bu_im", z, p['s5_bim'], a_w=BRANCH)
    scanned = _scan_cplx("s5_scan", p['lam_re'], p['lam_im'], bu_re, bu_im, bn=256, rider=riders.get('s5_scan'))
    r['h_re'], r['h_im'] = scanned[0], scanned[1]
    if 's5_scan' in riders:
        carried['s5_scan'] = scanned[2]
    r['hc_re'] = _mm("s5_hc_re", r['h_re'], p['s5_cre'])
    r['hc_im'] = _mm("s5_hc_im", r['h_im'], p['s5_cimn'])
    r['y1'], r['y1_bf'] = _ew("s5_y1", _with_copy(_f_s5_y1), [_full(r['hc_re']), _full(r['hc_im']), (z, 0, BRANCH)],
                              [p['s5_d']], [BRANCH, BRANCH], out_dtypes=[F32, BF16])
    r['pre'] = _mm("s5_glu_pre", r['y1_bf'], p['wglu'])
    (r['ys5'],) = _ew("s5_glu", _f_s5_glu, [_full(r['y1']), _full(r['pre'])], [p['bglu']], [BRANCH], out_dtypes=[BF16])
    xl = z[:, BRANCH:2 * BRANCH]
    r['xd'] = [_delay(xl, j) for j in range(1, CONV_WIDTH)]
    r['xc'], r['xc_bf'] = _ew("lru_conv", _with_copy(_f_conv), [(z, BRANCH, BRANCH)] + [_full(a) for a in r['xd']],
                              p['conv_w'] + [p['conv_b']], [BRANCH, BRANCH], out_dtypes=[F32, BF16])
    r['papx'] = _mm("lru_gate_mm", r['xc_bf'], p['wax'])
    r['a'], b = _ew("lru_gates", _f_lru_gates, [_full(r['xc']), (r['papx'], 0, BRANCH), (r['papx'], BRANCH, BRANCH)],
                    [p['b_a'], p['b_x'], p['lam']], [BRANCH, BRANCH])
    r['h'] = _scan_real("lru_scan", r['a'], b)
    (r['ylru'],) = _ew("lru_out", _f_lru_out, [(z, 2 * BRANCH, BRANCH), _full(r['h'])], [], [BRANCH], out_dtypes=[BF16])
    (logf,) = _ew("fox_logf", _f_logf, [(z, Z_FG0, FG_PAD)], [p['b_f']], [FG_PAD])
    cum = _scan_real("fox_cum", jnp.ones((length, FG_PAD), F32), logf)
    qa, ka, kat, v, vt = _fox_operands(z, cum)
    r['fox'] = (qa, ka, kat, v)
    r['ot'], r['lse'], brought = _fox_fwd(qa, ka, vt, rider=riders.get('fox_fwd'))
    if 'fox_fwd' in riders:
        carried['fox_fwd'] = brought
    r['yfox'] = r['ot'].reshape(BRANCH, length).T.astype(BF16)
    ys = [r['ys5'], r['ylru'], r['yfox']]
    r['proj'] = [_mm("proj_%d" % k, ys[k], p['wb'][k]) for k in range(3)]
    gate_rows = [(z, Z_GATE0 + k * D_MODEL, D_MODEL) for k in range(3)]
    (r['mix'],) = _ew("merge", _f_merge, [_full(a) for a in r['proj']] + gate_rows, p['b_gate'], [D_MODEL], tl=128,
                      out_dtypes=[BF16])
    r['mixed'] = _mm("w_out", r['mix'], p['wout'])
    two = dict(out_ws=[D_MODEL, D_MODEL], out_dtypes=[F32, BF16])
    x1, r['x1_bf'] = _ew("ln1", _with_copy(_f_ln), [_full(x), _full(r['mixed'])], p['ln1'], **two)
    r['x1'] = x1
    r['hgu'] = _mm("ffn_in", r['x1_bf'], p['wgu'], rider=riders.get('ffn_in'))
    if 'ffn_in' in riders:
        r['hgu'], carried['ffn_in'] = r['hgu']
    (r['hid'],) = _ew("swiglu", _f_swiglu, [(r['hgu'], 0, FFN_HIDDEN), (r['hgu'], FFN_HIDDEN, FFN_HIDDEN)], [],
                      [FFN_HIDDEN], tl=128, out_dtypes=[BF16])
    r['f'] = _mm("ffn_out", r['hid'], p['wd'], rider=riders.get('ffn_out'))
    if 'ffn_out' in riders:
        r['f'], carried['ffn_out'] = r['f']
    x2, x2_bf = _ew("ln2", _with_copy(_f_ln), [_full(x1), _full(r['f'])], p['ln2'], **two)
    return x2, x2_bf, r, carried


def _layer_bwd(dx2, r, p, riders=None):
    g, riders, carried = {}, riders or {}, {}
    x, z, x1 = r['x'], r['z'], r['x1']
    dx1_n, df, g['ln2_g'], g['ln2_b'] = _ew_bwd("ln2_bwd", _f_ln, [_full(x1), _full(r['f'])], p['ln2'], [dx2],
                                                [True, BF16])
    dhid = _mm("ffn_out_dx", df, p['wd'], tb=True, rider=riders.get('ffn_out_dx'))
    carried['ffn_out_dx'] = []
    if 'ffn_out_dx' in riders:
        dhid, carried['ffn_out_dx'] = dhid
    g['w_ffn_down'] = _mm("ffn_out_dw", r['hid'], df, ta=True, out_dtype=BF16)
    hgu_rows = [(r['hgu'], 0, FFN_HIDDEN), (r['hgu'], FFN_HIDDEN, FFN_HIDDEN)]
    dhg, dhu = _ew_bwd("swiglu_bwd", _f_swiglu, hgu_rows, [], [dhid], [BF16, BF16], tl=128)
    g['w_ffn_gate'] = _mm("ffn_gate_dw", r['x1_bf'], dhg, ta=True, out_dtype=BF16)
    g['w_ffn_up'] = _mm("ffn_up_dw", r['x1_bf'], dhu, ta=True, out_dtype=BF16)
    dx1 = _mm("ffn_gate_dx", dhg, p['wgu'], tb=True, b_w=FFN_HIDDEN, add=dx1_n)
    dx1 = _mm("ffn_up_dx", dhu, p['wgu'], tb=True, b_c0=FFN_HIDDEN, b_w=FFN_HIDDEN, add=dx1)
    dx_n, dmixed, g['ln1_g'], g['ln1_b'] = _ew_bwd("ln1_bwd", _f_ln, [_full(x), _full(r['mixed'])], p['ln1'], [dx1],
                                                   [True, BF16])
    dmix = _mm("w_out_dx", dmixed, p['wout'], tb=True)
    g['w_out'] = _mm("w_out_dw", r['mix'], dmixed, ta=True, out_dtype=BF16)
    gate_rows = [(z, Z_GATE0 + k * D_MODEL, D_MODEL) for k in range(3)]
    mg = _ew_bwd("merge_bwd", _f_merge, [_full(a) for a in r['proj']] + gate_rows, p['b_gate'], [dmix], [BF16] * 6,
                 tl=128)
    dproj, dzg = mg[0:3], mg[3:6]
    g['b_gate'] = jnp.concatenate([b.reshape(-1) for b in mg[6:9]])
    ys = [r['ys5'], r['ylru'], r['yfox']]
    dys = [_mm("proj_%d_dx" % k, dproj[k], p['wb'][k], tb=True, out_dtype=BF16 if k == 2 else F32) for k in range(3)]
    g['w_branch'] = jnp.stack([_mm("proj_%d_dw" % k, ys[k], dproj[k], ta=True, out_dtype=BF16) for k in range(3)])
    qa, ka, kat, v = r['fox']
    do = _heads(dys[2])
    dot = do.transpose(0, 2, 1)
    own = riders['own'](g) if 'own' in riders else {}
    dqt, dcq, carried['fox_bwd_q'] = _fox_bwd_q(qa, ka, kat, v, dot, r['ot'], r['lse'], rider=riders.get('fox_bwd_q'))
    dkh, dvh, dck, carried['fox_bwd_kv'] = _fox_bwd_kv(qa, ka, v, do, dot, r['ot'], r['lse'],
                                                       rider=own.get('fox_bwd_kv'))
    pad_heads = lambda a: jnp.pad(a.T, ((0, 0), (0, FG_PAD - FOX_HEADS)))
    dlogf = _scan_real("fox_cum_bwd", jnp.ones((x.shape[0], FG_PAD), F32), pad_heads(dcq[:, 0, :]),
                       pad_heads(-dck[:, :, 0]), reverse=True)
    dqkv = [dqt.reshape(BRANCH, x.shape[0]).T, _unheads(dkh), _unheads(dvh)]
    dzf, dbf = _ew_bwd("fox_logf_bwd", _f_logf, [(z, Z_FG0, FG_PAD)], [p['b_f']], [dlogf], [BF16])
    g['b_f'] = dbf[0, :FOX_HEADS]
    dgate, dh = _ew_bwd("lru_out_bwd", _f_lru_out, [(z, 2 * BRANCH, BRANCH), _full(r['h'])], [], [dys[1]], [BF16, True])
    db = _scan_real("lru_scan_bwd", _advance(r['a'], 1), dh, reverse=True)
    gates_rows = [_full(r['xc']), (r['papx'], 0, BRANCH), (r['papx'], BRANCH, BRANCH), _full(_delay(r['h'], 1))]
    dxc, dpa, dpx, db_a, db_x, dlam = _ew_bwd("lru_gates_bwd", _f_lru_step, gates_rows, [p['b_a'], p['b_x'], p['lam']],
                                              [db], [True, BF16, BF16, False])
    dxc = _mm("lru_a_dx", dpa, p['wax'], tb=True, b_w=BRANCH, add=dxc)
    dxc = _mm("lru_x_dx", dpx, p['wax'], tb=True, b_c0=BRANCH, b_w=BRANCH, add=dxc)
    take_heads = lambda d: _blockdiag_take(d, LRU_HEADS, LRU_HEAD_DIM, LRU_HEAD_DIM)
    g['lru_w_a'] = take_heads(_mm("lru_a_dw", r['xc_bf'], dpa, ta=True))
    g['lru_w_x'] = take_heads(_mm("lru_x_dw", r['xc_bf'], dpx, ta=True))
    g['lru_b_a'] = db_a.reshape(LRU_HEADS, LRU_HEAD_DIM)
    g['lru_b_x'] = db_x.reshape(LRU_HEADS, LRU_HEAD_DIM)
    g['lru_lambda'] = dlam.reshape(-1)
    conv_rows = [(z, BRANCH, BRANCH)] + [_full(a) for a in r['xd']]
    cw = _ew_bwd("lru_conv_dw", _f_conv, conv_rows, p['conv_w'] + [p['conv_b']], [dxc], [False] * CONV_WIDTH)
    g['lru_conv_w'] = jnp.concatenate([cw[CONV_WIDTH - 1 - k] for k in range(CONV_WIDTH)], axis=0)
    g['lru_conv_b'] = cw[CONV_WIDTH].reshape(-1)
    (dxl,) = _ew("lru_conv_dx", _f_conv_t, [_full(_advance(dxc, j)) for j in range(CONV_WIDTH)], p['conv_w'], [BRANCH],
                 out_dtypes=[BF16])
    dy1, dpre, dbglu = _ew_bwd("s5_glu_bwd", _f_s5_glu, [_full(r['y1']), _full(r['pre'])], [p['bglu']], [dys[0]],
                               [True, BF16])
    g['s5_b_glu'] = dbglu.reshape(-1)
    g['s5_w_glu'] = _mm("s5_glu_dw", r['y1_bf'], dpre, ta=True, out_dtype=BF16)
    dy1 = _mm("s5_glu_dx", dpre, p['wglu'], tb=True, add=dy1)
    dy0, du, dd = _ew_bwd("s5_y1_bwd", _f_s5_y1, [_full(r['hc_re']), _full(r['hc_im']), (z, 0, BRANCH)], [p['s5_d']],
                          [dy1], [BF16, False, True])
    g['s5_d'] = dd.reshape(-1)
    dh_re = _mm("s5_hc_re_dx", dy0, p['s5_cre'], tb=True)
    dh_im = _mm("s5_hc_im_dx", dy0, p['s5_cimn'], tb=True)
    take_c = lambda d: _blockdiag_take(d, S5_GROUPS, S5_STATE, S5_GROUP).transpose(0, 2, 1)
    g['s5_c_re'] = take_c(_mm("s5_hc_re_dw", r['h_re'], dy0, ta=True))
    g['s5_c_im'] = -take_c(_mm("s5_hc_im_dw", r['h_im'], dy0, ta=True))
    scanned = _scan_cplx("s5_scan_bwd", p['lam_re'], -p['lam_im'], dh_re, dh_im, reverse=True, h_re=r['h_re'],
                         h_im=r['h_im'], bn=256, rider=own.get('s5_scan_bwd'))
    gb_re, gb_im, dl_re, dl_im = scanned[:4]
    carried['s5_scan_bwd'] = scanned[4] if 's5_scan_bwd' in own else []
    du = _mm("s5_bu_re_dx", gb_re, p['s5_bre'], tb=True, add=du)
    du = _mm("s5_bu_im_dx", gb_im, p['s5_bim'], tb=True, add=du, out_dtype=BF16)
    take_b = lambda d: _blockdiag_take(d, S5_GROUPS, S5_GROUP, S5_STATE).transpose(0, 2, 1).reshape(S5_N, S5_GROUP)
    dbb_re = take_b(_mm("s5_bu_re_dw", z, gb_re, ta=True, a_w=BRANCH))
    dbb_im = take_b(_mm("s5_bu_im_dw", z, gb_im, ta=True, a_w=BRANCH))
    disc = _ew_bwd("s5_disc_bwd", _f_s5_disc, [_full(a) for a in p['disc_in']], [],
                   [dl_re.reshape(S5_N, 1), dl_im.reshape(S5_N, 1), dbb_re, dbb_im], [True] * 5, tl=S5_N)
    grp = (S5_GROUPS, S5_STATE)
    g['s5_a_re'], g['s5_a_im'] = disc[0].reshape(grp), disc[1].reshape(grp)
    g['s5_log_dt'] = disc[2].reshape(grp).sum(axis=1)
    g['s5_b_re'], g['s5_b_im'] = disc[3].reshape(grp + (S5_GROUP,)), disc[4].reshape(grp + (S5_GROUP,))
    dz = jnp.concatenate([du, dxl, dgate] + dqkv + list(dzg) + [dzf], axis=1)
    dwc = _mm("z_in_dw", r['x_bf'], dz, ta=True, out_dtype=BF16)
    g['w_in'] = jnp.concatenate([dwc[:, :Z_MAIN], dwc[:, Z_FG0:Z_FG0 + FOX_HEADS], dwc[:, Z_GATE0:Z_FG0]], axis=1)
    dx = _mm("z_in_dx", dz, p['wc'], tb=True, add=dx_n)
    return dx, g, carried


def _loss_head(y, target, tl=256):
    length, width = y.shape
    tl = min(tl, length)
    nt = length // tl

    def body(y_ref, t_ref, dy_ref, loss_ref, acc_sc):
        i = pl.program_id(0)

        @pl.when(i == 0)
        def _():
            acc_sc[...] = jnp.zeros_like(acc_sc)

        err = y_ref[...] - t_ref[...]
        dy_ref[...] = err / width
        acc_sc[...] += jnp.sum(jnp.square(err), axis=0, keepdims=True)

        @pl.when(i == nt - 1)
        def _():
            total = jnp.sum(acc_sc[...], axis=1, keepdims=True) * (0.5 / width)
            loss_ref[...] = jnp.broadcast_to(total, loss_ref.shape)

    spec = pl.BlockSpec((tl, width), lambda i: (i, 0))
    dy, loss = pl.pallas_call(
        body, grid=(nt,), in_specs=[spec, spec], out_specs=[spec, pl.BlockSpec((1, LANES), lambda i: (0, 0))],
        out_shape=[jax.ShapeDtypeStruct((length, width), F32), jax.ShapeDtypeStruct((1, LANES), F32)],
        scratch_shapes=[pltpu.VMEM((1, width), F32)], compiler_params=_params("arbitrary"), name="loss_head")(y, target)
    return loss[0, 0], dy


def _sum_parts(name, parts):
    count, rows, cols = parts.shape
    tr = 256

    def body(p_ref, o_ref):
        total = p_ref[0]
        for dev in range(1, count):
            total = total + p_ref[dev]
        o_ref[...] = total

    return pl.pallas_call(body, grid=(rows // tr,), in_specs=[pl.BlockSpec((count, tr, cols), lambda i: (0, i, 0))],
                          out_specs=pl.BlockSpec((tr, cols), lambda i: (i, 0)),
                          out_shape=jax.ShapeDtypeStruct((rows, cols), F32), compiler_params=_params("parallel"),
                          name=name)(parts)


def _adamw(name, w, parts, m, v):
    rows, cols = w.shape
    count = parts[0].shape[0]
    span = rows // len(parts)
    tr = span
    for cand in (256, 128, 64, 32, 16):
        if span % cand == 0:
            tr = cand
            break
    per_span = span // tr

    def body(*refs):
        w_ref, p_refs = refs[0], refs[1:1 + len(parts)]
        m_ref, v_ref, g_ref, d_ref, m2_ref, v2_ref = refs[1 + len(parts):]
        step = pl.program_id(0)
        grad = None
        for j, p_ref in enumerate(p_refs):
            total = p_ref[0].astype(F32)
            for dev in range(1, count):
                total = total + p_ref[dev].astype(F32)
            grad = total if grad is None else jnp.where(step >= j * per_span, total, grad)
        m2 = ADAM_B1 * m_ref[...] + (1.0 - ADAM_B1) * grad
        v2 = ADAM_B2 * v_ref[...] + (1.0 - ADAM_B2) * jnp.square(grad)
        m_hat = m2 / (1.0 - ADAM_B1 ** ADAM_STEP)
        v_hat = v2 / (1.0 - ADAM_B2 ** ADAM_STEP)
        g_ref[...] = grad
        d_ref[...] = -ADAM_LR * (m_hat / (jnp.sqrt(v_hat) + ADAM_EPS) + ADAM_WD * w_ref[...])
        m2_ref[...] = m2
        v2_ref[...] = v2

    spec = pl.BlockSpec((tr, cols), lambda i: (i, 0))
    pspecs = [pl.BlockSpec((count, tr, cols),
                           lambda i, j=j: (0, jnp.minimum(jnp.maximum(i - j * per_span, 0), per_span - 1), 0))
              for j in range(len(parts))]
    shape = jax.ShapeDtypeStruct((rows, cols), F32)
    return pl.pallas_call(body, grid=(rows // tr,), in_specs=[spec] + pspecs + [spec, spec], out_specs=[spec] * 4,
                          out_shape=[shape] * 4, compiler_params=_params("arbitrary"), name=name)(w, *parts, m, v)


class _NoExchange:
    def __init__(self, layers):
        self.layers = layers

    def weights(self, l, carried):
        return self.layers[l]

    def forward_riders(self, l):
        return {}

    def backward_riders(self, l):
        return {}

    def collect(self, l, grads, carried):
        pass


def _forward_backward(x, target, hooks):
    prepared, saved, carried = [], [], None
    x_bf = x.astype(BF16)
    for l in range(DEPTH):
        p = _prep_layer(hooks.weights(l, carried))
        x, x_bf, r, carried = _layer_fwd(x, x_bf, p, hooks.forward_riders(l))
        prepared.append(p)
        saved.append(r)
    loss, dx = _loss_head(x, target)
    grads = [None] * DEPTH
    for l in reversed(range(DEPTH)):
        dx, grads[l], carried = _layer_bwd(dx, saved[l], prepared[l], hooks.backward_riders(l))
        hooks.collect(l, grads[l], carried)
    return loss, dx, grads


def _exchange_copies(ins, outs, sems, scatter, with_arrivals):
    send_sems, recv_sems, local_sems = sems
    x, y, c = lax.axis_index("x"), lax.axis_index("y"), lax.axis_index("c")
    me = 4 * x + 2 * y + c
    local, sends, arrivals = [], [], []
    for a in range(len(ins)):
        local.append(pltpu.make_async_copy(ins[a].at[me] if scatter else ins[a], outs[a].at[me], local_sems.at[a]))
    for k in range(1, N_DEV):
        px = 1 - x if k & 4 else x
        py = 1 - y if k & 2 else y
        pc = 1 - c if k & 1 else c
        idx = 4 * px + 2 * py + pc
        for a in range(len(ins)):
            s = a * (N_DEV - 1) + k - 1
            src = ins[a].at[idx] if scatter else ins[a]
            common = dict(src_ref=src, send_sem=send_sems.at[s], recv_sem=recv_sems.at[s], device_id=(px, py, pc),
                          device_id_type=pl.DeviceIdType.MESH)
            sends.append(pltpu.make_async_remote_copy(dst_ref=outs[a].at[me], **common))
            if with_arrivals:
                arrivals.append(pltpu.make_async_remote_copy(dst_ref=outs[a].at[idx], **common))
    return local, sends, arrivals


def _exchange_start(ins, outs, sems, scatter):
    local, sends, _ = _exchange_copies(ins, outs, sems, scatter, False)
    for cp in local + sends:
        cp.start()


def _exchange_wait(ins, outs, sems, scatter):
    local, sends, arrivals = _exchange_copies(ins, outs, sems, scatter, True)
    for cp in local:
        cp.wait()
    for cp in sends:
        cp.wait_send()
    for cp in arrivals:
        cp.wait_recv()


def _exchange_parts(arrays, scatter):
    n = len(arrays)
    hbm = [pl.BlockSpec(memory_space=pltpu.HBM)] * n
    out_shape = [jax.ShapeDtypeStruct(a.shape if scatter else (N_DEV,) + a.shape, a.dtype) for a in arrays]
    nsem = n * (N_DEV - 1)
    sems = [pltpu.SemaphoreType.DMA((nsem,)), pltpu.SemaphoreType.DMA((nsem,)), pltpu.SemaphoreType.DMA((n,))]
    return hbm, out_shape, sems


def _exchange(name, arrays, scatter):
    n = len(arrays)
    hbm, out_shape, sems = _exchange_parts(arrays, scatter)

    def body(*refs):
        ins, outs, sem_refs = refs[:n], refs[n:2 * n], refs[2 * n:]
        _exchange_start(ins, outs, sem_refs, scatter)
        _exchange_wait(ins, outs, sem_refs, scatter)

    return pl.pallas_call(body, in_specs=hbm, out_specs=hbm, out_shape=out_shape, scratch_shapes=sems,
                          name=name)(*arrays)


def _call_carrying(name, body, grid, ins, in_specs, out_shape, out_specs, scratch, semantics, rider,
                   vmem=VMEM_LIMIT_BYTES):
    if rider is None:
        r_arrays, r_hbm, r_shape, r_sems = [], [], [], []
    else:
        r_arrays, scatter = rider
        r_hbm, r_shape, r_sems = _exchange_parts(r_arrays, scatter)
        semantics = ("arbitrary",) * len(grid)
    n_in, n_out, n_scr, n_r = len(ins), len(out_shape), len(scratch), len(r_arrays)

    def full_body(*refs):
        at = [0]

        def take(count):
            at[0] += count
            return refs[at[0] - count:at[0]]

        in_refs, r_in, out_refs, r_out, scr, r_scr = take(n_in), take(n_r), take(n_out), take(n_r), take(n_scr), take(3)
        ids = [pl.program_id(d) for d in range(len(grid))]
        if rider is not None:
            first = functools.reduce(jnp.logical_and, [i == 0 for i in ids])
            pl.when(first)(functools.partial(_exchange_start, r_in, r_out, r_scr, scatter))
        body(in_refs, out_refs, scr)
        if rider is not None:
            last = functools.reduce(jnp.logical_and, [i == g - 1 for i, g in zip(ids, grid)])
            pl.when(last)(functools.partial(_exchange_wait, r_in, r_out, r_scr, scatter))

    res = pl.pallas_call(
        full_body, grid=grid, in_specs=list(in_specs) + r_hbm, out_specs=list(out_specs) + r_hbm,
        out_shape=list(out_shape) + r_shape, scratch_shapes=list(scratch) + r_sems,
        compiler_params=_params(*semantics, vmem=vmem), name=name if rider is None else name + "_carrying")(
            *ins, *r_arrays)
    return res[:n_out], res[n_out:]


def _shard_2d(a):
    return a.reshape(-1, a.shape[-1])


def _full_layer_weight(name, t):
    if name in ('s5_w_glu', 'w_out', 'w_ffn_down'):
        return t.reshape(-1, t.shape[-1])
    if name == 'w_branch':
        return t.transpose(1, 2, 0, 3).reshape(3, BRANCH, D_MODEL)
    return t.transpose(1, 0, 2).reshape(t.shape[1], -1)


def _split_layer_grad(name, g):
    if name in ('s5_w_glu', 'w_out', 'w_ffn_down'):
        return g.reshape(N_DEV, g.shape[0] // N_DEV, g.shape[1])
    if name == 'w_branch':
        return g.reshape(3, BRANCH, N_DEV, D_MODEL // N_DEV).transpose(2, 0, 1, 3)
    return g.reshape(g.shape[0], N_DEV, g.shape[1] // N_DEV).transpose(1, 0, 2)


RIDING = [n for n in SHARDED if n != 'lru_conv_w']
FORWARD_CARRIERS = {'z_in': ['w_ffn_gate'], 's5_scan': ['w_ffn_down'], 'fox_fwd': ['w_in', 'w_branch'],
                    'ffn_in': ['w_ffn_up'], 'ffn_out': ['s5_w_glu', 'w_out']}
OWN_GRAD_CARRIERS = {'fox_bwd_kv': ['w_ffn_gate', 'w_ffn_up'], 's5_scan_bwd': ['w_ffn_down']}
NEXT_GRAD_CARRIERS = {'ffn_out_dx': ['s5_w_glu', 'w_out'], 'fox_bwd_q': ['w_in', 'w_branch']}
NEXT_GRADS = [n for names in NEXT_GRAD_CARRIERS.values() for n in names]


class _Fsdp:
    def __init__(self, weights):
        self.weights_in = weights
        self.shard = {n: _shard_2d(weights[n]).astype(BF16) for n in RIDING}
        self.rows = {n: self.shard[n].shape[0] // DEPTH for n in RIDING}
        first = _exchange("gather_first_layer", [self.layer_shard(n, 0) for n in RIDING]
                          + [_shard_2d(weights['lru_conv_w'])], scatter=False)
        self.first = first[:-1]
        self.conv = first[-1].reshape((N_DEV,) + weights['lru_conv_w'].shape)
        self.outgoing = None
        self.incoming = {n: [None] * DEPTH for n in RIDING}

    def layer_shard(self, n, l):
        return self.shard[n][l * self.rows[n]:(l + 1) * self.rows[n]]

    def weights(self, l, carried):
        if l == 0:
            got = dict(zip(RIDING, self.first))
        else:
            got = {n: t for c, names in FORWARD_CARRIERS.items() for n, t in zip(names, carried[c])}
        w = {n: self.weights_in[n][l] for n in REPLICATED}
        for n in RIDING:
            w[n] = _full_layer_weight(n, got[n].reshape((N_DEV,) + self.weights_in[n].shape[1:]))
        w['lru_conv_w'] = _full_layer_weight('lru_conv_w', self.conv[:, l])
        return w

    def forward_riders(self, l):
        if l + 1 == DEPTH:
            return {}
        return {c: ([self.layer_shard(n, l + 1) for n in names], False) for c, names in FORWARD_CARRIERS.items()}

    def blocks(self, grads, names):
        return [_split_layer_grad(n, grads[n]).reshape(N_DEV, self.rows[n], -1).astype(BF16) for n in names]

    def backward_riders(self, l):
        riders = {'own': lambda grads: {c: (self.blocks(grads, names), True)
                                        for c, names in OWN_GRAD_CARRIERS.items()}}
        if self.outgoing is not None:
            for c, names in NEXT_GRAD_CARRIERS.items():
                riders[c] = ([self.outgoing[n] for n in names], True)
        return riders

    def collect(self, l, grads, carried):
        for c, names in NEXT_GRAD_CARRIERS.items():
            for n, t in zip(names, carried[c]):
                self.incoming[n][l + 1] = t
        for c, names in OWN_GRAD_CARRIERS.items():
            for n, t in zip(names, carried[c]):
                self.incoming[n][l] = t
        self.outgoing = dict(zip(NEXT_GRADS, self.blocks(grads, NEXT_GRADS)))

    def finish(self, grads):
        conv = jnp.stack([_split_layer_grad('lru_conv_w', grads[l]['lru_conv_w']) for l in range(DEPTH)], axis=1)
        conv = conv.reshape(N_DEV, -1, conv.shape[-1]).astype(F32)
        last = _exchange("scatter_last_layer", [self.outgoing[n] for n in NEXT_GRADS] + [conv], scatter=True)
        for n, t in zip(NEXT_GRADS, last[:-1]):
            self.incoming[n][0] = t
        return {**self.incoming, 'lru_conv_w': [last[-1]]}


def kernel(x, w_in, b_f, b_gate, s5_a_re, s5_a_im, s5_log_dt, s5_b_re, s5_b_im, s5_c_re, s5_c_im, s5_d, s5_w_glu, s5_b_glu, lru_conv_w, lru_conv_b, lru_w_a, lru_b_a, lru_w_x, lru_b_x, lru_lambda, w_branch, w_out, ln1_g, ln1_b, w_ffn_gate, w_ffn_up, w_ffn_down, ln2_g, ln2_b, loss_target, m_w_in, m_b_f, m_b_gate, m_s5_a_re, m_s5_a_im, m_s5_log_dt, m_s5_b_re, m_s5_b_im, m_s5_c_re, m_s5_c_im, m_s5_d, m_s5_w_glu, m_s5_b_glu, m_lru_conv_w, m_lru_conv_b, m_lru_w_a, m_lru_b_a, m_lru_w_x, m_lru_b_x, m_lru_lambda, m_w_branch, m_w_out, m_ln1_g, m_ln1_b, m_w_ffn_gate, m_w_ffn_up, m_w_ffn_down, m_ln2_g, m_ln2_b, v_w_in, v_b_f, v_b_gate, v_s5_a_re, v_s5_a_im, v_s5_log_dt, v_s5_b_re, v_s5_b_im, v_s5_c_re, v_s5_c_im, v_s5_d, v_s5_w_glu, v_s5_b_glu, v_lru_conv_w, v_lru_conv_b, v_lru_w_a, v_lru_b_a, v_lru_w_x, v_lru_b_x, v_lru_lambda, v_w_branch, v_w_out, v_ln1_g, v_ln1_b, v_w_ffn_gate, v_w_ffn_up, v_w_ffn_down, v_ln2_g, v_ln2_b):
    given = dict(locals())
    weights = {n: given[n] for n in WEIGHTS}
    moments_m = {n: given['m_' + n] for n in WEIGHTS}
    moments_v = {n: given['v_' + n] for n in WEIGHTS}

    hooks = _Fsdp(weights)
    loss_local, dx, grads = _forward_backward(x[0], loss_target[0], hooks)
    loss = lax.psum(loss_local, MESH_AXES)
    incoming = hooks.finish(grads)

    new = {}
    for n in SHARDED:
        res = _adamw("adamw_" + n, _shard_2d(weights[n]), incoming[n], _shard_2d(moments_m[n]), _shard_2d(moments_v[n]))
        new[n] = [t.reshape(weights[n].shape) for t in res]

    flat = jnp.concatenate([jnp.stack([grads[l][n] for l in range(DEPTH)]).astype(F32).reshape(-1) for n in REPLICATED])
    rows = -(-flat.shape[0] // (LANES * 256)) * 256
    packed = jnp.pad(flat, (0, rows * LANES - flat.shape[0])).reshape(rows, LANES)
    (arrived,) = _exchange("gather_small_grads", [packed], scatter=False)
    total, at = _sum_parts("sum_small_grads", arrived).reshape(-1), 0
    for n in REPLICATED:
        w2 = _shard_2d(weights[n])
        grad = total[at:at + w2.size].reshape((1,) + w2.shape)
        at += w2.size
        res = _adamw("adamw_" + n, w2, [grad], _shard_2d(moments_m[n]), _shard_2d(moments_v[n]))
        new[n] = [t.reshape(weights[n].shape) for t in res]

    return (loss, dx[None], *[new[n][0] for n in WEIGHTS], *[new[n][1] for n in WEIGHTS],
            *[new[n][2] for n in WEIGHTS], *[new[n][3] for n in WEIGHTS])
```

```python
import functools
import math

import jax
import jax.numpy as jnp
from jax import lax
from jax.experimental import pallas as pl
from jax.experimental.pallas import tpu as pltpu

F32 = jnp.float32
BF16 = jnp.bfloat16

D_MODEL = 1024
DEPTH = 4
BRANCH = 512
S5_GROUPS, S5_GROUP, S5_STATE = 32, 16, 64
S5_N = S5_GROUPS * S5_STATE
LRU_HEADS, LRU_HEAD_DIM = 8, 64
LRU_C = 8.0
CONV_WIDTH = 4
FOX_HEADS, FOX_HEAD_DIM = 8, 64
FFN_HIDDEN = 2816
ALPHA = (2.0 * DEPTH) ** 0.25
LN_EPS = 1e-5
IN_TOTAL = 6 * BRANCH + FOX_HEADS + 3 * D_MODEL
FG_PAD = 128
Z_MAIN = 6 * BRANCH
Z_GATE0 = Z_MAIN
Z_FG0 = Z_MAIN + 3 * D_MODEL
Z_TOTAL = Z_FG0 + FG_PAD
N_DEV = 8
MESH_AXES = ("x", "y", "c")

ADAM_LR, ADAM_B1, ADAM_B2, ADAM_EPS, ADAM_WD, ADAM_STEP = 0.001, 0.9, 0.999, 1e-08, 0.01, 10

VMEM_LIMIT_BYTES = 48 * 1024 * 1024
SCAN_SLACK_BYTES = 6 * 1024 * 1024
SUBLANES = 8
LANES = 128
NEG_BIG = -1e30

WEIGHTS = ['w_in', 'b_f', 'b_gate', 's5_a_re', 's5_a_im', 's5_log_dt', 's5_b_re', 's5_b_im', 's5_c_re', 's5_c_im',
           's5_d', 's5_w_glu', 's5_b_glu', 'lru_conv_w', 'lru_conv_b', 'lru_w_a', 'lru_b_a', 'lru_w_x', 'lru_b_x',
           'lru_lambda', 'w_branch', 'w_out', 'ln1_g', 'ln1_b', 'w_ffn_gate', 'w_ffn_up', 'w_ffn_down', 'ln2_g',
           'ln2_b']
SHARDED = ['w_in', 's5_w_glu', 'lru_conv_w', 'w_branch', 'w_out', 'w_ffn_gate', 'w_ffn_up', 'w_ffn_down']
REPLICATED = [n for n in WEIGHTS if n not in SHARDED]


def _params(*sem, vmem=VMEM_LIMIT_BYTES):
    return pltpu.CompilerParams(dimension_semantics=sem, vmem_limit_bytes=vmem)


def _tile(dim, want):
    if dim % LANES:
        return dim
    t = min(want, dim) // LANES * LANES
    while dim % t:
        t -= LANES
    return t


MM_VMEM_BUDGET_BYTES = 30 * 1024 * 1024
MM_MAX_TILE = 1024


def _divisor_tiles(dim, cap, must_divide=0):
    if dim % LANES:
        return [dim]
    out = [t for t in range(min(cap, dim) // LANES * LANES, 0, -LANES) if dim % t == 0 and must_divide % t == 0]
    return out or [dim]


def _mm_tiles(m, n, k, a_bytes, b_bytes, o_bytes, has_add, m_c0, n_c0, k_c0):
    for tk in _divisor_tiles(k, k, k_c0):
        best = None
        for tm in _divisor_tiles(m, MM_MAX_TILE, m_c0):
            for tn in _divisor_tiles(n, MM_MAX_TILE, n_c0):
                used = 2 * (tm * tk * a_bytes + tk * tn * b_bytes + tm * tn * o_bytes) + tm * tn * 4
                used += tm * tn * 4 if tk < k else 0
                used += 2 * tm * tn * 4 if has_add else 0
                if used <= MM_VMEM_BUDGET_BYTES and (best is None or tm * tn / (tm + tn) > best[0]):
                    best = (tm * tn / (tm + tn), tm, tn)
        if best is not None and (min(best[1], best[2]) >= 256 or tk <= 512):
            return best[1], best[2], tk
    raise ValueError("no matmul tiling fits VMEM")


def _mm(name, a, b, *, ta=False, tb=False, a_c0=0, a_w=None, b_c0=0, b_w=None, add=None, out_dtype=F32, rider=None):
    a_w = a.shape[1] if a_w is None else a_w
    b_w = b.shape[1] if b_w is None else b_w
    m, k = (a_w, a.shape[0]) if ta else (a.shape[0], a_w)
    n = b.shape[0] if tb else b_w
    assert k == (b_w if tb else b.shape[0]), (name, a.shape, b.shape)
    tm, tn, tk = _mm_tiles(m, n, k, a.dtype.itemsize, b.dtype.itemsize, jnp.dtype(out_dtype).itemsize,
                           add is not None, a_c0 if ta else 0, 0 if tb else b_c0,
                           math.gcd(0 if ta else a_c0, b_c0 if tb else 0))
    nk = k // tk
    a_off = a_c0 // (tm if ta else tk)
    b_off = b_c0 // (tk if tb else tn)
    assert a_c0 % (tm if ta else tk) == 0 and b_c0 % (tk if tb else tn) == 0, name
    dims = (((0 if ta else 1,), (1 if tb else 0,)), ((), ()))
    a_total, b_total = m * k * a.dtype.itemsize, n * k * b.dtype.itemsize
    a_stays = a_total + b_total * (m // tm) <= b_total + a_total * (n // tn)
    if nk > 1:
        a_stays = True

    def mn(o, i):
        return (o, i) if a_stays else (i, o)

    def body(in_refs, out_refs, scratch_refs):
        a_ref, b_ref = in_refs[0], in_refs[1]
        add_ref = in_refs[2] if add is not None else None
        (o_ref,) = out_refs
        part = lax.dot_general(a_ref[...].astype(BF16), b_ref[...].astype(BF16), dims, preferred_element_type=F32)

        def finish(r):
            if add is not None:
                r = r + add_ref[...]
            o_ref[...] = r.astype(o_ref.dtype)

        if nk == 1:
            finish(part)
            return
        (acc_ref,) = scratch_refs
        kk = pl.program_id(2)

        @pl.when(kk == 0)
        def _():
            acc_ref[...] = part

        @pl.when(kk > 0)
        def _():
            acc_ref[...] += part

        @pl.when(kk == nk - 1)
        def _():
            finish(acc_ref[...])

    def a_map(o, i, kk):
        im = mn(o, i)[0]
        return (kk, im + a_off) if ta else (im, kk + a_off)

    def b_map(o, i, kk):
        jn = mn(o, i)[1]
        return (jn, kk + b_off) if tb else (kk, jn + b_off)

    a_spec = pl.BlockSpec((tk, tm) if ta else (tm, tk), a_map)
    b_spec = pl.BlockSpec((tn, tk) if tb else (tk, tn), b_map)
    o_spec = pl.BlockSpec((tm, tn), lambda o, i, kk: mn(o, i))
    ins, in_specs = [a, b], [a_spec, b_spec]
    if add is not None:
        ins.append(add)
        in_specs.append(o_spec)
    grid = (m // tm, n // tn, nk) if a_stays else (n // tn, m // tm, nk)
    (out,), carried = _call_carrying(
        name, body, grid, ins, in_specs, [jax.ShapeDtypeStruct((m, n), out_dtype)], [o_spec],
        [pltpu.VMEM((tm, tn), F32)] if nk > 1 else [], ("parallel", "parallel", "arbitrary"), rider)
    return out if rider is None else (out, carried)


def _row_spec(tl, c0, w):
    assert c0 % w == 0
    return pl.BlockSpec((tl, w), lambda i: (i, c0 // w))


def _whole_spec(p):
    return pl.BlockSpec(p.shape, lambda i: (0,) * p.ndim)


def _ew(name, f, rows, prm, out_ws, tl=256, out_dtypes=None):
    out_dtypes = out_dtypes or [F32] * len(out_ws)
    nrows, nprm = len(rows), len(prm)
    length = rows[0][0].shape[0]
    tl = min(tl, length)

    def body(*refs):
        vals = [r[...] for r in refs[:nrows + nprm]]
        outs = f(*vals)
        for o_ref, o in zip(refs[nrows + nprm:], outs):
            o_ref[...] = o.astype(o_ref.dtype)

    return pl.pallas_call(
        body, grid=(length // tl,),
        in_specs=[_row_spec(tl, c0, w) for (_, c0, w) in rows] + [_whole_spec(p) for p in prm],
        out_specs=[_row_spec(tl, 0, w) for w in out_ws],
        out_shape=[jax.ShapeDtypeStruct((length, w), dt) for w, dt in zip(out_ws, out_dtypes)],
        compiler_params=_params("parallel"), name=name)(*[r[0] for r in rows], *prm)


def _ew_bwd(name, f, rows, prm, douts, row_grad, tl=256):
    nrows, nprm, nd = len(rows), len(prm), len(douts)
    length = rows[0][0].shape[0]
    tl = min(tl, length)
    want = [i for i in range(nrows) if row_grad[i]]

    def body(*refs):
        vals = [r[...] for r in refs[:nrows + nprm]]
        cts = tuple(r[...] for r in refs[nrows + nprm:nrows + nprm + nd])
        out_refs = refs[nrows + nprm + nd:]
        _, vjp = jax.vjp(lambda *v: tuple(f(*v)), *vals)
        grads = vjp(cts)
        for o_ref, i in zip(out_refs[:len(want)], want):
            o_ref[...] = grads[i].astype(o_ref.dtype)

        @pl.when(pl.program_id(0) == 0)
        def _():
            for o_ref in out_refs[len(want):]:
                o_ref[...] = jnp.zeros_like(o_ref)

        for o_ref, g in zip(out_refs[len(want):], grads[nrows:]):
            o_ref[...] += g

    return pl.pallas_call(
        body, grid=(length // tl,),
        in_specs=([_row_spec(tl, c0, w) for (_, c0, w) in rows] + [_whole_spec(p) for p in prm]
                  + [_row_spec(tl, 0, d.shape[1]) for d in douts]),
        out_specs=[_row_spec(tl, 0, rows[i][2]) for i in want] + [_whole_spec(p) for p in prm],
        out_shape=([jax.ShapeDtypeStruct((length, rows[i][2]), F32 if row_grad[i] is True else row_grad[i])
                    for i in want]
                   + [jax.ShapeDtypeStruct(p.shape, F32) for p in prm]),
        compiler_params=_params("arbitrary"), name=name)(*[r[0] for r in rows], *prm, *douts)


def _row_ids(width):
    return lax.broadcasted_iota(jnp.int32, (SUBLANES, width), 0)


def _shift_rows(v, d, reverse):
    return pltpu.roll(v, (SUBLANES - d) if reverse else d, 0)


def _scan_real(name, a, b, b2=None, *, reverse=False, bn=256):
    length, n = a.shape
    bn = _tile(n, bn)
    nb = length // SUBLANES
    operands = [a, b] if b2 is None else [a, b, b2]

    def body(*refs):
        a_ref, b_ref, h_ref = refs[0], refs[1], refs[-1]
        rows = _row_ids(bn)

        def step(it, carry):
            i = (nb - 1 - it) if reverse else it
            sl = pl.ds(pl.multiple_of(i * SUBLANES, SUBLANES), SUBLANES)
            av, bv = a_ref[sl, :], b_ref[sl, :]
            if b2 is not None:
                bv = bv + refs[2][sl, :]
            for d in (1, 2, 4):
                live = (rows < SUBLANES - d) if reverse else (rows >= d)
                a_in = jnp.where(live, _shift_rows(av, d, reverse), 1.0)
                b_in = jnp.where(live, _shift_rows(bv, d, reverse), 0.0)
                bv = bv + av * b_in
                av = av * a_in
            hv = bv + av * carry
            h_ref[sl, :] = hv
            edge = hv[0:1, :] if reverse else hv[SUBLANES - 1:SUBLANES, :]
            return jnp.broadcast_to(edge, (SUBLANES, bn))

        lax.fori_loop(0, nb, step, jnp.zeros((SUBLANES, bn), F32))

    spec = pl.BlockSpec((length, bn), lambda j: (0, j))
    return pl.pallas_call(body, grid=(n // bn,), in_specs=[spec] * len(operands), out_specs=spec,
                          out_shape=jax.ShapeDtypeStruct((length, n), F32),
                          compiler_params=_params("parallel"), name=name)(*operands)


def _cmul(ar, ai, br, bi):
    return ar * br - ai * bi, ar * bi + ai * br


def _scan_cplx(name, lam_re, lam_im, x_re, x_im, *, reverse=False, h_re=None, h_im=None, bn=128, rider=None):
    length, n = x_re.shape
    bn = _tile(n, bn)
    nb = length // SUBLANES
    with_dot = h_re is not None

    def body(in_refs, out_refs, scratch_refs):
        refs = tuple(in_refs) + tuple(out_refs)
        if with_dot:
            lr_ref, li_ref, xr_ref, xi_ref, hr_ref, hi_ref, gr_ref, gi_ref, dr_ref, di_ref = refs
        else:
            lr_ref, li_ref, xr_ref, xi_ref, gr_ref, gi_ref = refs
        rows = _row_ids(bn)
        lr = jnp.broadcast_to(lr_ref[...], (SUBLANES, bn))
        li = jnp.broadcast_to(li_ref[...], (SUBLANES, bn))
        powers = [(lr, li)]
        for _ in range(SUBLANES - 1):
            powers.append(_cmul(powers[-1][0], powers[-1][1], lr, li))
        zero = jnp.zeros((SUBLANES, bn), F32)
        steps = []
        for d in (1, 2, 4):
            live = (rows < SUBLANES - d) if reverse else (rows >= d)
            steps.append((d, jnp.where(live, powers[d - 1][0], 0.0), jnp.where(live, powers[d - 1][1], 0.0)))
        cr, ci = zero, zero
        for r in range(SUBLANES):
            e = (SUBLANES - r) if reverse else (r + 1)
            cr = jnp.where(rows == r, powers[e - 1][0], cr)
            ci = jnp.where(rows == r, powers[e - 1][1], ci)

        def step(it, carry):
            i = (nb - 1 - it) if reverse else it
            sl = pl.ds(pl.multiple_of(i * SUBLANES, SUBLANES), SUBLANES)
            vr, vi = xr_ref[sl, :], xi_ref[sl, :]
            for d, pr, pi in steps:
                sr, si = _cmul(pr, pi, _shift_rows(vr, d, reverse), _shift_rows(vi, d, reverse))
                vr, vi = vr + sr, vi + si
            kr, ki = _cmul(cr, ci, carry[0], carry[1])
            vr, vi = vr + kr, vi + ki
            gr_ref[sl, :] = vr
            gi_ref[sl, :] = vi
            er = vr[0:1, :] if reverse else vr[SUBLANES - 1:SUBLANES, :]
            ei = vi[0:1, :] if reverse else vi[SUBLANES - 1:SUBLANES, :]
            new = (jnp.broadcast_to(er, (SUBLANES, bn)), jnp.broadcast_to(ei, (SUBLANES, bn)))
            if not with_dot:
                return new
            prev = pl.ds(pl.multiple_of(jnp.maximum(i - 1, 0) * SUBLANES, SUBLANES), SUBLANES)
            keep = jnp.where(i > 0, 1.0, 0.0)
            pr_ = jnp.broadcast_to(hr_ref[prev, :][SUBLANES - 1:SUBLANES, :], (SUBLANES, bn)) * keep
            pi_ = jnp.broadcast_to(hi_ref[prev, :][SUBLANES - 1:SUBLANES, :], (SUBLANES, bn)) * keep
            hr = jnp.where(rows == 0, pr_, pltpu.roll(hr_ref[sl, :], 1, 0))
            hi = jnp.where(rows == 0, pi_, pltpu.roll(hi_ref[sl, :], 1, 0))
            return new + (carry[2] + vr * hr + vi * hi, carry[3] + vi * hr - vr * hi)

        init = (zero, zero, zero, zero) if with_dot else (zero, zero)
        out = lax.fori_loop(0, nb, step, init)
        if with_dot:
            dr_ref[...] = jnp.sum(out[2], axis=0, keepdims=True)
            di_ref[...] = jnp.sum(out[3], axis=0, keepdims=True)

    col = pl.BlockSpec((length, bn), lambda j: (0, j))
    vec = pl.BlockSpec((1, bn), lambda j: (0, j))
    ins = [lam_re, lam_im, x_re, x_im] + ([h_re, h_im] if with_dot else [])
    in_specs = [vec, vec, col, col] + ([col, col] if with_dot else [])
    out_specs = [col, col] + ([vec, vec] if with_dot else [])
    full = jax.ShapeDtypeStruct((length, n), F32)
    row = jax.ShapeDtypeStruct((1, n), F32)
    out_shape = [full, full] + ([row, row] if with_dot else [])
    columns = 6 if with_dot else 4
    outs, carried = _call_carrying(name, body, (n // bn,), ins, in_specs, out_shape, out_specs, [], ("parallel",), rider,
                                   vmem=2 * columns * length * bn * 4 + SCAN_SLACK_BYTES)
    return list(outs) if rider is None else list(outs) + [carried]


FOX_SCALE = FOX_HEAD_DIM ** -0.5
FOX_AUG = 128
FOX_CQ0 = FOX_HEAD_DIM
FOX_CK0 = FOX_HEAD_DIM + 3
NT = (((1,), (1,)), ((), ()))


def _fox_logits_t(ka, qa, on_diagonal):
    st = lax.dot_general(ka, qa, NT, preferred_element_type=F32)
    if on_diagonal:
        key = lax.broadcasted_iota(jnp.int32, st.shape, 0)
        query = lax.broadcasted_iota(jnp.int32, st.shape, 1)
        st = jnp.where(key <= query, st, NEG_BIG)
    return st


def _fox_pair(s, nt, q_first):
    if q_first:
        qi = sum((s >= (m * (m + 1)) // 2).astype(jnp.int32) for m in range(1, nt))
        return qi, s - ((qi * (qi + 1)) >> 1)
    ki = sum((s >= m * nt - (m * (m - 1)) // 2).astype(jnp.int32) for m in range(1, nt))
    return ki + s - (ki * nt - ((ki * (ki - 1)) >> 1)), ki


FOX_HEADS_PER_STEP = 8


class _HeadView:
    def __init__(self, ref, head):
        self.ref, self.head = ref, head
        self.shape, self.dtype = ref.shape[1:], ref.dtype

    def _index(self, idx):
        idx = idx if isinstance(idx, tuple) else (idx,)
        return (self.head,) + (() if idx == (Ellipsis,) else idx)

    def __getitem__(self, idx):
        return self.ref[self._index(idx)]

    def __setitem__(self, idx, value):
        self.ref[self._index(idx)] = value


def _head_views(in_refs, out_refs, scratch_refs):
    return [tuple([_HeadView(r, h) for r in refs] for refs in (in_refs, out_refs, scratch_refs))
            for h in range(FOX_HEADS_PER_STEP)]


def _fox_specs(t, nt, q_first):
    hb = FOX_HEADS_PER_STEP
    q_idx = lambda s: _fox_pair(s, nt, q_first)[0]
    k_idx = lambda s: _fox_pair(s, nt, q_first)[1]
    rows = lambda idx, w: pl.BlockSpec((hb, t, w), lambda h, s: (h, idx(s), 0))
    cols = lambda idx, w: pl.BlockSpec((hb, w, t), lambda h, s: (h, 0, idx(s)))
    return rows, cols, q_idx, k_idx, (nt * (nt + 1)) // 2


def _fox_fwd(qa, ka, vt, t=512, rider=None):
    heads, length, _ = qa.shape
    dh = vt.shape[1]
    t = min(t, length)
    nt = length // t

    def body(in_refs, out_refs, scratch_refs):
        views = _head_views(in_refs, out_refs, scratch_refs)
        qi, ki = _fox_pair(pl.program_id(1), nt, True)

        @pl.when(ki == 0)
        def _():
            for _, _, (m_sc, l_sc, acc_sc) in views:
                m_sc[...] = jnp.full(m_sc.shape, NEG_BIG, F32)
                l_sc[...] = jnp.zeros(l_sc.shape, F32)
                acc_sc[...] = jnp.zeros(acc_sc.shape, F32)

        def step(on_diagonal):
            for (qa_ref, ka_ref, vt_ref), _, (m_sc, l_sc, acc_sc) in views:
                st = _fox_logits_t(ka_ref[...], qa_ref[...], on_diagonal)
                m_old = m_sc[...]
                m_new = jnp.maximum(m_old, jnp.max(st, axis=0, keepdims=True))
                pt = jnp.exp(st - m_new)
                scale = jnp.exp(m_old - m_new)
                l_sc[...] = scale * l_sc[...] + jnp.sum(pt, axis=0, keepdims=True)
                acc_sc[...] = scale * acc_sc[...] + jnp.dot(vt_ref[...], pt.astype(BF16), preferred_element_type=F32)
                m_sc[...] = m_new

        pl.when(ki < qi)(functools.partial(step, False))
        pl.when(ki == qi)(functools.partial(step, True))

        @pl.when(ki == qi)
        def _():
            for _, (o_ref, lse_ref), (m_sc, l_sc, acc_sc) in views:
                o_ref[...] = acc_sc[...] / l_sc[...]
                lse_ref[...] = m_sc[...] + jnp.log(l_sc[...])

    hb = FOX_HEADS_PER_STEP
    rows, cols, q_idx, k_idx, pairs = _fox_specs(t, nt, True)
    (ot, lse), carried = _call_carrying(
        "fox_fwd", body, (heads // hb, pairs), [qa, ka, vt],
        [rows(q_idx, FOX_AUG), rows(k_idx, FOX_AUG), cols(k_idx, dh)],
        [jax.ShapeDtypeStruct((heads, dh, length), F32), jax.ShapeDtypeStruct((heads, 1, length), F32)],
        [cols(q_idx, dh), cols(q_idx, 1)],
        [pltpu.VMEM((hb, 1, t), F32), pltpu.VMEM((hb, 1, t), F32), pltpu.VMEM((hb, dh, t), F32)],
        ("parallel", "arbitrary"), rider)
    return ot, lse, carried


def _fox_ds_t(qa_ref, ka_ref, v_ref, dot_ref, ot_ref, lse_ref, on_diagonal):
    pt = jnp.exp(_fox_logits_t(ka_ref[...], qa_ref[...], on_diagonal) - lse_ref[...])
    dpt = jnp.dot(v_ref[...], dot_ref[...], preferred_element_type=F32)
    delta = jnp.sum(dot_ref[...].astype(F32) * ot_ref[...], axis=0, keepdims=True)
    return pt, pt * (dpt - delta)


def _fox_bwd_q(qa, ka, kat, v, dot, ot, lse, t=512, rider=None):
    heads, length, _ = qa.shape
    dh = v.shape[2]
    t = min(t, length)
    nt = length // t

    def body(in_refs, out_refs, scratch_refs):
        views = _head_views(in_refs, out_refs, scratch_refs)
        qi, ki = _fox_pair(pl.program_id(1), nt, True)

        @pl.when(ki == 0)
        def _():
            for _, _, (acc_sc,) in views:
                acc_sc[...] = jnp.zeros(acc_sc.shape, F32)

        def step(on_diagonal):
            for (qa_ref, ka_ref, kat_ref, v_ref, dot_ref, ot_ref, lse_ref), _, (acc_sc,) in views:
                _, dst = _fox_ds_t(qa_ref, ka_ref, v_ref, dot_ref, ot_ref, lse_ref, on_diagonal)
                acc_sc[...] += jnp.dot(kat_ref[...], dst.astype(BF16), preferred_element_type=F32)

        pl.when(ki < qi)(functools.partial(step, False))
        pl.when(ki == qi)(functools.partial(step, True))

        @pl.when(ki == qi)
        def _():
            for _, (dq_ref, dc_ref), (acc_sc,) in views:
                dq_ref[...] = (acc_sc[0:dh, :] * FOX_SCALE).astype(dq_ref.dtype)
                dc_ref[...] = acc_sc[FOX_CQ0:FOX_CQ0 + 1, :]

    hb = FOX_HEADS_PER_STEP
    rows, cols, q_idx, k_idx, pairs = _fox_specs(t, nt, True)
    (dqt, dcq), carried = _call_carrying(
        "fox_bwd_q", body, (heads // hb, pairs), [qa, ka, kat, v, dot, ot, lse],
        [rows(q_idx, FOX_AUG), rows(k_idx, FOX_AUG), cols(k_idx, FOX_AUG), rows(k_idx, dh), cols(q_idx, dh),
         cols(q_idx, dh), cols(q_idx, 1)],
        [jax.ShapeDtypeStruct((heads, dh, length), BF16), jax.ShapeDtypeStruct((heads, 1, length), F32)],
        [cols(q_idx, dh), cols(q_idx, 1)],
        [pltpu.VMEM((hb, FOX_AUG, t), F32)], ("parallel", "arbitrary"), rider)
    return dqt, dcq, carried


def _fox_bwd_kv(qa, ka, v, do, dot, ot, lse, t=512, rider=None):
    heads, length, _ = qa.shape
    dh = v.shape[2]
    t = min(t, length)
    nt = length // t

    def body(in_refs, out_refs, scratch_refs):
        views = _head_views(in_refs, out_refs, scratch_refs)
        qi, ki = _fox_pair(pl.program_id(1), nt, False)

        @pl.when(qi == ki)
        def _():
            for _, _, (dka_sc, dv_sc) in views:
                dka_sc[...] = jnp.zeros(dka_sc.shape, F32)
                dv_sc[...] = jnp.zeros(dv_sc.shape, F32)

        def step(on_diagonal):
            for (qa_ref, ka_ref, v_ref, do_ref, dot_ref, ot_ref, lse_ref), _, (dka_sc, dv_sc) in views:
                pt, dst = _fox_ds_t(qa_ref, ka_ref, v_ref, dot_ref, ot_ref, lse_ref, on_diagonal)
                dv_sc[...] += jnp.dot(pt.astype(BF16), do_ref[...], preferred_element_type=F32)
                dka_sc[...] += jnp.dot(dst.astype(BF16), qa_ref[...], preferred_element_type=F32)

        pl.when(qi > ki)(functools.partial(step, False))
        pl.when(qi == ki)(functools.partial(step, True))

        @pl.when(qi == nt - 1)
        def _():
            for _, (dk_ref, dv_ref, dc_ref), (dka_sc, dv_sc) in views:
                dka = dka_sc[...]
                lane = lax.broadcasted_iota(jnp.int32, dka.shape, 1)
                dk_ref[...] = dka_sc[:, :dh].astype(dk_ref.dtype)
                dc_ref[...] = jnp.sum(jnp.where(lane == FOX_CK0, dka, 0.0), axis=1, keepdims=True)
                dv_ref[...] = dv_sc[...].astype(dv_ref.dtype)

    hb = FOX_HEADS_PER_STEP
    rows, cols, q_idx, k_idx, pairs = _fox_specs(t, nt, False)
    big = jax.ShapeDtypeStruct((heads, length, dh), BF16)
    (dk, dv, dc), carried = _call_carrying(
        "fox_bwd_kv", body, (heads // hb, pairs), [qa, ka, v, do, dot, ot, lse],
        [rows(q_idx, FOX_AUG), rows(k_idx, FOX_AUG), rows(k_idx, dh), rows(q_idx, dh), cols(q_idx, dh),
         cols(q_idx, dh), cols(q_idx, 1)],
        [big, big, jax.ShapeDtypeStruct((heads, length, 1), F32)], [rows(k_idx, dh), rows(k_idx, dh), rows(k_idx, 1)],
        [pltpu.VMEM((hb, t, FOX_AUG), F32), pltpu.VMEM((hb, t, dh), F32)], ("parallel", "arbitrary"), rider)
    return dk, dv, dc, carried


def _split3(x):
    hi = lax.reduce_precision(x, 8, 7)
    mid = lax.reduce_precision(x - hi, 8, 7)
    return [hi, mid, lax.reduce_precision(x - hi - mid, 8, 7)]


def _fox_operands(z, cum):
    length = z.shape[0]
    parts = jnp.stack(_split3(cum[:, :FOX_HEADS].T), axis=-1)
    ones = jnp.ones_like(parts)
    pad = jnp.zeros((FOX_HEADS, length, FOX_AUG - FOX_HEAD_DIM - 6), F32)
    qa = jnp.concatenate([_heads(z, 3 * BRANCH) * FOX_SCALE, parts, ones, pad], axis=-1).astype(BF16)
    ka = jnp.concatenate([_heads(z, 4 * BRANCH), ones, -parts, pad], axis=-1).astype(BF16)
    v = _heads(z, 5 * BRANCH).astype(BF16)
    return qa, ka, ka.transpose(0, 2, 1), v, v.transpose(0, 2, 1)


def _softplus(x):
    return jnp.maximum(x, 0.0) + jnp.log1p(jnp.exp(-jnp.abs(x)))


def _f_s5_disc(a_re, a_im, log_dt, b_re, b_im):
    dt = jnp.exp(log_dt)
    mag = jnp.exp(a_re * dt)
    lr, li = mag * jnp.cos(a_im * dt), mag * jnp.sin(a_im * dt)
    den = a_re * a_re + a_im * a_im
    qr = ((lr - 1.0) * a_re + li * a_im) / den
    qi = (li * a_re - (lr - 1.0) * a_im) / den
    return lr, li, qr * b_re - qi * b_im, qr * b_im + qi * b_re


def _f_s5_y1(hc_re, hc_im, u, d):
    return (jax.nn.gelu(hc_re + hc_im + d * u),)


def _f_s5_glu(y1, pre, b):
    return (y1 * jax.nn.sigmoid(pre + b),)


def _f_conv(x0, x1, x2, x3, w0, w1, w2, w3, b):
    return (b + w0 * x0 + w1 * x1 + w2 * x2 + w3 * x3,)


def _f_conv_t(d0, d1, d2, d3, w0, w1, w2, w3):
    return (w0 * d0 + w1 * d1 + w2 * d2 + w3 * d3,)


def _lru_coeffs(xc, pa, px, b_a, b_x, lam):
    r = jax.nn.sigmoid(pa + b_a)
    i = jax.nn.sigmoid(px + b_x)
    log_a = -LRU_C * _softplus(-lam) * r
    a = jnp.exp(log_a)
    mult = jnp.sqrt(-jnp.tanh(log_a) * (a * a + 1.0))
    return a, mult * (i * xc)


def _f_lru_gates(xc, pa, px, b_a, b_x, lam):
    return _lru_coeffs(xc, pa, px, b_a, b_x, lam)


def _f_lru_step(xc, pa, px, h_prev, b_a, b_x, lam):
    a, b = _lru_coeffs(xc, pa, px, b_a, b_x, lam)
    return (a * h_prev + b,)


def _f_lru_out(gate, h):
    return (jax.nn.gelu(gate) * h,)


def _f_logf(zf, bf):
    return (-_softplus(-(zf + bf)),)


def _f_merge(p0, p1, p2, z0, z1, z2, b0, b1, b2):
    return (jax.nn.sigmoid(z0 + b0) * p0 + jax.nn.sigmoid(z1 + b1) * p1 + jax.nn.sigmoid(z2 + b2) * p2,)


def _f_ln(x, r, g, b):
    s = ALPHA * x + r
    mu = jnp.mean(s, axis=-1, keepdims=True)
    var = jnp.mean(jnp.square(s - mu), axis=-1, keepdims=True)
    return ((s - mu) * lax.rsqrt(var + LN_EPS) * g + b,)


def _f_swiglu(hg, hu):
    return (jax.nn.silu(hg) * hu,)


def _full(a):
    return (a, 0, a.shape[1])


def _blockdiag(t):
    g, a, b = t.shape
    eye = jnp.eye(g, dtype=t.dtype)
    return (t[:, :, None, :] * eye[:, None, :, None]).reshape(g * a, g * b)


def _blockdiag_take(d, g, a, b):
    eye = jnp.eye(g, dtype=d.dtype)
    return (d.reshape(g, a, g, b) * eye[:, None, :, None]).sum(axis=2)


def _delay(a, j):
    return a if j == 0 else jnp.pad(a, ((j, 0), (0, 0)))[:a.shape[0]]


def _advance(a, j):
    return a if j == 0 else jnp.pad(a, ((0, j), (0, 0)))[j:]


def _heads(a, c0=0):
    length = a.shape[0]
    return a[:, c0:c0 + BRANCH].reshape(length, FOX_HEADS, FOX_HEAD_DIM).transpose(1, 0, 2)


def _unheads(a):
    return a.transpose(1, 0, 2).reshape(a.shape[1], BRANCH)


def _row(v):
    return v.reshape(1, -1).astype(F32)


def _col(v):
    return v.reshape(-1, 1).astype(F32)


def _prep_layer(w):
    p = {}
    w_in = w['w_in']
    p['wc'] = jnp.concatenate(
        [w_in[:, :Z_MAIN], w_in[:, Z_MAIN + FOX_HEADS:], w_in[:, Z_MAIN:Z_MAIN + FOX_HEADS],
         jnp.zeros((D_MODEL, FG_PAD - FOX_HEADS), w_in.dtype)], axis=1)
    p['b_f'] = jnp.pad(_row(w['b_f']), ((0, 0), (0, FG_PAD - FOX_HEADS)))
    p['b_gate'] = [_row(w['b_gate'][k * D_MODEL:(k + 1) * D_MODEL]) for k in range(3)]
    p['disc_in'] = [_col(w['s5_a_re']), _col(w['s5_a_im']), _col(jnp.repeat(w['s5_log_dt'], S5_STATE)),
                    w['s5_b_re'].reshape(S5_N, S5_GROUP), w['s5_b_im'].reshape(S5_N, S5_GROUP)]
    lam_re, lam_im, bb_re, bb_im = _ew("s5_disc", _f_s5_disc, [_full(a) for a in p['disc_in']], [],
                                       [1, 1, S5_GROUP, S5_GROUP], tl=S5_N)
    p['lam_re'], p['lam_im'] = lam_re.reshape(1, S5_N), lam_im.reshape(1, S5_N)
    to_blk = lambda t: _blockdiag(t.reshape(S5_GROUPS, S5_STATE, S5_GROUP).transpose(0, 2, 1)).astype(BF16)
    p['s5_bre'], p['s5_bim'] = to_blk(bb_re), to_blk(bb_im)
    p['s5_cre'] = _blockdiag(w['s5_c_re'].transpose(0, 2, 1)).astype(BF16)
    p['s5_cimn'] = _blockdiag(-w['s5_c_im'].transpose(0, 2, 1)).astype(BF16)
    p['s5_d'], p['wglu'], p['bglu'] = _row(w['s5_d']), w['s5_w_glu'], _row(w['s5_b_glu'])
    p['conv_w'] = [_row(w['lru_conv_w'][CONV_WIDTH - 1 - j]) for j in range(CONV_WIDTH)]
    p['conv_b'] = _row(w['lru_conv_b'])
    p['wax'] = jnp.concatenate([_blockdiag(w['lru_w_a']), _blockdiag(w['lru_w_x'])], axis=1).astype(BF16)
    p['b_a'], p['b_x'], p['lam'] = _row(w['lru_b_a']), _row(w['lru_b_x']), _row(w['lru_lambda'])
    p['wb'] = [w['w_branch'][k] for k in range(3)]
    p['wout'] = w['w_out']
    p['ln1'] = [_row(w['ln1_g']), _row(w['ln1_b'])]
    p['wgu'] = jnp.concatenate([w['w_ffn_gate'], w['w_ffn_up']], axis=1)
    p['wd'] = w['w_ffn_down']
    p['ln2'] = [_row(w['ln2_g']), _row(w['ln2_b'])]
    return p


def _with_copy(f):
    def g(*args):
        (y,) = f(*args)
        return y, y
    return g


def _layer_fwd(x, x_bf, p, riders=None):
    length = x.shape[0]
    riders, carried = riders or {}, {}
    r = {'x': x, 'x_bf': x_bf}
    z = _mm("z_in", x_bf, p['wc'], rider=riders.get('z_in'))
    if 'z_in' in riders:
        z, carried['z_in'] = z
    r['z'] = z
    bu_re = _mm("s5_bu_re", z, p['s5_bre'], a_w=BRANCH)
    bu_im = _mm("s5_bu_im", z, p['s5_bim'], a_w=BRANCH)
    scanned = _scan_cplx("s5_scan", p['lam_re'], p['lam_im'], bu_re, bu_im, bn=256, rider=riders.get('s5_scan'))
    r['h_re'], r['h_im'] = scanned[0], scanned[1]
    if 's5_scan' in riders:
        carried['s5_scan'] = scanned[2]
    r['hc_re'] = _mm("s5_hc_re", r['h_re'], p['s5_cre'])
    r['hc_im'] = _mm("s5_hc_im", r['h_im'], p['s5_cimn'])
    r['y1'], r['y1_bf'] = _ew("s5_y1", _with_copy(_f_s5_y1), [_full(r['hc_re']), _full(r['hc_im']), (z, 0, BRANCH)],
                              [p['s5_d']], [BRANCH, BRANCH], out_dtypes=[F32, BF16])
    r['pre'] = _mm("s5_glu_pre", r['y1_bf'], p['wglu'])
    (r['ys5'],) = _ew("s5_glu", _f_s5_glu, [_full(r['y1']), _full(r['pre'])], [p['bglu']], [BRANCH], out_dtypes=[BF16])
    xl = z[:, BRANCH:2 * BRANCH]
    r['xd'] = [_delay(xl, j) for j in range(1, CONV_WIDTH)]
    r['xc'], r['xc_bf'] = _ew("lru_conv", _with_copy(_f_conv), [(z, BRANCH, BRANCH)] + [_full(a) for a in r['xd']],
                              p['conv_w'] + [p['conv_b']], [BRANCH, BRANCH], out_dtypes=[F32, BF16])
    r['papx'] = _mm("lru_gate_mm", r['xc_bf'], p['wax'])
    r['a'], b = _ew("lru_gates", _f_lru_gates, [_full(r['xc']), (r['papx'], 0, BRANCH), (r['papx'], BRANCH, BRANCH)],
                    [p['b_a'], p['b_x'], p['lam']], [BRANCH, BRANCH])
    r['h'] = _scan_real("lru_scan", r['a'], b)
    (r['ylru'],) = _ew("lru_out", _f_lru_out, [(z, 2 * BRANCH, BRANCH), _full(r['h'])], [], [BRANCH], out_dtypes=[BF16])
    (logf,) = _ew("fox_logf", _f_logf, [(z, Z_FG0, FG_PAD)], [p['b_f']], [FG_PAD])
    cum = _scan_real("fox_cum", jnp.ones((length, FG_PAD), F32), logf)
    qa, ka, kat, v, vt = _fox_operands(z, cum)
    r['fox'] = (qa, ka, kat, v)
    r['ot'], r['lse'], brought = _fox_fwd(qa, ka, vt, rider=riders.get('fox_fwd'))
    if 'fox_fwd' in riders:
        carried['fox_fwd'] = brought
    r['yfox'] = r['ot'].reshape(BRANCH, length).T.astype(BF16)
    ys = [r['ys5'], r['ylru'], r['yfox']]
    r['proj'] = [_mm("proj_%d" % k, ys[k], p['wb'][k]) for k in range(3)]
    gate_rows = [(z, Z_GATE0 + k * D_MODEL, D_MODEL) for k in range(3)]
    (r['mix'],) = _ew("merge", _f_merge, [_full(a) for a in r['proj']] + gate_rows, p['b_gate'], [D_MODEL], tl=128,
                      out_dtypes=[BF16])
    r['mixed'] = _mm("w_out", r['mix'], p['wout'])
    two = dict(out_ws=[D_MODEL, D_MODEL], out_dtypes=[F32, BF16])
    x1, r['x1_bf'] = _ew("ln1", _with_copy(_f_ln), [_full(x), _full(r['mixed'])], p['ln1'], **two)
    r['x1'] = x1
    r['hgu'] = _mm("ffn_in", r['x1_bf'], p['wgu'], rider=riders.get('ffn_in'))
    if 'ffn_in' in riders:
        r['hgu'], carried['ffn_in'] = r['hgu']
    (r['hid'],) = _ew("swiglu", _f_swiglu, [(r['hgu'], 0, FFN_HIDDEN), (r['hgu'], FFN_HIDDEN, FFN_HIDDEN)], [],
                      [FFN_HIDDEN], tl=128, out_dtypes=[BF16])
    r['f'] = _mm("ffn_out", r['hid'], p['wd'], rider=riders.get('ffn_out'))
    if 'ffn_out' in riders:
        r['f'], carried['ffn_out'] = r['f']
    x2, x2_bf = _ew("ln2", _with_copy(_f_ln), [_full(x1), _full(r['f'])], p['ln2'], **two)
    return x2, x2_bf, r, carried


def _layer_bwd(dx2, r, p, riders=None):
    g, riders, carried = {}, riders or {}, {}
    x, z, x1 = r['x'], r['z'], r['x1']
    dx1_n, df, g['ln2_g'], g['ln2_b'] = _ew_bwd("ln2_bwd", _f_ln, [_full(x1), _full(r['f'])], p['ln2'], [dx2],
                                                [True, BF16])
    dhid = _mm("ffn_out_dx", df, p['wd'], tb=True, rider=riders.get('ffn_out_dx'))
    carried['ffn_out_dx'] = []
    if 'ffn_out_dx' in riders:
        dhid, carried['ffn_out_dx'] = dhid
    g['w_ffn_down'] = _mm("ffn_out_dw", r['hid'], df, ta=True, out_dtype=BF16)
    hgu_rows = [(r['hgu'], 0, FFN_HIDDEN), (r['hgu'], FFN_HIDDEN, FFN_HIDDEN)]
    dhg, dhu = _ew_bwd("swiglu_bwd", _f_swiglu, hgu_rows, [], [dhid], [BF16, BF16], tl=128)
    g['w_ffn_gate'] = _mm("ffn_gate_dw", r['x1_bf'], dhg, ta=True, out_dtype=BF16)
    g['w_ffn_up'] = _mm("ffn_up_dw", r['x1_bf'], dhu, ta=True, out_dtype=BF16)
    dx1 = _mm("ffn_gate_dx", dhg, p['wgu'], tb=True, b_w=FFN_HIDDEN, add=dx1_n)
    dx1 = _mm("ffn_up_dx", dhu, p['wgu'], tb=True, b_c0=FFN_HIDDEN, b_w=FFN_HIDDEN, add=dx1)
    dx_n, dmixed, g['ln1_g'], g['ln1_b'] = _ew_bwd("ln1_bwd", _f_ln, [_full(x), _full(r['mixed'])], p['ln1'], [dx1],
                                                   [True, BF16])
    dmix = _mm("w_out_dx", dmixed, p['wout'], tb=True)
    g['w_out'] = _mm("w_out_dw", r['mix'], dmixed, ta=True, out_dtype=BF16)
    gate_rows = [(z, Z_GATE0 + k * D_MODEL, D_MODEL) for k in range(3)]
    mg = _ew_bwd("merge_bwd", _f_merge, [_full(a) for a in r['proj']] + gate_rows, p['b_gate'], [dmix], [BF16] * 6,
                 tl=128)
    dproj, dzg = mg[0:3], mg[3:6]
    g['b_gate'] = jnp.concatenate([b.reshape(-1) for b in mg[6:9]])
    ys = [r['ys5'], r['ylru'], r['yfox']]
    dys = [_mm("proj_%d_dx" % k, dproj[k], p['wb'][k], tb=True, out_dtype=BF16 if k == 2 else F32) for k in range(3)]
    g['w_branch'] = jnp.stack([_mm("proj_%d_dw" % k, ys[k], dproj[k], ta=True, out_dtype=BF16) for k in range(3)])
    qa, ka, kat, v = r['fox']
    do = _heads(dys[2])
    dot = do.transpose(0, 2, 1)
    own = riders['own'](g) if 'own' in riders else {}
    dqt, dcq, carried['fox_bwd_q'] = _fox_bwd_q(qa, ka, kat, v, dot, r['ot'], r['lse'], rider=riders.get('fox_bwd_q'))
    dkh, dvh, dck, carried['fox_bwd_kv'] = _fox_bwd_kv(qa, ka, v, do, dot, r['ot'], r['lse'],
                                                       rider=own.get('fox_bwd_kv'))
    pad_heads = lambda a: jnp.pad(a.T, ((0, 0), (0, FG_PAD - FOX_HEADS)))
    dlogf = _scan_real("fox_cum_bwd", jnp.ones((x.shape[0], FG_PAD), F32), pad_heads(dcq[:, 0, :]),
                       pad_heads(-dck[:, :, 0]), reverse=True)
    dqkv = [dqt.reshape(BRANCH, x.shape[0]).T, _unheads(dkh), _unheads(dvh)]
    dzf, dbf = _ew_bwd("fox_logf_bwd", _f_logf, [(z, Z_FG0, FG_PAD)], [p['b_f']], [dlogf], [BF16])
    g['b_f'] = dbf[0, :FOX_HEADS]
    dgate, dh = _ew_bwd("lru_out_bwd", _f_lru_out, [(z, 2 * BRANCH, BRANCH), _full(r['h'])], [], [dys[1]], [BF16, True])
    db = _scan_real("lru_scan_bwd", _advance(r['a'], 1), dh, reverse=True)
    gates_rows = [_full(r['xc']), (r['papx'], 0, BRANCH), (r['papx'], BRANCH, BRANCH), _full(_delay(r['h'], 1))]
    dxc, dpa, dpx, db_a, db_x, dlam = _ew_bwd("lru_gates_bwd", _f_lru_step, gates_rows, [p['b_a'], p['b_x'], p['lam']],
                                              [db], [True, BF16, BF16, False])
    dxc = _mm("lru_a_dx", dpa, p['wax'], tb=True, b_w=BRANCH, add=dxc)
    dxc = _mm("lru_x_dx", dpx, p['wax'], tb=True, b_c0=BRANCH, b_w=BRANCH, add=dxc)
    take_heads = lambda d: _blockdiag_take(d, LRU_HEADS, LRU_HEAD_DIM, LRU_HEAD_DIM)
    g['lru_w_a'] = take_heads(_mm("lru_a_dw", r['xc_bf'], dpa, ta=True))
    g['lru_w_x'] = take_heads(_mm("lru_x_dw", r['xc_bf'], dpx, ta=True))
    g['lru_b_a'] = db_a.reshape(LRU_HEADS, LRU_HEAD_DIM)
    g['lru_b_x'] = db_x.reshape(LRU_HEADS, LRU_HEAD_DIM)
    g['lru_lambda'] = dlam.reshape(-1)
    conv_rows = [(z, BRANCH, BRANCH)] + [_full(a) for a in r['xd']]
    cw = _ew_bwd("lru_conv_dw", _f_conv, conv_rows, p['conv_w'] + [p['conv_b']], [dxc], [False] * CONV_WIDTH)
    g['lru_conv_w'] = jnp.concatenate([cw[CONV_WIDTH - 1 - k] for k in range(CONV_WIDTH)], axis=0)
    g['lru_conv_b'] = cw[CONV_WIDTH].reshape(-1)
    (dxl,) = _ew("lru_conv_dx", _f_conv_t, [_full(_advance(dxc, j)) for j in range(CONV_WIDTH)], p['conv_w'], [BRANCH],
                 out_dtypes=[BF16])
    dy1, dpre, dbglu = _ew_bwd("s5_glu_bwd", _f_s5_glu, [_full(r['y1']), _full(r['pre'])], [p['bglu']], [dys[0]],
                               [True, BF16])
    g['s5_b_glu'] = dbglu.reshape(-1)
    g['s5_w_glu'] = _mm("s5_glu_dw", r['y1_bf'], dpre, ta=True, out_dtype=BF16)
    dy1 = _mm("s5_glu_dx", dpre, p['wglu'], tb=True, add=dy1)
    dy0, du, dd = _ew_bwd("s5_y1_bwd", _f_s5_y1, [_full(r['hc_re']), _full(r['hc_im']), (z, 0, BRANCH)], [p['s5_d']],
                          [dy1], [BF16, False, True])
    g['s5_d'] = dd.reshape(-1)
    dh_re = _mm("s5_hc_re_dx", dy0, p['s5_cre'], tb=True)
    dh_im = _mm("s5_hc_im_dx", dy0, p['s5_cimn'], tb=True)
    take_c = lambda d: _blockdiag_take(d, S5_GROUPS, S5_STATE, S5_GROUP).transpose(0, 2, 1)
    g['s5_c_re'] = take_c(_mm("s5_hc_re_dw", r['h_re'], dy0, ta=True))
    g['s5_c_im'] = -take_c(_mm("s5_hc_im_dw", r['h_im'], dy0, ta=True))
    scanned = _scan_cplx("s5_scan_bwd", p['lam_re'], -p['lam_im'], dh_re, dh_im, reverse=True, h_re=r['h_re'],
                         h_im=r['h_im'], bn=256, rider=own.get('s5_scan_bwd'))
    gb_re, gb_im, dl_re, dl_im = scanned[:4]
    carried['s5_scan_bwd'] = scanned[4] if 's5_scan_bwd' in own else []
    du = _mm("s5_bu_re_dx", gb_re, p['s5_bre'], tb=True, add=du)
    du = _mm("s5_bu_im_dx", gb_im, p['s5_bim'], tb=True, add=du, out_dtype=BF16)
    take_b = lambda d: _blockdiag_take(d, S5_GROUPS, S5_GROUP, S5_STATE).transpose(0, 2, 1).reshape(S5_N, S5_GROUP)
    dbb_re = take_b(_mm("s5_bu_re_dw", z, gb_re, ta=True, a_w=BRANCH))
    dbb_im = take_b(_mm("s5_bu_im_dw", z, gb_im, ta=True, a_w=BRANCH))
    disc = _ew_bwd("s5_disc_bwd", _f_s5_disc, [_full(a) for a in p['disc_in']], [],
                   [dl_re.reshape(S5_N, 1), dl_im.reshape(S5_N, 1), dbb_re, dbb_im], [True] * 5, tl=S5_N)
    grp = (S5_GROUPS, S5_STATE)
    g['s5_a_re'], g['s5_a_im'] = disc[0].reshape(grp), disc[1].reshape(grp)
    g['s5_log_dt'] = disc[2].reshape(grp).sum(axis=1)
    g['s5_b_re'], g['s5_b_im'] = disc[3].reshape(grp + (S5_GROUP,)), disc[4].reshape(grp + (S5_GROUP,))
    dz = jnp.concatenate([du, dxl, dgate] + dqkv + list(dzg) + [dzf], axis=1)
    dwc = _mm("z_in_dw", r['x_bf'], dz, ta=True, out_dtype=BF16)
    g['w_in'] = jnp.concatenate([dwc[:, :Z_MAIN], dwc[:, Z_FG0:Z_FG0 + FOX_HEADS], dwc[:, Z_GATE0:Z_FG0]], axis=1)
    dx = _mm("z_in_dx", dz, p['wc'], tb=True, add=dx_n)
    return dx, g, carried


def _loss_head(y, target, tl=256):
    length, width = y.shape
    tl = min(tl, length)
    nt = length // tl

    def body(y_ref, t_ref, dy_ref, loss_ref, acc_sc):
        i = pl.program_id(0)

        @pl.when(i == 0)
        def _():
            acc_sc[...] = jnp.zeros_like(acc_sc)

        err = y_ref[...] - t_ref[...]
        dy_ref[...] = err / width
        acc_sc[...] += jnp.sum(jnp.square(err), axis=0, keepdims=True)

        @pl.when(i == nt - 1)
        def _():
            total = jnp.sum(acc_sc[...], axis=1, keepdims=True) * (0.5 / width)
            loss_ref[...] = jnp.broadcast_to(total, loss_ref.shape)

    spec = pl.BlockSpec((tl, width), lambda i: (i, 0))
    dy, loss = pl.pallas_call(
        body, grid=(nt,), in_specs=[spec, spec], out_specs=[spec, pl.BlockSpec((1, LANES), lambda i: (0, 0))],
        out_shape=[jax.ShapeDtypeStruct((length, width), F32), jax.ShapeDtypeStruct((1, LANES), F32)],
        scratch_shapes=[pltpu.VMEM((1, width), F32)], compiler_params=_params("arbitrary"), name="loss_head")(y, target)
    return loss[0, 0], dy


def _sum_parts(name, parts):
    count, rows, cols = parts.shape
    tr = 256

    def body(p_ref, o_ref):
        total = p_ref[0]
        for dev in range(1, count):
            total = total + p_ref[dev]
        o_ref[...] = total

    return pl.pallas_call(body, grid=(rows // tr,), in_specs=[pl.BlockSpec((count, tr, cols), lambda i: (0, i, 0))],
                          out_specs=pl.BlockSpec((tr, cols), lambda i: (i, 0)),
                          out_shape=jax.ShapeDtypeStruct((rows, cols), F32), compiler_params=_params("parallel"),
                          name=name)(parts)


def _adamw(name, w, parts, m, v):
    rows, cols = w.shape
    count = parts[0].shape[0]
    span = rows // len(parts)
    tr = span
    for cand in (256, 128, 64, 32, 16):
        if span % cand == 0:
            tr = cand
            break
    per_span = span // tr

    def body(*refs):
        w_ref, p_refs = refs[0], refs[1:1 + len(parts)]
        m_ref, v_ref, g_ref, d_ref, m2_ref, v2_ref = refs[1 + len(parts):]
        step = pl.program_id(0)
        grad = None
        for j, p_ref in enumerate(p_refs):
            total = p_ref[0].astype(F32)
            for dev in range(1, count):
                total = total + p_ref[dev].astype(F32)
            grad = total if grad is None else jnp.where(step >= j * per_span, total, grad)
        m2 = ADAM_B1 * m_ref[...] + (1.0 - ADAM_B1) * grad
        v2 = ADAM_B2 * v_ref[...] + (1.0 - ADAM_B2) * jnp.square(grad)
        m_hat = m2 / (1.0 - ADAM_B1 ** ADAM_STEP)
        v_hat = v2 / (1.0 - ADAM_B2 ** ADAM_STEP)
        g_ref[...] = grad
        d_ref[...] = -ADAM_LR * (m_hat / (jnp.sqrt(v_hat) + ADAM_EPS) + ADAM_WD * w_ref[...])
        m2_ref[...] = m2
        v2_ref[...] = v2

    spec = pl.BlockSpec((tr, cols), lambda i: (i, 0))
    pspecs = [pl.BlockSpec((count, tr, cols),
                           lambda i, j=j: (0, jnp.minimum(jnp.maximum(i - j * per_span, 0), per_span - 1), 0))
              for j in range(len(parts))]
    shape = jax.ShapeDtypeStruct((rows, cols), F32)
    return pl.pallas_call(body, grid=(rows // tr,), in_specs=[spec] + pspecs + [spec, spec], out_specs=[spec] * 4,
                          out_shape=[shape] * 4, compiler_params=_params("arbitrary"), name=name)(w, *parts, m, v)


class _NoExchange:
    def __init__(self, layers):
        self.layers = layers

    def weights(self, l, carried):
        return self.layers[l]

    def forward_riders(self, l):
        return {}

    def backward_riders(self, l):
        return {}

    def collect(self, l, grads, carried):
        pass


def _forward_backward(x, target, hooks):
    prepared, saved, carried = [], [], None
    x_bf = x.astype(BF16)
    for l in range(DEPTH):
        p = _prep_layer(hooks.weights(l, carried))
        x, x_bf, r, carried = _layer_fwd(x, x_bf, p, hooks.forward_riders(l))
        prepared.append(p)
        saved.append(r)
    loss, dx = _loss_head(x, target)
    grads = [None] * DEPTH
    for l in reversed(range(DEPTH)):
        dx, grads[l], carried = _layer_bwd(dx, saved[l], prepared[l], hooks.backward_riders(l))
        hooks.collect(l, grads[l], carried)
    return loss, dx, grads


def _exchange_copies(ins, outs, sems, scatter, with_arrivals):
    send_sems, recv_sems, local_sems = sems
    x, y, c = lax.axis_index("x"), lax.axis_index("y"), lax.axis_index("c")
    me = 4 * x + 2 * y + c
    local, sends, arrivals = [], [], []
    for a in range(len(ins)):
        local.append(pltpu.make_async_copy(ins[a].at[me] if scatter else ins[a], outs[a].at[me], local_sems.at[a]))
    for k in range(1, N_DEV):
        px = 1 - x if k & 4 else x
        py = 1 - y if k & 2 else y
        pc = 1 - c if k & 1 else c
        idx = 4 * px + 2 * py + pc
        for a in range(len(ins)):
            s = a * (N_DEV - 1) + k - 1
            src = ins[a].at[idx] if scatter else ins[a]
            common = dict(src_ref=src, send_sem=send_sems.at[s], recv_sem=recv_sems.at[s], device_id=(px, py, pc),
                          device_id_type=pl.DeviceIdType.MESH)
            sends.append(pltpu.make_async_remote_copy(dst_ref=outs[a].at[me], **common))
            if with_arrivals:
                arrivals.append(pltpu.make_async_remote_copy(dst_ref=outs[a].at[idx], **common))
    return local, sends, arrivals


def _exchange_start(ins, outs, sems, scatter):
    local, sends, _ = _exchange_copies(ins, outs, sems, scatter, False)
    for cp in local + sends:
        cp.start()


def _exchange_wait(ins, outs, sems, scatter):
    local, sends, arrivals = _exchange_copies(ins, outs, sems, scatter, True)
    for cp in local:
        cp.wait()
    for cp in sends:
        cp.wait_send()
    for cp in arrivals:
        cp.wait_recv()


def _exchange_parts(arrays, scatter):
    n = len(arrays)
    hbm = [pl.BlockSpec(memory_space=pltpu.HBM)] * n
    out_shape = [jax.ShapeDtypeStruct(a.shape if scatter else (N_DEV,) + a.shape, a.dtype) for a in arrays]
    nsem = n * (N_DEV - 1)
    sems = [pltpu.SemaphoreType.DMA((nsem,)), pltpu.SemaphoreType.DMA((nsem,)), pltpu.SemaphoreType.DMA((n,))]
    return hbm, out_shape, sems


def _exchange(name, arrays, scatter):
    n = len(arrays)
    hbm, out_shape, sems = _exchange_parts(arrays, scatter)

    def body(*refs):
        ins, outs, sem_refs = refs[:n], refs[n:2 * n], refs[2 * n:]
        _exchange_start(ins, outs, sem_refs, scatter)
        _exchange_wait(ins, outs, sem_refs, scatter)

    return pl.pallas_call(body, in_specs=hbm, out_specs=hbm, out_shape=out_shape, scratch_shapes=sems,
                          name=name)(*arrays)


def _call_carrying(name, body, grid, ins, in_specs, out_shape, out_specs, scratch, semantics, rider,
                   vmem=VMEM_LIMIT_BYTES):
    if rider is None:
        r_arrays, r_hbm, r_shape, r_sems = [], [], [], []
    else:
        r_arrays, scatter = rider
        r_hbm, r_shape, r_sems = _exchange_parts(r_arrays, scatter)
        semantics = ("arbitrary",) * len(grid)
    n_in, n_out, n_scr, n_r = len(ins), len(out_shape), len(scratch), len(r_arrays)

    def full_body(*refs):
        at = [0]

        def take(count):
            at[0] += count
            return refs[at[0] - count:at[0]]

        in_refs, r_in, out_refs, r_out, scr, r_scr = take(n_in), take(n_r), take(n_out), take(n_r), take(n_scr), take(3)
        ids = [pl.program_id(d) for d in range(len(grid))]
        if rider is not None:
            first = functools.reduce(jnp.logical_and, [i == 0 for i in ids])
            pl.when(first)(functools.partial(_exchange_start, r_in, r_out, r_scr, scatter))
        body(in_refs, out_refs, scr)
        if rider is not None:
            last = functools.reduce(jnp.logical_and, [i == g - 1 for i, g in zip(ids, grid)])
            pl.when(last)(functools.partial(_exchange_wait, r_in, r_out, r_scr, scatter))

    res = pl.pallas_call(
        full_body, grid=grid, in_specs=list(in_specs) + r_hbm, out_specs=list(out_specs) + r_hbm,
        out_shape=list(out_shape) + r_shape, scratch_shapes=list(scratch) + r_sems,
        compiler_params=_params(*semantics, vmem=vmem), name=name if rider is None else name + "_carrying")(
            *ins, *r_arrays)
    return res[:n_out], res[n_out:]


def _shard_2d(a):
    return a.reshape(-1, a.shape[-1])


def _full_layer_weight(name, t):
    if name in ('s5_w_glu', 'w_out', 'w_ffn_down'):
        return t.reshape(-1, t.shape[-1])
    if name == 'w_branch':
        return t.transpose(1, 2, 0, 3).reshape(3, BRANCH, D_MODEL)
    return t.transpose(1, 0, 2).reshape(t.shape[1], -1)


def _split_layer_grad(name, g):
    if name in ('s5_w_glu', 'w_out', 'w_ffn_down'):
        return g.reshape(N_DEV, g.shape[0] // N_DEV, g.shape[1])
    if name == 'w_branch':
        return g.reshape(3, BRANCH, N_DEV, D_MODEL // N_DEV).transpose(2, 0, 1, 3)
    return g.reshape(g.shape[0], N_DEV, g.shape[1] // N_DEV).transpose(1, 0, 2)


RIDING = [n for n in SHARDED if n != 'lru_conv_w']
FORWARD_CARRIERS = {'z_in': ['w_ffn_gate'], 's5_scan': ['w_ffn_down'], 'fox_fwd': ['w_in', 'w_branch'],
                    'ffn_in': ['w_ffn_up'], 'ffn_out': ['s5_w_glu', 'w_out']}
OWN_GRAD_CARRIERS = {'fox_bwd_kv': ['w_ffn_gate', 'w_ffn_up'], 's5_scan_bwd': ['w_ffn_down']}
NEXT_GRAD_CARRIERS = {'ffn_out_dx': ['s5_w_glu', 'w_out'], 'fox_bwd_q': ['w_in', 'w_branch']}
NEXT_GRADS = [n for names in NEXT_GRAD_CARRIERS.values() for n in names]


class _Fsdp:
    def __init__(self, weights):
        self.weights_in = weights
        self.shard = {n: _shard_2d(weights[n]).astype(BF16) for n in RIDING}
        self.rows = {n: self.shard[n].shape[0] // DEPTH for n in RIDING}
        first = _exchange("gather_first_layer", [self.layer_shard(n, 0) for n in RIDING]
                          + [_shard_2d(weights['lru_conv_w'])], scatter=False)
        self.first = first[:-1]
        self.conv = first[-1].reshape((N_DEV,) + weights['lru_conv_w'].shape)
        self.outgoing = None
        self.incoming = {n: [None] * DEPTH for n in RIDING}

    def layer_shard(self, n, l):
        return self.shard[n][l * self.rows[n]:(l + 1) * self.rows[n]]

    def weights(self, l, carried):
        if l == 0:
            got = dict(zip(RIDING, self.first))
        else:
            got = {n: t for c, names in FORWARD_CARRIERS.items() for n, t in zip(names, carried[c])}
        w = {n: self.weights_in[n][l] for n in REPLICATED}
        for n in RIDING:
            w[n] = _full_layer_weight(n, got[n].reshape((N_DEV,) + self.weights_in[n].shape[1:]))
        w['lru_conv_w'] = _full_layer_weight('lru_conv_w', self.conv[:, l])
        return w

    def forward_riders(self, l):
        if l + 1 == DEPTH:
            return {}
        return {c: ([self.layer_shard(n, l + 1) for n in names], False) for c, names in FORWARD_CARRIERS.items()}

    def blocks(self, grads, names):
        return [_split_layer_grad(n, grads[n]).reshape(N_DEV, self.rows[n], -1).astype(BF16) for n in names]

    def backward_riders(self, l):
        riders = {'own': lambda grads: {c: (self.blocks(grads, names), True)
                                        for c, names in OWN_GRAD_CARRIERS.items()}}
        if self.outgoing is not None:
            for c, names in NEXT_GRAD_CARRIERS.items():
                riders[c] = ([self.outgoing[n] for n in names], True)
        return riders

    def collect(self, l, grads, carried):
        for c, names in NEXT_GRAD_CARRIERS.items():
            for n, t in zip(names, carried[c]):
                self.incoming[n][l + 1] = t
        for c, names in OWN_GRAD_CARRIERS.items():
            for n, t in zip(names, carried[c]):
                self.incoming[n][l] = t
        self.outgoing = dict(zip(NEXT_GRADS, self.blocks(grads, NEXT_GRADS)))

    def finish(self, grads):
        conv = jnp.stack([_split_layer_grad('lru_conv_w', grads[l]['lru_conv_w']) for l in range(DEPTH)], axis=1)
        conv = conv.reshape(N_DEV, -1, conv.shape[-1]).astype(F32)
        last = _exchange("scatter_last_layer", [self.outgoing[n] for n in NEXT_GRADS] + [conv], scatter=True)
        for n, t in zip(NEXT_GRADS, last[:-1]):
            self.incoming[n][0] = t
        return {**self.incoming, 'lru_conv_w': [last[-1]]}


def kernel(x, w_in, b_f, b_gate, s5_a_re, s5_a_im, s5_log_dt, s5_b_re, s5_b_im, s5_c_re, s5_c_im, s5_d, s5_w_glu, s5_b_glu, lru_conv_w, lru_conv_b, lru_w_a, lru_b_a, lru_w_x, lru_b_x, lru_lambda, w_branch, w_out, ln1_g, ln1_b, w_ffn_gate, w_ffn_up, w_ffn_down, ln2_g, ln2_b, loss_target, m_w_in, m_b_f, m_b_gate, m_s5_a_re, m_s5_a_im, m_s5_log_dt, m_s5_b_re, m_s5_b_im, m_s5_c_re, m_s5_c_im, m_s5_d, m_s5_w_glu, m_s5_b_glu, m_lru_conv_w, m_lru_conv_b, m_lru_w_a, m_lru_b_a, m_lru_w_x, m_lru_b_x, m_lru_lambda, m_w_branch, m_w_out, m_ln1_g, m_ln1_b, m_w_ffn_gate, m_w_ffn_up, m_w_ffn_down, m_ln2_g, m_ln2_b, v_w_in, v_b_f, v_b_gate, v_s5_a_re, v_s5_a_im, v_s5_log_dt, v_s5_b_re, v_s5_b_im, v_s5_c_re, v_s5_c_im, v_s5_d, v_s5_w_glu, v_s5_b_glu, v_lru_conv_w, v_lru_conv_b, v_lru_w_a, v_lru_b_a, v_lru_w_x, v_lru_b_x, v_lru_lambda, v_w_branch, v_w_out, v_ln1_g, v_ln1_b, v_w_ffn_gate, v_w_ffn_up, v_w_ffn_down, v_ln2_g, v_ln2_b):
    given = dict(locals())
    weights = {n: given[n] for n in WEIGHTS}
    moments_m = {n: given['m_' + n] for n in WEIGHTS}
    moments_v = {n: given['v_' + n] for n in WEIGHTS}

    hooks = _Fsdp(weights)
    loss_local, dx, grads = _forward_backward(x[0], loss_target[0], hooks)
    loss = lax.psum(loss_local, MESH_AXES)
    incoming = hooks.finish(grads)

    new = {}
    for n in SHARDED:
        res = _adamw("adamw_" + n, _shard_2d(weights[n]), incoming[n], _shard_2d(moments_m[n]), _shard_2d(moments_v[n]))
        new[n] = [t.reshape(weights[n].shape) for t in res]

    flat = jnp.concatenate([jnp.stack([grads[l][n] for l in range(DEPTH)]).astype(F32).reshape(-1) for n in REPLICATED])
    rows = -(-flat.shape[0] // (LANES * 256)) * 256
    packed = jnp.pad(flat, (0, rows * LANES - flat.shape[0])).reshape(rows, LANES)
    (arrived,) = _exchange("gather_small_grads", [packed], scatter=False)
    total, at = _sum_parts("sum_small_grads", arrived).reshape(-1), 0
    for n in REPLICATED:
        w2 = _shard_2d(weights[n])
        grad = total[at:at + w2.size].reshape((1,) + w2.shape)
        at += w2.size
        res = _adamw("adamw_" + n, w2, [grad], _shard_2d(moments_m[n]), _shard_2d(moments_v[n]))
        new[n] = [t.reshape(weights[n].shape) for t in res]

    return (loss, dx[None], *[new[n][0] for n in WEIGHTS], *[new[n][1] for n in WEIGHTS],
            *[new[n][2] for n in WEIGHTS], *[new[n][3] for n in WEIGHTS])
```

```python
import functools
import math

import jax
import jax.numpy as jnp
from jax import lax
from jax.experimental import pallas as pl
from jax.experimental.pallas import tpu as pltpu

F32 = jnp.float32
BF16 = jnp.bfloat16

D_MODEL = 1024
DEPTH = 4
BRANCH = 512
S5_GROUPS, S5_GROUP, S5_STATE = 32, 16, 64
S5_N = S5_GROUPS * S5_STATE
LRU_HEADS, LRU_HEAD_DIM = 8, 64
LRU_C = 8.0
CONV_WIDTH = 4
FOX_HEADS, FOX_HEAD_DIM = 8, 64
FFN_HIDDEN = 2816
ALPHA = (2.0 * DEPTH) ** 0.25
LN_EPS = 1e-5
IN_TOTAL = 6 * BRANCH + FOX_HEADS + 3 * D_MODEL
FG_PAD = 128
Z_MAIN = 6 * BRANCH
Z_GATE0 = Z_MAIN
Z_FG0 = Z_MAIN + 3 * D_MODEL
Z_TOTAL = Z_FG0 + FG_PAD
N_DEV = 8
MESH_AXES = ("x", "y", "c")

ADAM_LR, ADAM_B1, ADAM_B2, ADAM_EPS, ADAM_WD, ADAM_STEP = 0.001, 0.9, 0.999, 1e-08, 0.01, 10

VMEM_LIMIT_BYTES = 48 * 1024 * 1024
SCAN_SLACK_BYTES = 6 * 1024 * 1024
SUBLANES = 8
LANES = 128
NEG_BIG = -1e30

WEIGHTS = ['w_in', 'b_f', 'b_gate', 's5_a_re', 's5_a_im', 's5_log_dt', 's5_b_re', 's5_b_im', 's5_c_re', 's5_c_im',
           's5_d', 's5_w_glu', 's5_b_glu', 'lru_conv_w', 'lru_conv_b', 'lru_w_a', 'lru_b_a', 'lru_w_x', 'lru_b_x',
           'lru_lambda', 'w_branch', 'w_out', 'ln1_g', 'ln1_b', 'w_ffn_gate', 'w_ffn_up', 'w_ffn_down', 'ln2_g',
           'ln2_b']
SHARDED = ['w_in', 's5_w_glu', 'lru_conv_w', 'w_branch', 'w_out', 'w_ffn_gate', 'w_ffn_up', 'w_ffn_down']
REPLICATED = [n for n in WEIGHTS if n not in SHARDED]


def _params(*sem, vmem=VMEM_LIMIT_BYTES):
    return pltpu.CompilerParams(dimension_semantics=sem, vmem_limit_bytes=vmem)


def _tile(dim, want):
    if dim % LANES:
        return dim
    t = min(want, dim) // LANES * LANES
    while dim % t:
        t -= LANES
    return t


MM_VMEM_BUDGET_BYTES = 30 * 1024 * 1024
MM_MAX_TILE = 1024


def _divisor_tiles(dim, cap, must_divide=0):
    if dim % LANES:
        return [dim]
    out = [t for t in range(min(cap, dim) // LANES * LANES, 0, -LANES) if dim % t == 0 and must_divide % t == 0]
    return out or [dim]


def _mm_tiles(m, n, k, a_bytes, b_bytes, o_bytes, has_add, m_c0, n_c0, k_c0):
    for tk in _divisor_tiles(k, k, k_c0):
        best = None
        for tm in _divisor_tiles(m, MM_MAX_TILE, m_c0):
            for tn in _divisor_tiles(n, MM_MAX_TILE, n_c0):
                used = 2 * (tm * tk * a_bytes + tk * tn * b_bytes + tm * tn * o_bytes) + tm * tn * 4
                used += tm * tn * 4 if tk < k else 0
                used += 2 * tm * tn * 4 if has_add else 0
                if used <= MM_VMEM_BUDGET_BYTES and (best is None or tm * tn / (tm + tn) > best[0]):
                    best = (tm * tn / (tm + tn), tm, tn)
        if best is not None and (min(best[1], best[2]) >= 256 or tk <= 512):
            return best[1], best[2], tk
    raise ValueError("no matmul tiling fits VMEM")


def _mm(name, a, b, *, ta=False, tb=False, a_c0=0, a_w=None, b_c0=0, b_w=None, add=None, out_dtype=F32, rider=None):
    a_w = a.shape[1] if a_w is None else a_w
    b_w = b.shape[1] if b_w is None else b_w
    m, k = (a_w, a.shape[0]) if ta else (a.shape[0], a_w)
    n = b.shape[0] if tb else b_w
    assert k == (b_w if tb else b.shape[0]), (name, a.shape, b.shape)
    tm, tn, tk = _mm_tiles(m, n, k, a.dtype.itemsize, b.dtype.itemsize, jnp.dtype(out_dtype).itemsize,
                           add is not None, a_c0 if ta else 0, 0 if tb else b_c0,
                           math.gcd(0 if ta else a_c0, b_c0 if tb else 0))
    nk = k // tk
    a_off = a_c0 // (tm if ta else tk)
    b_off = b_c0 // (tk if tb else tn)
    assert a_c0 % (tm if ta else tk) == 0 and b_c0 % (tk if tb else tn) == 0, name
    dims = (((0 if ta else 1,), (1 if tb else 0,)), ((), ()))
    a_total, b_total = m * k * a.dtype.itemsize, n * k * b.dtype.itemsize
    a_stays = a_total + b_total * (m // tm) <= b_total + a_total * (n // tn)
    if nk > 1:
        a_stays = True

    def mn(o, i):
        return (o, i) if a_stays else (i, o)

    def body(in_refs, out_refs, scratch_refs):
        a_ref, b_ref = in_refs[0], in_refs[1]
        add_ref = in_refs[2] if add is not None else None
        (o_ref,) = out_refs
        part = lax.dot_general(a_ref[...].astype(BF16), b_ref[...].astype(BF16), dims, preferred_element_type=F32)

        def finish(r):
            if add is not None:
                r = r + add_ref[...]
            o_ref[...] = r.astype(o_ref.dtype)

        if nk == 1:
            finish(part)
            return
        (acc_ref,) = scratch_refs
        kk = pl.program_id(2)

        @pl.when(kk == 0)
        def _():
            acc_ref[...] = part

        @pl.when(kk > 0)
        def _():
            acc_ref[...] += part

        @pl.when(kk == nk - 1)
        def _():
            finish(acc_ref[...])

    def a_map(o, i, kk):
        im = mn(o, i)[0]
        return (kk, im + a_off) if ta else (im, kk + a_off)

    def b_map(o, i, kk):
        jn = mn(o, i)[1]
        return (jn, kk + b_off) if tb else (kk, jn + b_off)

    a_spec = pl.BlockSpec((tk, tm) if ta else (tm, tk), a_map)
    b_spec = pl.BlockSpec((tn, tk) if tb else (tk, tn), b_map)
    o_spec = pl.BlockSpec((tm, tn), lambda o, i, kk: mn(o, i))
    ins, in_specs = [a, b], [a_spec, b_spec]
    if add is not None:
        ins.append(add)
        in_specs.append(o_spec)
    grid = (m // tm, n // tn, nk) if a_stays else (n // tn, m // tm, nk)
    (out,), carried = _call_carrying(
        name, body, grid, ins, in_specs, [jax.ShapeDtypeStruct((m, n), out_dtype)], [o_spec],
        [pltpu.VMEM((tm, tn), F32)] if nk > 1 else [], ("parallel", "parallel", "arbitrary"), rider)
    return out if rider is None else (out, carried)


def _row_spec(tl, c0, w):
    assert c0 % w == 0
    return pl.BlockSpec((tl, w), lambda i: (i, c0 // w))


def _whole_spec(p):
    return pl.BlockSpec(p.shape, lambda i: (0,) * p.ndim)


def _ew(name, f, rows, prm, out_ws, tl=256, out_dtypes=None):
    out_dtypes = out_dtypes or [F32] * len(out_ws)
    nrows, nprm = len(rows), len(prm)
    length = rows[0][0].shape[0]
    tl = min(tl, length)

    def body(*refs):
        vals = [r[...] for r in refs[:nrows + nprm]]
        outs = f(*vals)
        for o_ref, o in zip(refs[nrows + nprm:], outs):
            o_ref[...] = o.astype(o_ref.dtype)

    return pl.pallas_call(
        body, grid=(length // tl,),
        in_specs=[_row_spec(tl, c0, w) for (_, c0, w) in rows] + [_whole_spec(p) for p in prm],
        out_specs=[_row_spec(tl, 0, w) for w in out_ws],
        out_shape=[jax.ShapeDtypeStruct((length, w), dt) for w, dt in zip(out_ws, out_dtypes)],
        compiler_params=_params("parallel"), name=name)(*[r[0] for r in rows], *prm)


def _ew_bwd(name, f, rows, prm, douts, row_grad, tl=256):
    nrows, nprm, nd = len(rows), len(prm), len(douts)
    length = rows[0][0].shape[0]
    tl = min(tl, length)
    want = [i for i in range(nrows) if row_grad[i]]

    def body(*refs):
        vals = [r[...] for r in refs[:nrows + nprm]]
        cts = tuple(r[...] for r in refs[nrows + nprm:nrows + nprm + nd])
        out_refs = refs[nrows + nprm + nd:]
        _, vjp = jax.vjp(lambda *v: tuple(f(*v)), *vals)
        grads = vjp(cts)
        for o_ref, i in zip(out_refs[:len(want)], want):
            o_ref[...] = grads[i].astype(o_ref.dtype)

        @pl.when(pl.program_id(0) == 0)
        def _():
            for o_ref in out_refs[len(want):]:
                o_ref[...] = jnp.zeros_like(o_ref)

        for o_ref, g in zip(out_refs[len(want):], grads[nrows:]):
            o_ref[...] += g

    return pl.pallas_call(
        body, grid=(length // tl,),
        in_specs=([_row_spec(tl, c0, w) for (_, c0, w) in rows] + [_whole_spec(p) for p in prm]
                  + [_row_spec(tl, 0, d.shape[1]) for d in douts]),
        out_specs=[_row_spec(tl, 0, rows[i][2]) for i in want] + [_whole_spec(p) for p in prm],
        out_shape=([jax.ShapeDtypeStruct((length, rows[i][2]), F32 if row_grad[i] is True else row_grad[i])
                    for i in want]
                   + [jax.ShapeDtypeStruct(p.shape, F32) for p in prm]),
        compiler_params=_params("arbitrary"), name=name)(*[r[0] for r in rows], *prm, *douts)


def _row_ids(width):
    return lax.broadcasted_iota(jnp.int32, (SUBLANES, width), 0)


def _shift_rows(v, d, reverse):
    return pltpu.roll(v, (SUBLANES - d) if reverse else d, 0)


def _scan_real(name, a, b, b2=None, *, reverse=False, bn=256):
    length, n = a.shape
    bn = _tile(n, bn)
    nb = length // SUBLANES
    operands = [a, b] if b2 is None else [a, b, b2]

    def body(*refs):
        a_ref, b_ref, h_ref = refs[0], refs[1], refs[-1]
        rows = _row_ids(bn)

        def step(it, carry):
            i = (nb - 1 - it) if reverse else it
            sl = pl.ds(pl.multiple_of(i * SUBLANES, SUBLANES), SUBLANES)
            av, bv = a_ref[sl, :], b_ref[sl, :]
            if b2 is not None:
                bv = bv + refs[2][sl, :]
            for d in (1, 2, 4):
                live = (rows < SUBLANES - d) if reverse else (rows >= d)
                a_in = jnp.where(live, _shift_rows(av, d, reverse), 1.0)
                b_in = jnp.where(live, _shift_rows(bv, d, reverse), 0.0)
                bv = bv + av * b_in
                av = av * a_in
            hv = bv + av * carry
            h_ref[sl, :] = hv
            edge = hv[0:1, :] if reverse else hv[SUBLANES - 1:SUBLANES, :]
            return jnp.broadcast_to(edge, (SUBLANES, bn))

        lax.fori_loop(0, nb, step, jnp.zeros((SUBLANES, bn), F32))

    spec = pl.BlockSpec((length, bn), lambda j: (0, j))
    return pl.pallas_call(body, grid=(n // bn,), in_specs=[spec] * len(operands), out_specs=spec,
                          out_shape=jax.ShapeDtypeStruct((length, n), F32),
                          compiler_params=_params("parallel"), name=name)(*operands)


def _cmul(ar, ai, br, bi):
    return ar * br - ai * bi, ar * bi + ai * br


def _scan_cplx(name, lam_re, lam_im, x_re, x_im, *, reverse=False, h_re=None, h_im=None, bn=128, rider=None):
    length, n = x_re.shape
    bn = _tile(n, bn)
    nb = length // SUBLANES
    with_dot = h_re is not None

    def body(in_refs, out_refs, scratch_refs):
        refs = tuple(in_refs) + tuple(out_refs)
        if with_dot:
            lr_ref, li_ref, xr_ref, xi_ref, hr_ref, hi_ref, gr_ref, gi_ref, dr_ref, di_ref = refs
        else:
            lr_ref, li_ref, xr_ref, xi_ref, gr_ref, gi_ref = refs
        rows = _row_ids(bn)
        lr = jnp.broadcast_to(lr_ref[...], (SUBLANES, bn))
        li = jnp.broadcast_to(li_ref[...], (SUBLANES, bn))
        powers = [(lr, li)]
        for _ in range(SUBLANES - 1):
            powers.append(_cmul(powers[-1][0], powers[-1][1], lr, li))
        zero = jnp.zeros((SUBLANES, bn), F32)
        steps = []
        for d in (1, 2, 4):
            live = (rows < SUBLANES - d) if reverse else (rows >= d)
            steps.append((d, jnp.where(live, powers[d - 1][0], 0.0), jnp.where(live, powers[d - 1][1], 0.0)))
        cr, ci = zero, zero
        for r in range(SUBLANES):
            e = (SUBLANES - r) if reverse else (r + 1)
            cr = jnp.where(rows == r, powers[e - 1][0], cr)
            ci = jnp.where(rows == r, powers[e - 1][1], ci)

        def step(it, carry):
            i = (nb - 1 - it) if reverse else it
            sl = pl.ds(pl.multiple_of(i * SUBLANES, SUBLANES), SUBLANES)
            vr, vi = xr_ref[sl, :], xi_ref[sl, :]
            for d, pr, pi in steps:
                sr, si = _cmul(pr, pi, _shift_rows(vr, d, reverse), _shift_rows(vi, d, reverse))
                vr, vi = vr + sr, vi + si
            kr, ki = _cmul(cr, ci, carry[0], carry[1])
            vr, vi = vr + kr, vi + ki
            gr_ref[sl, :] = vr
            gi_ref[sl, :] = vi
            er = vr[0:1, :] if reverse else vr[SUBLANES - 1:SUBLANES, :]
            ei = vi[0:1, :] if reverse else vi[SUBLANES - 1:SUBLANES, :]
            new = (jnp.broadcast_to(er, (SUBLANES, bn)), jnp.broadcast_to(ei, (SUBLANES, bn)))
            if not with_dot:
                return new
            prev = pl.ds(pl.multiple_of(jnp.maximum(i - 1, 0) * SUBLANES, SUBLANES), SUBLANES)
            keep = jnp.where(i > 0, 1.0, 0.0)
            pr_ = jnp.broadcast_to(hr_ref[prev, :][SUBLANES - 1:SUBLANES, :], (SUBLANES, bn)) * keep
            pi_ = jnp.broadcast_to(hi_ref[prev, :][SUBLANES - 1:SUBLANES, :], (SUBLANES, bn)) * keep
            hr = jnp.where(rows == 0, pr_, pltpu.roll(hr_ref[sl, :], 1, 0))
            hi = jnp.where(rows == 0, pi_, pltpu.roll(hi_ref[sl, :], 1, 0))
            return new + (carry[2] + vr * hr + vi * hi, carry[3] + vi * hr - vr * hi)

        init = (zero, zero, zero, zero) if with_dot else (zero, zero)
        out = lax.fori_loop(0, nb, step, init)
        if with_dot:
            dr_ref[...] = jnp.sum(out[2], axis=0, keepdims=True)
            di_ref[...] = jnp.sum(out[3], axis=0, keepdims=True)

    col = pl.BlockSpec((length, bn), lambda j: (0, j))
    vec = pl.BlockSpec((1, bn), lambda j: (0, j))
    ins = [lam_re, lam_im, x_re, x_im] + ([h_re, h_im] if with_dot else [])
    in_specs = [vec, vec, col, col] + ([col, col] if with_dot else [])
    out_specs = [col, col] + ([vec, vec] if with_dot else [])
    full = jax.ShapeDtypeStruct((length, n), F32)
    row = jax.ShapeDtypeStruct((1, n), F32)
    out_shape = [full, full] + ([row, row] if with_dot else [])
    columns = 6 if with_dot else 4
    outs, carried = _call_carrying(name, body, (n // bn,), ins, in_specs, out_shape, out_specs, [], ("parallel",), rider,
                                   vmem=2 * columns * length * bn * 4 + SCAN_SLACK_BYTES)
    return list(outs) if rider is None else list(outs) + [carried]


FOX_SCALE = FOX_HEAD_DIM ** -0.5
FOX_AUG = 128
FOX_CQ0 = FOX_HEAD_DIM
FOX_CK0 = FOX_HEAD_DIM + 3
NT = (((1,), (1,)), ((), ()))


def _fox_logits_t(ka, qa, on_diagonal):
    st = lax.dot_general(ka, qa, NT, preferred_element_type=F32)
    if on_diagonal:
        key = lax.broadcasted_iota(jnp.int32, st.shape, 0)
        query = lax.broadcasted_iota(jnp.int32, st.shape, 1)
        st = jnp.where(key <= query, st, NEG_BIG)
    return st


def _fox_pair(s, nt, q_first):
    if q_first:
        qi = sum((s >= (m * (m + 1)) // 2).astype(jnp.int32) for m in range(1, nt))
        return qi, s - ((qi * (qi + 1)) >> 1)
    ki = sum((s >= m * nt - (m * (m - 1)) // 2).astype(jnp.int32) for m in range(1, nt))
    return ki + s - (ki * nt - ((ki * (ki - 1)) >> 1)), ki


FOX_HEADS_PER_STEP = 8


class _HeadView:
    def __init__(self, ref, head):
        self.ref, self.head = ref, head
        self.shape, self.dtype = ref.shape[1:], ref.dtype

    def _index(self, idx):
        idx = idx if isinstance(idx, tuple) else (idx,)
        return (self.head,) + (() if idx == (Ellipsis,) else idx)

    def __getitem__(self, idx):
        return self.ref[self._index(idx)]

    def __setitem__(self, idx, value):
        self.ref[self._index(idx)] = value


def _head_views(in_refs, out_refs, scratch_refs):
    return [tuple([_HeadView(r, h) for r in refs] for refs in (in_refs, out_refs, scratch_refs))
            for h in range(FOX_HEADS_PER_STEP)]


def _fox_specs(t, nt, q_first):
    hb = FOX_HEADS_PER_STEP
    q_idx = lambda s: _fox_pair(s, nt, q_first)[0]
    k_idx = lambda s: _fox_pair(s, nt, q_first)[1]
    rows = lambda idx, w: pl.BlockSpec((hb, t, w), lambda h, s: (h, idx(s), 0))
    cols = lambda idx, w: pl.BlockSpec((hb, w, t), lambda h, s: (h, 0, idx(s)))
    return rows, cols, q_idx, k_idx, (nt * (nt + 1)) // 2


def _fox_fwd(qa, ka, vt, t=512, rider=None):
    heads, length, _ = qa.shape
    dh = vt.shape[1]
    t = min(t, length)
    nt = length // t

    def body(in_refs, out_refs, scratch_refs):
        views = _head_views(in_refs, out_refs, scratch_refs)
        qi, ki = _fox_pair(pl.program_id(1), nt, True)

        @pl.when(ki == 0)
        def _():
            for _, _, (m_sc, l_sc, acc_sc) in views:
                m_sc[...] = jnp.full(m_sc.shape, NEG_BIG, F32)
                l_sc[...] = jnp.zeros(l_sc.shape, F32)
                acc_sc[...] = jnp.zeros(acc_sc.shape, F32)

        def step(on_diagonal):
            for (qa_ref, ka_ref, vt_ref), _, (m_sc, l_sc, acc_sc) in views:
                st = _fox_logits_t(ka_ref[...], qa_ref[...], on_diagonal)
                m_old = m_sc[...]
                m_new = jnp.maximum(m_old, jnp.max(st, axis=0, keepdims=True))
                pt = jnp.exp(st - m_new)
                scale = jnp.exp(m_old - m_new)
                l_sc[...] = scale * l_sc[...] + jnp.sum(pt, axis=0, keepdims=True)
                acc_sc[...] = scale * acc_sc[...] + jnp.dot(vt_ref[...], pt.astype(BF16), preferred_element_type=F32)
                m_sc[...] = m_new

        pl.when(ki < qi)(functools.partial(step, False))
        pl.when(ki == qi)(functools.partial(step, True))

        @pl.when(ki == qi)
        def _():
            for _, (o_ref, lse_ref), (m_sc, l_sc, acc_sc) in views:
                o_ref[...] = acc_sc[...] / l_sc[...]
                lse_ref[...] = m_sc[...] + jnp.log(l_sc[...])

    hb = FOX_HEADS_PER_STEP
    rows, cols, q_idx, k_idx, pairs = _fox_specs(t, nt, True)
    (ot, lse), carried = _call_carrying(
        "fox_fwd", body, (heads // hb, pairs), [qa, ka, vt],
        [rows(q_idx, FOX_AUG), rows(k_idx, FOX_AUG), cols(k_idx, dh)],
        [jax.ShapeDtypeStruct((heads, dh, length), F32), jax.ShapeDtypeStruct((heads, 1, length), F32)],
        [cols(q_idx, dh), cols(q_idx, 1)],
        [pltpu.VMEM((hb, 1, t), F32), pltpu.VMEM((hb, 1, t), F32), pltpu.VMEM((hb, dh, t), F32)],
        ("parallel", "arbitrary"), rider)
    return ot, lse, carried


def _fox_ds_t(qa_ref, ka_ref, v_ref, dot_ref, ot_ref, lse_ref, on_diagonal):
    pt = jnp.exp(_fox_logits_t(ka_ref[...], qa_ref[...], on_diagonal) - lse_ref[...])
    dpt = jnp.dot(v_ref[...], dot_ref[...], preferred_element_type=F32)
    delta = jnp.sum(dot_ref[...].astype(F32) * ot_ref[...], axis=0, keepdims=True)
    return pt, pt * (dpt - delta)


def _fox_bwd_q(qa, ka, kat, v, dot, ot, lse, t=512, rider=None):
    heads, length, _ = qa.shape
    dh = v.shape[2]
    t = min(t, length)
    nt = length // t

    def body(in_refs, out_refs, scratch_refs):
        views = _head_views(in_refs, out_refs, scratch_refs)
        qi, ki = _fox_pair(pl.program_id(1), nt, True)

        @pl.when(ki == 0)
        def _():
            for _, _, (acc_sc,) in views:
                acc_sc[...] = jnp.zeros(acc_sc.shape, F32)

        def step(on_diagonal):
            for (qa_ref, ka_ref, kat_ref, v_ref, dot_ref, ot_ref, lse_ref), _, (acc_sc,) in views:
                _, dst = _fox_ds_t(qa_ref, ka_ref, v_ref, dot_ref, ot_ref, lse_ref, on_diagonal)
                acc_sc[...] += jnp.dot(kat_ref[...], dst.astype(BF16), preferred_element_type=F32)

        pl.when(ki < qi)(functools.partial(step, False))
        pl.when(ki == qi)(functools.partial(step, True))

        @pl.when(ki == qi)
        def _():
            for _, (dq_ref, dc_ref), (acc_sc,) in views:
                dq_ref[...] = (acc_sc[0:dh, :] * FOX_SCALE).astype(dq_ref.dtype)
                dc_ref[...] = acc_sc[FOX_CQ0:FOX_CQ0 + 1, :]

    hb = FOX_HEADS_PER_STEP
    rows, cols, q_idx, k_idx, pairs = _fox_specs(t, nt, True)
    (dqt, dcq), carried = _call_carrying(
        "fox_bwd_q", body, (heads // hb, pairs), [qa, ka, kat, v, dot, ot, lse],
        [rows(q_idx, FOX_AUG), rows(k_idx, FOX_AUG), cols(k_idx, FOX_AUG), rows(k_idx, dh), cols(q_idx, dh),
         cols(q_idx, dh), cols(q_idx, 1)],
        [jax.ShapeDtypeStruct((heads, dh, length), BF16), jax.ShapeDtypeStruct((heads, 1, length), F32)],
        [cols(q_idx, dh), cols(q_idx, 1)],
        [pltpu.VMEM((hb, FOX_AUG, t), F32)], ("parallel", "arbitrary"), rider)
    return dqt, dcq, carried


def _fox_bwd_kv(qa, ka, v, do, dot, ot, lse, t=512, rider=None):
    heads, length, _ = qa.shape
    dh = v.shape[2]
    t = min(t, length)
    nt = length // t

    def body(in_refs, out_refs, scratch_refs):
        views = _head_views(in_refs, out_refs, scratch_refs)
        qi, ki = _fox_pair(pl.program_id(1), nt, False)

        @pl.when(qi == ki)
        def _():
            for _, _, (dka_sc, dv_sc) in views:
                dka_sc[...] = jnp.zeros(dka_sc.shape, F32)
                dv_sc[...] = jnp.zeros(dv_sc.shape, F32)

        def step(on_diagonal):
            for (qa_ref, ka_ref, v_ref, do_ref, dot_ref, ot_ref, lse_ref), _, (dka_sc, dv_sc) in views:
                pt, dst = _fox_ds_t(qa_ref, ka_ref, v_ref, dot_ref, ot_ref, lse_ref, on_diagonal)
                dv_sc[...] += jnp.dot(pt.astype(BF16), do_ref[...], preferred_element_type=F32)
                dka_sc[...] += jnp.dot(dst.astype(BF16), qa_ref[...], preferred_element_type=F32)

        pl.when(qi > ki)(functools.partial(step, False))
        pl.when(qi == ki)(functools.partial(step, True))

        @pl.when(qi == nt - 1)
        def _():
            for _, (dk_ref, dv_ref, dc_ref), (dka_sc, dv_sc) in views:
                dka = dka_sc[...]
                lane = lax.broadcasted_iota(jnp.int32, dka.shape, 1)
                dk_ref[...] = dka_sc[:, :dh].astype(dk_ref.dtype)
                dc_ref[...] = jnp.sum(jnp.where(lane == FOX_CK0, dka, 0.0), axis=1, keepdims=True)
                dv_ref[...] = dv_sc[...].astype(dv_ref.dtype)

    hb = FOX_HEADS_PER_STEP
    rows, cols, q_idx, k_idx, pairs = _fox_specs(t, nt, False)
    big = jax.ShapeDtypeStruct((heads, length, dh), BF16)
    (dk, dv, dc), carried = _call_carrying(
        "fox_bwd_kv", body, (heads // hb, pairs), [qa, ka, v, do, dot, ot, lse],
        [rows(q_idx, FOX_AUG), rows(k_idx, FOX_AUG), rows(k_idx, dh), rows(q_idx, dh), cols(q_idx, dh),
         cols(q_idx, dh), cols(q_idx, 1)],
        [big, big, jax.ShapeDtypeStruct((heads, length, 1), F32)], [rows(k_idx, dh), rows(k_idx, dh), rows(k_idx, 1)],
        [pltpu.VMEM((hb, t, FOX_AUG), F32), pltpu.VMEM((hb, t, dh), F32)], ("parallel", "arbitrary"), rider)
    return dk, dv, dc, carried


def _split3(x):
    hi = lax.reduce_precision(x, 8, 7)
    mid = lax.reduce_precision(x - hi, 8, 7)
    return [hi, mid, lax.reduce_precision(x - hi - mid, 8, 7)]


def _fox_operands(z, cum):
    length = z.shape[0]
    parts = jnp.stack(_split3(cum[:, :FOX_HEADS].T), axis=-1)
    ones = jnp.ones_like(parts)
    pad = jnp.zeros((FOX_HEADS, length, FOX_AUG - FOX_HEAD_DIM - 6), F32)
    qa = jnp.concatenate([_heads(z, 3 * BRANCH) * FOX_SCALE, parts, ones, pad], axis=-1).astype(BF16)
    ka = jnp.concatenate([_heads(z, 4 * BRANCH), ones, -parts, pad], axis=-1).astype(BF16)
    v = _heads(z, 5 * BRANCH).astype(BF16)
    return qa, ka, ka.transpose(0, 2, 1), v, v.transpose(0, 2, 1)


def _softplus(x):
    return jnp.maximum(x, 0.0) + jnp.log1p(jnp.exp(-jnp.abs(x)))


def _f_s5_disc(a_re, a_im, log_dt, b_re, b_im):
    dt = jnp.exp(log_dt)
    mag = jnp.exp(a_re * dt)
    lr, li = mag * jnp.cos(a_im * dt), mag * jnp.sin(a_im * dt)
    den = a_re * a_re + a_im * a_im
    qr = ((lr - 1.0) * a_re + li * a_im) / den
    qi = (li * a_re - (lr - 1.0) * a_im) / den
    return lr, li, qr * b_re - qi * b_im, qr * b_im + qi * b_re


def _f_s5_y1(hc_re, hc_im, u, d):
    return (jax.nn.gelu(hc_re + hc_im + d * u),)


def _f_s5_glu(y1, pre, b):
    return (y1 * jax.nn.sigmoid(pre + b),)


def _f_conv(x0, x1, x2, x3, w0, w1, w2, w3, b):
    return (b + w0 * x0 + w1 * x1 + w2 * x2 + w3 * x3,)


def _f_conv_t(d0, d1, d2, d3, w0, w1, w2, w3):
    return (w0 * d0 + w1 * d1 + w2 * d2 + w3 * d3,)


def _lru_coeffs(xc, pa, px, b_a, b_x, lam):
    r = jax.nn.sigmoid(pa + b_a)
    i = jax.nn.sigmoid(px + b_x)
    log_a = -LRU_C * _softplus(-lam) * r
    a = jnp.exp(log_a)
    mult = jnp.sqrt(-jnp.tanh(log_a) * (a * a + 1.0))
    return a, mult * (i * xc)


def _f_lru_gates(xc, pa, px, b_a, b_x, lam):
    return _lru_coeffs(xc, pa, px, b_a, b_x, lam)


def _f_lru_step(xc, pa, px, h_prev, b_a, b_x, lam):
    a, b = _lru_coeffs(xc, pa, px, b_a, b_x, lam)
    return (a * h_prev + b,)


def _f_lru_out(gate, h):
    return (jax.nn.gelu(gate) * h,)


def _f_logf(zf, bf):
    return (-_softplus(-(zf + bf)),)


def _f_merge(p0, p1, p2, z0, z1, z2, b0, b1, b2):
    return (jax.nn.sigmoid(z0 + b0) * p0 + jax.nn.sigmoid(z1 + b1) * p1 + jax.nn.sigmoid(z2 + b2) * p2,)


def _f_ln(x, r, g, b):
    s = ALPHA * x + r
    mu = jnp.mean(s, axis=-1, keepdims=True)
    var = jnp.mean(jnp.square(s - mu), axis=-1, keepdims=True)
    return ((s - mu) * lax.rsqrt(var + LN_EPS) * g + b,)


def _f_swiglu(hg, hu):
    return (jax.nn.silu(hg) * hu,)


def _full(a):
    return (a, 0, a.shape[1])


def _blockdiag(t):
    g, a, b = t.shape
    eye = jnp.eye(g, dtype=t.dtype)
    return (t[:, :, None, :] * eye[:, None, :, None]).reshape(g * a, g * b)


def _blockdiag_take(d, g, a, b):
    eye = jnp.eye(g, dtype=d.dtype)
    return (d.reshape(g, a, g, b) * eye[:, None, :, None]).sum(axis=2)


def _delay(a, j):
    return a if j == 0 else jnp.pad(a, ((j, 0), (0, 0)))[:a.shape[0]]


def _advance(a, j):
    return a if j == 0 else jnp.pad(a, ((0, j), (0, 0)))[j:]


def _heads(a, c0=0):
    length = a.shape[0]
    return a[:, c0:c0 + BRANCH].reshape(length, FOX_HEADS, FOX_HEAD_DIM).transpose(1, 0, 2)


def _unheads(a):
    return a.transpose(1, 0, 2).reshape(a.shape[1], BRANCH)


def _row(v):
    return v.reshape(1, -1).astype(F32)


def _col(v):
    return v.reshape(-1, 1).astype(F32)


def _prep_layer(w):
    p = {}
    w_in = w['w_in']
    p['wc'] = jnp.concatenate(
        [w_in[:, :Z_MAIN], w_in[:, Z_MAIN + FOX_HEADS:], w_in[:, Z_MAIN:Z_MAIN + FOX_HEADS],
         jnp.zeros((D_MODEL, FG_PAD - FOX_HEADS), w_in.dtype)], axis=1)
    p['b_f'] = jnp.pad(_row(w['b_f']), ((0, 0), (0, FG_PAD - FOX_HEADS)))
    p['b_gate'] = [_row(w['b_gate'][k * D_MODEL:(k + 1) * D_MODEL]) for k in range(3)]
    p['disc_in'] = [_col(w['s5_a_re']), _col(w['s5_a_im']), _col(jnp.repeat(w['s5_log_dt'], S5_STATE)),
                    w['s5_b_re'].reshape(S5_N, S5_GROUP), w['s5_b_im'].reshape(S5_N, S5_GROUP)]
    lam_re, lam_im, bb_re, bb_im = _ew("s5_disc", _f_s5_disc, [_full(a) for a in p['disc_in']], [],
                                       [1, 1, S5_GROUP, S5_GROUP], tl=S5_N)
    p['lam_re'], p['lam_im'] = lam_re.reshape(1, S5_N), lam_im.reshape(1, S5_N)
    to_blk = lambda t: _blockdiag(t.reshape(S5_GROUPS, S5_STATE, S5_GROUP).transpose(0, 2, 1)).astype(BF16)
    p['s5_bre'], p['s5_bim'] = to_blk(bb_re), to_blk(bb_im)
    p['s5_cre'] = _blockdiag(w['s5_c_re'].transpose(0, 2, 1)).astype(BF16)
    p['s5_cimn'] = _blockdiag(-w['s5_c_im'].transpose(0, 2, 1)).astype(BF16)
    p['s5_d'], p['wglu'], p['bglu'] = _row(w['s5_d']), w['s5_w_glu'], _row(w['s5_b_glu'])
    p['conv_w'] = [_row(w['lru_conv_w'][CONV_WIDTH - 1 - j]) for j in range(CONV_WIDTH)]
    p['conv_b'] = _row(w['lru_conv_b'])
    p['wax'] = jnp.concatenate([_blockdiag(w['lru_w_a']), _blockdiag(w['lru_w_x'])], axis=1).astype(BF16)
    p['b_a'], p['b_x'], p['lam'] = _row(w['lru_b_a']), _row(w['lru_b_x']), _row(w['lru_lambda'])
    p['wb'] = [w['w_branch'][k] for k in range(3)]
    p['wout'] = w['w_out']
    p['ln1'] = [_row(w['ln1_g']), _row(w['ln1_b'])]
    p['wgu'] = jnp.concatenate([w['w_ffn_gate'], w['w_ffn_up']], axis=1)
    p['wd'] = w['w_ffn_down']
    p['ln2'] = [_row(w['ln2_g']), _row(w['ln2_b'])]
    return p


def _with_copy(f):
    def g(*args):
        (y,) = f(*args)
        return y, y
    return g


def _layer_fwd(x, x_bf, p, riders=None):
    length = x.shape[0]
    riders, carried = riders or {}, {}
    r = {'x': x, 'x_bf': x_bf}
    z = _mm("z_in", x_bf, p['wc'], rider=riders.get('z_in'))
    if 'z_in' in riders:
        z, carried['z_in'] = z
    r['z'] = z
    bu_re = _mm("s5_bu_re", z, p['s5_bre'], a_w=BRANCH)
    bu_im = _mm("s5_bu_im", z, p['s5_bim'], a_w=BRANCH)
    scanned = _scan_cplx("s5_scan", p['lam_re'], p['lam_im'], bu_re, bu_im, bn=256, rider=riders.get('s5_scan'))
    r['h_re'], r['h_im'] = scanned[0], scanned[1]
    if 's5_scan' in riders:
        carried['s5_scan'] = scanned[2]
    r['hc_re'] = _mm("s5_hc_re", r['h_re'], p['s5_cre'])
    r['hc_im'] = _mm("s5_hc_im", r['h_im'], p['s5_cimn'])
    r['y1'], r['y1_bf'] = _ew("s5_y1", _with_copy(_f_s5_y1), [_full(r['hc_re']), _full(r['hc_im']), (z, 0, BRANCH)],
                              [p['s5_d']], [BRANCH, BRANCH], out_dtypes=[F32, BF16])
    r['pre'] = _mm("s5_glu_pre", r['y1_bf'], p['wglu'])
    (r['ys5'],) = _ew("s5_glu", _f_s5_glu, [_full(r['y1']), _full(r['pre'])], [p['bglu']], [BRANCH], out_dtypes=[BF16])
    xl = z[:, BRANCH:2 * BRANCH]
    r['xd'] = [_delay(xl, j) for j in range(1, CONV_WIDTH)]
    r['xc'], r['xc_bf'] = _ew("lru_conv", _with_copy(_f_conv), [(z, BRANCH, BRANCH)] + [_full(a) for a in r['xd']],
                              p['conv_w'] + [p['conv_b']], [BRANCH, BRANCH], out_dtypes=[F32, BF16])
    r['papx'] = _mm("lru_gate_mm", r['xc_bf'], p['wax'])
    r['a'], b = _ew("lru_gates", _f_lru_gates, [_full(r['xc']), (r['papx'], 0, BRANCH), (r['papx'], BRANCH, BRANCH)],
                    [p['b_a'], p['b_x'], p['lam']], [BRANCH, BRANCH])
    r['h'] = _scan_real("lru_scan", r['a'], b)
    (r['ylru'],) = _ew("lru_out", _f_lru_out, [(z, 2 * BRANCH, BRANCH), _full(r['h'])], [], [BRANCH], out_dtypes=[BF16])
    (logf,) = _ew("fox_logf", _f_logf, [(z, Z_FG0, FG_PAD)], [p['b_f']], [FG_PAD])
    cum = _scan_real("fox_cum", jnp.ones((length, FG_PAD), F32), logf)
    qa, ka, kat, v, vt = _fox_operands(z, cum)
    r['fox'] = (qa, ka, kat, v)
    r['ot'], r['lse'], brought = _fox_fwd(qa, ka, vt, rider=riders.get('fox_fwd'))
    if 'fox_fwd' in riders:
        carried['fox_fwd'] = brought
    r['yfox'] = r['ot'].reshape(BRANCH, length).T.astype(BF16)
    ys = [r['ys5'], r['ylru'], r['yfox']]
    r['proj'] = [_mm("proj_%d" % k, ys[k], p['wb'][k]) for k in range(3)]
    gate_rows = [(z, Z_GATE0 + k * D_MODEL, D_MODEL) for k in range(3)]
    (r['mix'],) = _ew("merge", _f_merge, [_full(a) for a in r['proj']] + gate_rows, p['b_gate'], [D_MODEL], tl=128,
                      out_dtypes=[BF16])
    r['mixed'] = _mm("w_out", r['mix'], p['wout'])
    two = dict(out_ws=[D_MODEL, D_MODEL], out_dtypes=[F32, BF16])
    x1, r['x1_bf'] = _ew("ln1", _with_copy(_f_ln), [_full(x), _full(r['mixed'])], p['ln1'], **two)
    r['x1'] = x1
    r['hgu'] = _mm("ffn_in", r['x1_bf'], p['wgu'], rider=riders.get('ffn_in'))
    if 'ffn_in' in riders:
        r['hgu'], carried['ffn_in'] = r['hgu']
    (r['hid'],) = _ew("swiglu", _f_swiglu, [(r['hgu'], 0, FFN_HIDDEN), (r['hgu'], FFN_HIDDEN, FFN_HIDDEN)], [],
                      [FFN_HIDDEN], tl=128, out_dtypes=[BF16])
    r['f'] = _mm("ffn_out", r['hid'], p['wd'], rider=riders.get('ffn_out'))
    if 'ffn_out' in riders:
        r['f'], carried['ffn_out'] = r['f']
    x2, x2_bf = _ew("ln2", _with_copy(_f_ln), [_full(x1), _full(r['f'])], p['ln2'], **two)
    return x2, x2_bf, r, carried


def _layer_bwd(dx2, r, p, riders=None):
    g, riders, carried = {}, riders or {}, {}
    x, z, x1 = r['x'], r['z'], r['x1']
    dx1_n, df, g['ln2_g'], g['ln2_b'] = _ew_bwd("ln2_bwd", _f_ln, [_full(x1), _full(r['f'])], p['ln2'], [dx2],
                                                [True, BF16])
    dhid = _mm("ffn_out_dx", df, p['wd'], tb=True, rider=riders.get('ffn_out_dx'))
    carried['ffn_out_dx'] = []
    if 'ffn_out_dx' in riders:
        dhid, carried['ffn_out_dx'] = dhid
    g['w_ffn_down'] = _mm("ffn_out_dw", r['hid'], df, ta=True, out_dtype=BF16)
    hgu_rows = [(r['hgu'], 0, FFN_HIDDEN), (r['hgu'], FFN_HIDDEN, FFN_HIDDEN)]
    dhg, dhu = _ew_bwd("swiglu_bwd", _f_swiglu, hgu_rows, [], [dhid], [BF16, BF16], tl=128)
    g['w_ffn_gate'] = _mm("ffn_gate_dw", r['x1_bf'], dhg, ta=True, out_dtype=BF16)
    g['w_ffn_up'] = _mm("ffn_up_dw", r['x1_bf'], dhu, ta=True, out_dtype=BF16)
    dx1 = _mm("ffn_gate_dx", dhg, p['wgu'], tb=True, b_w=FFN_HIDDEN, add=dx1_n)
    dx1 = _mm("ffn_up_dx", dhu, p['wgu'], tb=True, b_c0=FFN_HIDDEN, b_w=FFN_HIDDEN, add=dx1)
    dx_n, dmixed, g['ln1_g'], g['ln1_b'] = _ew_bwd("ln1_bwd", _f_ln, [_full(x), _full(r['mixed'])], p['ln1'], [dx1],
                                                   [True, BF16])
    dmix = _mm("w_out_dx", dmixed, p['wout'], tb=True)
    g['w_out'] = _mm("w_out_dw", r['mix'], dmixed, ta=True, out_dtype=BF16)
    gate_rows = [(z, Z_GATE0 + k * D_MODEL, D_MODEL) for k in range(3)]
    mg = _ew_bwd("merge_bwd", _f_merge, [_full(a) for a in r['proj']] + gate_rows, p['b_gate'], [dmix], [BF16] * 6,
                 tl=128)
    dproj, dzg = mg[0:3], mg[3:6]
    g['b_gate'] = jnp.concatenate([b.reshape(-1) for b in mg[6:9]])
    ys = [r['ys5'], r['ylru'], r['yfox']]
    dys = [_mm("proj_%d_dx" % k, dproj[k], p['wb'][k], tb=True, out_dtype=BF16 if k == 2 else F32) for k in range(3)]
    g['w_branch'] = jnp.stack([_mm("proj_%d_dw" % k, ys[k], dproj[k], ta=True, out_dtype=BF16) for k in range(3)])
    qa, ka, kat, v = r['fox']
    do = _heads(dys[2])
    dot = do.transpose(0, 2, 1)
    own = riders['own'](g) if 'own' in riders else {}
    dqt, dcq, carried['fox_bwd_q'] = _fox_bwd_q(qa, ka, kat, v, dot, r['ot'], r['lse'], rider=riders.get('fox_bwd_q'))
    dkh, dvh, dck, carried['fox_bwd_kv'] = _fox_bwd_kv(qa, ka, v, do, dot, r['ot'], r['lse'],
                                                       rider=own.get('fox_bwd_kv'))
    pad_heads = lambda a: jnp.pad(a.T, ((0, 0), (0, FG_PAD - FOX_HEADS)))
    dlogf = _scan_real("fox_cum_bwd", jnp.ones((x.shape[0], FG_PAD), F32), pad_heads(dcq[:, 0, :]),
                       pad_heads(-dck[:, :, 0]), reverse=True)
    dqkv = [dqt.reshape(BRANCH, x.shape[0]).T, _unheads(dkh), _unheads(dvh)]
    dzf, dbf = _ew_bwd("fox_logf_bwd", _f_logf, [(z, Z_FG0, FG_PAD)], [p['b_f']], [dlogf], [BF16])
    g['b_f'] = dbf[0, :FOX_HEADS]
    dgate, dh = _ew_bwd("lru_out_bwd", _f_lru_out, [(z, 2 * BRANCH, BRANCH), _full(r['h'])], [], [dys[1]], [BF16, True])
    db = _scan_real("lru_scan_bwd", _advance(r['a'], 1), dh, reverse=True)
    gates_rows = [_full(r['xc']), (r['papx'], 0, BRANCH), (r['papx'], BRANCH, BRANCH), _full(_delay(r['h'], 1))]
    dxc, dpa, dpx, db_a, db_x, dlam = _ew_bwd("lru_gates_bwd", _f_lru_step, gates_rows, [p['b_a'], p['b_x'], p['lam']],
                                              [db], [True, BF16, BF16, False])
    dxc = _mm("lru_a_dx", dpa, p['wax'], tb=True, b_w=BRANCH, add=dxc)
    dxc = _mm("lru_x_dx", dpx, p['wax'], tb=True, b_c0=BRANCH, b_w=BRANCH, add=dxc)
    take_heads = lambda d: _blockdiag_take(d, LRU_HEADS, LRU_HEAD_DIM, LRU_HEAD_DIM)
    g['lru_w_a'] = take_heads(_mm("lru_a_dw", r['xc_bf'], dpa, ta=True))
    g['lru_w_x'] = take_heads(_mm("lru_x_dw", r['xc_bf'], dpx, ta=True))
    g['lru_b_a'] = db_a.reshape(LRU_HEADS, LRU_HEAD_DIM)
    g['lru_b_x'] = db_x.reshape(LRU_HEADS, LRU_HEAD_DIM)
    g['lru_lambda'] = dlam.reshape(-1)
    conv_rows = [(z, BRANCH, BRANCH)] + [_full(a) for a in r['xd']]
    cw = _ew_bwd("lru_conv_dw", _f_conv, conv_rows, p['conv_w'] + [p['conv_b']], [dxc], [False] * CONV_WIDTH)
    g['lru_conv_w'] = jnp.concatenate([cw[CONV_WIDTH - 1 - k] for k in range(CONV_WIDTH)], axis=0)
    g['lru_conv_b'] = cw[CONV_WIDTH].reshape(-1)
    (dxl,) = _ew("lru_conv_dx", _f_conv_t, [_full(_advance(dxc, j)) for j in range(CONV_WIDTH)], p['conv_w'], [BRANCH],
                 out_dtypes=[BF16])
    dy1, dpre, dbglu = _ew_bwd("s5_glu_bwd", _f_s5_glu, [_full(r['y1']), _full(r['pre'])], [p['bglu']], [dys[0]],
                               [True, BF16])
    g['s5_b_glu'] = dbglu.reshape(-1)
    g['s5_w_glu'] = _mm("s5_glu_dw", r['y1_bf'], dpre, ta=True, out_dtype=BF16)
    dy1 = _mm("s5_glu_dx", dpre, p['wglu'], tb=True, add=dy1)
    dy0, du, dd = _ew_bwd("s5_y1_bwd", _f_s5_y1, [_full(r['hc_re']), _full(r['hc_im']), (z, 0, BRANCH)], [p['s5_d']],
                          [dy1], [BF16, False, True])
    g['s5_d'] = dd.reshape(-1)
    dh_re = _mm("s5_hc_re_dx", dy0, p['s5_cre'], tb=True)
    dh_im = _mm("s5_hc_im_dx", dy0, p['s5_cimn'], tb=True)
    take_c = lambda d: _blockdiag_take(d, S5_GROUPS, S5_STATE, S5_GROUP).transpose(0, 2, 1)
    g['s5_c_re'] = take_c(_mm("s5_hc_re_dw", r['h_re'], dy0, ta=True))
    g['s5_c_im'] = -take_c(_mm("s5_hc_im_dw", r['h_im'], dy0, ta=True))
    scanned = _scan_cplx("s5_scan_bwd", p['lam_re'], -p['lam_im'], dh_re, dh_im, reverse=True, h_re=r['h_re'],
                         h_im=r['h_im'], bn=256, rider=own.get('s5_scan_bwd'))
    gb_re, gb_im, dl_re, dl_im = scanned[:4]
    carried['s5_scan_bwd'] = scanned[4] if 's5_scan_bwd' in own else []
    du = _mm("s5_bu_re_dx", gb_re, p['s5_bre'], tb=True, add=du)
    du = _mm("s5_bu_im_dx", gb_im, p['s5_bim'], tb=True, add=du, out_dtype=BF16)
    take_b = lambda d: _blockdiag_take(d, S5_GROUPS, S5_GROUP, S5_STATE).transpose(0, 2, 1).reshape(S5_N, S5_GROUP)
    dbb_re = take_b(_mm("s5_bu_re_dw", z, gb_re, ta=True, a_w=BRANCH))
    dbb_im = take_b(_mm("s5_bu_im_dw", z, gb_im, ta=True, a_w=BRANCH))
    disc = _ew_bwd("s5_disc_bwd", _f_s5_disc, [_full(a) for a in p['disc_in']], [],
                   [dl_re.reshape(S5_N, 1), dl_im.reshape(S5_N, 1), dbb_re, dbb_im], [True] * 5, tl=S5_N)
    grp = (S5_GROUPS, S5_STATE)
    g['s5_a_re'], g['s5_a_im'] = disc[0].reshape(grp), disc[1].reshape(grp)
    g['s5_log_dt'] = disc[2].reshape(grp).sum(axis=1)
    g['s5_b_re'], g['s5_b_im'] = disc[3].reshape(grp + (S5_GROUP,)), disc[4].reshape(grp + (S5_GROUP,))
    dz = jnp.concatenate([du, dxl, dgate] + dqkv + list(dzg) + [dzf], axis=1)
    dwc = _mm("z_in_dw", r['x_bf'], dz, ta=True, out_dtype=BF16)
    g['w_in'] = jnp.concatenate([dwc[:, :Z_MAIN], dwc[:, Z_FG0:Z_FG0 + FOX_HEADS], dwc[:, Z_GATE0:Z_FG0]], axis=1)
    dx = _mm("z_in_dx", dz, p['wc'], tb=True, add=dx_n)
    return dx, g, carried


def _loss_head(y, target, tl=256):
    length, width = y.shape
    tl = min(tl, length)
    nt = length // tl

    def body(y_ref, t_ref, dy_ref, loss_ref, acc_sc):
        i = pl.program_id(0)

        @pl.when(i == 0)
        def _():
            acc_sc[...] = jnp.zeros_like(acc_sc)

        err = y_ref[...] - t_ref[...]
        dy_ref[...] = err / width
        acc_sc[...] += jnp.sum(jnp.square(err), axis=0, keepdims=True)

        @pl.when(i == nt - 1)
        def _():
            total = jnp.sum(acc_sc[...], axis=1, keepdims=True) * (0.5 / width)
            loss_ref[...] = jnp.broadcast_to(total, loss_ref.shape)

    spec = pl.BlockSpec((tl, width), lambda i: (i, 0))
    dy, loss = pl.pallas_call(
        body, grid=(nt,), in_specs=[spec, spec], out_specs=[spec, pl.BlockSpec((1, LANES), lambda i: (0, 0))],
        out_shape=[jax.ShapeDtypeStruct((length, width), F32), jax.ShapeDtypeStruct((1, LANES), F32)],
        scratch_shapes=[pltpu.VMEM((1, width), F32)], compiler_params=_params("arbitrary"), name="loss_head")(y, target)
    return loss[0, 0], dy


def _sum_parts(name, parts):
    count, rows, cols = parts.shape
    tr = 256

    def body(p_ref, o_ref):
        total = p_ref[0].astype(F32)
        for dev in range(1, count):
            total = total + p_ref[dev].astype(F32)
        o_ref[...] = total

    return pl.pallas_call(body, grid=(rows // tr,), in_specs=[pl.BlockSpec((count, tr, cols), lambda i: (0, i, 0))],
                          out_specs=pl.BlockSpec((tr, cols), lambda i: (i, 0)),
                          out_shape=jax.ShapeDtypeStruct((rows, cols), F32), compiler_params=_params("parallel"),
                          name=name)(parts)


def _adamw(name, w, parts, m, v):
    rows, cols = w.shape
    count = parts[0].shape[0]
    span = rows // len(parts)
    tr = span
    for cand in (256, 128, 64, 32, 16):
        if span % cand == 0:
            tr = cand
            break
    per_span = span // tr

    def body(*refs):
        w_ref, p_refs = refs[0], refs[1:1 + len(parts)]
        m_ref, v_ref, g_ref, d_ref, m2_ref, v2_ref = refs[1 + len(parts):]
        step = pl.program_id(0)
        grad = None
        for j, p_ref in enumerate(p_refs):
            total = p_ref[0].astype(F32)
            for dev in range(1, count):
                total = total + p_ref[dev].astype(F32)
            grad = total if grad is None else jnp.where(step >= j * per_span, total, grad)
        m2 = ADAM_B1 * m_ref[...] + (1.0 - ADAM_B1) * grad
        v2 = ADAM_B2 * v_ref[...] + (1.0 - ADAM_B2) * jnp.square(grad)
        m_hat = m2 / (1.0 - ADAM_B1 ** ADAM_STEP)
        v_hat = v2 / (1.0 - ADAM_B2 ** ADAM_STEP)
        g_ref[...] = grad
        d_ref[...] = -ADAM_LR * (m_hat / (jnp.sqrt(v_hat) + ADAM_EPS) + ADAM_WD * w_ref[...])
        m2_ref[...] = m2
        v2_ref[...] = v2

    spec = pl.BlockSpec((tr, cols), lambda i: (i, 0))
    pspecs = [pl.BlockSpec((count, tr, cols),
                           lambda i, j=j: (0, jnp.minimum(jnp.maximum(i - j * per_span, 0), per_span - 1), 0))
              for j in range(len(parts))]
    shape = jax.ShapeDtypeStruct((rows, cols), F32)
    return pl.pallas_call(body, grid=(rows // tr,), in_specs=[spec] + pspecs + [spec, spec], out_specs=[spec] * 4,
                          out_shape=[shape] * 4, compiler_params=_params("arbitrary"), name=name)(w, *parts, m, v)


class _NoExchange:
    def __init__(self, layers):
        self.layers = layers

    def weights(self, l, carried):
        return self.layers[l]

    def forward_riders(self, l):
        return {}

    def backward_riders(self, l):
        return {}

    def collect(self, l, grads, carried):
        pass


def _forward_backward(x, target, hooks):
    prepared, saved, carried = [], [], None
    x_bf = x.astype(BF16)
    for l in range(DEPTH):
        p = _prep_layer(hooks.weights(l, carried))
        x, x_bf, r, carried = _layer_fwd(x, x_bf, p, hooks.forward_riders(l))
        prepared.append(p)
        saved.append(r)
    loss, dx = _loss_head(x, target)
    grads = [None] * DEPTH
    for l in reversed(range(DEPTH)):
        dx, grads[l], carried = _layer_bwd(dx, saved[l], prepared[l], hooks.backward_riders(l))
        hooks.collect(l, grads[l], carried)
    return loss, dx, grads


def _exchange_copies(ins, outs, sems, scatter, with_arrivals):
    send_sems, recv_sems, local_sems = sems
    x, y, c = lax.axis_index("x"), lax.axis_index("y"), lax.axis_index("c")
    me = 4 * x + 2 * y + c
    local, sends, arrivals = [], [], []
    for a in range(len(ins)):
        local.append(pltpu.make_async_copy(ins[a].at[me] if scatter else ins[a], outs[a].at[me], local_sems.at[a]))
    for k in range(1, N_DEV):
        px = 1 - x if k & 4 else x
        py = 1 - y if k & 2 else y
        pc = 1 - c if k & 1 else c
        idx = 4 * px + 2 * py + pc
        for a in range(len(ins)):
            s = a * (N_DEV - 1) + k - 1
            src = ins[a].at[idx] if scatter else ins[a]
            common = dict(src_ref=src, send_sem=send_sems.at[s], recv_sem=recv_sems.at[s], device_id=(px, py, pc),
                          device_id_type=pl.DeviceIdType.MESH)
            sends.append(pltpu.make_async_remote_copy(dst_ref=outs[a].at[me], **common))
            if with_arrivals:
                arrivals.append(pltpu.make_async_remote_copy(dst_ref=outs[a].at[idx], **common))
    return local, sends, arrivals


def _exchange_start(ins, outs, sems, scatter):
    local, sends, _ = _exchange_copies(ins, outs, sems, scatter, False)
    for cp in local + sends:
        cp.start()


def _exchange_wait(ins, outs, sems, scatter):
    local, sends, arrivals = _exchange_copies(ins, outs, sems, scatter, True)
    for cp in local:
        cp.wait()
    for cp in sends:
        cp.wait_send()
    for cp in arrivals:
        cp.wait_recv()


def _exchange_parts(arrays, scatter):
    n = len(arrays)
    hbm = [pl.BlockSpec(memory_space=pltpu.HBM)] * n
    out_shape = [jax.ShapeDtypeStruct(a.shape if scatter else (N_DEV,) + a.shape, a.dtype) for a in arrays]
    nsem = n * (N_DEV - 1)
    sems = [pltpu.SemaphoreType.DMA((nsem,)), pltpu.SemaphoreType.DMA((nsem,)), pltpu.SemaphoreType.DMA((n,))]
    return hbm, out_shape, sems


def _exchange(name, arrays, scatter):
    n = len(arrays)
    hbm, out_shape, sems = _exchange_parts(arrays, scatter)

    def body(*refs):
        ins, outs, sem_refs = refs[:n], refs[n:2 * n], refs[2 * n:]
        _exchange_start(ins, outs, sem_refs, scatter)
        _exchange_wait(ins, outs, sem_refs, scatter)

    return pl.pallas_call(body, in_specs=hbm, out_specs=hbm, out_shape=out_shape, scratch_shapes=sems,
                          name=name)(*arrays)


def _call_carrying(name, body, grid, ins, in_specs, out_shape, out_specs, scratch, semantics, rider,
                   vmem=VMEM_LIMIT_BYTES):
    if rider is None:
        r_arrays, r_hbm, r_shape, r_sems = [], [], [], []
    else:
        r_arrays, scatter = rider
        r_hbm, r_shape, r_sems = _exchange_parts(r_arrays, scatter)
        semantics = ("arbitrary",) * len(grid)
    n_in, n_out, n_scr, n_r = len(ins), len(out_shape), len(scratch), len(r_arrays)

    def full_body(*refs):
        at = [0]

        def take(count):
            at[0] += count
            return refs[at[0] - count:at[0]]

        in_refs, r_in, out_refs, r_out, scr, r_scr = take(n_in), take(n_r), take(n_out), take(n_r), take(n_scr), take(3)
        ids = [pl.program_id(d) for d in range(len(grid))]
        if rider is not None:
            first = functools.reduce(jnp.logical_and, [i == 0 for i in ids])
            pl.when(first)(functools.partial(_exchange_start, r_in, r_out, r_scr, scatter))
        body(in_refs, out_refs, scr)
        if rider is not None:
            last = functools.reduce(jnp.logical_and, [i == g - 1 for i, g in zip(ids, grid)])
            pl.when(last)(functools.partial(_exchange_wait, r_in, r_out, r_scr, scatter))

    res = pl.pallas_call(
        full_body, grid=grid, in_specs=list(in_specs) + r_hbm, out_specs=list(out_specs) + r_hbm,
        out_shape=list(out_shape) + r_shape, scratch_shapes=list(scratch) + r_sems,
        compiler_params=_params(*semantics, vmem=vmem), name=name if rider is None else name + "_carrying")(
            *ins, *r_arrays)
    return res[:n_out], res[n_out:]


def _shard_2d(a):
    return a.reshape(-1, a.shape[-1])


def _full_layer_weight(name, t):
    if name in ('s5_w_glu', 'w_out', 'w_ffn_down'):
        return t.reshape(-1, t.shape[-1])
    if name == 'w_branch':
        return t.transpose(1, 2, 0, 3).reshape(3, BRANCH, D_MODEL)
    return t.transpose(1, 0, 2).reshape(t.shape[1], -1)


def _split_layer_grad(name, g):
    if name in ('s5_w_glu', 'w_out', 'w_ffn_down'):
        return g.reshape(N_DEV, g.shape[0] // N_DEV, g.shape[1])
    if name == 'w_branch':
        return g.reshape(3, BRANCH, N_DEV, D_MODEL // N_DEV).transpose(2, 0, 1, 3)
    return g.reshape(g.shape[0], N_DEV, g.shape[1] // N_DEV).transpose(1, 0, 2)


RIDING = [n for n in SHARDED if n != 'lru_conv_w']
FORWARD_CARRIERS = {'z_in': ['w_ffn_gate'], 's5_scan': ['w_ffn_down'], 'fox_fwd': ['w_in', 'w_branch'],
                    'ffn_in': ['w_ffn_up'], 'ffn_out': ['s5_w_glu', 'w_out']}
OWN_GRAD_CARRIERS = {'fox_bwd_kv': ['w_ffn_gate', 'w_ffn_up'], 's5_scan_bwd': ['w_ffn_down']}
NEXT_GRAD_CARRIERS = {'ffn_out_dx': ['s5_w_glu', 'w_out'], 'fox_bwd_q': ['w_in', 'w_branch']}
NEXT_GRADS = [n for names in NEXT_GRAD_CARRIERS.values() for n in names]


class _Fsdp:
    def __init__(self, weights):
        self.weights_in = weights
        self.shard = {n: _shard_2d(weights[n]).astype(BF16) for n in RIDING}
        self.rows = {n: self.shard[n].shape[0] // DEPTH for n in RIDING}
        first = _exchange("gather_first_layer", [self.layer_shard(n, 0) for n in RIDING]
                          + [_shard_2d(weights['lru_conv_w'])], scatter=False)
        self.first = first[:-1]
        self.conv = first[-1].reshape((N_DEV,) + weights['lru_conv_w'].shape)
        self.outgoing = None
        self.incoming = {n: [None] * DEPTH for n in RIDING}

    def layer_shard(self, n, l):
        return self.shard[n][l * self.rows[n]:(l + 1) * self.rows[n]]

    def weights(self, l, carried):
        if l == 0:
            got = dict(zip(RIDING, self.first))
        else:
            got = {n: t for c, names in FORWARD_CARRIERS.items() for n, t in zip(names, carried[c])}
        w = {n: self.weights_in[n][l] for n in REPLICATED}
        for n in RIDING:
            w[n] = _full_layer_weight(n, got[n].reshape((N_DEV,) + self.weights_in[n].shape[1:]))
        w['lru_conv_w'] = _full_layer_weight('lru_conv_w', self.conv[:, l])
        return w

    def forward_riders(self, l):
        if l + 1 == DEPTH:
            return {}
        return {c: ([self.layer_shard(n, l + 1) for n in names], False) for c, names in FORWARD_CARRIERS.items()}

    def blocks(self, grads, names):
        return [_split_layer_grad(n, grads[n]).reshape(N_DEV, self.rows[n], -1).astype(BF16) for n in names]

    def backward_riders(self, l):
        riders = {'own': lambda grads: {c: (self.blocks(grads, names), True)
                                        for c, names in OWN_GRAD_CARRIERS.items()}}
        if self.outgoing is not None:
            for c, names in NEXT_GRAD_CARRIERS.items():
                riders[c] = ([self.outgoing[n] for n in names], True)
        return riders

    def collect(self, l, grads, carried):
        for c, names in NEXT_GRAD_CARRIERS.items():
            for n, t in zip(names, carried[c]):
                self.incoming[n][l + 1] = t
        for c, names in OWN_GRAD_CARRIERS.items():
            for n, t in zip(names, carried[c]):
                self.incoming[n][l] = t
        self.outgoing = dict(zip(NEXT_GRADS, self.blocks(grads, NEXT_GRADS)))

    def finish(self, grads):
        conv = jnp.stack([_split_layer_grad('lru_conv_w', grads[l]['lru_conv_w']) for l in range(DEPTH)], axis=1)
        conv = conv.reshape(N_DEV, -1, conv.shape[-1]).astype(F32)
        last = _exchange("scatter_last_layer", [self.outgoing[n] for n in NEXT_GRADS] + [conv], scatter=True)
        for n, t in zip(NEXT_GRADS, last[:-1]):
            self.incoming[n][0] = t
        return {**self.incoming, 'lru_conv_w': [last[-1]]}


def kernel(x, w_in, b_f, b_gate, s5_a_re, s5_a_im, s5_log_dt, s5_b_re, s5_b_im, s5_c_re, s5_c_im, s5_d, s5_w_glu, s5_b_glu, lru_conv_w, lru_conv_b, lru_w_a, lru_b_a, lru_w_x, lru_b_x, lru_lambda, w_branch, w_out, ln1_g, ln1_b, w_ffn_gate, w_ffn_up, w_ffn_down, ln2_g, ln2_b, loss_target, m_w_in, m_b_f, m_b_gate, m_s5_a_re, m_s5_a_im, m_s5_log_dt, m_s5_b_re, m_s5_b_im, m_s5_c_re, m_s5_c_im, m_s5_d, m_s5_w_glu, m_s5_b_glu, m_lru_conv_w, m_lru_conv_b, m_lru_w_a, m_lru_b_a, m_lru_w_x, m_lru_b_x, m_lru_lambda, m_w_branch, m_w_out, m_ln1_g, m_ln1_b, m_w_ffn_gate, m_w_ffn_up, m_w_ffn_down, m_ln2_g, m_ln2_b, v_w_in, v_b_f, v_b_gate, v_s5_a_re, v_s5_a_im, v_s5_log_dt, v_s5_b_re, v_s5_b_im, v_s5_c_re, v_s5_c_im, v_s5_d, v_s5_w_glu, v_s5_b_glu, v_lru_conv_w, v_lru_conv_b, v_lru_w_a, v_lru_b_a, v_lru_w_x, v_lru_b_x, v_lru_lambda, v_w_branch, v_w_out, v_ln1_g, v_ln1_b, v_w_ffn_gate, v_w_ffn_up, v_w_ffn_down, v_ln2_g, v_ln2_b):
    given = dict(locals())
    weights = {n: given[n] for n in WEIGHTS}
    moments_m = {n: given['m_' + n] for n in WEIGHTS}
    moments_v = {n: given['v_' + n] for n in WEIGHTS}

    hooks = _Fsdp(weights)
    loss_local, dx, grads = _forward_backward(x[0], loss_target[0], hooks)
    loss = lax.psum(loss_local, MESH_AXES)
    incoming = hooks.finish(grads)

    new = {}
    for n in SHARDED:
        res = _adamw("adamw_" + n, _shard_2d(weights[n]), incoming[n], _shard_2d(moments_m[n]), _shard_2d(moments_v[n]))
        new[n] = [t.reshape(weights[n].shape) for t in res]

    flat = jnp.concatenate([jnp.stack([grads[l][n] for l in range(DEPTH)]).astype(F32).reshape(-1) for n in REPLICATED])
    rows = -(-flat.shape[0] // (LANES * 256)) * 256
    packed = jnp.pad(flat, (0, rows * LANES - flat.shape[0])).reshape(rows, LANES).astype(BF16)
    (arrived,) = _exchange("gather_small_grads", [packed], scatter=False)
    total, at = _sum_parts("sum_small_grads", arrived).reshape(-1), 0
    for n in REPLICATED:
        w2 = _shard_2d(weights[n])
        grad = total[at:at + w2.size].reshape((1,) + w2.shape)
        at += w2.size
        res = _adamw("adamw_" + n, w2, [grad], _shard_2d(moments_m[n]), _shard_2d(moments_v[n]))
        new[n] = [t.reshape(weights[n].shape) for t in res]

    return (loss, dx[None], *[new[n][0] for n in WEIGHTS], *[new[n][1] for n in WEIGHTS],
            *[new[n][2] for n in WEIGHTS], *[new[n][3] for n in WEIGHTS])
```
